```python
import math
import jax
import jax.numpy as jnp
from jax import lax
import numpy as np

D_MODEL = 2048
BATCH = 2
SEQ = 8192
DEPTH = 1

N_META = 16
HEAD_DIM = 64
ATTN_WIDTH = D_MODEL // 2
N_Q_HEADS = ATTN_WIDTH // HEAD_DIM
N_KV_HEADS = 4
Q_PER_KV = N_Q_HEADS // N_KV_HEADS
KV_WIDTH = N_KV_HEADS * HEAD_DIM
WINDOW = 128
ATTN_BLOCK = 128
ROPE_THETA = 10000.0
SSM_WIDTH = D_MODEL - ATTN_WIDTH
SSM_GROUP = 16
N_SSM_GROUPS = SSM_WIDTH // SSM_GROUP
SSM_STATE = 64
MIX_WIDTH = ATTN_WIDTH + SSM_WIDTH
IN_WIDTH = ATTN_WIDTH + 2 * KV_WIDTH + SSM_WIDTH
N_EXPERTS = 32
TOP_K = 4
D_FF = D_MODEL
SWIGLU_LIMIT = 7.0
SWIGLU_ALPHA = 1.702
MOE_ROW_BLOCK = 256
NORM_EPS = 1e-5
DT_MIN = 1e-3
DT_MAX = 1e-1
C_INIT_SCALE = 0.5

kernel_name = "hymba_swa_sink_s5_moe_layer"


def rms_norm(t, gain):
    t32 = t.astype(jnp.float32)
    y = t32 * lax.rsqrt(jnp.mean(t32 * t32, axis=-1, keepdims=True) + NORM_EPS)
    return (y * gain.astype(jnp.float32)).astype(t.dtype)


def rope(t, positions):
    half = HEAD_DIM // 2
    inv_freq = jnp.power(ROPE_THETA, -jnp.arange(half, dtype=jnp.float32) / half)
    ang = positions.astype(jnp.float32)[:, None] * inv_freq[None, :]
    cos = jnp.cos(ang)[None, :, None, :]
    sin = jnp.sin(ang)[None, :, None, :]
    t32 = t.astype(jnp.float32)
    t1, t2 = t32[..., :half], t32[..., half:]
    return jnp.concatenate([t1 * cos - t2 * sin, t2 * cos + t1 * sin], axis=-1).astype(t.dtype)


def sink_softmax(scores, mask, sink):
    scores = jnp.where(mask, scores, jnp.finfo(jnp.float32).min)
    sink_col = jnp.broadcast_to(sink[:, :, None, None], scores.shape[:-1] + (1,))
    p = jax.nn.softmax(jnp.concatenate([scores, sink_col], axis=-1), axis=-1)
    return p[..., :-1]


def sliding_window_attention(q, k, v, sinks):
    bsz, L = q.shape[0], q.shape[1]
    S = L - N_META
    nb = S // ATTN_BLOCK
    scale = HEAD_DIM ** -0.5
    sink = sinks.astype(jnp.float32).reshape(N_KV_HEADS, Q_PER_KV)
    q = q.reshape(bsz, L, N_KV_HEADS, Q_PER_KV, HEAD_DIM)
    qm, qr = q[:, :N_META], q[:, N_META:]
    km, kr = k[:, :N_META], k[:, N_META:]
    vm, vr = v[:, :N_META], v[:, N_META:]

    s_m = jnp.einsum('bqhgd,bkhd->bhgqk', qm, km).astype(jnp.float32) * scale
    mask_m = jnp.tril(jnp.ones((N_META, N_META), dtype=bool))
    p_m = sink_softmax(s_m, mask_m, sink).astype(v.dtype)
    o_m = jnp.einsum('bhgqk,bkhd->bqhgd', p_m, vm).reshape(bsz, N_META, ATTN_WIDTH)

    def banded(t_real, t_meta):
        tb = t_real.reshape(bsz, nb, ATTN_BLOCK, N_KV_HEADS, HEAD_DIM)
        prev = jnp.concatenate([jnp.zeros_like(tb[:, :1]), tb[:, :-1]], axis=1)
        meta = jnp.broadcast_to(t_meta[:, None], (bsz, nb, N_META, N_KV_HEADS, HEAD_DIM))
        return jnp.concatenate([meta, prev, tb], axis=2)

    kc = banded(kr, km)
    vc = banded(vr, vm)
    qb = qr.reshape(bsz, nb, ATTN_BLOCK, N_KV_HEADS, Q_PER_KV, HEAD_DIM)
    qi = jnp.arange(ATTN_BLOCK)[:, None]
    kj = jnp.arange(2 * ATTN_BLOCK)[None, :] - ATTN_BLOCK
    diff = qi - kj
    band = (diff >= 0) & (diff < WINDOW)
    has_prev = (jnp.arange(nb) > 0)[:, None, None]
    band = band[None] & (has_prev | (kj >= 0)[None])
    mask = jnp.concatenate([jnp.ones((nb, ATTN_BLOCK, N_META), dtype=bool), band], axis=-1)
    s = jnp.einsum('bnqhgd,bnkhd->bnhgqk', qb, kc).astype(jnp.float32) * scale
    p = sink_softmax(s, mask[None, :, None, None], sink).astype(v.dtype)
    o_r = jnp.einsum('bnhgqk,bnkhd->bnqhgd', p, vc).reshape(bsz, S, ATTN_WIDTH)
    return jnp.concatenate([o_m, o_r], axis=1)


def s5_ssm(u, a_re, a_im, log_dt, b_re, b_im, c_re, c_im, d_skip, w_glu, b_glu):
    f32 = jnp.float32
    bsz, L = u.shape[0], u.shape[1]
    ug = u.astype(f32).reshape(bsz, L, N_SSM_GROUPS, SSM_GROUP)
    dt = jnp.exp(log_dt.astype(f32))[:, None]
    lam_re = jnp.minimum(a_re.astype(f32), -1e-4)
    lam_im = a_im.astype(f32)
    z_re, z_im = lam_re * dt, lam_im * dt
    mag = jnp.exp(z_re)
    abar_re, abar_im = mag * jnp.cos(z_im), mag * jnp.sin(z_im)
    den = lam_re * lam_re + lam_im * lam_im
    n_re, n_im = abar_re - 1.0, abar_im
    coef_re = (n_re * lam_re + n_im * lam_im) / den
    coef_im = (n_im * lam_re - n_re * lam_im) / den
    br, bi = b_re.astype(f32), b_im.astype(f32)
    bb_re = coef_re[..., None] * br - coef_im[..., None] * bi
    bb_im = coef_re[..., None] * bi + coef_im[..., None] * br
    x_re = jnp.einsum('blgc,gpc->lbgp', ug, bb_re)
    x_im = jnp.einsum('blgc,gpc->lbgp', ug, bb_im)
    a_re_t = jnp.broadcast_to(abar_re[None, None], (L, 1, N_SSM_GROUPS, SSM_STATE))
    a_im_t = jnp.broadcast_to(abar_im[None, None], (L, 1, N_SSM_GROUPS, SSM_STATE))

    def combine(left, right):
        ar_l, ai_l, br_l, bi_l = left
        ar_r, ai_r, br_r, bi_r = right
        return (ar_r * ar_l - ai_r * ai_l,
                ar_r * ai_l + ai_r * ar_l,
                ar_r * br_l - ai_r * bi_l + br_r,
                ar_r * bi_l + ai_r * br_l + bi_r)

    _, _, h_re, h_im = lax.associative_scan(combine, (a_re_t, a_im_t, x_re, x_im), axis=0)
    y = (jnp.einsum('gcp,lbgp->blgc', c_re.astype(f32), h_re)
         - jnp.einsum('gcp,lbgp->blgc', c_im.astype(f32), h_im)
         + d_skip.astype(f32).reshape(N_SSM_GROUPS, SSM_GROUP) * ug)
    y = jax.nn.gelu(y.reshape(bsz, L, SSM_WIDTH))
    y = y * jax.nn.sigmoid(y @ w_glu.astype(f32) + b_glu.astype(f32))
    return y.astype(u.dtype)


def moe_ffn(x, w_router, b_router, w1, b1, w2, b2):
    bsz, L, D = x.shape
    N = bsz * L
    A = N * TOP_K
    xf = x.reshape(N, D)
    logits = (xf @ w_router + b_router).astype(jnp.float32)
    top_val, top_idx = lax.top_k(logits, TOP_K)
    gates = jax.nn.softmax(top_val, axis=-1)
    e_flat = top_idx.reshape(A)
    order = jnp.argsort(e_flat)
    e_sorted = e_flat[order]
    tok_sorted = order // TOP_K
    counts = jnp.bincount(e_flat, length=N_EXPERTS)
    starts = jnp.cumsum(counts) - counts
    padded = ((counts + MOE_ROW_BLOCK - 1) // MOE_ROW_BLOCK) * MOE_ROW_BLOCK
    pad_ends = jnp.cumsum(padded)
    pad_starts = pad_ends - padded
    dest = pad_starts[e_sorted] + (jnp.arange(A) - starts[e_sorted])
    n_blocks = (A + MOE_ROW_BLOCK - 1) // MOE_ROW_BLOCK + N_EXPERTS
    n_rows = n_blocks * MOE_ROW_BLOCK
    row_tok = jnp.full((n_rows,), N, dtype=jnp.int32).at[dest].set(tok_sorted.astype(jnp.int32))
    x_pad = jnp.concatenate([xf, jnp.zeros((1, D), xf.dtype)], axis=0)
    xs = x_pad[row_tok].reshape(n_blocks, MOE_ROW_BLOCK, D)
    block_start = jnp.arange(n_blocks) * MOE_ROW_BLOCK
    block_e = jnp.minimum(jnp.searchsorted(pad_ends, block_start, side='right'), N_EXPERTS - 1)

    def expert_rows(args):
        xb, e = args
        hdn = xb @ w1[e] + b1[e]
        g = jnp.minimum(hdn[:, 0::2], SWIGLU_LIMIT)
        up = jnp.clip(hdn[:, 1::2], -SWIGLU_LIMIT, SWIGLU_LIMIT)
        act = g * jax.nn.sigmoid(SWIGLU_ALPHA * g) * (up + 1.0)
        return act @ w2[e] + b2[e]

    ys = lax.map(expert_rows, (xs, block_e)).reshape(n_rows, D)
    y_sorted = ys[dest].astype(jnp.float32)
    w_sorted = gates.reshape(A)[order]
    out = jnp.zeros((N, D), jnp.float32).at[tok_sorted].add(y_sorted * w_sorted[:, None])
    return out.astype(x.dtype).reshape(bsz, L, D)


def setup_inputs(seed: int = 0) -> dict:
    key = jax.random.key(seed)
    ks = jax.random.split(key, 28)
    f32 = jnp.float32

    def nrm(k, shape, s):
        return jax.random.normal(k, shape, f32) * s

    n_idx = jnp.arange(SSM_STATE, dtype=f32)
    return {
        "x": nrm(ks[0], (BATCH, SEQ, D_MODEL), 1.0),
        "meta_tokens": nrm(ks[1], (N_META, D_MODEL), 1.0),
        "g_mix": 1.0 + nrm(ks[2], (DEPTH, D_MODEL), 0.02),
        "w_in": nrm(ks[3], (DEPTH, D_MODEL, IN_WIDTH), D_MODEL ** -0.5),
        "b_in": nrm(ks[4], (DEPTH, IN_WIDTH), 0.01),
        "attn_sinks": nrm(ks[5], (DEPTH, N_Q_HEADS), 0.5),
        "ssm_a_re": -0.5 + nrm(ks[6], (DEPTH, N_SSM_GROUPS, SSM_STATE), 0.01),
        "ssm_a_im": math.pi * n_idx + nrm(ks[7], (DEPTH, N_SSM_GROUPS, SSM_STATE), 0.01),
        "ssm_log_dt": jax.random.uniform(ks[8], (DEPTH, N_SSM_GROUPS), f32, math.log(DT_MIN), math.log(DT_MAX)),
        "ssm_b_re": nrm(ks[9], (DEPTH, N_SSM_GROUPS, SSM_STATE, SSM_GROUP), (2 * SSM_GROUP) ** -0.5),
        "ssm_b_im": nrm(ks[10], (DEPTH, N_SSM_GROUPS, SSM_STATE, SSM_GROUP), (2 * SSM_GROUP) ** -0.5),
        "ssm_c_re": nrm(ks[11], (DEPTH, N_SSM_GROUPS, SSM_GROUP, SSM_STATE), C_INIT_SCALE),
        "ssm_c_im": nrm(ks[12], (DEPTH, N_SSM_GROUPS, SSM_GROUP, SSM_STATE), C_INIT_SCALE),
        "ssm_d": nrm(ks[13], (DEPTH, SSM_WIDTH), 1.0),
        "w_glu": nrm(ks[14], (DEPTH, SSM_WIDTH, SSM_WIDTH), SSM_WIDTH ** -0.5),
        "b_glu": nrm(ks[15], (DEPTH, SSM_WIDTH), 0.01),
        "g_attn_out": 1.0 + nrm(ks[16], (DEPTH, ATTN_WIDTH), 0.02),
        "g_ssm_out": 1.0 + nrm(ks[17], (DEPTH, SSM_WIDTH), 0.02),
        "w_out": nrm(ks[18], (DEPTH, MIX_WIDTH, D_MODEL), MIX_WIDTH ** -0.5),
        "b_out": nrm(ks[19], (DEPTH, D_MODEL), 0.01),
        "g_ffn": 1.0 + nrm(ks[20], (DEPTH, D_MODEL), 0.02),
        "w_router": nrm(ks[21], (DEPTH, D_MODEL, N_EXPERTS), D_MODEL ** -0.5),
        "b_router": nrm(ks[22], (DEPTH, N_EXPERTS), 0.01),
        "w_mlp1": nrm(ks[23], (DEPTH, N_EXPERTS, D_MODEL, 2 * D_FF), D_MODEL ** -0.5),
        "b_mlp1": nrm(ks[24], (DEPTH, N_EXPERTS, 2 * D_FF), 0.01),
        "w_mlp2": nrm(ks[25], (DEPTH, N_EXPERTS, D_FF, D_MODEL), D_FF ** -0.5),
        "b_mlp2": nrm(ks[26], (DEPTH, N_EXPERTS, D_MODEL), 0.01),
        "g_final": 1.0 + nrm(ks[27], (D_MODEL,), 0.02),
    }


def reference(x, meta_tokens, g_mix, w_in, b_in, attn_sinks, ssm_a_re, ssm_a_im, ssm_log_dt,
              ssm_b_re, ssm_b_im, ssm_c_re, ssm_c_im, ssm_d, w_glu, b_glu, g_attn_out, g_ssm_out,
              w_out, b_out, g_ffn, w_router, b_router, w_mlp1, b_mlp1, w_mlp2, b_mlp2, g_final):
    bsz = x.shape[0]
    meta = jnp.broadcast_to(meta_tokens[None].astype(x.dtype), (bsz, N_META, D_MODEL))
    h = jnp.concatenate([meta, x], axis=1)
    L = h.shape[1]
    positions = jnp.arange(L, dtype=jnp.int32)
    splits = [ATTN_WIDTH, ATTN_WIDTH + KV_WIDTH, ATTN_WIDTH + 2 * KV_WIDTH]
    for l in range(DEPTH):
        n = rms_norm(h, g_mix[l])
        z = n @ w_in[l] + b_in[l]
        q, k, v, u = jnp.split(z, splits, axis=-1)
        q = rope(q.reshape(bsz, L, N_Q_HEADS, HEAD_DIM), positions)
        k = rope(k.reshape(bsz, L, N_KV_HEADS, HEAD_DIM), positions)
        v = v.reshape(bsz, L, N_KV_HEADS, HEAD_DIM)
        y_attn = sliding_window_attention(q, k, v, attn_sinks[l])
        y_ssm = s5_ssm(u, ssm_a_re[l], ssm_a_im[l], ssm_log_dt[l], ssm_b_re[l], ssm_b_im[l],
                       ssm_c_re[l], ssm_c_im[l], ssm_d[l], w_glu[l], b_glu[l])
        mixed = jnp.concatenate([rms_norm(y_attn, g_attn_out[l]), rms_norm(y_ssm, g_ssm_out[l])], axis=-1)
        h = h + mixed @ w_out[l] + b_out[l]
        h = h + moe_ffn(rms_norm(h, g_ffn[l]), w_router[l], b_router[l],
                        w_mlp1[l], b_mlp1[l], w_mlp2[l], b_mlp2[l])
    return rms_norm(h, g_final)[:, N_META:]
```

```python
import functools
import math

import jax
import jax.numpy as jnp
from jax import lax
from jax.experimental import pallas as pl
from jax.experimental.pallas import tpu as pltpu

F32 = jnp.float32
BF16 = jnp.bfloat16

D_MODEL = 2048
N_META = 16
HEAD_DIM = 64
ATTN_WIDTH = 1024
N_Q_HEADS = 16
N_KV_HEADS = 4
Q_PER_KV = 4
KV_WIDTH = 256
ATTN_BLOCK = 128
ROPE_THETA = 10000.0
SSM_WIDTH = 1024
SSM_GROUP = 16
N_SSM_GROUPS = 64
SSM_STATE = 64
SSM_LANES = N_SSM_GROUPS * SSM_STATE
IN_WIDTH = 2560
N_EXPERTS = 32
TOP_K = 4
D_FF = 2048
SWIGLU_LIMIT = 7.0
SWIGLU_ALPHA = 1.702
NORM_EPS = 1e-5

LANES = 128
SUBLANES = 8
VMEM_LIMIT = 56 * 1024 * 1024

ROW_TILE = 512
SSM_CHUNK = 256
SSM_COL = 256
SSM_COL_TILES = SSM_WIDTH // SSM_COL
SSM_COL_LANES = SSM_LANES // SSM_COL_TILES
SCAN_LANES = 512
MOE_ROWS = 512
MOE_FF_TILE = 512
ROUTER_LANES = 128


def _rms(t, gain):
    return t * lax.rsqrt(jnp.mean(t * t, axis=-1, keepdims=True) + NORM_EPS) * gain


def _sigmoid(t):
    return 1.0 / (1.0 + jnp.exp(-t))


def _inproj_kernel(x_ref, g_ref, w_ref, b_ref, cos_ref, sin_ref, q_ref, k_ref, v_ref, u_ref):
    n = _rms(x_ref[...], g_ref[...]).astype(BF16)
    cos = cos_ref[...]
    sin = sin_ref[...]
    lane = lax.broadcasted_iota(jnp.int32, cos.shape, 1)
    first_half = (lane % HEAD_DIM) < (HEAD_DIM // 2)

    def proj(c0, c1):
        return jnp.dot(n, w_ref[:, c0:c1], preferred_element_type=F32) + b_ref[:, c0:c1]

    def rope(t):
        partner = jnp.where(first_half, pltpu.roll(t, LANES - HEAD_DIM // 2, 1),
                            pltpu.roll(t, HEAD_DIM // 2, 1))
        return t * cos + partner * sin

    scale = HEAD_DIM ** -0.5
    for j in range(ATTN_WIDTH // LANES):
        z = proj(j * LANES, (j + 1) * LANES)
        q_ref[:, j * LANES:(j + 1) * LANES] = (rope(z) * scale).astype(BF16)
    for j in range(KV_WIDTH // LANES):
        c0 = ATTN_WIDTH + j * LANES
        k_ref[:, j * LANES:(j + 1) * LANES] = rope(proj(c0, c0 + LANES)).astype(BF16)
    c0 = ATTN_WIDTH + KV_WIDTH
    v_ref[...] = proj(c0, c0 + KV_WIDTH).astype(BF16)
    c0 = ATTN_WIDTH + 2 * KV_WIDTH
    u_ref[...] = proj(c0, c0 + SSM_WIDTH)


def _inproj(xf, g_mix, w_in_bf, b_in, cos_t, sin_t, row_tile):
    rows = xf.shape[0]
    tab_blocks = cos_t.shape[0] // row_tile
    row = lambda i: (i, 0)
    tab = lambda i: (i % tab_blocks, 0)
    const = lambda i: (0, 0)
    return pl.pallas_call(
        _inproj_kernel,
        grid=(rows // row_tile,),
        in_specs=[
            pl.BlockSpec((row_tile, D_MODEL), row),
            pl.BlockSpec((1, D_MODEL), const),
            pl.BlockSpec((D_MODEL, IN_WIDTH), const),
            pl.BlockSpec((1, IN_WIDTH), const),
            pl.BlockSpec((row_tile, LANES), tab),
            pl.BlockSpec((row_tile, LANES), tab),
        ],
        out_specs=[
            pl.BlockSpec((row_tile, ATTN_WIDTH), row),
            pl.BlockSpec((row_tile, KV_WIDTH), row),
            pl.BlockSpec((row_tile, KV_WIDTH), row),
            pl.BlockSpec((row_tile, SSM_WIDTH), row),
        ],
        out_shape=[
            jax.ShapeDtypeStruct((rows, ATTN_WIDTH), BF16),
            jax.ShapeDtypeStruct((rows, KV_WIDTH), BF16),
            jax.ShapeDtypeStruct((rows, KV_WIDTH), BF16),
            jax.ShapeDtypeStruct((rows, SSM_WIDTH), F32),
        ],
        compiler_params=pltpu.CompilerParams(
            dimension_semantics=("arbitrary",), vmem_limit_bytes=VMEM_LIMIT),
        name="inproj",
    )(xf, g_mix, w_in_bf, b_in, cos_t, sin_t)


def _rope_tables(positions):
    half = HEAD_DIM // 2
    inv_freq = jnp.power(ROPE_THETA, -jnp.arange(half, dtype=F32) / half)
    ang = positions.astype(F32)[:, None] * inv_freq[None, :]
    cos = jnp.tile(jnp.cos(ang), (1, LANES // half))
    sin = jnp.tile(jnp.sin(ang), (1, LANES // half))
    sign = jnp.where((jnp.arange(LANES) % HEAD_DIM) < half, -1.0, 1.0).astype(F32)
    return cos, sin * sign[None, :]


def _attn_kernel(q_ref, kc_ref, kp_ref, vc_ref, vp_ref, km_ref, vm_ref, sink_ref, g_ref, o_ref):
    n = pl.program_id(1)
    n_keys = N_META + 2 * ATTN_BLOCK
    qi = lax.broadcasted_iota(jnp.int32, (ATTN_BLOCK, n_keys), 0)
    col = lax.broadcasted_iota(jnp.int32, (ATTN_BLOCK, n_keys), 1)
    prev_j = col - N_META
    cur_j = col - N_META - ATTN_BLOCK
    no_prev = jnp.where(n > 0, 0, ATTN_BLOCK)
    valid = ((col < N_META) | ((cur_j >= 0) & (cur_j <= qi))
             | ((prev_j >= 0) & (cur_j < 0) & (prev_j > qi + no_prev)))
    heads = []
    for hk in range(N_KV_HEADS):
        ks = slice(hk * HEAD_DIM, (hk + 1) * HEAD_DIM)
        k_all = jnp.concatenate([km_ref[:, ks], kp_ref[:, ks], kc_ref[:, ks]], axis=0)
        v_all = jnp.concatenate([vm_ref[:, ks], vp_ref[:, ks], vc_ref[:, ks]], axis=0)
        for g in range(Q_PER_KV):
            h = hk * Q_PER_KV + g
            qh = q_ref[:, h * HEAD_DIM:(h + 1) * HEAD_DIM]
            s = lax.dot_general(qh, k_all, (((1,), (1,)), ((), ())),
                                preferred_element_type=F32)
            s = jnp.where(valid, s, -1e30)
            sink = sink_ref[:, h:h + 1]
            m = jnp.maximum(jnp.max(s, axis=1, keepdims=True), sink)
            p = jnp.exp(s - m)
            denom = jnp.sum(p, axis=1, keepdims=True) + jnp.exp(sink - m)
            o = jnp.dot(p.astype(BF16), v_all, preferred_element_type=F32)
            heads.append(o / denom)
    y = jnp.concatenate(heads, axis=1)
    o_ref[...] = _rms(y, g_ref[...]).astype(BF16)


def _attention(q, k, v, k_meta, v_meta, sinks, g_attn, bsz, seq):
    nb = seq // ATTN_BLOCK
    cur = lambda b, n: (b * nb + n, 0)
    prev = lambda b, n: (b * nb + jnp.maximum(n - 1, 0), 0)
    const = lambda b, n: (0, 0)
    return pl.pallas_call(
        _attn_kernel,
        grid=(bsz, nb),
        in_specs=[
            pl.BlockSpec((ATTN_BLOCK, ATTN_WIDTH), cur),
            pl.BlockSpec((ATTN_BLOCK, KV_WIDTH), cur),
            pl.BlockSpec((ATTN_BLOCK, KV_WIDTH), prev),
            pl.BlockSpec((ATTN_BLOCK, KV_WIDTH), cur),
            pl.BlockSpec((ATTN_BLOCK, KV_WIDTH), prev),
            pl.BlockSpec((N_META, KV_WIDTH), const),
            pl.BlockSpec((N_META, KV_WIDTH), const),
            pl.BlockSpec((1, N_Q_HEADS), const),
            pl.BlockSpec((1, ATTN_WIDTH), const),
        ],
        out_specs=pl.BlockSpec((ATTN_BLOCK, ATTN_WIDTH), cur),
        out_shape=jax.ShapeDtypeStruct((bsz * seq, ATTN_WIDTH), BF16),
        compiler_params=pltpu.CompilerParams(
            dimension_semantics=("arbitrary", "arbitrary"), vmem_limit_bytes=VMEM_LIMIT),
        name="attention",
    )(q, k, k, v, v, k_meta, v_meta, sinks, g_attn)


def _ssm_params(a_re, a_im, log_dt, b_re, b_im, c_re, c_im, seg_len):
    dt = jnp.exp(log_dt.astype(F32))[:, None]
    lam_re = jnp.minimum(a_re.astype(F32), -1e-4)
    lam_im = a_im.astype(F32)
    z_re, z_im = lam_re * dt, lam_im * dt
    mag = jnp.exp(z_re)
    abar_re, abar_im = mag * jnp.cos(z_im), mag * jnp.sin(z_im)
    den = lam_re * lam_re + lam_im * lam_im
    n_re, n_im = abar_re - 1.0, abar_im
    coef_re = (n_re * lam_re + n_im * lam_im) / den
    coef_im = (n_im * lam_re - n_re * lam_im) / den
    br, bi = b_re.astype(F32), b_im.astype(F32)
    bb_re = coef_re[..., None] * br - coef_im[..., None] * bi
    bb_im = coef_re[..., None] * bi + coef_im[..., None] * br

    groups_per_tile = SSM_COL // SSM_GROUP
    eye = jnp.eye(groups_per_tile, dtype=F32)

    def in_tile(bb):
        t = bb.reshape(SSM_COL_TILES, groups_per_tile, SSM_STATE, SSM_GROUP)
        t = jnp.einsum('tgpc,gh->tgchp', t, eye)
        return t.reshape(SSM_COL_TILES, SSM_COL, SSM_COL_LANES).astype(BF16)

    def out_tile(cc):
        t = cc.reshape(SSM_COL_TILES, groups_per_tile, SSM_GROUP, SSM_STATE)
        t = jnp.einsum('tgcp,gh->tgphc', t, eye)
        return t.reshape(SSM_COL_TILES, SSM_COL_LANES, SSM_COL).astype(BF16)

    def powers(exps):
        e = exps.astype(F32)[:, None, None]
        pm = jnp.exp(e * z_re[None])
        return jnp.stack([(pm * jnp.cos(e * z_im[None])).reshape(len(exps), SSM_LANES),
                          (pm * jnp.sin(e * z_im[None])).reshape(len(exps), SSM_LANES)])

    return dict(
        b_re=in_tile(bb_re), b_im=in_tile(bb_im),
        c_re=out_tile(c_re.astype(F32)), c_im=out_tile(-c_im.astype(F32)),
        a=jnp.stack([abar_re.reshape(1, SSM_LANES), abar_im.reshape(1, SSM_LANES)]),
        a_seg=powers(jnp.array([seg_len])),
        a_pow=powers(jnp.arange(1, seg_len + 1)),
        a_meta=powers(jnp.arange(N_META - 1, -1, -1)),
    )


def _gelu_tanh(t):
    return 0.5 * t * (1.0 + jnp.tanh(math.sqrt(2.0 / math.pi) * (t + 0.044715 * (t * t * t))))


def _ssm_kernel(u_ref, um_ref, bre_ref, bim_ref, cre_ref, cim_ref, a_ref, aseg_ref, apow_ref,
                ameta_ref, d_ref, wglu_ref, bglu_ref, g_ref, o_ref,
                xre, xim, hre, him, car_re, car_im, cin_re, cin_im, u_scr, y_scr):
    chunk = u_ref.shape[0]
    seg = chunk // SUBLANES

    @pl.when(pl.program_id(1) == 0)
    def _():
        um = um_ref[...].astype(BF16)
        for ct in range(SSM_COL_TILES):
            ub = um[:, ct * SSM_COL:(ct + 1) * SSM_COL]
            ls = slice(ct * SSM_COL_LANES, (ct + 1) * SSM_COL_LANES)
            xr = jnp.dot(ub, bre_ref[ct], preferred_element_type=F32)
            xi = jnp.dot(ub, bim_ref[ct], preferred_element_type=F32)
            pr = ameta_ref[0, :, ls]
            pi = ameta_ref[1, :, ls]
            car_re[:, ls] = jnp.sum(pr * xr - pi * xi, axis=0, keepdims=True)
            car_im[:, ls] = jnp.sum(pr * xi + pi * xr, axis=0, keepdims=True)

    n_lt = SSM_WIDTH // LANES
    for j in range(n_lt):
        u_scr[j] = u_ref[:, j * LANES:(j + 1) * LANES]
    up = jnp.concatenate(
        [jnp.concatenate([u_scr[j, pl.ds(k, SUBLANES, stride=seg), :] for j in range(n_lt)], axis=1)
         for k in range(seg)], axis=0).astype(BF16)
    for ct in range(SSM_COL_TILES):
        ub = up[:, ct * SSM_COL:(ct + 1) * SSM_COL]
        ls = slice(ct * SSM_COL_LANES, (ct + 1) * SSM_COL_LANES)
        xre[:, ls] = jnp.dot(ub, bre_ref[ct], preferred_element_type=F32)
        xim[:, ls] = jnp.dot(ub, bim_ref[ct], preferred_element_type=F32)

    for lb in range(SSM_LANES // SCAN_LANES):
        ls = slice(lb * SCAN_LANES, (lb + 1) * SCAN_LANES)
        ar = jnp.broadcast_to(a_ref[0, :, ls], (SUBLANES, SCAN_LANES))
        ai = jnp.broadcast_to(a_ref[1, :, ls], (SUBLANES, SCAN_LANES))

        def scan_body(k, carry):
            hr, hi = carry
            rows = pl.ds(pl.multiple_of(k * SUBLANES, SUBLANES), SUBLANES)
            nr = ar * hr - ai * hi + xre[rows, ls]
            ni = ar * hi + ai * hr + xim[rows, ls]
            xre[rows, ls] = nr
            xim[rows, ls] = ni
            return nr, ni

        zero = jnp.zeros((SUBLANES, SCAN_LANES), F32)
        lax.fori_loop(0, seg, scan_body, (zero, zero))

    cr = car_re[...]
    ci = car_im[...]
    sr = aseg_ref[0]
    si = aseg_ref[1]
    for r in range(SUBLANES):
        cin_re[r:r + 1, :] = cr
        cin_im[r:r + 1, :] = ci
        er = xre[chunk - SUBLANES + r:chunk - SUBLANES + r + 1, :]
        ei = xim[chunk - SUBLANES + r:chunk - SUBLANES + r + 1, :]
        cr, ci = sr * cr - si * ci + er, sr * ci + si * cr + ei
    car_re[...] = cr
    car_im[...] = ci

    pair = 2 * SUBLANES
    for lb in range(SSM_LANES // SCAN_LANES):
        ls = slice(lb * SCAN_LANES, (lb + 1) * SCAN_LANES)
        er = jnp.concatenate([cin_re[:, ls], cin_re[:, ls]], axis=0)
        ei = jnp.concatenate([cin_im[:, ls], cin_im[:, ls]], axis=0)

        def fix_body(k2, _):
            rows = pl.ds(pl.multiple_of(k2 * pair, pair), pair)
            k = 2 * k2
            pr = jnp.concatenate(
                [jnp.broadcast_to(apow_ref[0, pl.ds(k, 1), ls], (SUBLANES, SCAN_LANES)),
                 jnp.broadcast_to(apow_ref[0, pl.ds(k + 1, 1), ls], (SUBLANES, SCAN_LANES))], axis=0)
            pi = jnp.concatenate(
                [jnp.broadcast_to(apow_ref[1, pl.ds(k, 1), ls], (SUBLANES, SCAN_LANES)),
                 jnp.broadcast_to(apow_ref[1, pl.ds(k + 1, 1), ls], (SUBLANES, SCAN_LANES))], axis=0)
            hre[rows, ls] = (xre[rows, ls] + pr * er - pi * ei).astype(BF16)
            him[rows, ls] = (xim[rows, ls] + pr * ei + pi * er).astype(BF16)
            return 0

        lax.fori_loop(0, seg // 2, fix_body, 0)

    for ct in range(SSM_COL_TILES):
        ls = slice(ct * SSM_COL_LANES, (ct + 1) * SSM_COL_LANES)
        y = (jnp.dot(hre[:, ls], cre_ref[ct], preferred_element_type=F32)
             + jnp.dot(him[:, ls], cim_ref[ct], preferred_element_type=F32))
        for k in range(seg):
            for jj in range(SSM_COL // LANES):
                y_scr[ct * (SSM_COL // LANES) + jj, pl.ds(k, SUBLANES, stride=seg), :] = (
                    y[k * SUBLANES:(k + 1) * SUBLANES, jj * LANES:(jj + 1) * LANES])

    y = jnp.concatenate([y_scr[j] for j in range(n_lt)], axis=1)
    y = _gelu_tanh(y + d_ref[...] * u_ref[...])
    gate = jnp.dot(y.astype(BF16), wglu_ref[...], preferred_element_type=F32) + bglu_ref[...]
    y = y * _sigmoid(gate)
    o_ref[...] = _rms(y, g_ref[...]).astype(BF16)


def _ssm(u, u_meta, prm, d_skip, w_glu_bf, b_glu, g_ssm, bsz, seq, chunk):
    nc = seq // chunk
    seg = chunk // SUBLANES
    row = lambda b, c: (b * nc + c, 0)
    c2 = lambda b, c: (0, 0)
    c3 = lambda b, c: (0, 0, 0)
    return pl.pallas_call(
        _ssm_kernel,
        grid=(bsz, nc),
        in_specs=[
            pl.BlockSpec((chunk, SSM_WIDTH), row),
            pl.BlockSpec((N_META, SSM_WIDTH), c2),
            pl.BlockSpec((SSM_COL_TILES, SSM_COL, SSM_COL_LANES), c3),
            pl.BlockSpec((SSM_COL_TILES, SSM_COL, SSM_COL_LANES), c3),
            pl.BlockSpec((SSM_COL_TILES, SSM_COL_LANES, SSM_COL), c3),
            pl.BlockSpec((SSM_COL_TILES, SSM_COL_LANES, SSM_COL), c3),
            pl.BlockSpec((2, 1, SSM_LANES), c3),
            pl.BlockSpec((2, 1, SSM_LANES), c3),
            pl.BlockSpec((2, seg, SSM_LANES), c3),
            pl.BlockSpec((2, N_META, SSM_LANES), c3),
            pl.BlockSpec((1, SSM_WIDTH), c2),
            pl.BlockSpec((SSM_WIDTH, SSM_WIDTH), c2),
            pl.BlockSpec((1, SSM_WIDTH), c2),
            pl.BlockSpec((1, SSM_WIDTH), c2),
        ],
        out_specs=pl.BlockSpec((chunk, SSM_WIDTH), row),
        out_shape=jax.ShapeDtypeStruct((bsz * seq, SSM_WIDTH), BF16),
        scratch_shapes=[
            pltpu.VMEM((chunk, SSM_LANES), F32),
            pltpu.VMEM((chunk, SSM_LANES), F32),
            pltpu.VMEM((chunk, SSM_LANES), BF16),
            pltpu.VMEM((chunk, SSM_LANES), BF16),
            pltpu.VMEM((1, SSM_LANES), F32),
            pltpu.VMEM((1, SSM_LANES), F32),
            pltpu.VMEM((SUBLANES, SSM_LANES), F32),
            pltpu.VMEM((SUBLANES, SSM_LANES), F32),
            pltpu.VMEM((SSM_WIDTH // LANES, chunk, LANES), F32),
            pltpu.VMEM((SSM_WIDTH // LANES, chunk, LANES), F32),
        ],
        compiler_params=pltpu.CompilerParams(
            dimension_semantics=("arbitrary", "arbitrary"), vmem_limit_bytes=VMEM_LIMIT),
        name="ssm",
    )(u, u_meta, prm["b_re"], prm["b_im"], prm["c_re"], prm["c_im"], prm["a"], prm["a_seg"],
      prm["a_pow"], prm["a_meta"], d_skip, w_glu_bf, b_glu, g_ssm)


def _outproj_kernel(ya_ref, ys_ref, x_ref, wo_ref, bo_ref, gf_ref, wr_ref, br_ref,
                    h_ref, xn_ref, lg_ref):
    mix = (jnp.dot(ya_ref[...], wo_ref[:ATTN_WIDTH, :], preferred_element_type=F32)
           + jnp.dot(ys_ref[...], wo_ref[ATTN_WIDTH:, :], preferred_element_type=F32))
    h = x_ref[...] + mix + bo_ref[...]
    h_ref[...] = h
    n = _rms(h, gf_ref[...])
    xn_ref[...] = n.astype(BF16)
    lg_ref[...] = jnp.dot(n, wr_ref[...], preferred_element_type=F32,
                          precision=lax.Precision.HIGHEST) + br_ref[...]


def _outproj(y_attn, y_ssm, xf, w_out_bf, b_out, g_ffn, w_router_pad, b_router_pad, row_tile):
    rows = xf.shape[0]
    row = lambda i: (i, 0)
    const = lambda i: (0, 0)
    return pl.pallas_call(
        _outproj_kernel,
        grid=(rows // row_tile,),
        in_specs=[
            pl.BlockSpec((row_tile, ATTN_WIDTH), row),
            pl.BlockSpec((row_tile, SSM_WIDTH), row),
            pl.BlockSpec((row_tile, D_MODEL), row),
            pl.BlockSpec((D_MODEL, D_MODEL), const),
            pl.BlockSpec((1, D_MODEL), const),
            pl.BlockSpec((1, D_MODEL), const),
            pl.BlockSpec((D_MODEL, ROUTER_LANES), const),
            pl.BlockSpec((1, ROUTER_LANES), const),
        ],
        out_specs=[
            pl.BlockSpec((row_tile, D_MODEL), row),
            pl.BlockSpec((row_tile, D_MODEL), row),
            pl.BlockSpec((row_tile, ROUTER_LANES), row),
        ],
        out_shape=[
            jax.ShapeDtypeStruct((rows, D_MODEL), F32),
            jax.ShapeDtypeStruct((rows, D_MODEL), BF16),
            jax.ShapeDtypeStruct((rows, ROUTER_LANES), F32),
        ],
        compiler_params=pltpu.CompilerParams(
            dimension_semantics=("arbitrary",), vmem_limit_bytes=VMEM_LIMIT),
        name="outproj",
    )(y_attn, y_ssm, xf, w_out_bf, b_out, g_ffn, w_router_pad, b_router_pad)


def _moe_kernel(be_ref, nv_ref, x_ref, w1_ref, b1_ref, w2_ref, b2_ref, y_ref):
    i = pl.program_id(0)
    f = pl.program_id(1)
    tf = w2_ref.shape[1]

    @pl.when(i < nv_ref[0])
    def _():
        h = jnp.dot(x_ref[...], w1_ref[0], preferred_element_type=F32) + b1_ref[0]
        g = jnp.minimum(h[:, :tf], SWIGLU_LIMIT)
        up = jnp.clip(h[:, tf:], -SWIGLU_LIMIT, SWIGLU_LIMIT)
        act = g * _sigmoid(SWIGLU_ALPHA * g) * (up + 1.0)
        contrib = jnp.dot(act.astype(BF16), w2_ref[0], preferred_element_type=F32)

        @pl.when(f == 0)
        def _():
            y_ref[...] = contrib + b2_ref[0]

        @pl.when(f > 0)
        def _():
            y_ref[...] += contrib


def _experts(xs, block_e, n_valid, w1_t, b1_t, w2_bf, b2, rows_blk, tf):
    n_rows = xs.shape[0]
    n_blocks = n_rows // rows_blk
    nf = D_FF // tf

    def blk(i, nv):
        return jnp.minimum(i, nv[0] - 1)

    def ff(i, f, nv):
        return jnp.where(i < nv[0], f, nf - 1)

    grid_spec = pltpu.PrefetchScalarGridSpec(
        num_scalar_prefetch=2,
        grid=(n_blocks, nf),
        in_specs=[
            pl.BlockSpec((rows_blk, D_MODEL), lambda i, f, be, nv: (blk(i, nv), 0)),
            pl.BlockSpec((1, D_MODEL, 2 * tf), lambda i, f, be, nv: (be[blk(i, nv)], 0, ff(i, f, nv))),
            pl.BlockSpec((1, 1, 2 * tf), lambda i, f, be, nv: (be[blk(i, nv)], 0, ff(i, f, nv))),
            pl.BlockSpec((1, tf, D_MODEL), lambda i, f, be, nv: (be[blk(i, nv)], ff(i, f, nv), 0)),
            pl.BlockSpec((1, 1, D_MODEL), lambda i, f, be, nv: (be[blk(i, nv)], 0, 0)),
        ],
        out_specs=pl.BlockSpec((rows_blk, D_MODEL), lambda i, f, be, nv: (blk(i, nv), 0)),
    )
    return pl.pallas_call(
        _moe_kernel,
        grid_spec=grid_spec,
        out_shape=jax.ShapeDtypeStruct((n_rows, D_MODEL), F32),
        compiler_params=pltpu.CompilerParams(
            dimension_semantics=("arbitrary", "arbitrary"), vmem_limit_bytes=VMEM_LIMIT),
        name="experts",
    )(block_e, n_valid, xs, w1_t, b1_t, w2_bf, b2)


def _combine_kernel(h_ref, yg_ref, gate_ref, g_ref, o_ref):
    acc = h_ref[...]
    gates = gate_ref[...]
    for k in range(TOP_K):
        acc = acc + gates[:, k:k + 1] * yg_ref[:, k * D_MODEL:(k + 1) * D_MODEL]
    o_ref[...] = _rms(acc, g_ref[...])


def _combine(h1, yg, gates_pad, g_final, row_tile):
    rows = h1.shape[0]
    row = lambda i: (i, 0)
    const = lambda i: (0, 0)
    return pl.pallas_call(
        _combine_kernel,
        grid=(rows // row_tile,),
        in_specs=[
            pl.BlockSpec((row_tile, D_MODEL), row),
            pl.BlockSpec((row_tile, TOP_K * D_MODEL), row),
            pl.BlockSpec((row_tile, LANES), row),
            pl.BlockSpec((1, D_MODEL), const),
        ],
        out_specs=pl.BlockSpec((row_tile, D_MODEL), row),
        out_shape=jax.ShapeDtypeStruct((rows, D_MODEL), F32),
        compiler_params=pltpu.CompilerParams(
            dimension_semantics=("arbitrary",), vmem_limit_bytes=VMEM_LIMIT),
        name="combine",
    )(h1, yg, gates_pad, g_final)


def kernel(x, meta_tokens, g_mix, w_in, b_in, attn_sinks, ssm_a_re, ssm_a_im, ssm_log_dt,
           ssm_b_re, ssm_b_im, ssm_c_re, ssm_c_im, ssm_d, w_glu, b_glu, g_attn_out, g_ssm_out,
           w_out, b_out, g_ffn, w_router, b_router, w_mlp1, b_mlp1, w_mlp2, b_mlp2, g_final):
    bsz, seq, _ = x.shape
    rows = bsz * seq
    row_tile = min(ROW_TILE, seq)
    chunk = min(SSM_CHUNK, seq)
    assert seq % ATTN_BLOCK == 0 and seq % row_tile == 0 and seq % chunk == 0
    xf = x.reshape(rows, D_MODEL)

    w_in_bf = w_in[0].astype(BF16)
    cos_r, sin_r = _rope_tables(N_META + jnp.arange(seq))
    cos_m, sin_m = _rope_tables(jnp.arange(N_META))
    q, k, v, u = _inproj(xf, g_mix, w_in_bf, b_in, cos_r, sin_r, row_tile)
    _, k_meta, v_meta, u_meta = _inproj(meta_tokens, g_mix, w_in_bf, b_in, cos_m, sin_m, N_META)

    y_attn = _attention(q, k, v, k_meta, v_meta, attn_sinks, g_attn_out, bsz, seq)

    prm = _ssm_params(ssm_a_re[0], ssm_a_im[0], ssm_log_dt[0], ssm_b_re[0], ssm_b_im[0],
                      ssm_c_re[0], ssm_c_im[0], chunk // SUBLANES)
    y_ssm = _ssm(u, u_meta, prm, ssm_d, w_glu[0].astype(BF16), b_glu, g_ssm_out, bsz, seq, chunk)

    w_router_pad = jnp.pad(w_router[0], ((0, 0), (0, ROUTER_LANES - N_EXPERTS)))
    b_router_pad = jnp.pad(b_router, ((0, 0), (0, ROUTER_LANES - N_EXPERTS)))
    h1, xn, logits = _outproj(y_attn, y_ssm, xf, w_out[0].astype(BF16), b_out, g_ffn,
                              w_router_pad, b_router_pad, row_tile)

    top_val, top_idx = lax.top_k(logits[:, :N_EXPERTS], TOP_K)
    gates = jax.nn.softmax(top_val, axis=-1)
    chosen = (top_idx[:, :, None] == jnp.arange(N_EXPERTS)[None, None, :]).any(axis=1)
    chosen = chosen.astype(jnp.int32)
    rank = jnp.cumsum(chosen, axis=0) - chosen
    counts = jnp.sum(chosen, axis=0)
    padded = ((counts + MOE_ROWS - 1) // MOE_ROWS) * MOE_ROWS
    pad_ends = jnp.cumsum(padded)
    pad_starts = pad_ends - padded
    dest = pad_starts[top_idx] + jnp.take_along_axis(rank, top_idx, axis=1)
    n_assign = rows * TOP_K
    n_blocks = n_assign // MOE_ROWS + N_EXPERTS
    n_rows = n_blocks * MOE_ROWS
    tok = jnp.broadcast_to(jnp.arange(rows, dtype=jnp.int32)[:, None], (rows, TOP_K))
    row_tok = jnp.zeros((n_rows,), jnp.int32).at[dest.reshape(-1)].set(tok.reshape(-1))
    block_start = jnp.arange(n_blocks, dtype=jnp.int32) * MOE_ROWS
    block_e = jnp.minimum(jnp.searchsorted(pad_ends, block_start, side='right'),
                          N_EXPERTS - 1).astype(jnp.int32)
    n_valid = (pad_ends[-1] // MOE_ROWS).astype(jnp.int32).reshape(1)

    tf = MOE_FF_TILE
    nf = D_FF // tf
    w1_t = (w_mlp1[0].reshape(N_EXPERTS, D_MODEL, nf, tf, 2).transpose(0, 1, 2, 4, 3)
            .reshape(N_EXPERTS, D_MODEL, 2 * D_FF).astype(BF16))
    b1_t = (b_mlp1[0].reshape(N_EXPERTS, nf, tf, 2).transpose(0, 1, 3, 2)
            .reshape(N_EXPERTS, 1, 2 * D_FF))
    w2_bf = w_mlp2[0].astype(BF16)
    b2 = b_mlp2[0].reshape(N_EXPERTS, 1, D_MODEL)
    xs = xn[row_tok]
    ys = _experts(xs, block_e, n_valid, w1_t, b1_t, w2_bf, b2, MOE_ROWS, tf)

    yg = ys[dest.reshape(-1)].reshape(rows, TOP_K * D_MODEL)
    gates_pad = jnp.pad(gates, ((0, 0), (0, LANES - TOP_K)))
    out = _combine(h1, yg, gates_pad, g_final.reshape(1, D_MODEL), row_tile)
    return out.reshape(bsz, seq, D_MODEL)
```

```python
import functools
import math

import jax
import jax.numpy as jnp
from jax import lax
from jax.experimental import pallas as pl
from jax.experimental.pallas import tpu as pltpu

F32 = jnp.float32
BF16 = jnp.bfloat16

D_MODEL = 2048
N_META = 16
HEAD_DIM = 64
ATTN_WIDTH = 1024
N_Q_HEADS = 16
N_KV_HEADS = 4
Q_PER_KV = 4
KV_WIDTH = 256
ATTN_BLOCK = 128
ROPE_THETA = 10000.0
SSM_WIDTH = 1024
SSM_GROUP = 16
N_SSM_GROUPS = 64
SSM_STATE = 64
SSM_LANES = N_SSM_GROUPS * SSM_STATE
IN_WIDTH = 2560
N_EXPERTS = 32
TOP_K = 4
D_FF = 2048
SWIGLU_LIMIT = 7.0
SWIGLU_ALPHA = 1.702
NORM_EPS = 1e-5

LANES = 128
SUBLANES = 8
VMEM_LIMIT = 56 * 1024 * 1024

ROW_TILE = 512
SSM_CHUNK = 256
SSM_COL = 256
SSM_COL_TILES = SSM_WIDTH // SSM_COL
SSM_COL_LANES = SSM_LANES // SSM_COL_TILES
SCAN_LANES = 512
MOE_ROWS = 512
MOE_FF_TILE = 512
ROUTER_LANES = 128
PERM_COLS = 256


def _rms(t, gain):
    return t * lax.rsqrt(jnp.mean(t * t, axis=-1, keepdims=True) + NORM_EPS) * gain


def _sigmoid(t):
    return 1.0 / (1.0 + jnp.exp(-t))


def _inproj_kernel(x_ref, g_ref, w_ref, b_ref, cos_ref, sin_ref, q_ref, k_ref, v_ref, u_ref,
                   *, transposed):
    n = _rms(x_ref[...], g_ref[...]).astype(BF16)
    cos = cos_ref[...]
    sin = sin_ref[...]
    lane = lax.broadcasted_iota(jnp.int32, cos.shape, 1)
    first_half = (lane % HEAD_DIM) < (HEAD_DIM // 2)

    def proj(c0, c1):
        return jnp.dot(n, w_ref[:, c0:c1], preferred_element_type=F32) + b_ref[:, c0:c1]

    def rope(t):
        partner = jnp.where(first_half, pltpu.roll(t, LANES - HEAD_DIM // 2, 1),
                            pltpu.roll(t, HEAD_DIM // 2, 1))
        return t * cos + partner * sin

    def put(ref, j, t):
        if transposed:
            ref[j * LANES:(j + 1) * LANES, :] = t.T.astype(BF16)
        else:
            ref[:, j * LANES:(j + 1) * LANES] = t.astype(BF16)

    scale = HEAD_DIM ** -0.5
    for j in range(ATTN_WIDTH // LANES):
        put(q_ref, j, rope(proj(j * LANES, (j + 1) * LANES)) * scale)
    for j in range(KV_WIDTH // LANES):
        c0 = ATTN_WIDTH + j * LANES
        k_ref[:, j * LANES:(j + 1) * LANES] = rope(proj(c0, c0 + LANES)).astype(BF16)
    for j in range(KV_WIDTH // LANES):
        c0 = ATTN_WIDTH + KV_WIDTH + j * LANES
        put(v_ref, j, proj(c0, c0 + LANES))
    c0 = ATTN_WIDTH + 2 * KV_WIDTH
    u_ref[...] = proj(c0, c0 + SSM_WIDTH)


def _inproj(xf, g_mix, w_in_bf, b_in, cos_t, sin_t, row_tile, transposed):
    rows = xf.shape[0]
    tab_blocks = cos_t.shape[0] // row_tile
    row = lambda i: (i, 0)
    col = lambda i: (0, i)
    tab = lambda i: (i % tab_blocks, 0)
    const = lambda i: (0, 0)
    if transposed:
        q_spec, q_shape = pl.BlockSpec((ATTN_WIDTH, row_tile), col), (ATTN_WIDTH, rows)
        v_spec, v_shape = pl.BlockSpec((KV_WIDTH, row_tile), col), (KV_WIDTH, rows)
    else:
        q_spec, q_shape = pl.BlockSpec((row_tile, ATTN_WIDTH), row), (rows, ATTN_WIDTH)
        v_spec, v_shape = pl.BlockSpec((row_tile, KV_WIDTH), row), (rows, KV_WIDTH)
    return pl.pallas_call(
        functools.partial(_inproj_kernel, transposed=transposed),
        grid=(rows // row_tile,),
        in_specs=[
            pl.BlockSpec((row_tile, D_MODEL), row),
            pl.BlockSpec((1, D_MODEL), const),
            pl.BlockSpec((D_MODEL, IN_WIDTH), const),
            pl.BlockSpec((1, IN_WIDTH), const),
            pl.BlockSpec((row_tile, LANES), tab),
            pl.BlockSpec((row_tile, LANES), tab),
        ],
        out_specs=[
            q_spec,
            pl.BlockSpec((row_tile, KV_WIDTH), row),
            v_spec,
            pl.BlockSpec((row_tile, SSM_WIDTH), row),
        ],
        out_shape=[
            jax.ShapeDtypeStruct(q_shape, BF16),
            jax.ShapeDtypeStruct((rows, KV_WIDTH), BF16),
            jax.ShapeDtypeStruct(v_shape, BF16),
            jax.ShapeDtypeStruct((rows, SSM_WIDTH), F32),
        ],
        compiler_params=pltpu.CompilerParams(
            dimension_semantics=("arbitrary",), vmem_limit_bytes=VMEM_LIMIT),
        name="inproj",
    )(xf, g_mix, w_in_bf, b_in, cos_t, sin_t)


def _rope_tables(positions):
    half = HEAD_DIM // 2
    inv_freq = jnp.power(ROPE_THETA, -jnp.arange(half, dtype=F32) / half)
    ang = positions.astype(F32)[:, None] * inv_freq[None, :]
    cos = jnp.tile(jnp.cos(ang), (1, LANES // half))
    sin = jnp.tile(jnp.sin(ang), (1, LANES // half))
    sign = jnp.where((jnp.arange(LANES) % HEAD_DIM) < half, -1.0, 1.0).astype(F32)
    return cos, sin * sign[None, :]


def _attn_kernel(qt_ref, kp_ref, kc_ref, km_ref, vtp_ref, vtc_ref, vtm_ref, sink_ref, g_ref, o_ref):
    n = pl.program_id(1)
    n_keys = 2 * ATTN_BLOCK + N_META
    key = lax.broadcasted_iota(jnp.int32, (n_keys, ATTN_BLOCK), 0)
    qi = lax.broadcasted_iota(jnp.int32, (n_keys, ATTN_BLOCK), 1)
    no_prev = jnp.where(n > 0, 0, ATTN_BLOCK)
    cur_j = key - ATTN_BLOCK
    valid = ((key >= 2 * ATTN_BLOCK) | ((cur_j >= 0) & (cur_j <= qi))
             | ((key < ATTN_BLOCK) & (key > qi + no_prev)))
    bias = jnp.where(valid, 0.0, -1e30)
    zeros = jnp.zeros((HEAD_DIM, ATTN_BLOCK), BF16)
    heads = []
    for hk in range(N_KV_HEADS):
        lt = slice((hk // 2) * LANES, (hk // 2 + 1) * LANES)
        kt = jnp.concatenate([kp_ref[:, lt], kc_ref[:, lt], km_ref[:, lt]], axis=0)
        rs = slice(hk * HEAD_DIM, (hk + 1) * HEAD_DIM)
        vt = jnp.concatenate([vtp_ref[rs, :], vtc_ref[rs, :], vtm_ref[rs, :]], axis=1)
        for g in range(Q_PER_KV):
            h = hk * Q_PER_KV + g
            qh = qt_ref[h * HEAD_DIM:(h + 1) * HEAD_DIM, :]
            qpad = jnp.concatenate([qh, zeros] if hk % 2 == 0 else [zeros, qh], axis=0)
            s = jnp.dot(kt, qpad, preferred_element_type=F32) + bias
            sink = sink_ref[:, h:h + 1]
            m = jnp.maximum(jnp.max(s, axis=0, keepdims=True), sink)
            p = jnp.exp(s - m)
            denom = jnp.sum(p, axis=0, keepdims=True) + jnp.exp(sink - m)
            o = jnp.dot(vt, p.astype(BF16), preferred_element_type=F32)
            heads.append(o * (1.0 / denom))
    yt = jnp.concatenate(heads, axis=0)
    inv = lax.rsqrt(jnp.sum(yt * yt, axis=0, keepdims=True) * (1.0 / ATTN_WIDTH) + NORM_EPS)
    o_ref[...] = ((yt * inv).T * g_ref[...]).astype(BF16)


def _attention(qt, k, vt, k_meta, vt_meta, sinks, g_attn, bsz, seq):
    nb = seq // ATTN_BLOCK
    cur = lambda b, n: (b * nb + n, 0)
    prev = lambda b, n: (b * nb + jnp.maximum(n - 1, 0), 0)
    cur_t = lambda b, n: (0, b * nb + n)
    prev_t = lambda b, n: (0, b * nb + jnp.maximum(n - 1, 0))
    const = lambda b, n: (0, 0)
    return pl.pallas_call(
        _attn_kernel,
        grid=(bsz, nb),
        in_specs=[
            pl.BlockSpec((ATTN_WIDTH, ATTN_BLOCK), cur_t),
            pl.BlockSpec((ATTN_BLOCK, KV_WIDTH), prev),
            pl.BlockSpec((ATTN_BLOCK, KV_WIDTH), cur),
            pl.BlockSpec((N_META, KV_WIDTH), const),
            pl.BlockSpec((KV_WIDTH, ATTN_BLOCK), prev_t),
            pl.BlockSpec((KV_WIDTH, ATTN_BLOCK), cur_t),
            pl.BlockSpec((KV_WIDTH, N_META), const),
            pl.BlockSpec((1, N_Q_HEADS), const),
            pl.BlockSpec((1, ATTN_WIDTH), const),
        ],
        out_specs=pl.BlockSpec((ATTN_BLOCK, ATTN_WIDTH), cur),
        out_shape=jax.ShapeDtypeStruct((bsz * seq, ATTN_WIDTH), BF16),
        compiler_params=pltpu.CompilerParams(
            dimension_semantics=("arbitrary", "arbitrary"), vmem_limit_bytes=VMEM_LIMIT),
        name="attention",
    )(qt, k, k, k_meta, vt, vt, vt_meta, sinks, g_attn)


def _ssm_params(a_re, a_im, log_dt, b_re, b_im, c_re, c_im, seg_len):
    dt = jnp.exp(log_dt.astype(F32))[:, None]
    lam_re = jnp.minimum(a_re.astype(F32), -1e-4)
    lam_im = a_im.astype(F32)
    z_re, z_im = lam_re * dt, lam_im * dt
    mag = jnp.exp(z_re)
    abar_re, abar_im = mag * jnp.cos(z_im), mag * jnp.sin(z_im)
    den = lam_re * lam_re + lam_im * lam_im
    n_re, n_im = abar_re - 1.0, abar_im
    coef_re = (n_re * lam_re + n_im * lam_im) / den
    coef_im = (n_im * lam_re - n_re * lam_im) / den
    br, bi = b_re.astype(F32), b_im.astype(F32)
    bb_re = coef_re[..., None] * br - coef_im[..., None] * bi
    bb_im = coef_re[..., None] * bi + coef_im[..., None] * br

    groups_per_tile = SSM_COL // SSM_GROUP
    eye = jnp.eye(groups_per_tile, dtype=F32)

    def in_tile(bb):
        t = bb.reshape(SSM_COL_TILES, groups_per_tile, SSM_STATE, SSM_GROUP)
        t = jnp.einsum('tgpc,gh->tgchp', t, eye)
        return t.reshape(SSM_COL_TILES, SSM_COL, SSM_COL_LANES).astype(BF16)

    def out_tile(cc):
        t = cc.reshape(SSM_COL_TILES, groups_per_tile, SSM_GROUP, SSM_STATE)
        t = jnp.einsum('tgcp,gh->tgphc', t, eye)
        return t.reshape(SSM_COL_TILES, SSM_COL_LANES, SSM_COL).astype(BF16)

    def powers(exps):
        e = exps.astype(F32)[:, None, None]
        pm = jnp.exp(e * z_re[None])
        return jnp.stack([(pm * jnp.cos(e * z_im[None])).reshape(len(exps), SSM_LANES),
                          (pm * jnp.sin(e * z_im[None])).reshape(len(exps), SSM_LANES)])

    return dict(
        b_re=in_tile(bb_re), b_im=in_tile(bb_im),
        c_re=out_tile(c_re.astype(F32)), c_im=out_tile(-c_im.astype(F32)),
        a=jnp.stack([abar_re.reshape(1, SSM_LANES), abar_im.reshape(1, SSM_LANES)]),
        a_seg=powers(jnp.array([seg_len])),
        a_pow=powers(jnp.arange(1, seg_len + 1)),
        a_meta=powers(jnp.arange(N_META - 1, -1, -1)),
    )


def _gelu_tanh(t):
    return 0.5 * t * (1.0 + jnp.tanh(math.sqrt(2.0 / math.pi) * (t + 0.044715 * (t * t * t))))


def _ssm_kernel(u_ref, um_ref, bre_ref, bim_ref, cre_ref, cim_ref, a_ref, aseg_ref, apow_ref,
                ameta_ref, d_ref, wglu_ref, bglu_ref, g_ref, o_ref,
                xre, xim, hre, him, car_re, car_im, cin_re, cin_im, u_scr, y_scr):
    chunk = u_ref.shape[0]
    seg = chunk // SUBLANES

    @pl.when(pl.program_id(1) == 0)
    def _():
        um = um_ref[...].astype(BF16)
        for ct in range(SSM_COL_TILES):
            ub = um[:, ct * SSM_COL:(ct + 1) * SSM_COL]
            ls = slice(ct * SSM_COL_LANES, (ct + 1) * SSM_COL_LANES)
            xr = jnp.dot(ub, bre_ref[ct], preferred_element_type=F32)
            xi = jnp.dot(ub, bim_ref[ct], preferred_element_type=F32)
            pr = ameta_ref[0, :, ls]
            pi = ameta_ref[1, :, ls]
            car_re[:, ls] = jnp.sum(pr * xr - pi * xi, axis=0, keepdims=True)
            car_im[:, ls] = jnp.sum(pr * xi + pi * xr, axis=0, keepdims=True)

    n_lt = SSM_WIDTH // LANES
    for j in range(n_lt):
        u_scr[j] = u_ref[:, j * LANES:(j + 1) * LANES]
    up = jnp.concatenate(
        [jnp.concatenate([u_scr[j, pl.ds(k, SUBLANES, stride=seg), :] for j in range(n_lt)], axis=1)
         for k in range(seg)], axis=0).astype(BF16)
    for ct in range(SSM_COL_TILES):
        ub = up[:, ct * SSM_COL:(ct + 1) * SSM_COL]
        ls = slice(ct * SSM_COL_LANES, (ct + 1) * SSM_COL_LANES)
        xre[:, ls] = jnp.dot(ub, bre_ref[ct], preferred_element_type=F32)
        xim[:, ls] = jnp.dot(ub, bim_ref[ct], preferred_element_type=F32)

    for lb in range(SSM_LANES // SCAN_LANES):
        ls = slice(lb * SCAN_LANES, (lb + 1) * SCAN_LANES)
        ar = jnp.broadcast_to(a_ref[0, :, ls], (SUBLANES, SCAN_LANES))
        ai = jnp.broadcast_to(a_ref[1, :, ls], (SUBLANES, SCAN_LANES))

        def scan_body(k, carry):
            hr, hi = carry
            rows = pl.ds(pl.multiple_of(k * SUBLANES, SUBLANES), SUBLANES)
            nr = ar * hr - ai * hi + xre[rows, ls]
            ni = ar * hi + ai * hr + xim[rows, ls]
            xre[rows, ls] = nr
            xim[rows, ls] = ni
            return nr, ni

        zero = jnp.zeros((SUBLANES, SCAN_LANES), F32)
        lax.fori_loop(0, seg, scan_body, (zero, zero))

    cr = car_re[...]
    ci = car_im[...]
    sr = aseg_ref[0]
    si = aseg_ref[1]
    for r in range(SUBLANES):
        cin_re[r:r + 1, :] = cr
        cin_im[r:r + 1, :] = ci
        er = xre[chunk - SUBLANES + r:chunk - SUBLANES + r + 1, :]
        ei = xim[chunk - SUBLANES + r:chunk - SUBLANES + r + 1, :]
        cr, ci = sr * cr - si * ci + er, sr * ci + si * cr + ei
    car_re[...] = cr
    car_im[...] = ci

    pair = 2 * SUBLANES
    for lb in range(SSM_LANES // SCAN_LANES):
        ls = slice(lb * SCAN_LANES, (lb + 1) * SCAN_LANES)
        er = jnp.concatenate([cin_re[:, ls], cin_re[:, ls]], axis=0)
        ei = jnp.concatenate([cin_im[:, ls], cin_im[:, ls]], axis=0)

        def fix_body(k2, _):
            rows = pl.ds(pl.multiple_of(k2 * pair, pair), pair)
            k = 2 * k2
            pr = jnp.concatenate(
                [jnp.broadcast_to(apow_ref[0, pl.ds(k, 1), ls], (SUBLANES, SCAN_LANES)),
                 jnp.broadcast_to(apow_ref[0, pl.ds(k + 1, 1), ls], (SUBLANES, SCAN_LANES))], axis=0)
            pi = jnp.concatenate(
                [jnp.broadcast_to(apow_ref[1, pl.ds(k, 1), ls], (SUBLANES, SCAN_LANES)),
                 jnp.broadcast_to(apow_ref[1, pl.ds(k + 1, 1), ls], (SUBLANES, SCAN_LANES))], axis=0)
            hre[rows, ls] = (xre[rows, ls] + pr * er - pi * ei).astype(BF16)
            him[rows, ls] = (xim[rows, ls] + pr * ei + pi * er).astype(BF16)
            return 0

        lax.fori_loop(0, seg // 2, fix_body, 0)

    for ct in range(SSM_COL_TILES):
        ls = slice(ct * SSM_COL_LANES, (ct + 1) * SSM_COL_LANES)
        y = (jnp.dot(hre[:, ls], cre_ref[ct], preferred_element_type=F32)
             + jnp.dot(him[:, ls], cim_ref[ct], preferred_element_type=F32))
        for k in range(seg):
            for jj in range(SSM_COL // LANES):
                y_scr[ct * (SSM_COL // LANES) + jj, pl.ds(k, SUBLANES, stride=seg), :] = (
                    y[k * SUBLANES:(k + 1) * SUBLANES, jj * LANES:(jj + 1) * LANES])

    y = jnp.concatenate([y_scr[j] for j in range(n_lt)], axis=1)
    y = _gelu_tanh(y + d_ref[...] * u_ref[...])
    gate = jnp.dot(y.astype(BF16), wglu_ref[...], preferred_element_type=F32) + bglu_ref[...]
    y = y * _sigmoid(gate)
    o_ref[...] = _rms(y, g_ref[...]).astype(BF16)


def _ssm(u, u_meta, prm, d_skip, w_glu_bf, b_glu, g_ssm, bsz, seq, chunk):
    nc = seq // chunk
    seg = chunk // SUBLANES
    row = lambda b, c: (b * nc + c, 0)
    c2 = lambda b, c: (0, 0)
    c3 = lambda b, c: (0, 0, 0)
    return pl.pallas_call(
        _ssm_kernel,
        grid=(bsz, nc),
        in_specs=[
            pl.BlockSpec((chunk, SSM_WIDTH), row),
            pl.BlockSpec((N_META, SSM_WIDTH), c2),
            pl.BlockSpec((SSM_COL_TILES, SSM_COL, SSM_COL_LANES), c3),
            pl.BlockSpec((SSM_COL_TILES, SSM_COL, SSM_COL_LANES), c3),
            pl.BlockSpec((SSM_COL_TILES, SSM_COL_LANES, SSM_COL), c3),
            pl.BlockSpec((SSM_COL_TILES, SSM_COL_LANES, SSM_COL), c3),
            pl.BlockSpec((2, 1, SSM_LANES), c3),
            pl.BlockSpec((2, 1, SSM_LANES), c3),
            pl.BlockSpec((2, seg, SSM_LANES), c3),
            pl.BlockSpec((2, N_META, SSM_LANES), c3),
            pl.BlockSpec((1, SSM_WIDTH), c2),
            pl.BlockSpec((SSM_WIDTH, SSM_WIDTH), c2),
            pl.BlockSpec((1, SSM_WIDTH), c2),
            pl.BlockSpec((1, SSM_WIDTH), c2),
        ],
        out_specs=pl.BlockSpec((chunk, SSM_WIDTH), row),
        out_shape=jax.ShapeDtypeStruct((bsz * seq, SSM_WIDTH), BF16),
        scratch_shapes=[
            pltpu.VMEM((chunk, SSM_LANES), F32),
            pltpu.VMEM((chunk, SSM_LANES), F32),
            pltpu.VMEM((chunk, SSM_LANES), BF16),
            pltpu.VMEM((chunk, SSM_LANES), BF16),
            pltpu.VMEM((1, SSM_LANES), F32),
            pltpu.VMEM((1, SSM_LANES), F32),
            pltpu.VMEM((SUBLANES, SSM_LANES), F32),
            pltpu.VMEM((SUBLANES, SSM_LANES), F32),
            pltpu.VMEM((SSM_WIDTH // LANES, chunk, LANES), F32),
            pltpu.VMEM((SSM_WIDTH // LANES, chunk, LANES), F32),
        ],
        compiler_params=pltpu.CompilerParams(
            dimension_semantics=("arbitrary", "arbitrary"), vmem_limit_bytes=VMEM_LIMIT),
        name="ssm",
    )(u, u_meta, prm["b_re"], prm["b_im"], prm["c_re"], prm["c_im"], prm["a"], prm["a_seg"],
      prm["a_pow"], prm["a_meta"], d_skip, w_glu_bf, b_glu, g_ssm)


def _outproj_kernel(ya_ref, ys_ref, x_ref, wo_ref, bo_ref, gf_ref, wr_ref, br_ref,
                    h_ref, xn_ref, lg_ref):
    mix = (jnp.dot(ya_ref[...], wo_ref[:ATTN_WIDTH, :], preferred_element_type=F32)
           + jnp.dot(ys_ref[...], wo_ref[ATTN_WIDTH:, :], preferred_element_type=F32))
    h = x_ref[...] + mix + bo_ref[...]
    h_ref[...] = h
    n = _rms(h, gf_ref[...])
    xn_ref[...] = n.astype(BF16)
    lg_ref[...] = jnp.dot(n, wr_ref[...], preferred_element_type=F32,
                          precision=lax.Precision.HIGHEST) + br_ref[...]


def _outproj(y_attn, y_ssm, xf, w_out_bf, b_out, g_ffn, w_router_pad, b_router_pad, row_tile):
    rows = xf.shape[0]
    row = lambda i: (i, 0)
    const = lambda i: (0, 0)
    return pl.pallas_call(
        _outproj_kernel,
        grid=(rows // row_tile,),
        in_specs=[
            pl.BlockSpec((row_tile, ATTN_WIDTH), row),
            pl.BlockSpec((row_tile, SSM_WIDTH), row),
            pl.BlockSpec((row_tile, D_MODEL), row),
            pl.BlockSpec((D_MODEL, D_MODEL), const),
            pl.BlockSpec((1, D_MODEL), const),
            pl.BlockSpec((1, D_MODEL), const),
            pl.BlockSpec((D_MODEL, ROUTER_LANES), const),
            pl.BlockSpec((1, ROUTER_LANES), const),
        ],
        out_specs=[
            pl.BlockSpec((row_tile, D_MODEL), row),
            pl.BlockSpec((row_tile, D_MODEL), row),
            pl.BlockSpec((row_tile, ROUTER_LANES), row),
        ],
        out_shape=[
            jax.ShapeDtypeStruct((rows, D_MODEL), F32),
            jax.ShapeDtypeStruct((rows, D_MODEL), BF16),
            jax.ShapeDtypeStruct((rows, ROUTER_LANES), F32),
        ],
        compiler_params=pltpu.CompilerParams(
            dimension_semantics=("arbitrary",), vmem_limit_bytes=VMEM_LIMIT),
        name="outproj",
    )(y_attn, y_ssm, xf, w_out_bf, b_out, g_ffn, w_router_pad, b_router_pad)


def _w1_prep_kernel(w_ref, p_ref, o_ref):
    tf = o_ref.shape[2] // 2
    half = PERM_COLS // 2
    for c in range(2 * tf // PERM_COLS):
        t = jnp.dot(w_ref[0, :, c * PERM_COLS:(c + 1) * PERM_COLS].astype(BF16), p_ref[...],
                    preferred_element_type=F32)
        o_ref[0, :, c * half:(c + 1) * half] = t[:, :half].astype(BF16)
        o_ref[0, :, tf + c * half:tf + (c + 1) * half] = t[:, half:].astype(BF16)


def _w1_prep(w1, tf):
    idx = jnp.arange(PERM_COLS)
    dst = jnp.where(idx % 2 == 0, idx // 2, PERM_COLS // 2 + idx // 2)
    perm = (dst[:, None] == jnp.arange(PERM_COLS)[None, :]).astype(BF16)
    slab = lambda e, f: (e, 0, f)
    return pl.pallas_call(
        _w1_prep_kernel,
        grid=(N_EXPERTS, D_FF // tf),
        in_specs=[
            pl.BlockSpec((1, D_MODEL, 2 * tf), slab),
            pl.BlockSpec((PERM_COLS, PERM_COLS), lambda e, f: (0, 0)),
        ],
        out_specs=pl.BlockSpec((1, D_MODEL, 2 * tf), slab),
        out_shape=jax.ShapeDtypeStruct((N_EXPERTS, D_MODEL, 2 * D_FF), BF16),
        compiler_params=pltpu.CompilerParams(
            dimension_semantics=("arbitrary", "arbitrary"), vmem_limit_bytes=VMEM_LIMIT),
        name="w1_prep",
    )(w1, perm)


def _moe_kernel(be_ref, nv_ref, x_ref, w1_ref, b1_ref, w2_ref, b2_ref, y_ref):
    i = pl.program_id(0)
    f = pl.program_id(1)
    tf = w2_ref.shape[1]

    @pl.when(i < nv_ref[0])
    def _():
        h = jnp.dot(x_ref[...], w1_ref[0], preferred_element_type=F32) + b1_ref[0]
        g = jnp.minimum(h[:, :tf], SWIGLU_LIMIT)
        up = jnp.clip(h[:, tf:], -SWIGLU_LIMIT, SWIGLU_LIMIT)
        act = g * _sigmoid(SWIGLU_ALPHA * g) * (up + 1.0)
        contrib = jnp.dot(act.astype(BF16), w2_ref[0].astype(BF16), preferred_element_type=F32)

        @pl.when(f == 0)
        def _():
            y_ref[...] = contrib + b2_ref[0]

        @pl.when(f > 0)
        def _():
            y_ref[...] += contrib


def _experts(xs, block_e, n_valid, w1_t, b1_t, w2_bf, b2, rows_blk, tf):
    n_rows = xs.shape[0]
    n_blocks = n_rows // rows_blk
    nf = D_FF // tf

    def blk(i, nv):
        return jnp.minimum(i, nv[0] - 1)

    def ff(i, f, nv):
        return jnp.where(i < nv[0], f, nf - 1)

    grid_spec = pltpu.PrefetchScalarGridSpec(
        num_scalar_prefetch=2,
        grid=(n_blocks, nf),
        in_specs=[
            pl.BlockSpec((rows_blk, D_MODEL), lambda i, f, be, nv: (blk(i, nv), 0)),
            pl.BlockSpec((1, D_MODEL, 2 * tf), lambda i, f, be, nv: (be[blk(i, nv)], 0, ff(i, f, nv))),
            pl.BlockSpec((1, 1, 2 * tf), lambda i, f, be, nv: (be[blk(i, nv)], 0, ff(i, f, nv))),
            pl.BlockSpec((1, tf, D_MODEL), lambda i, f, be, nv: (be[blk(i, nv)], ff(i, f, nv), 0)),
            pl.BlockSpec((1, 1, D_MODEL), lambda i, f, be, nv: (be[blk(i, nv)], 0, 0)),
        ],
        out_specs=pl.BlockSpec((rows_blk, D_MODEL), lambda i, f, be, nv: (blk(i, nv), 0)),
    )
    return pl.pallas_call(
        _moe_kernel,
        grid_spec=grid_spec,
        out_shape=jax.ShapeDtypeStruct((n_rows, D_MODEL), F32),
        compiler_params=pltpu.CompilerParams(
            dimension_semantics=("arbitrary", "arbitrary"), vmem_limit_bytes=VMEM_LIMIT),
        name="experts",
    )(block_e, n_valid, xs, w1_t, b1_t, w2_bf, b2)


def _combine_kernel(h_ref, yg_ref, gate_ref, g_ref, o_ref):
    acc = h_ref[...]
    gates = gate_ref[...]
    for k in range(TOP_K):
        acc = acc + gates[:, k:k + 1] * yg_ref[:, k * D_MODEL:(k + 1) * D_MODEL]
    o_ref[...] = _rms(acc, g_ref[...])


def _combine(h1, yg, gates_pad, g_final, row_tile):
    rows = h1.shape[0]
    row = lambda i: (i, 0)
    const = lambda i: (0, 0)
    return pl.pallas_call(
        _combine_kernel,
        grid=(rows // row_tile,),
        in_specs=[
            pl.BlockSpec((row_tile, D_MODEL), row),
            pl.BlockSpec((row_tile, TOP_K * D_MODEL), row),
            pl.BlockSpec((row_tile, LANES), row),
            pl.BlockSpec((1, D_MODEL), const),
        ],
        out_specs=pl.BlockSpec((row_tile, D_MODEL), row),
        out_shape=jax.ShapeDtypeStruct((rows, D_MODEL), F32),
        compiler_params=pltpu.CompilerParams(
            dimension_semantics=("arbitrary",), vmem_limit_bytes=VMEM_LIMIT),
        name="combine",
    )(h1, yg, gates_pad, g_final)


def kernel(x, meta_tokens, g_mix, w_in, b_in, attn_sinks, ssm_a_re, ssm_a_im, ssm_log_dt,
           ssm_b_re, ssm_b_im, ssm_c_re, ssm_c_im, ssm_d, w_glu, b_glu, g_attn_out, g_ssm_out,
           w_out, b_out, g_ffn, w_router, b_router, w_mlp1, b_mlp1, w_mlp2, b_mlp2, g_final):
    bsz, seq, _ = x.shape
    rows = bsz * seq
    row_tile = min(ROW_TILE, seq)
    chunk = min(SSM_CHUNK, seq)
    assert seq % ATTN_BLOCK == 0 and seq % row_tile == 0 and seq % chunk == 0
    xf = x.reshape(rows, D_MODEL)

    w_in_bf = w_in[0].astype(BF16)
    cos_r, sin_r = _rope_tables(N_META + jnp.arange(seq))
    cos_m, sin_m = _rope_tables(jnp.arange(N_META))
    qt, k, vt, u = _inproj(xf, g_mix, w_in_bf, b_in, cos_r, sin_r, row_tile, True)
    _, k_meta, v_meta, u_meta = _inproj(meta_tokens, g_mix, w_in_bf, b_in, cos_m, sin_m,
                                        N_META, False)

    y_attn = _attention(qt, k, vt, k_meta, v_meta.T, attn_sinks, g_attn_out, bsz, seq)

    prm = _ssm_params(ssm_a_re[0], ssm_a_im[0], ssm_log_dt[0], ssm_b_re[0], ssm_b_im[0],
                      ssm_c_re[0], ssm_c_im[0], chunk // SUBLANES)
    y_ssm = _ssm(u, u_meta, prm, ssm_d, w_glu[0].astype(BF16), b_glu, g_ssm_out, bsz, seq, chunk)

    w_router_pad = jnp.pad(w_router[0], ((0, 0), (0, ROUTER_LANES - N_EXPERTS)))
    b_router_pad = jnp.pad(b_router, ((0, 0), (0, ROUTER_LANES - N_EXPERTS)))
    h1, xn, logits = _outproj(y_attn, y_ssm, xf, w_out[0].astype(BF16), b_out, g_ffn,
                              w_router_pad, b_router_pad, row_tile)

    top_val, top_idx = lax.top_k(logits[:, :N_EXPERTS], TOP_K)
    gates = jax.nn.softmax(top_val, axis=-1)
    chosen = (top_idx[:, :, None] == jnp.arange(N_EXPERTS)[None, None, :]).any(axis=1)
    chosen = chosen.astype(jnp.int32)
    rank = jnp.cumsum(chosen, axis=0) - chosen
    counts = jnp.sum(chosen, axis=0)
    padded = ((counts + MOE_ROWS - 1) // MOE_ROWS) * MOE_ROWS
    pad_ends = jnp.cumsum(padded)
    pad_starts = pad_ends - padded
    dest = pad_starts[top_idx] + jnp.take_along_axis(rank, top_idx, axis=1)
    n_assign = rows * TOP_K
    n_blocks = n_assign // MOE_ROWS + N_EXPERTS
    n_rows = n_blocks * MOE_ROWS
    tok = jnp.broadcast_to(jnp.arange(rows, dtype=jnp.int32)[:, None], (rows, TOP_K))
    row_tok = jnp.zeros((n_rows,), jnp.int32).at[dest.reshape(-1)].set(tok.reshape(-1))
    block_start = jnp.arange(n_blocks, dtype=jnp.int32) * MOE_ROWS
    block_e = jnp.minimum(jnp.searchsorted(pad_ends, block_start, side='right'),
                          N_EXPERTS - 1).astype(jnp.int32)
    n_valid = (pad_ends[-1] // MOE_ROWS).astype(jnp.int32).reshape(1)

    tf = MOE_FF_TILE
    nf = D_FF // tf
    w1_t = _w1_prep(w_mlp1[0], tf)
    b1_t = (b_mlp1[0].reshape(N_EXPERTS, nf, tf, 2).transpose(0, 1, 3, 2)
            .reshape(N_EXPERTS, 1, 2 * D_FF))
    b2 = b_mlp2[0].reshape(N_EXPERTS, 1, D_MODEL)
    xs = xn[row_tok]
    ys = _experts(xs, block_e, n_valid, w1_t, b1_t, w_mlp2[0], b2, MOE_ROWS, tf)

    yg = ys[dest.reshape(-1)].reshape(rows, TOP_K * D_MODEL)
    gates_pad = jnp.pad(gates, ((0, 0), (0, LANES - TOP_K)))
    out = _combine(h1, yg, gates_pad, g_final.reshape(1, D_MODEL), row_tile)
    return out.reshape(bsz, seq, D_MODEL)
```

```python
import functools
import math

import jax
import jax.numpy as jnp
from jax import lax
from jax.experimental import pallas as pl
from jax.experimental.pallas import tpu as pltpu

F32 = jnp.float32
BF16 = jnp.bfloat16

D_MODEL = 2048
N_META = 16
HEAD_DIM = 64
ATTN_WIDTH = 1024
N_Q_HEADS = 16
N_KV_HEADS = 4
Q_PER_KV = 4
KV_WIDTH = 256
ATTN_BLOCK = 128
ROPE_THETA = 10000.0
SSM_WIDTH = 1024
SSM_GROUP = 16
N_SSM_GROUPS = 64
SSM_STATE = 64
SSM_LANES = N_SSM_GROUPS * SSM_STATE
IN_WIDTH = 2560
N_EXPERTS = 32
TOP_K = 4
D_FF = 2048
SWIGLU_LIMIT = 7.0
SWIGLU_ALPHA = 1.702
NORM_EPS = 1e-5

LANES = 128
SUBLANES = 8
VMEM_LIMIT = 56 * 1024 * 1024

ROW_TILE = 512
SSM_CHUNK = 256
SSM_COL = 256
SSM_COL_TILES = SSM_WIDTH // SSM_COL
SSM_COL_LANES = SSM_LANES // SSM_COL_TILES
SCAN_LANES = 512
MOE_PAD = 256
MOE_BLOCK = 512
MOE_ITEM_ROWS = 2560
MOE_FF_TILE = 256
ROUTER_LANES = 128
PERM_COLS = 256


def _rms(t, gain):
    return t * lax.rsqrt(jnp.mean(t * t, axis=-1, keepdims=True) + NORM_EPS) * gain


def _sigmoid(t):
    return 1.0 / (1.0 + jnp.exp(-t))


def _inproj_kernel(x_ref, g_ref, w_ref, b_ref, cos_ref, sin_ref, q_ref, k_ref, v_ref, u_ref,
                   *, transposed):
    n = _rms(x_ref[...], g_ref[...]).astype(BF16)
    cos = cos_ref[...]
    sin = sin_ref[...]
    lane = lax.broadcasted_iota(jnp.int32, cos.shape, 1)
    first_half = (lane % HEAD_DIM) < (HEAD_DIM // 2)

    def proj(c0, c1):
        return jnp.dot(n, w_ref[:, c0:c1], preferred_element_type=F32) + b_ref[:, c0:c1]

    def rope(t):
        partner = jnp.where(first_half, pltpu.roll(t, LANES - HEAD_DIM // 2, 1),
                            pltpu.roll(t, HEAD_DIM // 2, 1))
        return t * cos + partner * sin

    def put(ref, j, t):
        if transposed:
            ref[j * LANES:(j + 1) * LANES, :] = t.T.astype(BF16)
        else:
            ref[:, j * LANES:(j + 1) * LANES] = t.astype(BF16)

    scale = HEAD_DIM ** -0.5
    for j in range(ATTN_WIDTH // LANES):
        put(q_ref, j, rope(proj(j * LANES, (j + 1) * LANES)) * scale)
    for j in range(KV_WIDTH // LANES):
        c0 = ATTN_WIDTH + j * LANES
        k_ref[:, j * LANES:(j + 1) * LANES] = rope(proj(c0, c0 + LANES)).astype(BF16)
    for j in range(KV_WIDTH // LANES):
        c0 = ATTN_WIDTH + KV_WIDTH + j * LANES
        put(v_ref, j, proj(c0, c0 + LANES))
    c0 = ATTN_WIDTH + 2 * KV_WIDTH
    u_ref[...] = proj(c0, c0 + SSM_WIDTH)


def _inproj(xf, g_mix, w_in_bf, b_in, cos_t, sin_t, row_tile, transposed):
    rows = xf.shape[0]
    tab_blocks = cos_t.shape[0] // row_tile
    row = lambda i: (i, 0)
    col = lambda i: (0, i)
    tab = lambda i: (i % tab_blocks, 0)
    const = lambda i: (0, 0)
    if transposed:
        q_spec, q_shape = pl.BlockSpec((ATTN_WIDTH, row_tile), col), (ATTN_WIDTH, rows)
        v_spec, v_shape = pl.BlockSpec((KV_WIDTH, row_tile), col), (KV_WIDTH, rows)
    else:
        q_spec, q_shape = pl.BlockSpec((row_tile, ATTN_WIDTH), row), (rows, ATTN_WIDTH)
        v_spec, v_shape = pl.BlockSpec((row_tile, KV_WIDTH), row), (rows, KV_WIDTH)
    return pl.pallas_call(
        functools.partial(_inproj_kernel, transposed=transposed),
        grid=(rows // row_tile,),
        in_specs=[
            pl.BlockSpec((row_tile, D_MODEL), row),
            pl.BlockSpec((1, D_MODEL), const),
            pl.BlockSpec((D_MODEL, IN_WIDTH), const),
            pl.BlockSpec((1, IN_WIDTH), const),
            pl.BlockSpec((row_tile, LANES), tab),
            pl.BlockSpec((row_tile, LANES), tab),
        ],
        out_specs=[
            q_spec,
            pl.BlockSpec((row_tile, KV_WIDTH), row),
            v_spec,
            pl.BlockSpec((row_tile, SSM_WIDTH), row),
        ],
        out_shape=[
            jax.ShapeDtypeStruct(q_shape, BF16),
            jax.ShapeDtypeStruct((rows, KV_WIDTH), BF16),
            jax.ShapeDtypeStruct(v_shape, BF16),
            jax.ShapeDtypeStruct((rows, SSM_WIDTH), F32),
        ],
        compiler_params=pltpu.CompilerParams(
            dimension_semantics=("arbitrary",), vmem_limit_bytes=VMEM_LIMIT),
        name="inproj",
    )(xf, g_mix, w_in_bf, b_in, cos_t, sin_t)


def _rope_tables(positions):
    half = HEAD_DIM // 2
    inv_freq = jnp.power(ROPE_THETA, -jnp.arange(half, dtype=F32) / half)
    ang = positions.astype(F32)[:, None] * inv_freq[None, :]
    cos = jnp.tile(jnp.cos(ang), (1, LANES // half))
    sin = jnp.tile(jnp.sin(ang), (1, LANES // half))
    sign = jnp.where((jnp.arange(LANES) % HEAD_DIM) < half, -1.0, 1.0).astype(F32)
    return cos, sin * sign[None, :]


def _attn_kernel(qt_ref, kp_ref, kc_ref, km_ref, vtp_ref, vtc_ref, vtm_ref, sink_ref, g_ref, o_ref):
    n = pl.program_id(1)
    n_keys = 2 * ATTN_BLOCK + N_META
    key = lax.broadcasted_iota(jnp.int32, (n_keys, ATTN_BLOCK), 0)
    qi = lax.broadcasted_iota(jnp.int32, (n_keys, ATTN_BLOCK), 1)
    no_prev = jnp.where(n > 0, 0, ATTN_BLOCK)
    cur_j = key - ATTN_BLOCK
    valid = ((key >= 2 * ATTN_BLOCK) | ((cur_j >= 0) & (cur_j <= qi))
             | ((key < ATTN_BLOCK) & (key > qi + no_prev)))
    bias = jnp.where(valid, 0.0, -1e30)
    zeros = jnp.zeros((HEAD_DIM, ATTN_BLOCK), BF16)
    heads = []
    for hk in range(N_KV_HEADS):
        lt = slice((hk // 2) * LANES, (hk // 2 + 1) * LANES)
        kt = jnp.concatenate([kp_ref[:, lt], kc_ref[:, lt], km_ref[:, lt]], axis=0)
        rs = slice(hk * HEAD_DIM, (hk + 1) * HEAD_DIM)
        vt = jnp.concatenate([vtp_ref[rs, :], vtc_ref[rs, :], vtm_ref[rs, :]], axis=1)
        for g in range(Q_PER_KV):
            h = hk * Q_PER_KV + g
            qh = qt_ref[h * HEAD_DIM:(h + 1) * HEAD_DIM, :]
            qpad = jnp.concatenate([qh, zeros] if hk % 2 == 0 else [zeros, qh], axis=0)
            s = jnp.dot(kt, qpad, preferred_element_type=F32) + bias
            sink = sink_ref[:, h:h + 1]
            m = jnp.maximum(jnp.max(s, axis=0, keepdims=True), sink)
            p = jnp.exp(s - m)
            denom = jnp.sum(p, axis=0, keepdims=True) + jnp.exp(sink - m)
            o = jnp.dot(vt, p.astype(BF16), preferred_element_type=F32)
            heads.append(o * (1.0 / denom))
    yt = jnp.concatenate(heads, axis=0)
    inv = lax.rsqrt(jnp.sum(yt * yt, axis=0, keepdims=True) * (1.0 / ATTN_WIDTH) + NORM_EPS)
    o_ref[...] = ((yt * inv).T * g_ref[...]).astype(BF16)


def _attention(qt, k, vt, k_meta, vt_meta, sinks, g_attn, bsz, seq):
    nb = seq // ATTN_BLOCK
    cur = lambda b, n: (b * nb + n, 0)
    prev = lambda b, n: (b * nb + jnp.maximum(n - 1, 0), 0)
    cur_t = lambda b, n: (0, b * nb + n)
    prev_t = lambda b, n: (0, b * nb + jnp.maximum(n - 1, 0))
    const = lambda b, n: (0, 0)
    return pl.pallas_call(
        _attn_kernel,
        grid=(bsz, nb),
        in_specs=[
            pl.BlockSpec((ATTN_WIDTH, ATTN_BLOCK), cur_t),
            pl.BlockSpec((ATTN_BLOCK, KV_WIDTH), prev),
            pl.BlockSpec((ATTN_BLOCK, KV_WIDTH), cur),
            pl.BlockSpec((N_META, KV_WIDTH), const),
            pl.BlockSpec((KV_WIDTH, ATTN_BLOCK), prev_t),
            pl.BlockSpec((KV_WIDTH, ATTN_BLOCK), cur_t),
            pl.BlockSpec((KV_WIDTH, N_META), const),
            pl.BlockSpec((1, N_Q_HEADS), const),
            pl.BlockSpec((1, ATTN_WIDTH), const),
        ],
        out_specs=pl.BlockSpec((ATTN_BLOCK, ATTN_WIDTH), cur),
        out_shape=jax.ShapeDtypeStruct((bsz * seq, ATTN_WIDTH), BF16),
        compiler_params=pltpu.CompilerParams(
            dimension_semantics=("arbitrary", "arbitrary"), vmem_limit_bytes=VMEM_LIMIT),
        name="attention",
    )(qt, k, k, k_meta, vt, vt, vt_meta, sinks, g_attn)


def _ssm_params(a_re, a_im, log_dt, b_re, b_im, c_re, c_im, seg_len):
    dt = jnp.exp(log_dt.astype(F32))[:, None]
    lam_re = jnp.minimum(a_re.astype(F32), -1e-4)
    lam_im = a_im.astype(F32)
    z_re, z_im = lam_re * dt, lam_im * dt
    mag = jnp.exp(z_re)
    abar_re, abar_im = mag * jnp.cos(z_im), mag * jnp.sin(z_im)
    den = lam_re * lam_re + lam_im * lam_im
    n_re, n_im = abar_re - 1.0, abar_im
    coef_re = (n_re * lam_re + n_im * lam_im) / den
    coef_im = (n_im * lam_re - n_re * lam_im) / den
    br, bi = b_re.astype(F32), b_im.astype(F32)
    bb_re = coef_re[..., None] * br - coef_im[..., None] * bi
    bb_im = coef_re[..., None] * bi + coef_im[..., None] * br

    groups_per_tile = SSM_COL // SSM_GROUP
    eye = jnp.eye(groups_per_tile, dtype=F32)

    def in_tile(bb):
        t = bb.reshape(SSM_COL_TILES, groups_per_tile, SSM_STATE, SSM_GROUP)
        t = jnp.einsum('tgpc,gh->tgchp', t, eye)
        return t.reshape(SSM_COL_TILES, SSM_COL, SSM_COL_LANES).astype(BF16)

    def out_tile(cc):
        t = cc.reshape(SSM_COL_TILES, groups_per_tile, SSM_GROUP, SSM_STATE)
        t = jnp.einsum('tgcp,gh->tgphc', t, eye)
        return t.reshape(SSM_COL_TILES, SSM_COL_LANES, SSM_COL).astype(BF16)

    def powers(exps):
        e = exps.astype(F32)[:, None, None]
        pm = jnp.exp(e * z_re[None])
        return jnp.stack([(pm * jnp.cos(e * z_im[None])).reshape(len(exps), SSM_LANES),
                          (pm * jnp.sin(e * z_im[None])).reshape(len(exps), SSM_LANES)])

    return dict(
        b_re=in_tile(bb_re), b_im=in_tile(bb_im),
        c_re=out_tile(c_re.astype(F32)), c_im=out_tile(-c_im.astype(F32)),
        a=jnp.stack([abar_re.reshape(1, SSM_LANES), abar_im.reshape(1, SSM_LANES)]),
        a_seg=powers(jnp.array([seg_len])),
        a_pow=powers(jnp.arange(1, seg_len + 1)),
        a_meta=powers(jnp.arange(N_META - 1, -1, -1)),
    )


def _gelu_tanh(t):
    return 0.5 * t * (1.0 + jnp.tanh(math.sqrt(2.0 / math.pi) * (t + 0.044715 * (t * t * t))))


def _ssm_kernel(u_ref, um_ref, bre_ref, bim_ref, cre_ref, cim_ref, a_ref, aseg_ref, apow_ref,
                ameta_ref, d_ref, wglu_ref, bglu_ref, g_ref, o_ref,
                xre, xim, hre, him, car_re, car_im, cin_re, cin_im, u_scr, y_scr):
    chunk = u_ref.shape[0]
    seg = chunk // SUBLANES

    @pl.when(pl.program_id(1) == 0)
    def _():
        um = um_ref[...].astype(BF16)
        for ct in range(SSM_COL_TILES):
            ub = um[:, ct * SSM_COL:(ct + 1) * SSM_COL]
            ls = slice(ct * SSM_COL_LANES, (ct + 1) * SSM_COL_LANES)
            xr = jnp.dot(ub, bre_ref[ct], preferred_element_type=F32)
            xi = jnp.dot(ub, bim_ref[ct], preferred_element_type=F32)
            pr = ameta_ref[0, :, ls]
            pi = ameta_ref[1, :, ls]
            car_re[:, ls] = jnp.sum(pr * xr - pi * xi, axis=0, keepdims=True)
            car_im[:, ls] = jnp.sum(pr * xi + pi * xr, axis=0, keepdims=True)

    n_lt = SSM_WIDTH // LANES
    for j in range(n_lt):
        u_scr[j] = u_ref[:, j * LANES:(j + 1) * LANES]
    up = jnp.concatenate(
        [jnp.concatenate([u_scr[j, pl.ds(k, SUBLANES, stride=seg), :] for j in range(n_lt)], axis=1)
         for k in range(seg)], axis=0).astype(BF16)
    for ct in range(SSM_COL_TILES):
        ub = up[:, ct * SSM_COL:(ct + 1) * SSM_COL]
        ls = slice(ct * SSM_COL_LANES, (ct + 1) * SSM_COL_LANES)
        xre[:, ls] = jnp.dot(ub, bre_ref[ct], preferred_element_type=F32)
        xim[:, ls] = jnp.dot(ub, bim_ref[ct], preferred_element_type=F32)

    for lb in range(SSM_LANES // SCAN_LANES):
        ls = slice(lb * SCAN_LANES, (lb + 1) * SCAN_LANES)
        ar = jnp.broadcast_to(a_ref[0, :, ls], (SUBLANES, SCAN_LANES))
        ai = jnp.broadcast_to(a_ref[1, :, ls], (SUBLANES, SCAN_LANES))

        def scan_body(k, carry):
            hr, hi = carry
            rows = pl.ds(pl.multiple_of(k * SUBLANES, SUBLANES), SUBLANES)
            nr = ar * hr - ai * hi + xre[rows, ls]
            ni = ar * hi + ai * hr + xim[rows, ls]
            xre[rows, ls] = nr
            xim[rows, ls] = ni
            return nr, ni

        zero = jnp.zeros((SUBLANES, SCAN_LANES), F32)
        lax.fori_loop(0, seg, scan_body, (zero, zero))

    cr = car_re[...]
    ci = car_im[...]
    sr = aseg_ref[0]
    si = aseg_ref[1]
    for r in range(SUBLANES):
        cin_re[r:r + 1, :] = cr
        cin_im[r:r + 1, :] = ci
        er = xre[chunk - SUBLANES + r:chunk - SUBLANES + r + 1, :]
        ei = xim[chunk - SUBLANES + r:chunk - SUBLANES + r + 1, :]
        cr, ci = sr * cr - si * ci + er, sr * ci + si * cr + ei
    car_re[...] = cr
    car_im[...] = ci

    pair = 2 * SUBLANES
    for lb in range(SSM_LANES // SCAN_LANES):
        ls = slice(lb * SCAN_LANES, (lb + 1) * SCAN_LANES)
        er = jnp.concatenate([cin_re[:, ls], cin_re[:, ls]], axis=0)
        ei = jnp.concatenate([cin_im[:, ls], cin_im[:, ls]], axis=0)

        def fix_body(k2, _):
            rows = pl.ds(pl.multiple_of(k2 * pair, pair), pair)
            k = 2 * k2
            pr = jnp.concatenate(
                [jnp.broadcast_to(apow_ref[0, pl.ds(k, 1), ls], (SUBLANES, SCAN_LANES)),
                 jnp.broadcast_to(apow_ref[0, pl.ds(k + 1, 1), ls], (SUBLANES, SCAN_LANES))], axis=0)
            pi = jnp.concatenate(
                [jnp.broadcast_to(apow_ref[1, pl.ds(k, 1), ls], (SUBLANES, SCAN_LANES)),
                 jnp.broadcast_to(apow_ref[1, pl.ds(k + 1, 1), ls], (SUBLANES, SCAN_LANES))], axis=0)
            hre[rows, ls] = (xre[rows, ls] + pr * er - pi * ei).astype(BF16)
            him[rows, ls] = (xim[rows, ls] + pr * ei + pi * er).astype(BF16)
            return 0

        lax.fori_loop(0, seg // 2, fix_body, 0)

    for ct in range(SSM_COL_TILES):
        ls = slice(ct * SSM_COL_LANES, (ct + 1) * SSM_COL_LANES)
        y = (jnp.dot(hre[:, ls], cre_ref[ct], preferred_element_type=F32)
             + jnp.dot(him[:, ls], cim_ref[ct], preferred_element_type=F32))
        for k in range(seg):
            for jj in range(SSM_COL // LANES):
                y_scr[ct * (SSM_COL // LANES) + jj, pl.ds(k, SUBLANES, stride=seg), :] = (
                    y[k * SUBLANES:(k + 1) * SUBLANES, jj * LANES:(jj + 1) * LANES])

    y = jnp.concatenate([y_scr[j] for j in range(n_lt)], axis=1)
    y = _gelu_tanh(y + d_ref[...] * u_ref[...])
    gate = jnp.dot(y.astype(BF16), wglu_ref[...], preferred_element_type=F32) + bglu_ref[...]
    y = y * _sigmoid(gate)
    o_ref[...] = _rms(y, g_ref[...]).astype(BF16)


def _ssm(u, u_meta, prm, d_skip, w_glu_bf, b_glu, g_ssm, bsz, seq, chunk):
    nc = seq // chunk
    seg = chunk // SUBLANES
    row = lambda b, c: (b * nc + c, 0)
    c2 = lambda b, c: (0, 0)
    c3 = lambda b, c: (0, 0, 0)
    return pl.pallas_call(
        _ssm_kernel,
        grid=(bsz, nc),
        in_specs=[
            pl.BlockSpec((chunk, SSM_WIDTH), row),
            pl.BlockSpec((N_META, SSM_WIDTH), c2),
            pl.BlockSpec((SSM_COL_TILES, SSM_COL, SSM_COL_LANES), c3),
            pl.BlockSpec((SSM_COL_TILES, SSM_COL, SSM_COL_LANES), c3),
            pl.BlockSpec((SSM_COL_TILES, SSM_COL_LANES, SSM_COL), c3),
            pl.BlockSpec((SSM_COL_TILES, SSM_COL_LANES, SSM_COL), c3),
            pl.BlockSpec((2, 1, SSM_LANES), c3),
            pl.BlockSpec((2, 1, SSM_LANES), c3),
            pl.BlockSpec((2, seg, SSM_LANES), c3),
            pl.BlockSpec((2, N_META, SSM_LANES), c3),
            pl.BlockSpec((1, SSM_WIDTH), c2),
            pl.BlockSpec((SSM_WIDTH, SSM_WIDTH), c2),
            pl.BlockSpec((1, SSM_WIDTH), c2),
            pl.BlockSpec((1, SSM_WIDTH), c2),
        ],
        out_specs=pl.BlockSpec((chunk, SSM_WIDTH), row),
        out_shape=jax.ShapeDtypeStruct((bsz * seq, SSM_WIDTH), BF16),
        scratch_shapes=[
            pltpu.VMEM((chunk, SSM_LANES), F32),
            pltpu.VMEM((chunk, SSM_LANES), F32),
            pltpu.VMEM((chunk, SSM_LANES), BF16),
            pltpu.VMEM((chunk, SSM_LANES), BF16),
            pltpu.VMEM((1, SSM_LANES), F32),
            pltpu.VMEM((1, SSM_LANES), F32),
            pltpu.VMEM((SUBLANES, SSM_LANES), F32),
            pltpu.VMEM((SUBLANES, SSM_LANES), F32),
            pltpu.VMEM((SSM_WIDTH // LANES, chunk, LANES), F32),
            pltpu.VMEM((SSM_WIDTH // LANES, chunk, LANES), F32),
        ],
        compiler_params=pltpu.CompilerParams(
            dimension_semantics=("arbitrary", "arbitrary"), vmem_limit_bytes=VMEM_LIMIT),
        name="ssm",
    )(u, u_meta, prm["b_re"], prm["b_im"], prm["c_re"], prm["c_im"], prm["a"], prm["a_seg"],
      prm["a_pow"], prm["a_meta"], d_skip, w_glu_bf, b_glu, g_ssm)


def _outproj_kernel(ya_ref, ys_ref, x_ref, wo_ref, bo_ref, gf_ref, wr_ref, br_ref,
                    h_ref, xn_ref, lg_ref):
    mix = (jnp.dot(ya_ref[...], wo_ref[:ATTN_WIDTH, :], preferred_element_type=F32)
           + jnp.dot(ys_ref[...], wo_ref[ATTN_WIDTH:, :], preferred_element_type=F32))
    h = x_ref[...] + mix + bo_ref[...]
    h_ref[...] = h
    n = _rms(h, gf_ref[...])
    xn_ref[...] = n.astype(BF16)
    lg_ref[...] = jnp.dot(n, wr_ref[...], preferred_element_type=F32,
                          precision=lax.Precision.HIGHEST) + br_ref[...]


def _outproj(y_attn, y_ssm, xf, w_out_bf, b_out, g_ffn, w_router_pad, b_router_pad, row_tile):
    rows = xf.shape[0]
    row = lambda i: (i, 0)
    const = lambda i: (0, 0)
    return pl.pallas_call(
        _outproj_kernel,
        grid=(rows // row_tile,),
        in_specs=[
            pl.BlockSpec((row_tile, ATTN_WIDTH), row),
            pl.BlockSpec((row_tile, SSM_WIDTH), row),
            pl.BlockSpec((row_tile, D_MODEL), row),
            pl.BlockSpec((D_MODEL, D_MODEL), const),
            pl.BlockSpec((1, D_MODEL), const),
            pl.BlockSpec((1, D_MODEL), const),
            pl.BlockSpec((D_MODEL, ROUTER_LANES), const),
            pl.BlockSpec((1, ROUTER_LANES), const),
        ],
        out_specs=[
            pl.BlockSpec((row_tile, D_MODEL), row),
            pl.BlockSpec((row_tile, D_MODEL), row),
            pl.BlockSpec((row_tile, ROUTER_LANES), row),
        ],
        out_shape=[
            jax.ShapeDtypeStruct((rows, D_MODEL), F32),
            jax.ShapeDtypeStruct((rows, D_MODEL), BF16),
            jax.ShapeDtypeStruct((rows, ROUTER_LANES), F32),
        ],
        compiler_params=pltpu.CompilerParams(
            dimension_semantics=("arbitrary",), vmem_limit_bytes=VMEM_LIMIT),
        name="outproj",
    )(y_attn, y_ssm, xf, w_out_bf, b_out, g_ffn, w_router_pad, b_router_pad)


def _moe_kernel(ie_ref, ir_ref, in_ref, ni_ref,
                xs_hbm, w1_ref, b1g_ref, b1u_ref, w2_ref, b2_ref, perm_ref, ys_hbm,
                x_scr, acc, w1g, w1u, w2b, sem_in, sem_out):
    i = pl.program_id(0)
    f = pl.program_id(1)
    nf = pl.num_programs(1)
    tf = w2_ref.shape[1]
    n_chunks = x_scr.shape[0] // MOE_PAD

    @pl.when(i < ni_ref[0])
    def _():
        r0 = pl.multiple_of(ir_ref[i], MOE_PAD)
        n = in_ref[i]

        def in_copy(c):
            return pltpu.make_async_copy(xs_hbm.at[pl.ds(r0 + c * MOE_PAD, MOE_PAD), :],
                                         x_scr.at[pl.ds(c * MOE_PAD, MOE_PAD), :], sem_in)

        def out_copy(c):
            return pltpu.make_async_copy(acc.at[pl.ds(c * MOE_PAD, MOE_PAD), :],
                                         ys_hbm.at[pl.ds(r0 + c * MOE_PAD, MOE_PAD), :], sem_out)

        def for_live_chunks(fn):
            for c in range(n_chunks):
                @pl.when(c * MOE_PAD < n)
                def _(c=c):
                    fn(c)

        @pl.when(f == 0)
        def _():
            for_live_chunks(lambda c: in_copy(c).start())
            bias = jnp.broadcast_to(b2_ref[0], (MOE_PAD, D_MODEL))

            def init(c, _):
                acc[pl.ds(pl.multiple_of(c * MOE_PAD, MOE_PAD), MOE_PAD), :] = bias
                return 0

            lax.fori_loop(0, n // MOE_PAD, init, 0)
            for_live_chunks(lambda c: in_copy(c).wait())

        half = PERM_COLS // 2
        for c in range(2 * tf // PERM_COLS):
            t = jnp.dot(w1_ref[0, :, c * PERM_COLS:(c + 1) * PERM_COLS].astype(BF16), perm_ref[...],
                        preferred_element_type=F32)
            w1g[:, c * half:(c + 1) * half] = t[:, :half].astype(BF16)
            w1u[:, c * half:(c + 1) * half] = t[:, half:].astype(BF16)
        w2b[...] = w2_ref[0].astype(BF16)

        def block(r, rows):
            sl = pl.ds(r, rows)
            xb = x_scr[sl, :]
            g = jnp.dot(xb, w1g[...], preferred_element_type=F32) + b1g_ref[0]
            up = jnp.dot(xb, w1u[...], preferred_element_type=F32) + b1u_ref[0]
            g = jnp.minimum(g, SWIGLU_LIMIT)
            up = jnp.clip(up, -SWIGLU_LIMIT, SWIGLU_LIMIT)
            act = g * _sigmoid(SWIGLU_ALPHA * g) * (up + 1.0)
            acc[sl, :] += jnp.dot(act.astype(BF16), w2b[...], preferred_element_type=F32)

        def big(b, _):
            block(pl.multiple_of(b * MOE_BLOCK, MOE_BLOCK), MOE_BLOCK)
            return 0

        lax.fori_loop(0, n // MOE_BLOCK, big, 0)

        @pl.when(n % MOE_BLOCK != 0)
        def _():
            block(pl.multiple_of((n // MOE_BLOCK) * MOE_BLOCK, MOE_PAD), MOE_PAD)

        @pl.when(f == nf - 1)
        def _():
            for_live_chunks(lambda c: out_copy(c).start())
            for_live_chunks(lambda c: out_copy(c).wait())


def _experts(xs, item_e, item_r0, item_n, n_items, w1, b1g, b1u, w2, b2, tf):
    n_rows = xs.shape[0]
    max_items = item_e.shape[0]
    nf = D_FF // tf
    idx = jnp.arange(PERM_COLS)
    dst = jnp.where(idx % 2 == 0, idx // 2, PERM_COLS // 2 + idx // 2)
    perm = (dst[:, None] == jnp.arange(PERM_COLS)[None, :]).astype(BF16)

    def ex(i, ie, ni):
        return ie[jnp.minimum(i, ni[0] - 1)]

    def ff(i, f, ni):
        return jnp.where(i < ni[0], f, nf - 1)

    grid_spec = pltpu.PrefetchScalarGridSpec(
        num_scalar_prefetch=4,
        grid=(max_items, nf),
        in_specs=[
            pl.BlockSpec(memory_space=pl.ANY),
            pl.BlockSpec((1, D_MODEL, 2 * tf), lambda i, f, ie, ir, im, ni: (ex(i, ie, ni), 0, ff(i, f, ni))),
            pl.BlockSpec((1, 1, tf), lambda i, f, ie, ir, im, ni: (ex(i, ie, ni), 0, ff(i, f, ni))),
            pl.BlockSpec((1, 1, tf), lambda i, f, ie, ir, im, ni: (ex(i, ie, ni), 0, ff(i, f, ni))),
            pl.BlockSpec((1, tf, D_MODEL), lambda i, f, ie, ir, im, ni: (ex(i, ie, ni), ff(i, f, ni), 0)),
            pl.BlockSpec((1, 1, D_MODEL), lambda i, f, ie, ir, im, ni: (ex(i, ie, ni), 0, 0)),
            pl.BlockSpec((PERM_COLS, PERM_COLS), lambda i, f, ie, ir, im, ni: (0, 0)),
        ],
        out_specs=pl.BlockSpec(memory_space=pl.ANY),
        scratch_shapes=[
            pltpu.VMEM((MOE_ITEM_ROWS, D_MODEL), BF16),
            pltpu.VMEM((MOE_ITEM_ROWS, D_MODEL), F32),
            pltpu.VMEM((D_MODEL, tf), BF16),
            pltpu.VMEM((D_MODEL, tf), BF16),
            pltpu.VMEM((tf, D_MODEL), BF16),
            pltpu.SemaphoreType.DMA(()),
            pltpu.SemaphoreType.DMA(()),
        ],
    )
    return pl.pallas_call(
        _moe_kernel,
        grid_spec=grid_spec,
        out_shape=jax.ShapeDtypeStruct((n_rows, D_MODEL), F32),
        compiler_params=pltpu.CompilerParams(
            dimension_semantics=("arbitrary", "arbitrary"), vmem_limit_bytes=VMEM_LIMIT),
        name="experts",
    )(item_e, item_r0, item_n, n_items, xs, w1, b1g, b1u, w2, b2, perm)


def _combine_kernel(h_ref, y0_ref, y1_ref, y2_ref, y3_ref, gate_ref, g_ref, o_ref):
    acc = h_ref[...]
    gates = gate_ref[...]
    for k, y_ref in enumerate((y0_ref, y1_ref, y2_ref, y3_ref)):
        acc = acc + gates[:, k:k + 1] * y_ref[...]
    o_ref[...] = _rms(acc, g_ref[...])


def _combine(h1, ygs, gates_pad, g_final, row_tile):
    rows = h1.shape[0]
    row = lambda i: (i, 0)
    const = lambda i: (0, 0)
    return pl.pallas_call(
        _combine_kernel,
        grid=(rows // row_tile,),
        in_specs=[pl.BlockSpec((row_tile, D_MODEL), row)] * (1 + TOP_K) + [
            pl.BlockSpec((row_tile, LANES), row),
            pl.BlockSpec((1, D_MODEL), const),
        ],
        out_specs=pl.BlockSpec((row_tile, D_MODEL), row),
        out_shape=jax.ShapeDtypeStruct((rows, D_MODEL), F32),
        compiler_params=pltpu.CompilerParams(
            dimension_semantics=("arbitrary",), vmem_limit_bytes=VMEM_LIMIT),
        name="combine",
    )(h1, *ygs, gates_pad, g_final)


def kernel(x, meta_tokens, g_mix, w_in, b_in, attn_sinks, ssm_a_re, ssm_a_im, ssm_log_dt,
           ssm_b_re, ssm_b_im, ssm_c_re, ssm_c_im, ssm_d, w_glu, b_glu, g_attn_out, g_ssm_out,
           w_out, b_out, g_ffn, w_router, b_router, w_mlp1, b_mlp1, w_mlp2, b_mlp2, g_final):
    bsz, seq, _ = x.shape
    rows = bsz * seq
    row_tile = min(ROW_TILE, seq)
    chunk = min(SSM_CHUNK, seq)
    assert seq % ATTN_BLOCK == 0 and seq % row_tile == 0 and seq % chunk == 0
    xf = x.reshape(rows, D_MODEL)

    w_in_bf = w_in[0].astype(BF16)
    cos_r, sin_r = _rope_tables(N_META + jnp.arange(seq))
    cos_m, sin_m = _rope_tables(jnp.arange(N_META))
    qt, k, vt, u = _inproj(xf, g_mix, w_in_bf, b_in, cos_r, sin_r, row_tile, True)
    _, k_meta, v_meta, u_meta = _inproj(meta_tokens, g_mix, w_in_bf, b_in, cos_m, sin_m,
                                        N_META, False)

    y_attn = _attention(qt, k, vt, k_meta, v_meta.T, attn_sinks, g_attn_out, bsz, seq)

    prm = _ssm_params(ssm_a_re[0], ssm_a_im[0], ssm_log_dt[0], ssm_b_re[0], ssm_b_im[0],
                      ssm_c_re[0], ssm_c_im[0], chunk // SUBLANES)
    y_ssm = _ssm(u, u_meta, prm, ssm_d, w_glu[0].astype(BF16), b_glu, g_ssm_out, bsz, seq, chunk)

    w_router_pad = jnp.pad(w_router[0], ((0, 0), (0, ROUTER_LANES - N_EXPERTS)))
    b_router_pad = jnp.pad(b_router, ((0, 0), (0, ROUTER_LANES - N_EXPERTS)))
    h1, xn, logits = _outproj(y_attn, y_ssm, xf, w_out[0].astype(BF16), b_out, g_ffn,
                              w_router_pad, b_router_pad, row_tile)

    top_val, top_idx = lax.top_k(logits[:, :N_EXPERTS], TOP_K)
    gates = jax.nn.softmax(top_val, axis=-1)
    chosen = (top_idx[:, :, None] == jnp.arange(N_EXPERTS)[None, None, :]).any(axis=1)
    chosen = chosen.astype(jnp.int32)
    rank = jnp.cumsum(chosen, axis=0) - chosen
    counts = jnp.sum(chosen, axis=0)
    padded = ((counts + MOE_PAD - 1) // MOE_PAD) * MOE_PAD
    pad_ends = jnp.cumsum(padded)
    pad_starts = pad_ends - padded
    dest = pad_starts[top_idx] + jnp.take_along_axis(rank, top_idx, axis=1)
    n_assign = rows * TOP_K
    n_rows = n_assign + N_EXPERTS * MOE_PAD
    tok = jnp.broadcast_to(jnp.arange(rows, dtype=jnp.int32)[:, None], (rows, TOP_K))
    row_tok = jnp.zeros((n_rows,), jnp.int32).at[dest.reshape(-1)].set(tok.reshape(-1))
    per_e = (padded + MOE_ITEM_ROWS - 1) // MOE_ITEM_ROWS
    item_ends = jnp.cumsum(per_e)
    max_items = N_EXPERTS + n_assign // MOE_ITEM_ROWS
    slot = jnp.arange(max_items, dtype=jnp.int32)
    item_e = jnp.minimum(jnp.searchsorted(item_ends, slot, side='right'), N_EXPERTS - 1)
    piece = slot - (item_ends - per_e)[item_e]
    item_r0 = (pad_starts[item_e] + piece * MOE_ITEM_ROWS).astype(jnp.int32)
    item_n = jnp.clip(padded[item_e] - piece * MOE_ITEM_ROWS, 0, MOE_ITEM_ROWS).astype(jnp.int32)
    n_items = item_ends[-1].astype(jnp.int32).reshape(1)

    tf = MOE_FF_TILE
    b1 = b_mlp1[0].reshape(N_EXPERTS, 1, D_FF, 2)
    b2 = b_mlp2[0].reshape(N_EXPERTS, 1, D_MODEL)
    xs = xn[row_tok]
    ys = _experts(xs, item_e.astype(jnp.int32), item_r0, item_n, n_items,
                  w_mlp1[0], b1[..., 0], b1[..., 1], w_mlp2[0], b2, tf)

    ygs = [ys[dest[:, k]] for k in range(TOP_K)]
    gates_pad = jnp.pad(gates, ((0, 0), (0, LANES - TOP_K)))
    out = _combine(h1, ygs, gates_pad, g_final.reshape(1, D_MODEL), row_tile)
    return out.reshape(bsz, seq, D_MODEL)
```

```python
import functools
import math

import jax
import jax.numpy as jnp
from jax import lax
from jax.experimental import pallas as pl
from jax.experimental.pallas import tpu as pltpu

F32 = jnp.float32
BF16 = jnp.bfloat16

D_MODEL = 2048
N_META = 16
HEAD_DIM = 64
ATTN_WIDTH = 1024
N_Q_HEADS = 16
N_KV_HEADS = 4
Q_PER_KV = 4
KV_WIDTH = 256
ATTN_BLOCK = 128
ROPE_THETA = 10000.0
SSM_WIDTH = 1024
SSM_GROUP = 16
N_SSM_GROUPS = 64
SSM_STATE = 64
SSM_LANES = N_SSM_GROUPS * SSM_STATE
IN_WIDTH = 2560
N_EXPERTS = 32
TOP_K = 4
D_FF = 2048
SWIGLU_LIMIT = 7.0
SWIGLU_ALPHA = 1.702
NORM_EPS = 1e-5

LANES = 128
SUBLANES = 8
VMEM_LIMIT = 56 * 1024 * 1024

ROW_TILE = 512
SSM_CHUNK = 256
SSM_COL = 256
SSM_COL_TILES = SSM_WIDTH // SSM_COL
SSM_COL_LANES = SSM_LANES // SSM_COL_TILES
SCAN_LANES = 512
MOE_PAD = 256
MOE_BLOCK = 512
MOE_ITEM_ROWS = 2560
MOE_FF_TILE = 256
ROUTER_LANES = 128
PERM_COLS = 256


def _rms(t, gain):
    return t * lax.rsqrt(jnp.mean(t * t, axis=-1, keepdims=True) + NORM_EPS) * gain


def _sigmoid(t):
    return 1.0 / (1.0 + jnp.exp(-t))


def _inproj_kernel(x_ref, g_ref, w_ref, b_ref, cos_ref, sin_ref, q_ref, k_ref, v_ref, u_ref,
                   *, transposed):
    n = _rms(x_ref[...], g_ref[...]).astype(BF16)
    cos = cos_ref[...]
    sin = sin_ref[...]
    lane = lax.broadcasted_iota(jnp.int32, cos.shape, 1)
    first_half = (lane % HEAD_DIM) < (HEAD_DIM // 2)

    def proj(c0, c1):
        return jnp.dot(n, w_ref[:, c0:c1], preferred_element_type=F32) + b_ref[:, c0:c1]

    def rope(t):
        partner = jnp.where(first_half, pltpu.roll(t, LANES - HEAD_DIM // 2, 1),
                            pltpu.roll(t, HEAD_DIM // 2, 1))
        return t * cos + partner * sin

    def put(ref, j, t):
        if transposed:
            ref[j * LANES:(j + 1) * LANES, :] = t.T.astype(BF16)
        else:
            ref[:, j * LANES:(j + 1) * LANES] = t.astype(BF16)

    scale = HEAD_DIM ** -0.5
    for j in range(ATTN_WIDTH // LANES):
        put(q_ref, j, rope(proj(j * LANES, (j + 1) * LANES)) * scale)
    for j in range(KV_WIDTH // LANES):
        c0 = ATTN_WIDTH + j * LANES
        k_ref[:, j * LANES:(j + 1) * LANES] = rope(proj(c0, c0 + LANES)).astype(BF16)
    for j in range(KV_WIDTH // LANES):
        c0 = ATTN_WIDTH + KV_WIDTH + j * LANES
        put(v_ref, j, proj(c0, c0 + LANES))
    c0 = ATTN_WIDTH + 2 * KV_WIDTH
    u_ref[...] = proj(c0, c0 + SSM_WIDTH)


def _inproj(xf, g_mix, w_in_bf, b_in, cos_t, sin_t, row_tile, transposed):
    rows = xf.shape[0]
    tab_blocks = cos_t.shape[0] // row_tile
    row = lambda i: (i, 0)
    col = lambda i: (0, i)
    tab = lambda i: (i % tab_blocks, 0)
    const = lambda i: (0, 0)
    if transposed:
        q_spec, q_shape = pl.BlockSpec((ATTN_WIDTH, row_tile), col), (ATTN_WIDTH, rows)
        v_spec, v_shape = pl.BlockSpec((KV_WIDTH, row_tile), col), (KV_WIDTH, rows)
    else:
        q_spec, q_shape = pl.BlockSpec((row_tile, ATTN_WIDTH), row), (rows, ATTN_WIDTH)
        v_spec, v_shape = pl.BlockSpec((row_tile, KV_WIDTH), row), (rows, KV_WIDTH)
    return pl.pallas_call(
        functools.partial(_inproj_kernel, transposed=transposed),
        grid=(rows // row_tile,),
        in_specs=[
            pl.BlockSpec((row_tile, D_MODEL), row),
            pl.BlockSpec((1, D_MODEL), const),
            pl.BlockSpec((D_MODEL, IN_WIDTH), const),
            pl.BlockSpec((1, IN_WIDTH), const),
            pl.BlockSpec((row_tile, LANES), tab),
            pl.BlockSpec((row_tile, LANES), tab),
        ],
        out_specs=[
            q_spec,
            pl.BlockSpec((row_tile, KV_WIDTH), row),
            v_spec,
            pl.BlockSpec((row_tile, SSM_WIDTH), row),
        ],
        out_shape=[
            jax.ShapeDtypeStruct(q_shape, BF16),
            jax.ShapeDtypeStruct((rows, KV_WIDTH), BF16),
            jax.ShapeDtypeStruct(v_shape, BF16),
            jax.ShapeDtypeStruct((rows, SSM_WIDTH), F32),
        ],
        compiler_params=pltpu.CompilerParams(
            dimension_semantics=("arbitrary",), vmem_limit_bytes=VMEM_LIMIT),
        name="inproj",
    )(xf, g_mix, w_in_bf, b_in, cos_t, sin_t)


def _rope_tables(positions):
    half = HEAD_DIM // 2
    inv_freq = jnp.power(ROPE_THETA, -jnp.arange(half, dtype=F32) / half)
    ang = positions.astype(F32)[:, None] * inv_freq[None, :]
    cos = jnp.tile(jnp.cos(ang), (1, LANES // half))
    sin = jnp.tile(jnp.sin(ang), (1, LANES // half))
    sign = jnp.where((jnp.arange(LANES) % HEAD_DIM) < half, -1.0, 1.0).astype(F32)
    return cos, sin * sign[None, :]


def _attn_kernel(qt_ref, kp_ref, kc_ref, km_ref, vtp_ref, vtc_ref, vtm_ref, sink_ref, g_ref, o_ref):
    n = pl.program_id(1)
    n_keys = 2 * ATTN_BLOCK + N_META
    key = lax.broadcasted_iota(jnp.int32, (n_keys, ATTN_BLOCK), 0)
    qi = lax.broadcasted_iota(jnp.int32, (n_keys, ATTN_BLOCK), 1)
    no_prev = jnp.where(n > 0, 0, ATTN_BLOCK)
    cur_j = key - ATTN_BLOCK
    valid = ((key >= 2 * ATTN_BLOCK) | ((cur_j >= 0) & (cur_j <= qi))
             | ((key < ATTN_BLOCK) & (key > qi + no_prev)))
    bias = jnp.where(valid, 0.0, -1e30)
    zeros = jnp.zeros((HEAD_DIM, ATTN_BLOCK), BF16)
    heads = []
    for hk in range(N_KV_HEADS):
        lt = slice((hk // 2) * LANES, (hk // 2 + 1) * LANES)
        kt = jnp.concatenate([kp_ref[:, lt], kc_ref[:, lt], km_ref[:, lt]], axis=0)
        rs = slice(hk * HEAD_DIM, (hk + 1) * HEAD_DIM)
        vt = jnp.concatenate([vtp_ref[rs, :], vtc_ref[rs, :], vtm_ref[rs, :]], axis=1)
        for g in range(Q_PER_KV):
            h = hk * Q_PER_KV + g
            qh = qt_ref[h * HEAD_DIM:(h + 1) * HEAD_DIM, :]
            qpad = jnp.concatenate([qh, zeros] if hk % 2 == 0 else [zeros, qh], axis=0)
            s = jnp.dot(kt, qpad, preferred_element_type=F32) + bias
            sink = sink_ref[:, h:h + 1]
            m = jnp.maximum(jnp.max(s, axis=0, keepdims=True), sink)
            p = jnp.exp(s - m)
            denom = jnp.sum(p, axis=0, keepdims=True) + jnp.exp(sink - m)
            o = jnp.dot(vt, p.astype(BF16), preferred_element_type=F32)
            heads.append(o * (1.0 / denom))
    yt = jnp.concatenate(heads, axis=0)
    inv = lax.rsqrt(jnp.sum(yt * yt, axis=0, keepdims=True) * (1.0 / ATTN_WIDTH) + NORM_EPS)
    o_ref[...] = ((yt * inv).T * g_ref[...]).astype(BF16)


def _attention(qt, k, vt, k_meta, vt_meta, sinks, g_attn, bsz, seq):
    nb = seq // ATTN_BLOCK
    cur = lambda b, n: (b * nb + n, 0)
    prev = lambda b, n: (b * nb + jnp.maximum(n - 1, 0), 0)
    cur_t = lambda b, n: (0, b * nb + n)
    prev_t = lambda b, n: (0, b * nb + jnp.maximum(n - 1, 0))
    const = lambda b, n: (0, 0)
    return pl.pallas_call(
        _attn_kernel,
        grid=(bsz, nb),
        in_specs=[
            pl.BlockSpec((ATTN_WIDTH, ATTN_BLOCK), cur_t),
            pl.BlockSpec((ATTN_BLOCK, KV_WIDTH), prev),
            pl.BlockSpec((ATTN_BLOCK, KV_WIDTH), cur),
            pl.BlockSpec((N_META, KV_WIDTH), const),
            pl.BlockSpec((KV_WIDTH, ATTN_BLOCK), prev_t),
            pl.BlockSpec((KV_WIDTH, ATTN_BLOCK), cur_t),
            pl.BlockSpec((KV_WIDTH, N_META), const),
            pl.BlockSpec((1, N_Q_HEADS), const),
            pl.BlockSpec((1, ATTN_WIDTH), const),
        ],
        out_specs=pl.BlockSpec((ATTN_BLOCK, ATTN_WIDTH), cur),
        out_shape=jax.ShapeDtypeStruct((bsz * seq, ATTN_WIDTH), BF16),
        compiler_params=pltpu.CompilerParams(
            dimension_semantics=("arbitrary", "arbitrary"), vmem_limit_bytes=VMEM_LIMIT),
        name="attention",
    )(qt, k, k, k_meta, vt, vt, vt_meta, sinks, g_attn)


def _ssm_params(a_re, a_im, log_dt, b_re, b_im, c_re, c_im, seg_len):
    dt = jnp.exp(log_dt.astype(F32))[:, None]
    lam_re = jnp.minimum(a_re.astype(F32), -1e-4)
    lam_im = a_im.astype(F32)
    z_re, z_im = lam_re * dt, lam_im * dt
    mag = jnp.exp(z_re)
    abar_re, abar_im = mag * jnp.cos(z_im), mag * jnp.sin(z_im)
    den = lam_re * lam_re + lam_im * lam_im
    n_re, n_im = abar_re - 1.0, abar_im
    coef_re = (n_re * lam_re + n_im * lam_im) / den
    coef_im = (n_im * lam_re - n_re * lam_im) / den
    br, bi = b_re.astype(F32), b_im.astype(F32)
    bb_re = coef_re[..., None] * br - coef_im[..., None] * bi
    bb_im = coef_re[..., None] * bi + coef_im[..., None] * br

    groups_per_tile = SSM_COL // SSM_GROUP
    eye = jnp.eye(groups_per_tile, dtype=F32)

    def in_tile(bb):
        t = bb.reshape(SSM_COL_TILES, groups_per_tile, SSM_STATE, SSM_GROUP)
        t = jnp.einsum('tgpc,gh->tgchp', t, eye)
        return t.reshape(SSM_COL_TILES, SSM_COL, SSM_COL_LANES).astype(BF16)

    def out_tile(cc):
        t = cc.reshape(SSM_COL_TILES, groups_per_tile, SSM_GROUP, SSM_STATE)
        t = jnp.einsum('tgcp,gh->tgphc', t, eye)
        return t.reshape(SSM_COL_TILES, SSM_COL_LANES, SSM_COL).astype(BF16)

    def powers(exps):
        e = exps.astype(F32)[:, None, None]
        pm = jnp.exp(e * z_re[None])
        return jnp.stack([(pm * jnp.cos(e * z_im[None])).reshape(len(exps), SSM_LANES),
                          (pm * jnp.sin(e * z_im[None])).reshape(len(exps), SSM_LANES)])

    return dict(
        b_re=in_tile(bb_re), b_im=in_tile(bb_im),
        c_re=out_tile(c_re.astype(F32)), c_im=out_tile(-c_im.astype(F32)),
        a=jnp.stack([abar_re.reshape(1, SSM_LANES), abar_im.reshape(1, SSM_LANES)]),
        a_seg=powers(jnp.array([seg_len])),
        a_pow=powers(jnp.arange(1, seg_len + 1)),
        a_meta=powers(jnp.arange(N_META - 1, -1, -1)),
    )


def _gelu_tanh(t):
    return 0.5 * t * (1.0 + jnp.tanh(math.sqrt(2.0 / math.pi) * (t + 0.044715 * (t * t * t))))


def _ssm_kernel(u_ref, um_ref, bre_ref, bim_ref, cre_ref, cim_ref, a_ref, aseg_ref, apow_ref,
                ameta_ref, d_ref, wglu_ref, bglu_ref, g_ref, o_ref,
                xre, xim, hre, him, car_re, car_im, cin_re, cin_im, u_scr, y_scr):
    chunk = u_ref.shape[0]
    seg = chunk // SUBLANES

    @pl.when(pl.program_id(1) == 0)
    def _():
        um = um_ref[...].astype(BF16)
        for ct in range(SSM_COL_TILES):
            ub = um[:, ct * SSM_COL:(ct + 1) * SSM_COL]
            ls = slice(ct * SSM_COL_LANES, (ct + 1) * SSM_COL_LANES)
            xr = jnp.dot(ub, bre_ref[ct], preferred_element_type=F32)
            xi = jnp.dot(ub, bim_ref[ct], preferred_element_type=F32)
            pr = ameta_ref[0, :, ls]
            pi = ameta_ref[1, :, ls]
            car_re[:, ls] = jnp.sum(pr * xr - pi * xi, axis=0, keepdims=True)
            car_im[:, ls] = jnp.sum(pr * xi + pi * xr, axis=0, keepdims=True)

    n_lt = SSM_WIDTH // LANES
    for j in range(n_lt):
        u_scr[j] = u_ref[:, j * LANES:(j + 1) * LANES]
    up = jnp.concatenate(
        [jnp.concatenate([u_scr[j, pl.ds(k, SUBLANES, stride=seg), :] for j in range(n_lt)], axis=1)
         for k in range(seg)], axis=0).astype(BF16)
    for ct in range(SSM_COL_TILES):
        ub = up[:, ct * SSM_COL:(ct + 1) * SSM_COL]
        ls = slice(ct * SSM_COL_LANES, (ct + 1) * SSM_COL_LANES)
        xre[:, ls] = jnp.dot(ub, bre_ref[ct], preferred_element_type=F32)
        xim[:, ls] = jnp.dot(ub, bim_ref[ct], preferred_element_type=F32)

    for lb in range(SSM_LANES // SCAN_LANES):
        ls = slice(lb * SCAN_LANES, (lb + 1) * SCAN_LANES)
        ar = jnp.broadcast_to(a_ref[0, :, ls], (SUBLANES, SCAN_LANES))
        ai = jnp.broadcast_to(a_ref[1, :, ls], (SUBLANES, SCAN_LANES))

        def scan_body(k, carry):
            hr, hi = carry
            rows = pl.ds(pl.multiple_of(k * SUBLANES, SUBLANES), SUBLANES)
            nr = ar * hr - ai * hi + xre[rows, ls]
            ni = ar * hi + ai * hr + xim[rows, ls]
            xre[rows, ls] = nr
            xim[rows, ls] = ni
            return nr, ni

        zero = jnp.zeros((SUBLANES, SCAN_LANES), F32)
        lax.fori_loop(0, seg, scan_body, (zero, zero))

    cr = car_re[...]
    ci = car_im[...]
    sr = aseg_ref[0]
    si = aseg_ref[1]
    for r in range(SUBLANES):
        cin_re[r:r + 1, :] = cr
        cin_im[r:r + 1, :] = ci
        er = xre[chunk - SUBLANES + r:chunk - SUBLANES + r + 1, :]
        ei = xim[chunk - SUBLANES + r:chunk - SUBLANES + r + 1, :]
        cr, ci = sr * cr - si * ci + er, sr * ci + si * cr + ei
    car_re[...] = cr
    car_im[...] = ci

    pair = 2 * SUBLANES
    for lb in range(SSM_LANES // SCAN_LANES):
        ls = slice(lb * SCAN_LANES, (lb + 1) * SCAN_LANES)
        er = jnp.concatenate([cin_re[:, ls], cin_re[:, ls]], axis=0)
        ei = jnp.concatenate([cin_im[:, ls], cin_im[:, ls]], axis=0)

        def fix_body(k2, _):
            rows = pl.ds(pl.multiple_of(k2 * pair, pair), pair)
            k = 2 * k2
            pr = jnp.concatenate(
                [jnp.broadcast_to(apow_ref[0, pl.ds(k, 1), ls], (SUBLANES, SCAN_LANES)),
                 jnp.broadcast_to(apow_ref[0, pl.ds(k + 1, 1), ls], (SUBLANES, SCAN_LANES))], axis=0)
            pi = jnp.concatenate(
                [jnp.broadcast_to(apow_ref[1, pl.ds(k, 1), ls], (SUBLANES, SCAN_LANES)),
                 jnp.broadcast_to(apow_ref[1, pl.ds(k + 1, 1), ls], (SUBLANES, SCAN_LANES))], axis=0)
            hre[rows, ls] = (xre[rows, ls] + pr * er - pi * ei).astype(BF16)
            him[rows, ls] = (xim[rows, ls] + pr * ei + pi * er).astype(BF16)
            return 0

        lax.fori_loop(0, seg // 2, fix_body, 0)

    for ct in range(SSM_COL_TILES):
        ls = slice(ct * SSM_COL_LANES, (ct + 1) * SSM_COL_LANES)
        y = (jnp.dot(hre[:, ls], cre_ref[ct], preferred_element_type=F32)
             + jnp.dot(him[:, ls], cim_ref[ct], preferred_element_type=F32))
        for k in range(seg):
            for jj in range(SSM_COL // LANES):
                y_scr[ct * (SSM_COL // LANES) + jj, pl.ds(k, SUBLANES, stride=seg), :] = (
                    y[k * SUBLANES:(k + 1) * SUBLANES, jj * LANES:(jj + 1) * LANES])

    y = jnp.concatenate([y_scr[j] for j in range(n_lt)], axis=1)
    y = _gelu_tanh(y + d_ref[...] * u_ref[...])
    gate = jnp.dot(y.astype(BF16), wglu_ref[...], preferred_element_type=F32) + bglu_ref[...]
    y = y * _sigmoid(gate)
    o_ref[...] = _rms(y, g_ref[...]).astype(BF16)


def _ssm(u, u_meta, prm, d_skip, w_glu_bf, b_glu, g_ssm, bsz, seq, chunk):
    nc = seq // chunk
    seg = chunk // SUBLANES
    row = lambda b, c: (b * nc + c, 0)
    c2 = lambda b, c: (0, 0)
    c3 = lambda b, c: (0, 0, 0)
    return pl.pallas_call(
        _ssm_kernel,
        grid=(bsz, nc),
        in_specs=[
            pl.BlockSpec((chunk, SSM_WIDTH), row),
            pl.BlockSpec((N_META, SSM_WIDTH), c2),
            pl.BlockSpec((SSM_COL_TILES, SSM_COL, SSM_COL_LANES), c3),
            pl.BlockSpec((SSM_COL_TILES, SSM_COL, SSM_COL_LANES), c3),
            pl.BlockSpec((SSM_COL_TILES, SSM_COL_LANES, SSM_COL), c3),
            pl.BlockSpec((SSM_COL_TILES, SSM_COL_LANES, SSM_COL), c3),
            pl.BlockSpec((2, 1, SSM_LANES), c3),
            pl.BlockSpec((2, 1, SSM_LANES), c3),
            pl.BlockSpec((2, seg, SSM_LANES), c3),
            pl.BlockSpec((2, N_META, SSM_LANES), c3),
            pl.BlockSpec((1, SSM_WIDTH), c2),
            pl.BlockSpec((SSM_WIDTH, SSM_WIDTH), c2),
            pl.BlockSpec((1, SSM_WIDTH), c2),
            pl.BlockSpec((1, SSM_WIDTH), c2),
        ],
        out_specs=pl.BlockSpec((chunk, SSM_WIDTH), row),
        out_shape=jax.ShapeDtypeStruct((bsz * seq, SSM_WIDTH), BF16),
        scratch_shapes=[
            pltpu.VMEM((chunk, SSM_LANES), F32),
            pltpu.VMEM((chunk, SSM_LANES), F32),
            pltpu.VMEM((chunk, SSM_LANES), BF16),
            pltpu.VMEM((chunk, SSM_LANES), BF16),
            pltpu.VMEM((1, SSM_LANES), F32),
            pltpu.VMEM((1, SSM_LANES), F32),
            pltpu.VMEM((SUBLANES, SSM_LANES), F32),
            pltpu.VMEM((SUBLANES, SSM_LANES), F32),
            pltpu.VMEM((SSM_WIDTH // LANES, chunk, LANES), F32),
            pltpu.VMEM((SSM_WIDTH // LANES, chunk, LANES), F32),
        ],
        compiler_params=pltpu.CompilerParams(
            dimension_semantics=("arbitrary", "arbitrary"), vmem_limit_bytes=VMEM_LIMIT),
        name="ssm",
    )(u, u_meta, prm["b_re"], prm["b_im"], prm["c_re"], prm["c_im"], prm["a"], prm["a_seg"],
      prm["a_pow"], prm["a_meta"], d_skip, w_glu_bf, b_glu, g_ssm)


def _outproj_kernel(ya_ref, ys_ref, x_ref, wo_ref, bo_ref, gf_ref, wr_ref, br_ref,
                    h_ref, xn_ref, lg_ref):
    mix = (jnp.dot(ya_ref[...], wo_ref[:ATTN_WIDTH, :], preferred_element_type=F32)
           + jnp.dot(ys_ref[...], wo_ref[ATTN_WIDTH:, :], preferred_element_type=F32))
    h = x_ref[...] + mix + bo_ref[...]
    h_ref[...] = h
    n = _rms(h, gf_ref[...])
    n_hi = n.astype(BF16)
    xn_ref[...] = n_hi
    n_lo = (n - n_hi.astype(F32)).astype(BF16)
    hi = jnp.dot(n_hi, wr_ref[...], preferred_element_type=F32)
    lo = jnp.dot(n_lo, wr_ref[:, :ROUTER_LANES], preferred_element_type=F32)
    lg_ref[...] = hi[:, :ROUTER_LANES] + hi[:, ROUTER_LANES:] + lo + br_ref[...]


def _outproj(y_attn, y_ssm, xf, w_out_bf, b_out, g_ffn, w_router_pad, b_router_pad, row_tile):
    rows = xf.shape[0]
    row = lambda i: (i, 0)
    const = lambda i: (0, 0)
    return pl.pallas_call(
        _outproj_kernel,
        grid=(rows // row_tile,),
        in_specs=[
            pl.BlockSpec((row_tile, ATTN_WIDTH), row),
            pl.BlockSpec((row_tile, SSM_WIDTH), row),
            pl.BlockSpec((row_tile, D_MODEL), row),
            pl.BlockSpec((D_MODEL, D_MODEL), const),
            pl.BlockSpec((1, D_MODEL), const),
            pl.BlockSpec((1, D_MODEL), const),
            pl.BlockSpec((D_MODEL, 2 * ROUTER_LANES), const),
            pl.BlockSpec((1, ROUTER_LANES), const),
        ],
        out_specs=[
            pl.BlockSpec((row_tile, D_MODEL), row),
            pl.BlockSpec((row_tile, D_MODEL), row),
            pl.BlockSpec((row_tile, ROUTER_LANES), row),
        ],
        out_shape=[
            jax.ShapeDtypeStruct((rows, D_MODEL), F32),
            jax.ShapeDtypeStruct((rows, D_MODEL), BF16),
            jax.ShapeDtypeStruct((rows, ROUTER_LANES), F32),
        ],
        compiler_params=pltpu.CompilerParams(
            dimension_semantics=("arbitrary",), vmem_limit_bytes=VMEM_LIMIT),
        name="outproj",
    )(y_attn, y_ssm, xf, w_out_bf, b_out, g_ffn, w_router_pad, b_router_pad)


def _moe_kernel(ie_ref, ir_ref, in_ref, ni_ref,
                xs_hbm, w1a_ref, w1b_ref, b1g_ref, b1u_ref, w2_ref, b2_ref, perm_ref, ys_hbm,
                x_scr, acc, w1g, w1u, w2b, sem_in, sem_out):
    i = pl.program_id(0)
    f = pl.program_id(1)
    nf = pl.num_programs(1)
    tf = w2_ref.shape[1]
    n_chunks = x_scr.shape[0] // MOE_PAD

    @pl.when(i < ni_ref[0])
    def _():
        r0 = pl.multiple_of(ir_ref[i], MOE_PAD)
        n = in_ref[i]

        n_big = n // MOE_BLOCK
        tail_r = pl.multiple_of(n_big * MOE_BLOCK, MOE_PAD)
        has_tail = n % MOE_BLOCK != 0

        def in_copy(c):
            return pltpu.make_async_copy(
                xs_hbm.at[pl.ds(pl.multiple_of(r0 + c * MOE_PAD, MOE_PAD), MOE_PAD), :],
                x_scr.at[pl.ds(pl.multiple_of(c * MOE_PAD, MOE_PAD), MOE_PAD), :], sem_in.at[c])

        def out_copy(r, rows):
            return pltpu.make_async_copy(
                acc.at[pl.ds(r, rows), :],
                ys_hbm.at[pl.ds(pl.multiple_of(r0 + r, MOE_PAD), rows), :], sem_out)

        @pl.when(f == 0)
        def _():
            for c in range(n_chunks):
                @pl.when(c * MOE_PAD < n)
                def _(c=c):
                    in_copy(c).start()
            bias = jnp.broadcast_to(b2_ref[0], (MOE_PAD, D_MODEL))

            def init(c, _):
                acc[pl.ds(pl.multiple_of(c * MOE_PAD, MOE_PAD), MOE_PAD), :] = bias
                return 0

            lax.fori_loop(0, n // MOE_PAD, init, 0)

        half = PERM_COLS // 2
        k_half = D_MODEL // 2
        for hh, w1_ref in enumerate((w1a_ref, w1b_ref)):
            for c in range(2 * tf // PERM_COLS):
                t = jnp.dot(w1_ref[0, :, c * PERM_COLS:(c + 1) * PERM_COLS].astype(BF16),
                            perm_ref[...], preferred_element_type=F32)
                rs = slice(hh * k_half, (hh + 1) * k_half)
                w1g[rs, c * half:(c + 1) * half] = t[:, :half].astype(BF16)
                w1u[rs, c * half:(c + 1) * half] = t[:, half:].astype(BF16)
        w2b[...] = w2_ref[0].astype(BF16)

        def block(r, rows, c0):
            @pl.when(f == 0)
            def _():
                for j in range(rows // MOE_PAD):
                    in_copy(c0 + j).wait()

            sl = pl.ds(r, rows)
            xb = x_scr[sl, :]
            g = jnp.dot(xb, w1g[...], preferred_element_type=F32) + b1g_ref[0]
            up = jnp.dot(xb, w1u[...], preferred_element_type=F32) + b1u_ref[0]
            g = jnp.minimum(g, SWIGLU_LIMIT)
            up = jnp.clip(up, -SWIGLU_LIMIT, SWIGLU_LIMIT)
            act = g * _sigmoid(SWIGLU_ALPHA * g) * (up + 1.0)
            acc[sl, :] += jnp.dot(act.astype(BF16), w2b[...], preferred_element_type=F32)

            @pl.when(f == nf - 1)
            def _():
                out_copy(r, rows).start()

        def big(b, _):
            block(pl.multiple_of(b * MOE_BLOCK, MOE_BLOCK), MOE_BLOCK, b * (MOE_BLOCK // MOE_PAD))
            return 0

        lax.fori_loop(0, n_big, big, 0)

        @pl.when(has_tail)
        def _():
            block(tail_r, MOE_PAD, n_big * (MOE_BLOCK // MOE_PAD))

        @pl.when(f == nf - 1)
        def _():
            def drain(b, _):
                out_copy(pl.multiple_of(b * MOE_BLOCK, MOE_BLOCK), MOE_BLOCK).wait()
                return 0

            lax.fori_loop(0, n_big, drain, 0)

            @pl.when(has_tail)
            def _():
                out_copy(tail_r, MOE_PAD).wait()


def _experts(xs, item_e, item_r0, item_n, n_items, w1, b1g, b1u, w2, b2, tf):
    n_rows = xs.shape[0]
    max_items = item_e.shape[0]
    nf = D_FF // tf
    idx = jnp.arange(PERM_COLS)
    dst = jnp.where(idx % 2 == 0, idx // 2, PERM_COLS // 2 + idx // 2)
    perm = (dst[:, None] == jnp.arange(PERM_COLS)[None, :]).astype(BF16)

    def ex(i, ie, ni):
        return ie[jnp.minimum(i, ni[0] - 1)]

    def ff(i, f, ni):
        return jnp.where(i < ni[0], f, nf - 1)

    grid_spec = pltpu.PrefetchScalarGridSpec(
        num_scalar_prefetch=4,
        grid=(max_items, nf),
        in_specs=[
            pl.BlockSpec(memory_space=pl.ANY),
            pl.BlockSpec((1, D_MODEL // 2, 2 * tf), lambda i, f, ie, ir, im, ni: (ex(i, ie, ni), 0, ff(i, f, ni))),
            pl.BlockSpec((1, D_MODEL // 2, 2 * tf), lambda i, f, ie, ir, im, ni: (ex(i, ie, ni), 1, ff(i, f, ni))),
            pl.BlockSpec((1, 1, tf), lambda i, f, ie, ir, im, ni: (ex(i, ie, ni), 0, ff(i, f, ni))),
            pl.BlockSpec((1, 1, tf), lambda i, f, ie, ir, im, ni: (ex(i, ie, ni), 0, ff(i, f, ni))),
            pl.BlockSpec((1, tf, D_MODEL), lambda i, f, ie, ir, im, ni: (ex(i, ie, ni), ff(i, f, ni), 0)),
            pl.BlockSpec((1, 1, D_MODEL), lambda i, f, ie, ir, im, ni: (ex(i, ie, ni), 0, 0)),
            pl.BlockSpec((PERM_COLS, PERM_COLS), lambda i, f, ie, ir, im, ni: (0, 0)),
        ],
        out_specs=pl.BlockSpec(memory_space=pl.ANY),
        scratch_shapes=[
            pltpu.VMEM((MOE_ITEM_ROWS, D_MODEL), BF16),
            pltpu.VMEM((MOE_ITEM_ROWS, D_MODEL), F32),
            pltpu.VMEM((D_MODEL, tf), BF16),
            pltpu.VMEM((D_MODEL, tf), BF16),
            pltpu.VMEM((tf, D_MODEL), BF16),
            pltpu.SemaphoreType.DMA((MOE_ITEM_ROWS // MOE_PAD,)),
            pltpu.SemaphoreType.DMA(()),
        ],
    )
    return pl.pallas_call(
        _moe_kernel,
        grid_spec=grid_spec,
        out_shape=jax.ShapeDtypeStruct((n_rows, D_MODEL), F32),
        compiler_params=pltpu.CompilerParams(
            dimension_semantics=("arbitrary", "arbitrary"), vmem_limit_bytes=VMEM_LIMIT),
        name="experts",
    )(item_e, item_r0, item_n, n_items, xs, w1, w1, b1g, b1u, w2, b2, perm)


def _combine_kernel(h_ref, y0_ref, y1_ref, y2_ref, y3_ref, gate_ref, g_ref, o_ref):
    acc = h_ref[...]
    gates = gate_ref[...]
    for k, y_ref in enumerate((y0_ref, y1_ref, y2_ref, y3_ref)):
        acc = acc + gates[:, k:k + 1] * y_ref[...]
    o_ref[...] = _rms(acc, g_ref[...])


def _combine(h1, ygs, gates_pad, g_final, row_tile):
    rows = h1.shape[0]
    row = lambda i: (i, 0)
    const = lambda i: (0, 0)
    return pl.pallas_call(
        _combine_kernel,
        grid=(rows // row_tile,),
        in_specs=[pl.BlockSpec((row_tile, D_MODEL), row)] * (1 + TOP_K) + [
            pl.BlockSpec((row_tile, LANES), row),
            pl.BlockSpec((1, D_MODEL), const),
        ],
        out_specs=pl.BlockSpec((row_tile, D_MODEL), row),
        out_shape=jax.ShapeDtypeStruct((rows, D_MODEL), F32),
        compiler_params=pltpu.CompilerParams(
            dimension_semantics=("arbitrary",), vmem_limit_bytes=VMEM_LIMIT),
        name="combine",
    )(h1, *ygs, gates_pad, g_final)


def kernel(x, meta_tokens, g_mix, w_in, b_in, attn_sinks, ssm_a_re, ssm_a_im, ssm_log_dt,
           ssm_b_re, ssm_b_im, ssm_c_re, ssm_c_im, ssm_d, w_glu, b_glu, g_attn_out, g_ssm_out,
           w_out, b_out, g_ffn, w_router, b_router, w_mlp1, b_mlp1, w_mlp2, b_mlp2, g_final):
    bsz, seq, _ = x.shape
    rows = bsz * seq
    row_tile = min(ROW_TILE, seq)
    chunk = min(SSM_CHUNK, seq)
    assert seq % ATTN_BLOCK == 0 and seq % row_tile == 0 and seq % chunk == 0
    xf = x.reshape(rows, D_MODEL)

    w_in_bf = w_in[0].astype(BF16)
    cos_r, sin_r = _rope_tables(N_META + jnp.arange(seq))
    cos_m, sin_m = _rope_tables(jnp.arange(N_META))
    qt, k, vt, u = _inproj(xf, g_mix, w_in_bf, b_in, cos_r, sin_r, row_tile, True)
    _, k_meta, v_meta, u_meta = _inproj(meta_tokens, g_mix, w_in_bf, b_in, cos_m, sin_m,
                                        N_META, False)

    y_attn = _attention(qt, k, vt, k_meta, v_meta.T, attn_sinks, g_attn_out, bsz, seq)

    prm = _ssm_params(ssm_a_re[0], ssm_a_im[0], ssm_log_dt[0], ssm_b_re[0], ssm_b_im[0],
                      ssm_c_re[0], ssm_c_im[0], chunk // SUBLANES)
    y_ssm = _ssm(u, u_meta, prm, ssm_d, w_glu[0].astype(BF16), b_glu, g_ssm_out, bsz, seq, chunk)

    w_router_pad = jnp.pad(w_router[0], ((0, 0), (0, ROUTER_LANES - N_EXPERTS)))
    b_router_pad = jnp.pad(b_router, ((0, 0), (0, ROUTER_LANES - N_EXPERTS)))
    w_router_hi = w_router_pad.astype(BF16)
    w_router_lo = (w_router_pad - w_router_hi.astype(F32)).astype(BF16)
    w_router_pad = jnp.concatenate([w_router_hi, w_router_lo], axis=1)
    h1, xn, logits = _outproj(y_attn, y_ssm, xf, w_out[0].astype(BF16), b_out, g_ffn,
                              w_router_pad, b_router_pad, row_tile)

    top_val, top_idx = lax.top_k(logits[:, :N_EXPERTS], TOP_K)
    gates = jax.nn.softmax(top_val, axis=-1)
    chosen = (top_idx[:, :, None] == jnp.arange(N_EXPERTS)[None, None, :]).any(axis=1)
    chosen = chosen.astype(jnp.int32)
    rank = jnp.cumsum(chosen, axis=0) - chosen
    counts = jnp.sum(chosen, axis=0)
    padded = ((counts + MOE_PAD - 1) // MOE_PAD) * MOE_PAD
    pad_ends = jnp.cumsum(padded)
    pad_starts = pad_ends - padded
    dest = pad_starts[top_idx] + jnp.take_along_axis(rank, top_idx, axis=1)
    n_assign = rows * TOP_K
    n_rows = n_assign + N_EXPERTS * MOE_PAD
    tok = jnp.broadcast_to(jnp.arange(rows, dtype=jnp.int32)[:, None], (rows, TOP_K))
    row_tok = jnp.zeros((n_rows,), jnp.int32).at[dest.reshape(-1)].set(tok.reshape(-1))
    per_e = (padded + MOE_ITEM_ROWS - 1) // MOE_ITEM_ROWS
    item_ends = jnp.cumsum(per_e)
    max_items = N_EXPERTS + n_assign // MOE_ITEM_ROWS
    slot = jnp.arange(max_items, dtype=jnp.int32)
    item_e = jnp.minimum(jnp.searchsorted(item_ends, slot, side='right'), N_EXPERTS - 1)
    piece = slot - (item_ends - per_e)[item_e]
    item_r0 = (pad_starts[item_e] + piece * MOE_ITEM_ROWS).astype(jnp.int32)
    item_n = jnp.clip(padded[item_e] - piece * MOE_ITEM_ROWS, 0, MOE_ITEM_ROWS).astype(jnp.int32)
    n_items = item_ends[-1].astype(jnp.int32).reshape(1)

    tf = MOE_FF_TILE
    b1 = b_mlp1[0].reshape(N_EXPERTS, 1, D_FF, 2)
    b2 = b_mlp2[0].reshape(N_EXPERTS, 1, D_MODEL)
    xs = xn[row_tok]
    ys = _experts(xs, item_e.astype(jnp.int32), item_r0, item_n, n_items,
                  w_mlp1[0], b1[..., 0], b1[..., 1], w_mlp2[0], b2, tf)

    ygs = [ys[dest[:, k]] for k in range(TOP_K)]
    gates_pad = jnp.pad(gates, ((0, 0), (0, LANES - TOP_K)))
    out = _combine(h1, ygs, gates_pad, g_final.reshape(1, D_MODEL), row_tile)
    return out.reshape(bsz, seq, D_MODEL)
```

```python
import functools
import math

import jax
import jax.numpy as jnp
from jax import lax
from jax.experimental import pallas as pl
from jax.experimental.pallas import tpu as pltpu

F32 = jnp.float32
BF16 = jnp.bfloat16

D_MODEL = 2048
N_META = 16
HEAD_DIM = 64
ATTN_WIDTH = 1024
N_Q_HEADS = 16
N_KV_HEADS = 4
Q_PER_KV = 4
KV_WIDTH = 256
ATTN_BLOCK = 128
ROPE_THETA = 10000.0
SSM_WIDTH = 1024
SSM_GROUP = 16
N_SSM_GROUPS = 64
SSM_STATE = 64
SSM_LANES = N_SSM_GROUPS * SSM_STATE
IN_WIDTH = 2560
N_EXPERTS = 32
TOP_K = 4
D_FF = 2048
SWIGLU_LIMIT = 7.0
SWIGLU_ALPHA = 1.702
NORM_EPS = 1e-5

LANES = 128
SUBLANES = 8
VMEM_LIMIT = 56 * 1024 * 1024

ROW_TILE = 512
SSM_CHUNK = 256
SSM_COL = 256
SSM_COL_TILES = SSM_WIDTH // SSM_COL
SSM_COL_LANES = SSM_LANES // SSM_COL_TILES
SCAN_LANES = 512
MOE_PAD = 256
MOE_BLOCK = 512
MOE_ITEM_ROWS = 2560
MOE_FF_TILE = 256
MOE_STAGE_SLOTS = 4
ROUTER_LANES = 128
PERM_COLS = 256


def _rms(t, gain):
    return t * lax.rsqrt(jnp.mean(t * t, axis=-1, keepdims=True) + NORM_EPS) * gain


def _sigmoid(t):
    return 1.0 / (1.0 + jnp.exp(-t))


def _inproj_kernel(x_ref, g_ref, w_ref, b_ref, cos_ref, sin_ref, q_ref, k_ref, v_ref, u_ref,
                   *, transposed):
    n = _rms(x_ref[...], g_ref[...]).astype(BF16)
    cos = cos_ref[...]
    sin = sin_ref[...]
    lane = lax.broadcasted_iota(jnp.int32, cos.shape, 1)
    first_half = (lane % HEAD_DIM) < (HEAD_DIM // 2)

    def proj(c0, c1):
        return jnp.dot(n, w_ref[:, c0:c1], preferred_element_type=F32) + b_ref[:, c0:c1]

    def rope(t):
        partner = jnp.where(first_half, pltpu.roll(t, LANES - HEAD_DIM // 2, 1),
                            pltpu.roll(t, HEAD_DIM // 2, 1))
        return t * cos + partner * sin

    def put(ref, j, t):
        if transposed:
            ref[j * LANES:(j + 1) * LANES, :] = t.T.astype(BF16)
        else:
            ref[:, j * LANES:(j + 1) * LANES] = t.astype(BF16)

    scale = HEAD_DIM ** -0.5
    for j in range(ATTN_WIDTH // LANES):
        put(q_ref, j, rope(proj(j * LANES, (j + 1) * LANES)) * scale)
    for j in range(KV_WIDTH // LANES):
        c0 = ATTN_WIDTH + j * LANES
        k_ref[:, j * LANES:(j + 1) * LANES] = rope(proj(c0, c0 + LANES)).astype(BF16)
    for j in range(KV_WIDTH // LANES):
        c0 = ATTN_WIDTH + KV_WIDTH + j * LANES
        put(v_ref, j, proj(c0, c0 + LANES))
    c0 = ATTN_WIDTH + 2 * KV_WIDTH
    u_ref[...] = proj(c0, c0 + SSM_WIDTH)


def _inproj(xf, g_mix, w_in_bf, b_in, cos_t, sin_t, row_tile, transposed):
    rows = xf.shape[0]
    tab_blocks = cos_t.shape[0] // row_tile
    row = lambda i: (i, 0)
    col = lambda i: (0, i)
    tab = lambda i: (i % tab_blocks, 0)
    const = lambda i: (0, 0)
    if transposed:
        q_spec, q_shape = pl.BlockSpec((ATTN_WIDTH, row_tile), col), (ATTN_WIDTH, rows)
        v_spec, v_shape = pl.BlockSpec((KV_WIDTH, row_tile), col), (KV_WIDTH, rows)
    else:
        q_spec, q_shape = pl.BlockSpec((row_tile, ATTN_WIDTH), row), (rows, ATTN_WIDTH)
        v_spec, v_shape = pl.BlockSpec((row_tile, KV_WIDTH), row), (rows, KV_WIDTH)
    return pl.pallas_call(
        functools.partial(_inproj_kernel, transposed=transposed),
        grid=(rows // row_tile,),
        in_specs=[
            pl.BlockSpec((row_tile, D_MODEL), row),
            pl.BlockSpec((1, D_MODEL), const),
            pl.BlockSpec((D_MODEL, IN_WIDTH), const),
            pl.BlockSpec((1, IN_WIDTH), const),
            pl.BlockSpec((row_tile, LANES), tab),
            pl.BlockSpec((row_tile, LANES), tab),
        ],
        out_specs=[
            q_spec,
            pl.BlockSpec((row_tile, KV_WIDTH), row),
            v_spec,
            pl.BlockSpec((row_tile, SSM_WIDTH), row),
        ],
        out_shape=[
            jax.ShapeDtypeStruct(q_shape, BF16),
            jax.ShapeDtypeStruct((rows, KV_WIDTH), BF16),
            jax.ShapeDtypeStruct(v_shape, BF16),
            jax.ShapeDtypeStruct((rows, SSM_WIDTH), F32),
        ],
        compiler_params=pltpu.CompilerParams(
            dimension_semantics=("arbitrary",), vmem_limit_bytes=VMEM_LIMIT),
        name="inproj",
    )(xf, g_mix, w_in_bf, b_in, cos_t, sin_t)


def _rope_tables(positions):
    half = HEAD_DIM // 2
    inv_freq = jnp.power(ROPE_THETA, -jnp.arange(half, dtype=F32) / half)
    ang = positions.astype(F32)[:, None] * inv_freq[None, :]
    cos = jnp.tile(jnp.cos(ang), (1, LANES // half))
    sin = jnp.tile(jnp.sin(ang), (1, LANES // half))
    sign = jnp.where((jnp.arange(LANES) % HEAD_DIM) < half, -1.0, 1.0).astype(F32)
    return cos, sin * sign[None, :]


def _attn_kernel(qt_ref, kp_ref, kc_ref, km_ref, vtp_ref, vtc_ref, vtm_ref, sink_ref, g_ref, o_ref):
    n = pl.program_id(1)
    n_keys = 2 * ATTN_BLOCK + N_META
    key = lax.broadcasted_iota(jnp.int32, (n_keys, ATTN_BLOCK), 0)
    qi = lax.broadcasted_iota(jnp.int32, (n_keys, ATTN_BLOCK), 1)
    no_prev = jnp.where(n > 0, 0, ATTN_BLOCK)
    cur_j = key - ATTN_BLOCK
    valid = ((key >= 2 * ATTN_BLOCK) | ((cur_j >= 0) & (cur_j <= qi))
             | ((key < ATTN_BLOCK) & (key > qi + no_prev)))
    bias = jnp.where(valid, 0.0, -1e30)
    zeros = jnp.zeros((HEAD_DIM, ATTN_BLOCK), BF16)
    heads = []
    for hk in range(N_KV_HEADS):
        lt = slice((hk // 2) * LANES, (hk // 2 + 1) * LANES)
        kt = jnp.concatenate([kp_ref[:, lt], kc_ref[:, lt], km_ref[:, lt]], axis=0)
        rs = slice(hk * HEAD_DIM, (hk + 1) * HEAD_DIM)
        vt = jnp.concatenate([vtp_ref[rs, :], vtc_ref[rs, :], vtm_ref[rs, :]], axis=1)
        for g in range(Q_PER_KV):
            h = hk * Q_PER_KV + g
            qh = qt_ref[h * HEAD_DIM:(h + 1) * HEAD_DIM, :]
            qpad = jnp.concatenate([qh, zeros] if hk % 2 == 0 else [zeros, qh], axis=0)
            s = jnp.dot(kt, qpad, preferred_element_type=F32) + bias
            sink = sink_ref[:, h:h + 1]
            m = jnp.maximum(jnp.max(s, axis=0, keepdims=True), sink)
            p = jnp.exp(s - m)
            denom = jnp.sum(p, axis=0, keepdims=True) + jnp.exp(sink - m)
            o = jnp.dot(vt, p.astype(BF16), preferred_element_type=F32)
            heads.append(o * (1.0 / denom))
    yt = jnp.concatenate(heads, axis=0)
    inv = lax.rsqrt(jnp.sum(yt * yt, axis=0, keepdims=True) * (1.0 / ATTN_WIDTH) + NORM_EPS)
    o_ref[...] = ((yt * inv).T * g_ref[...]).astype(BF16)


def _attention(qt, k, vt, k_meta, vt_meta, sinks, g_attn, bsz, seq):
    nb = seq // ATTN_BLOCK
    cur = lambda b, n: (b * nb + n, 0)
    prev = lambda b, n: (b * nb + jnp.maximum(n - 1, 0), 0)
    cur_t = lambda b, n: (0, b * nb + n)
    prev_t = lambda b, n: (0, b * nb + jnp.maximum(n - 1, 0))
    const = lambda b, n: (0, 0)
    return pl.pallas_call(
        _attn_kernel,
        grid=(bsz, nb),
        in_specs=[
            pl.BlockSpec((ATTN_WIDTH, ATTN_BLOCK), cur_t),
            pl.BlockSpec((ATTN_BLOCK, KV_WIDTH), prev),
            pl.BlockSpec((ATTN_BLOCK, KV_WIDTH), cur),
            pl.BlockSpec((N_META, KV_WIDTH), const),
            pl.BlockSpec((KV_WIDTH, ATTN_BLOCK), prev_t),
            pl.BlockSpec((KV_WIDTH, ATTN_BLOCK), cur_t),
            pl.BlockSpec((KV_WIDTH, N_META), const),
            pl.BlockSpec((1, N_Q_HEADS), const),
            pl.BlockSpec((1, ATTN_WIDTH), const),
        ],
        out_specs=pl.BlockSpec((ATTN_BLOCK, ATTN_WIDTH), cur),
        out_shape=jax.ShapeDtypeStruct((bsz * seq, ATTN_WIDTH), BF16),
        compiler_params=pltpu.CompilerParams(
            dimension_semantics=("arbitrary", "arbitrary"), vmem_limit_bytes=VMEM_LIMIT),
        name="attention",
    )(qt, k, k, k_meta, vt, vt, vt_meta, sinks, g_attn)


def _ssm_params(a_re, a_im, log_dt, b_re, b_im, c_re, c_im, seg_len):
    dt = jnp.exp(log_dt.astype(F32))[:, None]
    lam_re = jnp.minimum(a_re.astype(F32), -1e-4)
    lam_im = a_im.astype(F32)
    z_re, z_im = lam_re * dt, lam_im * dt
    mag = jnp.exp(z_re)
    abar_re, abar_im = mag * jnp.cos(z_im), mag * jnp.sin(z_im)
    den = lam_re * lam_re + lam_im * lam_im
    n_re, n_im = abar_re - 1.0, abar_im
    coef_re = (n_re * lam_re + n_im * lam_im) / den
    coef_im = (n_im * lam_re - n_re * lam_im) / den
    br, bi = b_re.astype(F32), b_im.astype(F32)
    bb_re = coef_re[..., None] * br - coef_im[..., None] * bi
    bb_im = coef_re[..., None] * bi + coef_im[..., None] * br

    groups_per_tile = SSM_COL // SSM_GROUP
    eye = jnp.eye(groups_per_tile, dtype=F32)

    def in_tile(bb):
        t = bb.reshape(SSM_COL_TILES, groups_per_tile, SSM_STATE, SSM_GROUP)
        t = jnp.einsum('tgpc,gh->tgchp', t, eye)
        return t.reshape(SSM_COL_TILES, SSM_COL, SSM_COL_LANES).astype(BF16)

    def out_tile(cc):
        t = cc.reshape(SSM_COL_TILES, groups_per_tile, SSM_GROUP, SSM_STATE)
        t = jnp.einsum('tgcp,gh->tgphc', t, eye)
        return t.reshape(SSM_COL_TILES, SSM_COL_LANES, SSM_COL).astype(BF16)

    def powers(exps):
        e = exps.astype(F32)[:, None, None]
        pm = jnp.exp(e * z_re[None])
        return jnp.stack([(pm * jnp.cos(e * z_im[None])).reshape(len(exps), SSM_LANES),
                          (pm * jnp.sin(e * z_im[None])).reshape(len(exps), SSM_LANES)])

    return dict(
        b_re=in_tile(bb_re), b_im=in_tile(bb_im),
        c_re=out_tile(c_re.astype(F32)), c_im=out_tile(-c_im.astype(F32)),
        a=jnp.stack([abar_re.reshape(1, SSM_LANES), abar_im.reshape(1, SSM_LANES)]),
        a_seg=powers(jnp.array([seg_len])),
        a_pow=powers(jnp.arange(1, seg_len + 1)),
        a_meta=powers(jnp.arange(N_META - 1, -1, -1)),
    )


def _gelu_tanh(t):
    return 0.5 * t * (1.0 + jnp.tanh(math.sqrt(2.0 / math.pi) * (t + 0.044715 * (t * t * t))))


def _ssm_kernel(u_ref, um_ref, bre_ref, bim_ref, cre_ref, cim_ref, a_ref, aseg_ref, apow_ref,
                ameta_ref, d_ref, wglu_ref, bglu_ref, g_ref, o_ref,
                xre, xim, hre, him, car_re, car_im, cin_re, cin_im, u_scr, y_scr):
    chunk = u_ref.shape[0]
    seg = chunk // SUBLANES

    @pl.when(pl.program_id(1) == 0)
    def _():
        um = um_ref[...].astype(BF16)
        for ct in range(SSM_COL_TILES):
            ub = um[:, ct * SSM_COL:(ct + 1) * SSM_COL]
            ls = slice(ct * SSM_COL_LANES, (ct + 1) * SSM_COL_LANES)
            xr = jnp.dot(ub, bre_ref[ct], preferred_element_type=F32)
            xi = jnp.dot(ub, bim_ref[ct], preferred_element_type=F32)
            pr = ameta_ref[0, :, ls]
            pi = ameta_ref[1, :, ls]
            car_re[:, ls] = jnp.sum(pr * xr - pi * xi, axis=0, keepdims=True)
            car_im[:, ls] = jnp.sum(pr * xi + pi * xr, axis=0, keepdims=True)

    n_lt = SSM_WIDTH // LANES
    for j in range(n_lt):
        u_scr[j] = u_ref[:, j * LANES:(j + 1) * LANES]
    up = jnp.concatenate(
        [jnp.concatenate([u_scr[j, pl.ds(k, SUBLANES, stride=seg), :] for j in range(n_lt)], axis=1)
         for k in range(seg)], axis=0).astype(BF16)
    for ct in range(SSM_COL_TILES):
        ub = up[:, ct * SSM_COL:(ct + 1) * SSM_COL]
        ls = slice(ct * SSM_COL_LANES, (ct + 1) * SSM_COL_LANES)
        xre[:, ls] = jnp.dot(ub, bre_ref[ct], preferred_element_type=F32)
        xim[:, ls] = jnp.dot(ub, bim_ref[ct], preferred_element_type=F32)

    for lb in range(SSM_LANES // SCAN_LANES):
        ls = slice(lb * SCAN_LANES, (lb + 1) * SCAN_LANES)
        ar = jnp.broadcast_to(a_ref[0, :, ls], (SUBLANES, SCAN_LANES))
        ai = jnp.broadcast_to(a_ref[1, :, ls], (SUBLANES, SCAN_LANES))

        def scan_body(k, carry):
            hr, hi = carry
            rows = pl.ds(pl.multiple_of(k * SUBLANES, SUBLANES), SUBLANES)
            nr = ar * hr - ai * hi + xre[rows, ls]
            ni = ar * hi + ai * hr + xim[rows, ls]
            xre[rows, ls] = nr
            xim[rows, ls] = ni
            return nr, ni

        zero = jnp.zeros((SUBLANES, SCAN_LANES), F32)
        lax.fori_loop(0, seg, scan_body, (zero, zero))

    cr = car_re[...]
    ci = car_im[...]
    sr = aseg_ref[0]
    si = aseg_ref[1]
    for r in range(SUBLANES):
        cin_re[r:r + 1, :] = cr
        cin_im[r:r + 1, :] = ci
        er = xre[chunk - SUBLANES + r:chunk - SUBLANES + r + 1, :]
        ei = xim[chunk - SUBLANES + r:chunk - SUBLANES + r + 1, :]
        cr, ci = sr * cr - si * ci + er, sr * ci + si * cr + ei
    car_re[...] = cr
    car_im[...] = ci

    pair = 2 * SUBLANES
    for lb in range(SSM_LANES // SCAN_LANES):
        ls = slice(lb * SCAN_LANES, (lb + 1) * SCAN_LANES)
        er = jnp.concatenate([cin_re[:, ls], cin_re[:, ls]], axis=0)
        ei = jnp.concatenate([cin_im[:, ls], cin_im[:, ls]], axis=0)

        def fix_body(k2, _):
            rows = pl.ds(pl.multiple_of(k2 * pair, pair), pair)
            k = 2 * k2
            pr = jnp.concatenate(
                [jnp.broadcast_to(apow_ref[0, pl.ds(k, 1), ls], (SUBLANES, SCAN_LANES)),
                 jnp.broadcast_to(apow_ref[0, pl.ds(k + 1, 1), ls], (SUBLANES, SCAN_LANES))], axis=0)
            pi = jnp.concatenate(
                [jnp.broadcast_to(apow_ref[1, pl.ds(k, 1), ls], (SUBLANES, SCAN_LANES)),
                 jnp.broadcast_to(apow_ref[1, pl.ds(k + 1, 1), ls], (SUBLANES, SCAN_LANES))], axis=0)
            hre[rows, ls] = (xre[rows, ls] + pr * er - pi * ei).astype(BF16)
            him[rows, ls] = (xim[rows, ls] + pr * ei + pi * er).astype(BF16)
            return 0

        lax.fori_loop(0, seg // 2, fix_body, 0)

    for ct in range(SSM_COL_TILES):
        ls = slice(ct * SSM_COL_LANES, (ct + 1) * SSM_COL_LANES)
        y = (jnp.dot(hre[:, ls], cre_ref[ct], preferred_element_type=F32)
             + jnp.dot(him[:, ls], cim_ref[ct], preferred_element_type=F32))
        for k in range(seg):
            for jj in range(SSM_COL // LANES):
                y_scr[ct * (SSM_COL // LANES) + jj, pl.ds(k, SUBLANES, stride=seg), :] = (
                    y[k * SUBLANES:(k + 1) * SUBLANES, jj * LANES:(jj + 1) * LANES])

    y = jnp.concatenate([y_scr[j] for j in range(n_lt)], axis=1)
    y = _gelu_tanh(y + d_ref[...] * u_ref[...])
    gate = jnp.dot(y.astype(BF16), wglu_ref[...], preferred_element_type=F32) + bglu_ref[...]
    y = y * _sigmoid(gate)
    o_ref[...] = _rms(y, g_ref[...]).astype(BF16)


def _ssm(u, u_meta, prm, d_skip, w_glu_bf, b_glu, g_ssm, bsz, seq, chunk):
    nc = seq // chunk
    seg = chunk // SUBLANES
    row = lambda b, c: (b * nc + c, 0)
    c2 = lambda b, c: (0, 0)
    c3 = lambda b, c: (0, 0, 0)
    return pl.pallas_call(
        _ssm_kernel,
        grid=(bsz, nc),
        in_specs=[
            pl.BlockSpec((chunk, SSM_WIDTH), row),
            pl.BlockSpec((N_META, SSM_WIDTH), c2),
            pl.BlockSpec((SSM_COL_TILES, SSM_COL, SSM_COL_LANES), c3),
            pl.BlockSpec((SSM_COL_TILES, SSM_COL, SSM_COL_LANES), c3),
            pl.BlockSpec((SSM_COL_TILES, SSM_COL_LANES, SSM_COL), c3),
            pl.BlockSpec((SSM_COL_TILES, SSM_COL_LANES, SSM_COL), c3),
            pl.BlockSpec((2, 1, SSM_LANES), c3),
            pl.BlockSpec((2, 1, SSM_LANES), c3),
            pl.BlockSpec((2, seg, SSM_LANES), c3),
            pl.BlockSpec((2, N_META, SSM_LANES), c3),
            pl.BlockSpec((1, SSM_WIDTH), c2),
            pl.BlockSpec((SSM_WIDTH, SSM_WIDTH), c2),
            pl.BlockSpec((1, SSM_WIDTH), c2),
            pl.BlockSpec((1, SSM_WIDTH), c2),
        ],
        out_specs=pl.BlockSpec((chunk, SSM_WIDTH), row),
        out_shape=jax.ShapeDtypeStruct((bsz * seq, SSM_WIDTH), BF16),
        scratch_shapes=[
            pltpu.VMEM((chunk, SSM_LANES), F32),
            pltpu.VMEM((chunk, SSM_LANES), F32),
            pltpu.VMEM((chunk, SSM_LANES), BF16),
            pltpu.VMEM((chunk, SSM_LANES), BF16),
            pltpu.VMEM((1, SSM_LANES), F32),
            pltpu.VMEM((1, SSM_LANES), F32),
            pltpu.VMEM((SUBLANES, SSM_LANES), F32),
            pltpu.VMEM((SUBLANES, SSM_LANES), F32),
            pltpu.VMEM((SSM_WIDTH // LANES, chunk, LANES), F32),
            pltpu.VMEM((SSM_WIDTH // LANES, chunk, LANES), F32),
        ],
        compiler_params=pltpu.CompilerParams(
            dimension_semantics=("arbitrary", "arbitrary"), vmem_limit_bytes=VMEM_LIMIT),
        name="ssm",
    )(u, u_meta, prm["b_re"], prm["b_im"], prm["c_re"], prm["c_im"], prm["a"], prm["a_seg"],
      prm["a_pow"], prm["a_meta"], d_skip, w_glu_bf, b_glu, g_ssm)


def _outproj_kernel(ya_ref, ys_ref, x_ref, wo_ref, bo_ref, gf_ref, wr_ref, br_ref,
                    h_ref, xn_ref, lg_ref):
    mix = (jnp.dot(ya_ref[...], wo_ref[:ATTN_WIDTH, :], preferred_element_type=F32)
           + jnp.dot(ys_ref[...], wo_ref[ATTN_WIDTH:, :], preferred_element_type=F32))
    h = x_ref[...] + mix + bo_ref[...]
    h_ref[...] = h
    n = _rms(h, gf_ref[...])
    n_hi = n.astype(BF16)
    bits = lax.bitcast_convert_type(n_hi.astype(F32), jnp.uint32)
    xn_ref[...] = (bits[:, :D_MODEL // 2] >> 16) | (bits[:, D_MODEL // 2:] & jnp.uint32(0xFFFF0000))
    n_lo = (n - n_hi.astype(F32)).astype(BF16)
    hi = jnp.dot(n_hi, wr_ref[...], preferred_element_type=F32)
    lo = jnp.dot(n_lo, wr_ref[:, :ROUTER_LANES], preferred_element_type=F32)
    lg_ref[...] = hi[:, :ROUTER_LANES] + hi[:, ROUTER_LANES:] + lo + br_ref[...]


def _outproj(y_attn, y_ssm, xf, w_out_bf, b_out, g_ffn, w_router_pad, b_router_pad, row_tile):
    rows = xf.shape[0]
    row = lambda i: (i, 0)
    const = lambda i: (0, 0)
    return pl.pallas_call(
        _outproj_kernel,
        grid=(rows // row_tile,),
        in_specs=[
            pl.BlockSpec((row_tile, ATTN_WIDTH), row),
            pl.BlockSpec((row_tile, SSM_WIDTH), row),
            pl.BlockSpec((row_tile, D_MODEL), row),
            pl.BlockSpec((D_MODEL, D_MODEL), const),
            pl.BlockSpec((1, D_MODEL), const),
            pl.BlockSpec((1, D_MODEL), const),
            pl.BlockSpec((D_MODEL, 2 * ROUTER_LANES), const),
            pl.BlockSpec((1, ROUTER_LANES), const),
        ],
        out_specs=[
            pl.BlockSpec((row_tile, D_MODEL), row),
            pl.BlockSpec((row_tile, D_MODEL // 2), row),
            pl.BlockSpec((row_tile, ROUTER_LANES), row),
        ],
        out_shape=[
            jax.ShapeDtypeStruct((rows, D_MODEL), F32),
            jax.ShapeDtypeStruct((rows, D_MODEL // 2), jnp.uint32),
            jax.ShapeDtypeStruct((rows, ROUTER_LANES), F32),
        ],
        compiler_params=pltpu.CompilerParams(
            dimension_semantics=("arbitrary",), vmem_limit_bytes=VMEM_LIMIT),
        name="outproj",
    )(y_attn, y_ssm, xf, w_out_bf, b_out, g_ffn, w_router_pad, b_router_pad)


def _moe_kernel(ie_ref, ir_ref, in_ref, ni_ref,
                xs_hbm, w1a_ref, w1b_ref, b1g_ref, b1u_ref, w2_ref, b2_ref, perm_ref, ys_hbm,
                stage, x_scr, acc, w1g, w1u, w2b, sem_in, sem_out):
    i = pl.program_id(0)
    f = pl.program_id(1)
    nf = pl.num_programs(1)
    tf = w2_ref.shape[1]
    n_stage = stage.shape[0]

    @pl.when(i < ni_ref[0])
    def _():
        r0 = pl.multiple_of(ir_ref[i], MOE_PAD)
        n = in_ref[i]

        n_big = n // MOE_BLOCK
        tail_r = pl.multiple_of(n_big * MOE_BLOCK, MOE_PAD)
        has_tail = n % MOE_BLOCK != 0

        def in_copy(c):
            slot = c % n_stage
            return pltpu.make_async_copy(
                xs_hbm.at[pl.ds(pl.multiple_of(r0 + c * MOE_PAD, MOE_PAD), MOE_PAD), :],
                stage.at[slot], sem_in.at[slot])

        def unpack(c):
            u = stage[c % n_stage]
            rows_c = pl.ds(pl.multiple_of(c * MOE_PAD, MOE_PAD), MOE_PAD)
            x_scr[rows_c, :D_MODEL // 2] = lax.bitcast_convert_type(u << 16, F32).astype(BF16)
            x_scr[rows_c, D_MODEL // 2:] = lax.bitcast_convert_type(
                u & jnp.uint32(0xFFFF0000), F32).astype(BF16)

        def out_copy(r, rows):
            return pltpu.make_async_copy(
                acc.at[pl.ds(r, rows), :],
                ys_hbm.at[pl.ds(pl.multiple_of(r0 + r, MOE_PAD), rows), :], sem_out)

        @pl.when(f == 0)
        def _():
            for c in range(n_stage):
                @pl.when(c * MOE_PAD < n)
                def _(c=c):
                    in_copy(c).start()
            bias = jnp.broadcast_to(b2_ref[0], (MOE_PAD, D_MODEL))

            def init(c, _):
                acc[pl.ds(pl.multiple_of(c * MOE_PAD, MOE_PAD), MOE_PAD), :] = bias
                return 0

            lax.fori_loop(0, n // MOE_PAD, init, 0)

        half = PERM_COLS // 2
        k_half = D_MODEL // 2
        for hh, w1_ref in enumerate((w1a_ref, w1b_ref)):
            for c in range(2 * tf // PERM_COLS):
                t = jnp.dot(w1_ref[0, :, c * PERM_COLS:(c + 1) * PERM_COLS].astype(BF16),
                            perm_ref[...], preferred_element_type=F32)
                rs = slice(hh * k_half, (hh + 1) * k_half)
                w1g[rs, c * half:(c + 1) * half] = t[:, :half].astype(BF16)
                w1u[rs, c * half:(c + 1) * half] = t[:, half:].astype(BF16)
        w2b[...] = w2_ref[0].astype(BF16)

        def block(r, rows, c0):
            @pl.when(f == 0)
            def _():
                for j in range(rows // MOE_PAD):
                    c = c0 + j
                    in_copy(c).wait()
                    unpack(c)

                    @pl.when((c + n_stage) * MOE_PAD < n)
                    def _(c=c):
                        in_copy(c + n_stage).start()

            sl = pl.ds(r, rows)
            xb = x_scr[sl, :]
            g = jnp.dot(xb, w1g[...], preferred_element_type=F32) + b1g_ref[0]
            up = jnp.dot(xb, w1u[...], preferred_element_type=F32) + b1u_ref[0]
            g = jnp.minimum(g, SWIGLU_LIMIT)
            up = jnp.clip(up, -SWIGLU_LIMIT, SWIGLU_LIMIT)
            act = g * _sigmoid(SWIGLU_ALPHA * g) * (up + 1.0)
            acc[sl, :] += jnp.dot(act.astype(BF16), w2b[...], preferred_element_type=F32)

            @pl.when(f == nf - 1)
            def _():
                out_copy(r, rows).start()

        def big(b, _):
            block(pl.multiple_of(b * MOE_BLOCK, MOE_BLOCK), MOE_BLOCK, b * (MOE_BLOCK // MOE_PAD))
            return 0

        lax.fori_loop(0, n_big, big, 0)

        @pl.when(has_tail)
        def _():
            block(tail_r, MOE_PAD, n_big * (MOE_BLOCK // MOE_PAD))

        @pl.when(f == nf - 1)
        def _():
            def drain(b, _):
                out_copy(pl.multiple_of(b * MOE_BLOCK, MOE_BLOCK), MOE_BLOCK).wait()
                return 0

            lax.fori_loop(0, n_big, drain, 0)

            @pl.when(has_tail)
            def _():
                out_copy(tail_r, MOE_PAD).wait()


def _experts(xs, item_e, item_r0, item_n, n_items, w1, b1g, b1u, w2, b2, tf):
    n_rows = xs.shape[0]
    max_items = item_e.shape[0]
    nf = D_FF // tf
    idx = jnp.arange(PERM_COLS)
    dst = jnp.where(idx % 2 == 0, idx // 2, PERM_COLS // 2 + idx // 2)
    perm = (dst[:, None] == jnp.arange(PERM_COLS)[None, :]).astype(BF16)

    def ex(i, ie, ni):
        return ie[jnp.minimum(i, ni[0] - 1)]

    def ff(i, f, ni):
        return jnp.where(i < ni[0], f, nf - 1)

    grid_spec = pltpu.PrefetchScalarGridSpec(
        num_scalar_prefetch=4,
        grid=(max_items, nf),
        in_specs=[
            pl.BlockSpec(memory_space=pl.ANY),
            pl.BlockSpec((1, D_MODEL // 2, 2 * tf), lambda i, f, ie, ir, im, ni: (ex(i, ie, ni), 0, ff(i, f, ni))),
            pl.BlockSpec((1, D_MODEL // 2, 2 * tf), lambda i, f, ie, ir, im, ni: (ex(i, ie, ni), 1, ff(i, f, ni))),
            pl.BlockSpec((1, 1, tf), lambda i, f, ie, ir, im, ni: (ex(i, ie, ni), 0, ff(i, f, ni))),
            pl.BlockSpec((1, 1, tf), lambda i, f, ie, ir, im, ni: (ex(i, ie, ni), 0, ff(i, f, ni))),
            pl.BlockSpec((1, tf, D_MODEL), lambda i, f, ie, ir, im, ni: (ex(i, ie, ni), ff(i, f, ni), 0)),
            pl.BlockSpec((1, 1, D_MODEL), lambda i, f, ie, ir, im, ni: (ex(i, ie, ni), 0, 0)),
            pl.BlockSpec((PERM_COLS, PERM_COLS), lambda i, f, ie, ir, im, ni: (0, 0)),
        ],
        out_specs=pl.BlockSpec(memory_space=pl.ANY),
        scratch_shapes=[
            pltpu.VMEM((MOE_STAGE_SLOTS, MOE_PAD, D_MODEL // 2), jnp.uint32),
            pltpu.VMEM((MOE_ITEM_ROWS, D_MODEL), BF16),
            pltpu.VMEM((MOE_ITEM_ROWS, D_MODEL), F32),
            pltpu.VMEM((D_MODEL, tf), BF16),
            pltpu.VMEM((D_MODEL, tf), BF16),
            pltpu.VMEM((tf, D_MODEL), BF16),
            pltpu.SemaphoreType.DMA((MOE_STAGE_SLOTS,)),
            pltpu.SemaphoreType.DMA(()),
        ],
    )
    return pl.pallas_call(
        _moe_kernel,
        grid_spec=grid_spec,
        out_shape=jax.ShapeDtypeStruct((n_rows, D_MODEL), F32),
        compiler_params=pltpu.CompilerParams(
            dimension_semantics=("arbitrary", "arbitrary"), vmem_limit_bytes=VMEM_LIMIT),
        name="experts",
    )(item_e, item_r0, item_n, n_items, xs, w1, w1, b1g, b1u, w2, b2, perm)


def _combine_kernel(h_ref, y0_ref, y1_ref, y2_ref, y3_ref, gate_ref, g_ref, o_ref):
    acc = h_ref[...]
    gates = gate_ref[...]
    for k, y_ref in enumerate((y0_ref, y1_ref, y2_ref, y3_ref)):
        acc = acc + gates[:, k:k + 1] * y_ref[...]
    o_ref[...] = _rms(acc, g_ref[...])


def _combine(h1, ygs, gates_pad, g_final, row_tile):
    rows = h1.shape[0]
    row = lambda i: (i, 0)
    const = lambda i: (0, 0)
    return pl.pallas_call(
        _combine_kernel,
        grid=(rows // row_tile,),
        in_specs=[pl.BlockSpec((row_tile, D_MODEL), row)] * (1 + TOP_K) + [
            pl.BlockSpec((row_tile, LANES), row),
            pl.BlockSpec((1, D_MODEL), const),
        ],
        out_specs=pl.BlockSpec((row_tile, D_MODEL), row),
        out_shape=jax.ShapeDtypeStruct((rows, D_MODEL), F32),
        compiler_params=pltpu.CompilerParams(
            dimension_semantics=("arbitrary",), vmem_limit_bytes=VMEM_LIMIT),
        name="combine",
    )(h1, *ygs, gates_pad, g_final)


def kernel(x, meta_tokens, g_mix, w_in, b_in, attn_sinks, ssm_a_re, ssm_a_im, ssm_log_dt,
           ssm_b_re, ssm_b_im, ssm_c_re, ssm_c_im, ssm_d, w_glu, b_glu, g_attn_out, g_ssm_out,
           w_out, b_out, g_ffn, w_router, b_router, w_mlp1, b_mlp1, w_mlp2, b_mlp2, g_final):
    bsz, seq, _ = x.shape
    rows = bsz * seq
    row_tile = min(ROW_TILE, seq)
    chunk = min(SSM_CHUNK, seq)
    assert seq % ATTN_BLOCK == 0 and seq % row_tile == 0 and seq % chunk == 0
    xf = x.reshape(rows, D_MODEL)

    w_in_bf = w_in[0].astype(BF16)
    cos_r, sin_r = _rope_tables(N_META + jnp.arange(seq))
    cos_m, sin_m = _rope_tables(jnp.arange(N_META))
    qt, k, vt, u = _inproj(xf, g_mix, w_in_bf, b_in, cos_r, sin_r, row_tile, True)
    _, k_meta, v_meta, u_meta = _inproj(meta_tokens, g_mix, w_in_bf, b_in, cos_m, sin_m,
                                        N_META, False)

    y_attn = _attention(qt, k, vt, k_meta, v_meta.T, attn_sinks, g_attn_out, bsz, seq)

    prm = _ssm_params(ssm_a_re[0], ssm_a_im[0], ssm_log_dt[0], ssm_b_re[0], ssm_b_im[0],
                      ssm_c_re[0], ssm_c_im[0], chunk // SUBLANES)
    y_ssm = _ssm(u, u_meta, prm, ssm_d, w_glu[0].astype(BF16), b_glu, g_ssm_out, bsz, seq, chunk)

    w_router_pad = jnp.pad(w_router[0], ((0, 0), (0, ROUTER_LANES - N_EXPERTS)))
    b_router_pad = jnp.pad(b_router, ((0, 0), (0, ROUTER_LANES - N_EXPERTS)))
    w_router_hi = w_router_pad.astype(BF16)
    w_router_lo = (w_router_pad - w_router_hi.astype(F32)).astype(BF16)
    w_router_pad = jnp.concatenate([w_router_hi, w_router_lo], axis=1)
    h1, xn, logits = _outproj(y_attn, y_ssm, xf, w_out[0].astype(BF16), b_out, g_ffn,
                              w_router_pad, b_router_pad, row_tile)

    top_val, top_idx = lax.top_k(logits[:, :N_EXPERTS], TOP_K)
    gates = jax.nn.softmax(top_val, axis=-1)
    chosen = (top_idx[:, :, None] == jnp.arange(N_EXPERTS)[None, None, :]).any(axis=1)
    chosen = chosen.astype(jnp.int32)
    rank = jnp.cumsum(chosen, axis=0) - chosen
    counts = jnp.sum(chosen, axis=0)
    padded = ((counts + MOE_PAD - 1) // MOE_PAD) * MOE_PAD
    pad_ends = jnp.cumsum(padded)
    pad_starts = pad_ends - padded
    dest = pad_starts[top_idx] + jnp.take_along_axis(rank, top_idx, axis=1)
    n_assign = rows * TOP_K
    n_rows = n_assign + N_EXPERTS * MOE_PAD
    order = jnp.argsort(top_idx.reshape(-1), stable=True).astype(jnp.int32)
    starts = jnp.cumsum(counts) - counts
    r = jnp.arange(n_rows, dtype=jnp.int32)
    r_e = jnp.minimum(jnp.searchsorted(pad_ends, r, side='right'), N_EXPERTS - 1)
    r_j = r - pad_starts[r_e]
    row_tok = jnp.where(r_j < counts[r_e],
                        order[jnp.minimum(starts[r_e] + r_j, n_assign - 1)] // TOP_K, 0)
    per_e = (padded + MOE_ITEM_ROWS - 1) // MOE_ITEM_ROWS
    item_ends = jnp.cumsum(per_e)
    max_items = N_EXPERTS + n_assign // MOE_ITEM_ROWS
    slot = jnp.arange(max_items, dtype=jnp.int32)
    item_e = jnp.minimum(jnp.searchsorted(item_ends, slot, side='right'), N_EXPERTS - 1)
    piece = slot - (item_ends - per_e)[item_e]
    item_r0 = (pad_starts[item_e] + piece * MOE_ITEM_ROWS).astype(jnp.int32)
    item_n = jnp.clip(padded[item_e] - piece * MOE_ITEM_ROWS, 0, MOE_ITEM_ROWS).astype(jnp.int32)
    n_items = item_ends[-1].astype(jnp.int32).reshape(1)

    tf = MOE_FF_TILE
    b1 = b_mlp1[0].reshape(N_EXPERTS, 1, D_FF, 2)
    b2 = b_mlp2[0].reshape(N_EXPERTS, 1, D_MODEL)
    xs = xn[row_tok]
    ys = _experts(xs, item_e.astype(jnp.int32), item_r0, item_n, n_items,
                  w_mlp1[0], b1[..., 0], b1[..., 1], w_mlp2[0], b2, tf)

    ygs = [ys[dest[:, k]] for k in range(TOP_K)]
    gates_pad = jnp.pad(gates, ((0, 0), (0, LANES - TOP_K)))
    out = _combine(h1, ygs, gates_pad, g_final.reshape(1, D_MODEL), row_tile)
    return out.reshape(bsz, seq, D_MODEL)
```

```python
import functools
import math

import jax
import jax.numpy as jnp
from jax import lax
from jax.experimental import pallas as pl
from jax.experimental.pallas import tpu as pltpu

F32 = jnp.float32
BF16 = jnp.bfloat16

D_MODEL = 2048
N_META = 16
HEAD_DIM = 64
ATTN_WIDTH = 1024
N_Q_HEADS = 16
N_KV_HEADS = 4
Q_PER_KV = 4
KV_WIDTH = 256
ATTN_BLOCK = 128
ROPE_THETA = 10000.0
SSM_WIDTH = 1024
SSM_GROUP = 16
N_SSM_GROUPS = 64
SSM_STATE = 64
SSM_LANES = N_SSM_GROUPS * SSM_STATE
IN_WIDTH = 2560
N_EXPERTS = 32
TOP_K = 4
D_FF = 2048
SWIGLU_LIMIT = 7.0
SWIGLU_ALPHA = 1.702
NORM_EPS = 1e-5

LANES = 128
SUBLANES = 8
VMEM_LIMIT = 56 * 1024 * 1024

ROW_TILE = 512
SSM_CHUNK = 256
SSM_COL = 256
SSM_COL_TILES = SSM_WIDTH // SSM_COL
SSM_COL_LANES = SSM_LANES // SSM_COL_TILES
SCAN_LANES = 512
MOE_PAD = 256
MOE_BLOCK = 512
MOE_ITEM_ROWS = 2560
MOE_FF_TILE = 256
MOE_STAGE_SLOTS = 4
ROUTER_LANES = 128
PERM_COLS = 256


def _rms(t, gain):
    return t * lax.rsqrt(jnp.mean(t * t, axis=-1, keepdims=True) + NORM_EPS) * gain


def _sigmoid(t):
    return 1.0 / (1.0 + jnp.exp(-t))


def _inproj_kernel(x_ref, g_ref, w_ref, b_ref, cos_ref, sin_ref, q_ref, k_ref, v_ref, u_ref,
                   *, transposed):
    n = _rms(x_ref[...], g_ref[...]).astype(BF16)
    cos = cos_ref[...]
    sin = sin_ref[...]
    lane = lax.broadcasted_iota(jnp.int32, cos.shape, 1)
    first_half = (lane % HEAD_DIM) < (HEAD_DIM // 2)

    def proj(c0, c1):
        return jnp.dot(n, w_ref[:, c0:c1], preferred_element_type=F32) + b_ref[:, c0:c1]

    def rope(t):
        partner = jnp.where(first_half, pltpu.roll(t, LANES - HEAD_DIM // 2, 1),
                            pltpu.roll(t, HEAD_DIM // 2, 1))
        return t * cos + partner * sin

    def put(ref, j, t):
        if transposed:
            ref[j * LANES:(j + 1) * LANES, :] = t.T.astype(BF16)
        else:
            ref[:, j * LANES:(j + 1) * LANES] = t.astype(BF16)

    scale = HEAD_DIM ** -0.5
    for j in range(ATTN_WIDTH // LANES):
        put(q_ref, j, rope(proj(j * LANES, (j + 1) * LANES)) * scale)
    for j in range(KV_WIDTH // LANES):
        c0 = ATTN_WIDTH + j * LANES
        k_ref[:, j * LANES:(j + 1) * LANES] = rope(proj(c0, c0 + LANES)).astype(BF16)
    for j in range(KV_WIDTH // LANES):
        c0 = ATTN_WIDTH + KV_WIDTH + j * LANES
        put(v_ref, j, proj(c0, c0 + LANES))
    c0 = ATTN_WIDTH + 2 * KV_WIDTH
    u_ref[...] = proj(c0, c0 + SSM_WIDTH)


def _inproj(xf, g_mix, w_in_bf, b_in, cos_t, sin_t, row_tile, transposed):
    rows = xf.shape[0]
    tab_blocks = cos_t.shape[0] // row_tile
    row = lambda i: (i, 0)
    col = lambda i: (0, i)
    tab = lambda i: (i % tab_blocks, 0)
    const = lambda i: (0, 0)
    if transposed:
        q_spec, q_shape = pl.BlockSpec((ATTN_WIDTH, row_tile), col), (ATTN_WIDTH, rows)
        v_spec, v_shape = pl.BlockSpec((KV_WIDTH, row_tile), col), (KV_WIDTH, rows)
    else:
        q_spec, q_shape = pl.BlockSpec((row_tile, ATTN_WIDTH), row), (rows, ATTN_WIDTH)
        v_spec, v_shape = pl.BlockSpec((row_tile, KV_WIDTH), row), (rows, KV_WIDTH)
    return pl.pallas_call(
        functools.partial(_inproj_kernel, transposed=transposed),
        grid=(rows // row_tile,),
        in_specs=[
            pl.BlockSpec((row_tile, D_MODEL), row),
            pl.BlockSpec((1, D_MODEL), const),
            pl.BlockSpec((D_MODEL, IN_WIDTH), const),
            pl.BlockSpec((1, IN_WIDTH), const),
            pl.BlockSpec((row_tile, LANES), tab),
            pl.BlockSpec((row_tile, LANES), tab),
        ],
        out_specs=[
            q_spec,
            pl.BlockSpec((row_tile, KV_WIDTH), row),
            v_spec,
            pl.BlockSpec((row_tile, SSM_WIDTH), row),
        ],
        out_shape=[
            jax.ShapeDtypeStruct(q_shape, BF16),
            jax.ShapeDtypeStruct((rows, KV_WIDTH), BF16),
            jax.ShapeDtypeStruct(v_shape, BF16),
            jax.ShapeDtypeStruct((rows, SSM_WIDTH), F32),
        ],
        compiler_params=pltpu.CompilerParams(
            dimension_semantics=("arbitrary",), vmem_limit_bytes=VMEM_LIMIT),
        name="inproj",
    )(xf, g_mix, w_in_bf, b_in, cos_t, sin_t)


def _rope_tables(positions):
    half = HEAD_DIM // 2
    inv_freq = jnp.power(ROPE_THETA, -jnp.arange(half, dtype=F32) / half)
    ang = positions.astype(F32)[:, None] * inv_freq[None, :]
    cos = jnp.tile(jnp.cos(ang), (1, LANES // half))
    sin = jnp.tile(jnp.sin(ang), (1, LANES // half))
    sign = jnp.where((jnp.arange(LANES) % HEAD_DIM) < half, -1.0, 1.0).astype(F32)
    return cos, sin * sign[None, :]


def _attn_kernel(qt_ref, kp_ref, kc_ref, km_ref, vtp_ref, vtc_ref, vtm_ref, sink_ref, g_ref, o_ref,
                 yt_scr):
    n = pl.program_id(1)
    n_keys = 2 * ATTN_BLOCK + N_META
    key = lax.broadcasted_iota(jnp.int32, (n_keys, ATTN_BLOCK), 0)
    qi = lax.broadcasted_iota(jnp.int32, (n_keys, ATTN_BLOCK), 1)
    no_prev = jnp.where(n > 0, 0, ATTN_BLOCK)
    cur_j = key - ATTN_BLOCK
    valid = ((key >= 2 * ATTN_BLOCK) | ((cur_j >= 0) & (cur_j <= qi))
             | ((key < ATTN_BLOCK) & (key > qi + no_prev)))
    bias = jnp.where(valid, 0.0, -1e30)
    bias = jnp.concatenate([bias] * Q_PER_KV, axis=1)
    zeros = jnp.zeros((HEAD_DIM, ATTN_BLOCK), BF16)
    ssq = jnp.zeros((1, ATTN_BLOCK), F32)
    for hk in range(N_KV_HEADS):
        lt = slice((hk // 2) * LANES, (hk // 2 + 1) * LANES)
        kt = jnp.concatenate([kp_ref[:, lt], kc_ref[:, lt], km_ref[:, lt]], axis=0)
        rs = slice(hk * HEAD_DIM, (hk + 1) * HEAD_DIM)
        vt = jnp.concatenate([vtp_ref[rs, :], vtc_ref[rs, :], vtm_ref[rs, :]], axis=1)
        qs, sinks = [], []
        for g in range(Q_PER_KV):
            h = hk * Q_PER_KV + g
            qh = qt_ref[h * HEAD_DIM:(h + 1) * HEAD_DIM, :]
            qs.append(jnp.concatenate([qh, zeros] if hk % 2 == 0 else [zeros, qh], axis=0))
            sinks.append(jnp.broadcast_to(sink_ref[:, h:h + 1], (1, ATTN_BLOCK)))
        sink = jnp.concatenate(sinks, axis=1)
        s = jnp.dot(kt, jnp.concatenate(qs, axis=1), preferred_element_type=F32) + bias
        m = jnp.maximum(jnp.max(s, axis=0, keepdims=True), sink)
        p = jnp.exp(s - m)
        denom = jnp.sum(p, axis=0, keepdims=True) + jnp.exp(sink - m)
        o = jnp.dot(vt, p.astype(BF16), preferred_element_type=F32) * (1.0 / denom)
        for g in range(Q_PER_KV):
            h = hk * Q_PER_KV + g
            og = o[:, g * ATTN_BLOCK:(g + 1) * ATTN_BLOCK]
            yt_scr[h * HEAD_DIM:(h + 1) * HEAD_DIM, :] = og
            ssq = ssq + jnp.sum(og * og, axis=0, keepdims=True)
    inv = lax.rsqrt(ssq * (1.0 / ATTN_WIDTH) + NORM_EPS)
    o_ref[...] = ((yt_scr[...] * inv).T * g_ref[...]).astype(BF16)


def _attention(qt, k, vt, k_meta, vt_meta, sinks, g_attn, bsz, seq):
    nb = seq // ATTN_BLOCK
    cur = lambda b, n: (b * nb + n, 0)
    prev = lambda b, n: (b * nb + jnp.maximum(n - 1, 0), 0)
    cur_t = lambda b, n: (0, b * nb + n)
    prev_t = lambda b, n: (0, b * nb + jnp.maximum(n - 1, 0))
    const = lambda b, n: (0, 0)
    return pl.pallas_call(
        _attn_kernel,
        grid=(bsz, nb),
        in_specs=[
            pl.BlockSpec((ATTN_WIDTH, ATTN_BLOCK), cur_t),
            pl.BlockSpec((ATTN_BLOCK, KV_WIDTH), prev),
            pl.BlockSpec((ATTN_BLOCK, KV_WIDTH), cur),
            pl.BlockSpec((N_META, KV_WIDTH), const),
            pl.BlockSpec((KV_WIDTH, ATTN_BLOCK), prev_t),
            pl.BlockSpec((KV_WIDTH, ATTN_BLOCK), cur_t),
            pl.BlockSpec((KV_WIDTH, N_META), const),
            pl.BlockSpec((1, N_Q_HEADS), const),
            pl.BlockSpec((1, ATTN_WIDTH), const),
        ],
        out_specs=pl.BlockSpec((ATTN_BLOCK, ATTN_WIDTH), cur),
        out_shape=jax.ShapeDtypeStruct((bsz * seq, ATTN_WIDTH), BF16),
        scratch_shapes=[pltpu.VMEM((ATTN_WIDTH, ATTN_BLOCK), F32)],
        compiler_params=pltpu.CompilerParams(
            dimension_semantics=("arbitrary", "arbitrary"), vmem_limit_bytes=VMEM_LIMIT),
        name="attention",
    )(qt, k, k, k_meta, vt, vt, vt_meta, sinks, g_attn)


def _ssm_params(a_re, a_im, log_dt, b_re, b_im, c_re, c_im, seg_len):
    dt = jnp.exp(log_dt.astype(F32))[:, None]
    lam_re = jnp.minimum(a_re.astype(F32), -1e-4)
    lam_im = a_im.astype(F32)
    z_re, z_im = lam_re * dt, lam_im * dt
    mag = jnp.exp(z_re)
    abar_re, abar_im = mag * jnp.cos(z_im), mag * jnp.sin(z_im)
    den = lam_re * lam_re + lam_im * lam_im
    n_re, n_im = abar_re - 1.0, abar_im
    coef_re = (n_re * lam_re + n_im * lam_im) / den
    coef_im = (n_im * lam_re - n_re * lam_im) / den
    br, bi = b_re.astype(F32), b_im.astype(F32)
    bb_re = coef_re[..., None] * br - coef_im[..., None] * bi
    bb_im = coef_re[..., None] * bi + coef_im[..., None] * br

    groups_per_tile = SSM_COL // SSM_GROUP
    eye = jnp.eye(groups_per_tile, dtype=F32)

    def in_tile(bb):
        t = bb.reshape(SSM_COL_TILES, groups_per_tile, SSM_STATE, SSM_GROUP)
        t = jnp.einsum('tgpc,gh->tgchp', t, eye)
        return t.reshape(SSM_COL_TILES, SSM_COL, SSM_COL_LANES).astype(BF16)

    def out_tile(cc):
        t = cc.reshape(SSM_COL_TILES, groups_per_tile, SSM_GROUP, SSM_STATE)
        t = jnp.einsum('tgcp,gh->tgphc', t, eye)
        return t.reshape(SSM_COL_TILES, SSM_COL_LANES, SSM_COL).astype(BF16)

    def powers(exps):
        e = exps.astype(F32)[:, None, None]
        pm = jnp.exp(e * z_re[None])
        return jnp.stack([(pm * jnp.cos(e * z_im[None])).reshape(len(exps), SSM_LANES),
                          (pm * jnp.sin(e * z_im[None])).reshape(len(exps), SSM_LANES)])

    return dict(
        b_re=in_tile(bb_re), b_im=in_tile(bb_im),
        c_re=out_tile(c_re.astype(F32)), c_im=out_tile(-c_im.astype(F32)),
        a=jnp.stack([abar_re.reshape(1, SSM_LANES), abar_im.reshape(1, SSM_LANES)]),
        a_seg=powers(jnp.array([seg_len])),
        a_pow=powers(jnp.arange(1, seg_len + 1)),
        a_meta=powers(jnp.arange(N_META - 1, -1, -1)),
    )


def _gelu_tanh(t):
    return 0.5 * t * (1.0 + jnp.tanh(math.sqrt(2.0 / math.pi) * (t + 0.044715 * (t * t * t))))


def _ssm_kernel(u_ref, um_ref, bre_ref, bim_ref, cre_ref, cim_ref, a_ref, aseg_ref, apow_ref,
                ameta_ref, d_ref, wglu_ref, bglu_ref, g_ref, o_ref,
                xre, xim, hre, him, car_re, car_im, cin_re, cin_im, u_scr, y_scr):
    chunk = u_ref.shape[0]
    seg = chunk // SUBLANES

    @pl.when(pl.program_id(1) == 0)
    def _():
        um = um_ref[...].astype(BF16)
        for ct in range(SSM_COL_TILES):
            ub = um[:, ct * SSM_COL:(ct + 1) * SSM_COL]
            ls = slice(ct * SSM_COL_LANES, (ct + 1) * SSM_COL_LANES)
            xr = jnp.dot(ub, bre_ref[ct], preferred_element_type=F32)
            xi = jnp.dot(ub, bim_ref[ct], preferred_element_type=F32)
            pr = ameta_ref[0, :, ls]
            pi = ameta_ref[1, :, ls]
            car_re[:, ls] = jnp.sum(pr * xr - pi * xi, axis=0, keepdims=True)
            car_im[:, ls] = jnp.sum(pr * xi + pi * xr, axis=0, keepdims=True)

    n_lt = SSM_WIDTH // LANES
    for j in range(n_lt):
        u_scr[j] = u_ref[:, j * LANES:(j + 1) * LANES]
    up = jnp.concatenate(
        [jnp.concatenate([u_scr[j, pl.ds(k, SUBLANES, stride=seg), :] for j in range(n_lt)], axis=1)
         for k in range(seg)], axis=0).astype(BF16)
    for ct in range(SSM_COL_TILES):
        ub = up[:, ct * SSM_COL:(ct + 1) * SSM_COL]
        ls = slice(ct * SSM_COL_LANES, (ct + 1) * SSM_COL_LANES)
        xre[:, ls] = jnp.dot(ub, bre_ref[ct], preferred_element_type=F32)
        xim[:, ls] = jnp.dot(ub, bim_ref[ct], preferred_element_type=F32)

    for lb in range(SSM_LANES // SCAN_LANES):
        ls = slice(lb * SCAN_LANES, (lb + 1) * SCAN_LANES)
        ar = jnp.broadcast_to(a_ref[0, :, ls], (SUBLANES, SCAN_LANES))
        ai = jnp.broadcast_to(a_ref[1, :, ls], (SUBLANES, SCAN_LANES))

        def scan_body(k, carry):
            hr, hi = carry
            rows = pl.ds(pl.multiple_of(k * SUBLANES, SUBLANES), SUBLANES)
            nr = ar * hr - ai * hi + xre[rows, ls]
            ni = ar * hi + ai * hr + xim[rows, ls]
            xre[rows, ls] = nr
            xim[rows, ls] = ni
            return nr, ni

        zero = jnp.zeros((SUBLANES, SCAN_LANES), F32)
        lax.fori_loop(0, seg, scan_body, (zero, zero))

    cr = car_re[...]
    ci = car_im[...]
    sr = aseg_ref[0]
    si = aseg_ref[1]
    for r in range(SUBLANES):
        cin_re[r:r + 1, :] = cr
        cin_im[r:r + 1, :] = ci
        er = xre[chunk - SUBLANES + r:chunk - SUBLANES + r + 1, :]
        ei = xim[chunk - SUBLANES + r:chunk - SUBLANES + r + 1, :]
        cr, ci = sr * cr - si * ci + er, sr * ci + si * cr + ei
    car_re[...] = cr
    car_im[...] = ci

    pair = 2 * SUBLANES
    for lb in range(SSM_LANES // SCAN_LANES):
        ls = slice(lb * SCAN_LANES, (lb + 1) * SCAN_LANES)
        er = jnp.concatenate([cin_re[:, ls], cin_re[:, ls]], axis=0)
        ei = jnp.concatenate([cin_im[:, ls], cin_im[:, ls]], axis=0)

        def fix_body(k2, _):
            rows = pl.ds(pl.multiple_of(k2 * pair, pair), pair)
            k = 2 * k2
            pr = jnp.concatenate(
                [jnp.broadcast_to(apow_ref[0, pl.ds(k, 1), ls], (SUBLANES, SCAN_LANES)),
                 jnp.broadcast_to(apow_ref[0, pl.ds(k + 1, 1), ls], (SUBLANES, SCAN_LANES))], axis=0)
            pi = jnp.concatenate(
                [jnp.broadcast_to(apow_ref[1, pl.ds(k, 1), ls], (SUBLANES, SCAN_LANES)),
                 jnp.broadcast_to(apow_ref[1, pl.ds(k + 1, 1), ls], (SUBLANES, SCAN_LANES))], axis=0)
            hre[rows, ls] = (xre[rows, ls] + pr * er - pi * ei).astype(BF16)
            him[rows, ls] = (xim[rows, ls] + pr * ei + pi * er).astype(BF16)
            return 0

        lax.fori_loop(0, seg // 2, fix_body, 0)

    for ct in range(SSM_COL_TILES):
        ls = slice(ct * SSM_COL_LANES, (ct + 1) * SSM_COL_LANES)
        y = (jnp.dot(hre[:, ls], cre_ref[ct], preferred_element_type=F32)
             + jnp.dot(him[:, ls], cim_ref[ct], preferred_element_type=F32))
        for k in range(seg):
            for jj in range(SSM_COL // LANES):
                y_scr[ct * (SSM_COL // LANES) + jj, pl.ds(k, SUBLANES, stride=seg), :] = (
                    y[k * SUBLANES:(k + 1) * SUBLANES, jj * LANES:(jj + 1) * LANES])

    y = jnp.concatenate([y_scr[j] for j in range(n_lt)], axis=1)
    y = _gelu_tanh(y + d_ref[...] * u_ref[...])
    gate = jnp.dot(y.astype(BF16), wglu_ref[...], preferred_element_type=F32) + bglu_ref[...]
    y = y * _sigmoid(gate)
    o_ref[...] = _rms(y, g_ref[...]).astype(BF16)


def _ssm(u, u_meta, prm, d_skip, w_glu_bf, b_glu, g_ssm, bsz, seq, chunk):
    nc = seq // chunk
    seg = chunk // SUBLANES
    row = lambda b, c: (b * nc + c, 0)
    c2 = lambda b, c: (0, 0)
    c3 = lambda b, c: (0, 0, 0)
    return pl.pallas_call(
        _ssm_kernel,
        grid=(bsz, nc),
        in_specs=[
            pl.BlockSpec((chunk, SSM_WIDTH), row),
            pl.BlockSpec((N_META, SSM_WIDTH), c2),
            pl.BlockSpec((SSM_COL_TILES, SSM_COL, SSM_COL_LANES), c3),
            pl.BlockSpec((SSM_COL_TILES, SSM_COL, SSM_COL_LANES), c3),
            pl.BlockSpec((SSM_COL_TILES, SSM_COL_LANES, SSM_COL), c3),
            pl.BlockSpec((SSM_COL_TILES, SSM_COL_LANES, SSM_COL), c3),
            pl.BlockSpec((2, 1, SSM_LANES), c3),
            pl.BlockSpec((2, 1, SSM_LANES), c3),
            pl.BlockSpec((2, seg, SSM_LANES), c3),
            pl.BlockSpec((2, N_META, SSM_LANES), c3),
            pl.BlockSpec((1, SSM_WIDTH), c2),
            pl.BlockSpec((SSM_WIDTH, SSM_WIDTH), c2),
            pl.BlockSpec((1, SSM_WIDTH), c2),
            pl.BlockSpec((1, SSM_WIDTH), c2),
        ],
        out_specs=pl.BlockSpec((chunk, SSM_WIDTH), row),
        out_shape=jax.ShapeDtypeStruct((bsz * seq, SSM_WIDTH), BF16),
        scratch_shapes=[
            pltpu.VMEM((chunk, SSM_LANES), F32),
            pltpu.VMEM((chunk, SSM_LANES), F32),
            pltpu.VMEM((chunk, SSM_LANES), BF16),
            pltpu.VMEM((chunk, SSM_LANES), BF16),
            pltpu.VMEM((1, SSM_LANES), F32),
            pltpu.VMEM((1, SSM_LANES), F32),
            pltpu.VMEM((SUBLANES, SSM_LANES), F32),
            pltpu.VMEM((SUBLANES, SSM_LANES), F32),
            pltpu.VMEM((SSM_WIDTH // LANES, chunk, LANES), F32),
            pltpu.VMEM((SSM_WIDTH // LANES, chunk, LANES), F32),
        ],
        compiler_params=pltpu.CompilerParams(
            dimension_semantics=("arbitrary", "arbitrary"), vmem_limit_bytes=VMEM_LIMIT),
        name="ssm",
    )(u, u_meta, prm["b_re"], prm["b_im"], prm["c_re"], prm["c_im"], prm["a"], prm["a_seg"],
      prm["a_pow"], prm["a_meta"], d_skip, w_glu_bf, b_glu, g_ssm)


def _outproj_kernel(ya_ref, ys_ref, x_ref, wo_ref, bo_ref, gf_ref, wr_ref, br_ref,
                    h_ref, xn_ref, lg_ref):
    mix = (jnp.dot(ya_ref[...], wo_ref[:ATTN_WIDTH, :], preferred_element_type=F32)
           + jnp.dot(ys_ref[...], wo_ref[ATTN_WIDTH:, :], preferred_element_type=F32))
    h = x_ref[...] + mix + bo_ref[...]
    h_ref[...] = h
    n = _rms(h, gf_ref[...])
    n_hi = n.astype(BF16)
    bits = lax.bitcast_convert_type(n_hi.astype(F32), jnp.uint32)
    xn_ref[...] = (bits[:, :D_MODEL // 2] >> 16) | (bits[:, D_MODEL // 2:] & jnp.uint32(0xFFFF0000))
    n_lo = (n - n_hi.astype(F32)).astype(BF16)
    hi = jnp.dot(n_hi, wr_ref[...], preferred_element_type=F32)
    lo = jnp.dot(n_lo, wr_ref[:, :ROUTER_LANES], preferred_element_type=F32)
    lg_ref[...] = hi[:, :ROUTER_LANES] + hi[:, ROUTER_LANES:] + lo + br_ref[...]


def _outproj(y_attn, y_ssm, xf, w_out_bf, b_out, g_ffn, w_router_pad, b_router_pad, row_tile):
    rows = xf.shape[0]
    row = lambda i: (i, 0)
    const = lambda i: (0, 0)
    return pl.pallas_call(
        _outproj_kernel,
        grid=(rows // row_tile,),
        in_specs=[
            pl.BlockSpec((row_tile, ATTN_WIDTH), row),
            pl.BlockSpec((row_tile, SSM_WIDTH), row),
            pl.BlockSpec((row_tile, D_MODEL), row),
            pl.BlockSpec((D_MODEL, D_MODEL), const),
            pl.BlockSpec((1, D_MODEL), const),
            pl.BlockSpec((1, D_MODEL), const),
            pl.BlockSpec((D_MODEL, 2 * ROUTER_LANES), const),
            pl.BlockSpec((1, ROUTER_LANES), const),
        ],
        out_specs=[
            pl.BlockSpec((row_tile, D_MODEL), row),
            pl.BlockSpec((row_tile, D_MODEL // 2), row),
            pl.BlockSpec((row_tile, ROUTER_LANES), row),
        ],
        out_shape=[
            jax.ShapeDtypeStruct((rows, D_MODEL), F32),
            jax.ShapeDtypeStruct((rows, D_MODEL // 2), jnp.uint32),
            jax.ShapeDtypeStruct((rows, ROUTER_LANES), F32),
        ],
        compiler_params=pltpu.CompilerParams(
            dimension_semantics=("arbitrary",), vmem_limit_bytes=VMEM_LIMIT),
        name="outproj",
    )(y_attn, y_ssm, xf, w_out_bf, b_out, g_ffn, w_router_pad, b_router_pad)


def _moe_kernel(ie_ref, ir_ref, in_ref, ni_ref,
                xs_hbm, w1a_ref, w1b_ref, b1g_ref, b1u_ref, w2_ref, b2_ref, perm_ref, ys_hbm,
                stage, x_scr, acc, w1g, w1u, w2b, sem_in, sem_out):
    i = pl.program_id(0)
    f = pl.program_id(1)
    nf = pl.num_programs(1)
    tf = w2_ref.shape[1]
    n_stage = stage.shape[0]

    @pl.when(i < ni_ref[0])
    def _():
        r0 = pl.multiple_of(ir_ref[i], MOE_PAD)
        n = in_ref[i]

        n_big = n // MOE_BLOCK
        tail_r = pl.multiple_of(n_big * MOE_BLOCK, MOE_PAD)
        has_tail = n % MOE_BLOCK != 0

        def in_copy(c):
            slot = c % n_stage
            return pltpu.make_async_copy(
                xs_hbm.at[pl.ds(pl.multiple_of(r0 + c * MOE_PAD, MOE_PAD), MOE_PAD), :],
                stage.at[slot], sem_in.at[slot])

        def unpack(c):
            u = stage[c % n_stage]
            rows_c = pl.ds(pl.multiple_of(c * MOE_PAD, MOE_PAD), MOE_PAD)
            x_scr[rows_c, :D_MODEL // 2] = lax.bitcast_convert_type(u << 16, F32).astype(BF16)
            x_scr[rows_c, D_MODEL // 2:] = lax.bitcast_convert_type(
                u & jnp.uint32(0xFFFF0000), F32).astype(BF16)

        def out_copy(r, rows):
            return pltpu.make_async_copy(
                acc.at[pl.ds(r, rows), :],
                ys_hbm.at[pl.ds(pl.multiple_of(r0 + r, MOE_PAD), rows), :], sem_out)

        @pl.when(f == 0)
        def _():
            for c in range(n_stage):
                @pl.when(c * MOE_PAD < n)
                def _(c=c):
                    in_copy(c).start()
            bias = jnp.broadcast_to(b2_ref[0], (MOE_PAD, D_MODEL))

            def init(c, _):
                acc[pl.ds(pl.multiple_of(c * MOE_PAD, MOE_PAD), MOE_PAD), :] = bias
                return 0

            lax.fori_loop(0, n // MOE_PAD, init, 0)

        half = PERM_COLS // 2
        k_half = D_MODEL // 2
        for hh, w1_ref in enumerate((w1a_ref, w1b_ref)):
            for c in range(2 * tf // PERM_COLS):
                t = jnp.dot(w1_ref[0, :, c * PERM_COLS:(c + 1) * PERM_COLS].astype(BF16),
                            perm_ref[...], preferred_element_type=F32)
                rs = slice(hh * k_half, (hh + 1) * k_half)
                w1g[rs, c * half:(c + 1) * half] = t[:, :half].astype(BF16)
                w1u[rs, c * half:(c + 1) * half] = t[:, half:].astype(BF16)
        w2b[...] = w2_ref[0].astype(BF16)

        def block(r, rows, c0):
            @pl.when(f == 0)
            def _():
                for j in range(rows // MOE_PAD):
                    c = c0 + j
                    in_copy(c).wait()
                    unpack(c)

                    @pl.when((c + n_stage) * MOE_PAD < n)
                    def _(c=c):
                        in_copy(c + n_stage).start()

            sl = pl.ds(r, rows)
            xb = x_scr[sl, :]
            g = jnp.dot(xb, w1g[...], preferred_element_type=F32) + b1g_ref[0]
            up = jnp.dot(xb, w1u[...], preferred_element_type=F32) + b1u_ref[0]
            g = jnp.minimum(g, SWIGLU_LIMIT)
            up = jnp.clip(up, -SWIGLU_LIMIT, SWIGLU_LIMIT)
            act = g * _sigmoid(SWIGLU_ALPHA * g) * (up + 1.0)
            acc[sl, :] += jnp.dot(act.astype(BF16), w2b[...], preferred_element_type=F32)

            @pl.when(f == nf - 1)
            def _():
                out_copy(r, rows).start()

        def big(b, _):
            block(pl.multiple_of(b * MOE_BLOCK, MOE_BLOCK), MOE_BLOCK, b * (MOE_BLOCK // MOE_PAD))
            return 0

        lax.fori_loop(0, n_big, big, 0)

        @pl.when(has_tail)
        def _():
            block(tail_r, MOE_PAD, n_big * (MOE_BLOCK // MOE_PAD))

        @pl.when(f == nf - 1)
        def _():
            def drain(b, _):
                out_copy(pl.multiple_of(b * MOE_BLOCK, MOE_BLOCK), MOE_BLOCK).wait()
                return 0

            lax.fori_loop(0, n_big, drain, 0)

            @pl.when(has_tail)
            def _():
                out_copy(tail_r, MOE_PAD).wait()


def _experts(xs, item_e, item_r0, item_n, n_items, w1, b1g, b1u, w2, b2, tf):
    n_rows = xs.shape[0]
    max_items = item_e.shape[0]
    nf = D_FF // tf
    idx = jnp.arange(PERM_COLS)
    dst = jnp.where(idx % 2 == 0, idx // 2, PERM_COLS // 2 + idx // 2)
    perm = (dst[:, None] == jnp.arange(PERM_COLS)[None, :]).astype(BF16)

    def ex(i, ie, ni):
        return ie[jnp.minimum(i, ni[0] - 1)]

    def ff(i, f, ni):
        return jnp.where(i < ni[0], f, nf - 1)

    grid_spec = pltpu.PrefetchScalarGridSpec(
        num_scalar_prefetch=4,
        grid=(max_items, nf),
        in_specs=[
            pl.BlockSpec(memory_space=pl.ANY),
            pl.BlockSpec((1, D_MODEL // 2, 2 * tf), lambda i, f, ie, ir, im, ni: (ex(i, ie, ni), 0, ff(i, f, ni))),
            pl.BlockSpec((1, D_MODEL // 2, 2 * tf), lambda i, f, ie, ir, im, ni: (ex(i, ie, ni), 1, ff(i, f, ni))),
            pl.BlockSpec((1, 1, tf), lambda i, f, ie, ir, im, ni: (ex(i, ie, ni), 0, ff(i, f, ni))),
            pl.BlockSpec((1, 1, tf), lambda i, f, ie, ir, im, ni: (ex(i, ie, ni), 0, ff(i, f, ni))),
            pl.BlockSpec((1, tf, D_MODEL), lambda i, f, ie, ir, im, ni: (ex(i, ie, ni), ff(i, f, ni), 0)),
            pl.BlockSpec((1, 1, D_MODEL), lambda i, f, ie, ir, im, ni: (ex(i, ie, ni), 0, 0)),
            pl.BlockSpec((PERM_COLS, PERM_COLS), lambda i, f, ie, ir, im, ni: (0, 0)),
        ],
        out_specs=pl.BlockSpec(memory_space=pl.ANY),
        scratch_shapes=[
            pltpu.VMEM((MOE_STAGE_SLOTS, MOE_PAD, D_MODEL // 2), jnp.uint32),
            pltpu.VMEM((MOE_ITEM_ROWS, D_MODEL), BF16),
            pltpu.VMEM((MOE_ITEM_ROWS, D_MODEL), F32),
            pltpu.VMEM((D_MODEL, tf), BF16),
            pltpu.VMEM((D_MODEL, tf), BF16),
            pltpu.VMEM((tf, D_MODEL), BF16),
            pltpu.SemaphoreType.DMA((MOE_STAGE_SLOTS,)),
            pltpu.SemaphoreType.DMA(()),
        ],
    )
    return pl.pallas_call(
        _moe_kernel,
        grid_spec=grid_spec,
        out_shape=jax.ShapeDtypeStruct((n_rows, D_MODEL), F32),
        compiler_params=pltpu.CompilerParams(
            dimension_semantics=("arbitrary", "arbitrary"), vmem_limit_bytes=VMEM_LIMIT),
        name="experts",
    )(item_e, item_r0, item_n, n_items, xs, w1, w1, b1g, b1u, w2, b2, perm)


def _combine_kernel(h_ref, y0_ref, y1_ref, y2_ref, y3_ref, gate_ref, g_ref, o_ref):
    acc = h_ref[...]
    gates = gate_ref[...]
    for k, y_ref in enumerate((y0_ref, y1_ref, y2_ref, y3_ref)):
        acc = acc + gates[:, k:k + 1] * y_ref[...]
    o_ref[...] = _rms(acc, g_ref[...])


def _combine(h1, ygs, gates_pad, g_final, row_tile):
    rows = h1.shape[0]
    row = lambda i: (i, 0)
    const = lambda i: (0, 0)
    return pl.pallas_call(
        _combine_kernel,
        grid=(rows // row_tile,),
        in_specs=[pl.BlockSpec((row_tile, D_MODEL), row)] * (1 + TOP_K) + [
            pl.BlockSpec((row_tile, LANES), row),
            pl.BlockSpec((1, D_MODEL), const),
        ],
        out_specs=pl.BlockSpec((row_tile, D_MODEL), row),
        out_shape=jax.ShapeDtypeStruct((rows, D_MODEL), F32),
        compiler_params=pltpu.CompilerParams(
            dimension_semantics=("arbitrary",), vmem_limit_bytes=VMEM_LIMIT),
        name="combine",
    )(h1, *ygs, gates_pad, g_final)


def _exclusive_count(chosen, tile=512):
    rows, width = chosen.shape
    tile = min(tile, rows)
    c = chosen.reshape(rows // tile, tile, width).astype(BF16)
    tri = (jnp.arange(tile)[:, None] > jnp.arange(tile)[None, :]).astype(BF16)
    within = jnp.einsum('ts,bsn->btn', tri, c, preferred_element_type=F32)
    totals = jnp.sum(c.astype(F32), axis=1)
    offsets = jnp.cumsum(totals, axis=0) - totals
    return (within + offsets[:, None, :]).astype(jnp.int32).reshape(rows, width)


def kernel(x, meta_tokens, g_mix, w_in, b_in, attn_sinks, ssm_a_re, ssm_a_im, ssm_log_dt,
           ssm_b_re, ssm_b_im, ssm_c_re, ssm_c_im, ssm_d, w_glu, b_glu, g_attn_out, g_ssm_out,
           w_out, b_out, g_ffn, w_router, b_router, w_mlp1, b_mlp1, w_mlp2, b_mlp2, g_final):
    bsz, seq, _ = x.shape
    rows = bsz * seq
    row_tile = min(ROW_TILE, seq)
    chunk = min(SSM_CHUNK, seq)
    assert seq % ATTN_BLOCK == 0 and seq % row_tile == 0 and seq % chunk == 0
    xf = x.reshape(rows, D_MODEL)

    w_in_bf = w_in[0].astype(BF16)
    cos_r, sin_r = _rope_tables(N_META + jnp.arange(seq))
    cos_m, sin_m = _rope_tables(jnp.arange(N_META))
    qt, k, vt, u = _inproj(xf, g_mix, w_in_bf, b_in, cos_r, sin_r, row_tile, True)
    _, k_meta, v_meta, u_meta = _inproj(meta_tokens, g_mix, w_in_bf, b_in, cos_m, sin_m,
                                        N_META, False)

    y_attn = _attention(qt, k, vt, k_meta, v_meta.T, attn_sinks, g_attn_out, bsz, seq)

    prm = _ssm_params(ssm_a_re[0], ssm_a_im[0], ssm_log_dt[0], ssm_b_re[0], ssm_b_im[0],
                      ssm_c_re[0], ssm_c_im[0], chunk // SUBLANES)
    y_ssm = _ssm(u, u_meta, prm, ssm_d, w_glu[0].astype(BF16), b_glu, g_ssm_out, bsz, seq, chunk)

    w_router_pad = jnp.pad(w_router[0], ((0, 0), (0, ROUTER_LANES - N_EXPERTS)))
    b_router_pad = jnp.pad(b_router, ((0, 0), (0, ROUTER_LANES - N_EXPERTS)))
    w_router_hi = w_router_pad.astype(BF16)
    w_router_lo = (w_router_pad - w_router_hi.astype(F32)).astype(BF16)
    w_router_pad = jnp.concatenate([w_router_hi, w_router_lo], axis=1)
    h1, xn, logits = _outproj(y_attn, y_ssm, xf, w_out[0].astype(BF16), b_out, g_ffn,
                              w_router_pad, b_router_pad, row_tile)

    top_val, top_idx = lax.top_k(logits[:, :N_EXPERTS], TOP_K)
    gates = jax.nn.softmax(top_val, axis=-1)
    chosen = (top_idx[:, :, None] == jnp.arange(N_EXPERTS)[None, None, :]).any(axis=1)
    chosen = chosen.astype(jnp.int32)
    rank = _exclusive_count(chosen)
    counts = jnp.sum(chosen, axis=0)
    padded = ((counts + MOE_PAD - 1) // MOE_PAD) * MOE_PAD
    pad_ends = jnp.cumsum(padded)
    pad_starts = pad_ends - padded
    dest = pad_starts[top_idx] + jnp.take_along_axis(rank, top_idx, axis=1)
    n_assign = rows * TOP_K
    n_rows = n_assign + N_EXPERTS * MOE_PAD
    order = jnp.argsort(top_idx.reshape(-1), stable=True).astype(jnp.int32)
    starts = jnp.cumsum(counts) - counts
    r = jnp.arange(n_rows, dtype=jnp.int32)
    r_e = jnp.minimum(jnp.sum(r[:, None] >= pad_ends[None, :], axis=1), N_EXPERTS - 1)
    r_j = r - pad_starts[r_e]
    row_tok = jnp.where(r_j < counts[r_e],
                        order[jnp.minimum(starts[r_e] + r_j, n_assign - 1)] // TOP_K, 0)
    per_e = (padded + MOE_ITEM_ROWS - 1) // MOE_ITEM_ROWS
    item_ends = jnp.cumsum(per_e)
    max_items = N_EXPERTS + n_assign // MOE_ITEM_ROWS
    slot = jnp.arange(max_items, dtype=jnp.int32)
    item_e = jnp.minimum(jnp.searchsorted(item_ends, slot, side='right'), N_EXPERTS - 1)
    piece = slot - (item_ends - per_e)[item_e]
    item_r0 = (pad_starts[item_e] + piece * MOE_ITEM_ROWS).astype(jnp.int32)
    item_n = jnp.clip(padded[item_e] - piece * MOE_ITEM_ROWS, 0, MOE_ITEM_ROWS).astype(jnp.int32)
    n_items = item_ends[-1].astype(jnp.int32).reshape(1)

    tf = MOE_FF_TILE
    b1 = b_mlp1[0].reshape(N_EXPERTS, 1, D_FF, 2)
    b2 = b_mlp2[0].reshape(N_EXPERTS, 1, D_MODEL)
    xs = xn[row_tok]
    ys = _experts(xs, item_e.astype(jnp.int32), item_r0, item_n, n_items,
                  w_mlp1[0], b1[..., 0], b1[..., 1], w_mlp2[0], b2, tf)

    ygs = [ys[dest[:, k]] for k in range(TOP_K)]
    gates_pad = jnp.pad(gates, ((0, 0), (0, LANES - TOP_K)))
    out = _combine(h1, ygs, gates_pad, g_final.reshape(1, D_MODEL), row_tile)
    return out.reshape(bsz, seq, D_MODEL)
```

```python
import functools
import math

import jax
import jax.numpy as jnp
from jax import lax
from jax.experimental import pallas as pl
from jax.experimental.pallas import tpu as pltpu

F32 = jnp.float32
BF16 = jnp.bfloat16

D_MODEL = 2048
N_META = 16
HEAD_DIM = 64
ATTN_WIDTH = 1024
N_Q_HEADS = 16
N_KV_HEADS = 4
Q_PER_KV = 4
KV_WIDTH = 256
ATTN_BLOCK = 128
ROPE_THETA = 10000.0
SSM_WIDTH = 1024
SSM_GROUP = 16
N_SSM_GROUPS = 64
SSM_STATE = 64
SSM_LANES = N_SSM_GROUPS * SSM_STATE
IN_WIDTH = 2560
N_EXPERTS = 32
TOP_K = 4
D_FF = 2048
SWIGLU_LIMIT = 7.0
SWIGLU_ALPHA = 1.702
NORM_EPS = 1e-5

LANES = 128
SUBLANES = 8
VMEM_LIMIT = 56 * 1024 * 1024

ROW_TILE = 512
SSM_CHUNK = 256
SSM_COL = 256
SSM_COL_TILES = SSM_WIDTH // SSM_COL
SSM_COL_LANES = SSM_LANES // SSM_COL_TILES
SCAN_LANES = 512
MOE_PAD = 128
MOE_BLOCK = 512
MOE_ITEM_ROWS = 2176
MOE_FF_TILE = 256
MOE_DUMP_ROWS = N_EXPERTS * MOE_PAD
ROUTER_LANES = 128
PERM_COLS = 256


def _rms(t, gain):
    return t * lax.rsqrt(jnp.mean(t * t, axis=-1, keepdims=True) + NORM_EPS) * gain


def _sigmoid(t):
    return 1.0 / (1.0 + jnp.exp(-t))


def _inproj_kernel(x_ref, g_ref, w_ref, b_ref, cos_ref, sin_ref, q_ref, k_ref, v_ref, u_ref,
                   *, transposed):
    n = _rms(x_ref[...], g_ref[...]).astype(BF16)
    cos = cos_ref[...]
    sin = sin_ref[...]
    lane = lax.broadcasted_iota(jnp.int32, cos.shape, 1)
    first_half = (lane % HEAD_DIM) < (HEAD_DIM // 2)

    def proj(c0, c1):
        return jnp.dot(n, w_ref[:, c0:c1], preferred_element_type=F32) + b_ref[:, c0:c1]

    def rope(t):
        partner = jnp.where(first_half, pltpu.roll(t, LANES - HEAD_DIM // 2, 1),
                            pltpu.roll(t, HEAD_DIM // 2, 1))
        return t * cos + partner * sin

    def put(ref, j, t):
        if transposed:
            ref[j * LANES:(j + 1) * LANES, :] = t.T.astype(BF16)
        else:
            ref[:, j * LANES:(j + 1) * LANES] = t.astype(BF16)

    scale = HEAD_DIM ** -0.5
    for j in range(ATTN_WIDTH // LANES):
        put(q_ref, j, rope(proj(j * LANES, (j + 1) * LANES)) * scale)
    for j in range(KV_WIDTH // LANES):
        c0 = ATTN_WIDTH + j * LANES
        k_ref[:, j * LANES:(j + 1) * LANES] = rope(proj(c0, c0 + LANES)).astype(BF16)
    for j in range(KV_WIDTH // LANES):
        c0 = ATTN_WIDTH + KV_WIDTH + j * LANES
        put(v_ref, j, proj(c0, c0 + LANES))
    c0 = ATTN_WIDTH + 2 * KV_WIDTH
    u_ref[...] = proj(c0, c0 + SSM_WIDTH)


def _inproj(xf, g_mix, w_in_bf, b_in, cos_t, sin_t, row_tile, transposed):
    rows = xf.shape[0]
    tab_blocks = cos_t.shape[0] // row_tile
    row = lambda i: (i, 0)
    col = lambda i: (0, i)
    tab = lambda i: (i % tab_blocks, 0)
    const = lambda i: (0, 0)
    if transposed:
        q_spec, q_shape = pl.BlockSpec((ATTN_WIDTH, row_tile), col), (ATTN_WIDTH, rows)
        v_spec, v_shape = pl.BlockSpec((KV_WIDTH, row_tile), col), (KV_WIDTH, rows)
    else:
        q_spec, q_shape = pl.BlockSpec((row_tile, ATTN_WIDTH), row), (rows, ATTN_WIDTH)
        v_spec, v_shape = pl.BlockSpec((row_tile, KV_WIDTH), row), (rows, KV_WIDTH)
    return pl.pallas_call(
        functools.partial(_inproj_kernel, transposed=transposed),
        grid=(rows // row_tile,),
        in_specs=[
            pl.BlockSpec((row_tile, D_MODEL), row),
            pl.BlockSpec((1, D_MODEL), const),
            pl.BlockSpec((D_MODEL, IN_WIDTH), const),
            pl.BlockSpec((1, IN_WIDTH), const),
            pl.BlockSpec((row_tile, LANES), tab),
            pl.BlockSpec((row_tile, LANES), tab),
        ],
        out_specs=[
            q_spec,
            pl.BlockSpec((row_tile, KV_WIDTH), row),
            v_spec,
            pl.BlockSpec((row_tile, SSM_WIDTH), row),
        ],
        out_shape=[
            jax.ShapeDtypeStruct(q_shape, BF16),
            jax.ShapeDtypeStruct((rows, KV_WIDTH), BF16),
            jax.ShapeDtypeStruct(v_shape, BF16),
            jax.ShapeDtypeStruct((rows, SSM_WIDTH), F32),
        ],
        compiler_params=pltpu.CompilerParams(
            dimension_semantics=("arbitrary",), vmem_limit_bytes=VMEM_LIMIT),
        name="inproj",
    )(xf, g_mix, w_in_bf, b_in, cos_t, sin_t)


def _rope_tables(positions):
    half = HEAD_DIM // 2
    inv_freq = jnp.power(ROPE_THETA, -jnp.arange(half, dtype=F32) / half)
    ang = positions.astype(F32)[:, None] * inv_freq[None, :]
    cos = jnp.tile(jnp.cos(ang), (1, LANES // half))
    sin = jnp.tile(jnp.sin(ang), (1, LANES // half))
    sign = jnp.where((jnp.arange(LANES) % HEAD_DIM) < half, -1.0, 1.0).astype(F32)
    return cos, sin * sign[None, :]


def _attn_kernel(qt_ref, kp_ref, kc_ref, km_ref, vtp_ref, vtc_ref, vtm_ref, sink_ref, g_ref, o_ref,
                 yt_scr):
    n = pl.program_id(1)
    n_keys = 2 * ATTN_BLOCK + N_META
    key = lax.broadcasted_iota(jnp.int32, (n_keys, ATTN_BLOCK), 0)
    qi = lax.broadcasted_iota(jnp.int32, (n_keys, ATTN_BLOCK), 1)
    no_prev = jnp.where(n > 0, 0, ATTN_BLOCK)
    cur_j = key - ATTN_BLOCK
    valid = ((key >= 2 * ATTN_BLOCK) | ((cur_j >= 0) & (cur_j <= qi))
             | ((key < ATTN_BLOCK) & (key > qi + no_prev)))
    bias = jnp.where(valid, 0.0, -1e30)
    bias = jnp.concatenate([bias] * Q_PER_KV, axis=1)
    zeros = jnp.zeros((HEAD_DIM, ATTN_BLOCK), BF16)
    ssq = jnp.zeros((1, ATTN_BLOCK), F32)
    for hk in range(N_KV_HEADS):
        lt = slice((hk // 2) * LANES, (hk // 2 + 1) * LANES)
        kt = jnp.concatenate([kp_ref[:, lt], kc_ref[:, lt], km_ref[:, lt]], axis=0)
        rs = slice(hk * HEAD_DIM, (hk + 1) * HEAD_DIM)
        vt = jnp.concatenate([vtp_ref[rs, :], vtc_ref[rs, :], vtm_ref[rs, :]], axis=1)
        qs, sinks = [], []
        for g in range(Q_PER_KV):
            h = hk * Q_PER_KV + g
            qh = qt_ref[h * HEAD_DIM:(h + 1) * HEAD_DIM, :]
            qs.append(jnp.concatenate([qh, zeros] if hk % 2 == 0 else [zeros, qh], axis=0))
            sinks.append(jnp.broadcast_to(sink_ref[:, h:h + 1], (1, ATTN_BLOCK)))
        sink = jnp.concatenate(sinks, axis=1)
        s = jnp.dot(kt, jnp.concatenate(qs, axis=1), preferred_element_type=F32) + bias
        m = jnp.maximum(jnp.max(s, axis=0, keepdims=True), sink)
        p = jnp.exp(s - m)
        denom = jnp.sum(p, axis=0, keepdims=True) + jnp.exp(sink - m)
        o = jnp.dot(vt, p.astype(BF16), preferred_element_type=F32) * (1.0 / denom)
        for g in range(Q_PER_KV):
            h = hk * Q_PER_KV + g
            og = o[:, g * ATTN_BLOCK:(g + 1) * ATTN_BLOCK]
            yt_scr[h * HEAD_DIM:(h + 1) * HEAD_DIM, :] = og
            ssq = ssq + jnp.sum(og * og, axis=0, keepdims=True)
    inv = lax.rsqrt(ssq * (1.0 / ATTN_WIDTH) + NORM_EPS)
    o_ref[...] = ((yt_scr[...] * inv).T * g_ref[...]).astype(BF16)


def _attention(qt, k, vt, k_meta, vt_meta, sinks, g_attn, bsz, seq):
    nb = seq // ATTN_BLOCK
    cur = lambda b, n: (b * nb + n, 0)
    prev = lambda b, n: (b * nb + jnp.maximum(n - 1, 0), 0)
    cur_t = lambda b, n: (0, b * nb + n)
    prev_t = lambda b, n: (0, b * nb + jnp.maximum(n - 1, 0))
    const = lambda b, n: (0, 0)
    return pl.pallas_call(
        _attn_kernel,
        grid=(bsz, nb),
        in_specs=[
            pl.BlockSpec((ATTN_WIDTH, ATTN_BLOCK), cur_t),
            pl.BlockSpec((ATTN_BLOCK, KV_WIDTH), prev),
            pl.BlockSpec((ATTN_BLOCK, KV_WIDTH), cur),
            pl.BlockSpec((N_META, KV_WIDTH), const),
            pl.BlockSpec((KV_WIDTH, ATTN_BLOCK), prev_t),
            pl.BlockSpec((KV_WIDTH, ATTN_BLOCK), cur_t),
            pl.BlockSpec((KV_WIDTH, N_META), const),
            pl.BlockSpec((1, N_Q_HEADS), const),
            pl.BlockSpec((1, ATTN_WIDTH), const),
        ],
        out_specs=pl.BlockSpec((ATTN_BLOCK, ATTN_WIDTH), cur),
        out_shape=jax.ShapeDtypeStruct((bsz * seq, ATTN_WIDTH), BF16),
        scratch_shapes=[pltpu.VMEM((ATTN_WIDTH, ATTN_BLOCK), F32)],
        compiler_params=pltpu.CompilerParams(
            dimension_semantics=("arbitrary", "arbitrary"), vmem_limit_bytes=VMEM_LIMIT),
        name="attention",
    )(qt, k, k, k_meta, vt, vt, vt_meta, sinks, g_attn)


def _ssm_params(a_re, a_im, log_dt, b_re, b_im, c_re, c_im, seg_len):
    dt = jnp.exp(log_dt.astype(F32))[:, None]
    lam_re = jnp.minimum(a_re.astype(F32), -1e-4)
    lam_im = a_im.astype(F32)
    z_re, z_im = lam_re * dt, lam_im * dt
    mag = jnp.exp(z_re)
    abar_re, abar_im = mag * jnp.cos(z_im), mag * jnp.sin(z_im)
    den = lam_re * lam_re + lam_im * lam_im
    n_re, n_im = abar_re - 1.0, abar_im
    coef_re = (n_re * lam_re + n_im * lam_im) / den
    coef_im = (n_im * lam_re - n_re * lam_im) / den
    br, bi = b_re.astype(F32), b_im.astype(F32)
    bb_re = coef_re[..., None] * br - coef_im[..., None] * bi
    bb_im = coef_re[..., None] * bi + coef_im[..., None] * br

    groups_per_tile = SSM_COL // SSM_GROUP
    eye = jnp.eye(groups_per_tile, dtype=F32)

    def in_tile(bb):
        t = bb.reshape(SSM_COL_TILES, groups_per_tile, SSM_STATE, SSM_GROUP)
        t = jnp.einsum('tgpc,gh->tgchp', t, eye)
        return t.reshape(SSM_COL_TILES, SSM_COL, SSM_COL_LANES).astype(BF16)

    def out_tile(cc):
        t = cc.reshape(SSM_COL_TILES, groups_per_tile, SSM_GROUP, SSM_STATE)
        t = jnp.einsum('tgcp,gh->tgphc', t, eye)
        return t.reshape(SSM_COL_TILES, SSM_COL_LANES, SSM_COL).astype(BF16)

    def powers(exps):
        e = exps.astype(F32)[:, None, None]
        pm = jnp.exp(e * z_re[None])
        return jnp.stack([(pm * jnp.cos(e * z_im[None])).reshape(len(exps), SSM_LANES),
                          (pm * jnp.sin(e * z_im[None])).reshape(len(exps), SSM_LANES)])

    return dict(
        b_re=in_tile(bb_re), b_im=in_tile(bb_im),
        c_re=out_tile(c_re.astype(F32)), c_im=out_tile(-c_im.astype(F32)),
        a=jnp.stack([abar_re.reshape(1, SSM_LANES), abar_im.reshape(1, SSM_LANES)]),
        a_seg=powers(jnp.array([seg_len])),
        a_pow=powers(jnp.arange(1, seg_len + 1)),
        a_meta=powers(jnp.arange(N_META - 1, -1, -1)),
    )


def _gelu_tanh(t):
    return 0.5 * t * (1.0 + jnp.tanh(math.sqrt(2.0 / math.pi) * (t + 0.044715 * (t * t * t))))


def _ssm_kernel(u_ref, um_ref, bre_ref, bim_ref, cre_ref, cim_ref, a_ref, aseg_ref, apow_ref,
                ameta_ref, d_ref, wglu_ref, bglu_ref, g_ref, o_ref,
                xre, xim, hre, him, car_re, car_im, cin_re, cin_im, u_scr, y_scr):
    chunk = u_ref.shape[0]
    seg = chunk // SUBLANES

    @pl.when(pl.program_id(1) == 0)
    def _():
        um = um_ref[...].astype(BF16)
        for ct in range(SSM_COL_TILES):
            ub = um[:, ct * SSM_COL:(ct + 1) * SSM_COL]
            ls = slice(ct * SSM_COL_LANES, (ct + 1) * SSM_COL_LANES)
            xr = jnp.dot(ub, bre_ref[ct], preferred_element_type=F32)
            xi = jnp.dot(ub, bim_ref[ct], preferred_element_type=F32)
            pr = ameta_ref[0, :, ls]
            pi = ameta_ref[1, :, ls]
            car_re[:, ls] = jnp.sum(pr * xr - pi * xi, axis=0, keepdims=True)
            car_im[:, ls] = jnp.sum(pr * xi + pi * xr, axis=0, keepdims=True)

    n_lt = SSM_WIDTH // LANES
    for j in range(n_lt):
        u_scr[j] = u_ref[:, j * LANES:(j + 1) * LANES]
    up = jnp.concatenate(
        [jnp.concatenate([u_scr[j, pl.ds(k, SUBLANES, stride=seg), :] for j in range(n_lt)], axis=1)
         for k in range(seg)], axis=0).astype(BF16)
    for ct in range(SSM_COL_TILES):
        ub = up[:, ct * SSM_COL:(ct + 1) * SSM_COL]
        ls = slice(ct * SSM_COL_LANES, (ct + 1) * SSM_COL_LANES)
        xre[:, ls] = jnp.dot(ub, bre_ref[ct], preferred_element_type=F32)
        xim[:, ls] = jnp.dot(ub, bim_ref[ct], preferred_element_type=F32)

    for lb in range(SSM_LANES // SCAN_LANES):
        ls = slice(lb * SCAN_LANES, (lb + 1) * SCAN_LANES)
        ar = jnp.broadcast_to(a_ref[0, :, ls], (SUBLANES, SCAN_LANES))
        ai = jnp.broadcast_to(a_ref[1, :, ls], (SUBLANES, SCAN_LANES))

        def scan_body(k, carry):
            hr, hi = carry
            rows = pl.ds(pl.multiple_of(k * SUBLANES, SUBLANES), SUBLANES)
            nr = ar * hr - ai * hi + xre[rows, ls]
            ni = ar * hi + ai * hr + xim[rows, ls]
            xre[rows, ls] = nr
            xim[rows, ls] = ni
            return nr, ni

        zero = jnp.zeros((SUBLANES, SCAN_LANES), F32)
        lax.fori_loop(0, seg, scan_body, (zero, zero))

    cr = car_re[...]
    ci = car_im[...]
    sr = aseg_ref[0]
    si = aseg_ref[1]
    for r in range(SUBLANES):
        cin_re[r:r + 1, :] = cr
        cin_im[r:r + 1, :] = ci
        er = xre[chunk - SUBLANES + r:chunk - SUBLANES + r + 1, :]
        ei = xim[chunk - SUBLANES + r:chunk - SUBLANES + r + 1, :]
        cr, ci = sr * cr - si * ci + er, sr * ci + si * cr + ei
    car_re[...] = cr
    car_im[...] = ci

    pair = 2 * SUBLANES
    for lb in range(SSM_LANES // SCAN_LANES):
        ls = slice(lb * SCAN_LANES, (lb + 1) * SCAN_LANES)
        er = jnp.concatenate([cin_re[:, ls], cin_re[:, ls]], axis=0)
        ei = jnp.concatenate([cin_im[:, ls], cin_im[:, ls]], axis=0)

        def fix_body(k2, _):
            rows = pl.ds(pl.multiple_of(k2 * pair, pair), pair)
            k = 2 * k2
            pr = jnp.concatenate(
                [jnp.broadcast_to(apow_ref[0, pl.ds(k, 1), ls], (SUBLANES, SCAN_LANES)),
                 jnp.broadcast_to(apow_ref[0, pl.ds(k + 1, 1), ls], (SUBLANES, SCAN_LANES))], axis=0)
            pi = jnp.concatenate(
                [jnp.broadcast_to(apow_ref[1, pl.ds(k, 1), ls], (SUBLANES, SCAN_LANES)),
                 jnp.broadcast_to(apow_ref[1, pl.ds(k + 1, 1), ls], (SUBLANES, SCAN_LANES))], axis=0)
            hre[rows, ls] = (xre[rows, ls] + pr * er - pi * ei).astype(BF16)
            him[rows, ls] = (xim[rows, ls] + pr * ei + pi * er).astype(BF16)
            return 0

        lax.fori_loop(0, seg // 2, fix_body, 0)

    for ct in range(SSM_COL_TILES):
        ls = slice(ct * SSM_COL_LANES, (ct + 1) * SSM_COL_LANES)
        y = (jnp.dot(hre[:, ls], cre_ref[ct], preferred_element_type=F32)
             + jnp.dot(him[:, ls], cim_ref[ct], preferred_element_type=F32))
        for k in range(seg):
            for jj in range(SSM_COL // LANES):
                y_scr[ct * (SSM_COL // LANES) + jj, pl.ds(k, SUBLANES, stride=seg), :] = (
                    y[k * SUBLANES:(k + 1) * SUBLANES, jj * LANES:(jj + 1) * LANES])

    y = jnp.concatenate([y_scr[j] for j in range(n_lt)], axis=1)
    y = _gelu_tanh(y + d_ref[...] * u_ref[...])
    gate = jnp.dot(y.astype(BF16), wglu_ref[...], preferred_element_type=F32) + bglu_ref[...]
    y = y * _sigmoid(gate)
    o_ref[...] = _rms(y, g_ref[...]).astype(BF16)


def _ssm(u, u_meta, prm, d_skip, w_glu_bf, b_glu, g_ssm, bsz, seq, chunk):
    nc = seq // chunk
    seg = chunk // SUBLANES
    row = lambda b, c: (b * nc + c, 0)
    c2 = lambda b, c: (0, 0)
    c3 = lambda b, c: (0, 0, 0)
    return pl.pallas_call(
        _ssm_kernel,
        grid=(bsz, nc),
        in_specs=[
            pl.BlockSpec((chunk, SSM_WIDTH), row),
            pl.BlockSpec((N_META, SSM_WIDTH), c2),
            pl.BlockSpec((SSM_COL_TILES, SSM_COL, SSM_COL_LANES), c3),
            pl.BlockSpec((SSM_COL_TILES, SSM_COL, SSM_COL_LANES), c3),
            pl.BlockSpec((SSM_COL_TILES, SSM_COL_LANES, SSM_COL), c3),
            pl.BlockSpec((SSM_COL_TILES, SSM_COL_LANES, SSM_COL), c3),
            pl.BlockSpec((2, 1, SSM_LANES), c3),
            pl.BlockSpec((2, 1, SSM_LANES), c3),
            pl.BlockSpec((2, seg, SSM_LANES), c3),
            pl.BlockSpec((2, N_META, SSM_LANES), c3),
            pl.BlockSpec((1, SSM_WIDTH), c2),
            pl.BlockSpec((SSM_WIDTH, SSM_WIDTH), c2),
            pl.BlockSpec((1, SSM_WIDTH), c2),
            pl.BlockSpec((1, SSM_WIDTH), c2),
        ],
        out_specs=pl.BlockSpec((chunk, SSM_WIDTH), row),
        out_shape=jax.ShapeDtypeStruct((bsz * seq, SSM_WIDTH), BF16),
        scratch_shapes=[
            pltpu.VMEM((chunk, SSM_LANES), F32),
            pltpu.VMEM((chunk, SSM_LANES), F32),
            pltpu.VMEM((chunk, SSM_LANES), BF16),
            pltpu.VMEM((chunk, SSM_LANES), BF16),
            pltpu.VMEM((1, SSM_LANES), F32),
            pltpu.VMEM((1, SSM_LANES), F32),
            pltpu.VMEM((SUBLANES, SSM_LANES), F32),
            pltpu.VMEM((SUBLANES, SSM_LANES), F32),
            pltpu.VMEM((SSM_WIDTH // LANES, chunk, LANES), F32),
            pltpu.VMEM((SSM_WIDTH // LANES, chunk, LANES), F32),
        ],
        compiler_params=pltpu.CompilerParams(
            dimension_semantics=("arbitrary", "arbitrary"), vmem_limit_bytes=VMEM_LIMIT),
        name="ssm",
    )(u, u_meta, prm["b_re"], prm["b_im"], prm["c_re"], prm["c_im"], prm["a"], prm["a_seg"],
      prm["a_pow"], prm["a_meta"], d_skip, w_glu_bf, b_glu, g_ssm)


def _outproj_kernel(ya_ref, ys_ref, x_ref, wo_ref, bo_ref, gf_ref, wr_ref, br_ref,
                    h_ref, xn_ref, lg_ref):
    mix = (jnp.dot(ya_ref[...], wo_ref[:ATTN_WIDTH, :], preferred_element_type=F32)
           + jnp.dot(ys_ref[...], wo_ref[ATTN_WIDTH:, :], preferred_element_type=F32))
    h = x_ref[...] + mix + bo_ref[...]
    h_ref[...] = h
    n = _rms(h, gf_ref[...])
    n_hi = n.astype(BF16)
    bits = lax.bitcast_convert_type(n_hi.astype(F32), jnp.uint32)
    xn_ref[...] = (bits[:, :D_MODEL // 2] >> 16) | (bits[:, D_MODEL // 2:] & jnp.uint32(0xFFFF0000))
    n_lo = (n - n_hi.astype(F32)).astype(BF16)
    hi = jnp.dot(n_hi, wr_ref[...], preferred_element_type=F32)
    lo = jnp.dot(n_lo, wr_ref[:, :ROUTER_LANES], preferred_element_type=F32)
    lg_ref[...] = hi[:, :ROUTER_LANES] + hi[:, ROUTER_LANES:] + lo + br_ref[...]


def _outproj(y_attn, y_ssm, xf, w_out_bf, b_out, g_ffn, w_router_pad, b_router_pad, row_tile):
    rows = xf.shape[0]
    row = lambda i: (i, 0)
    const = lambda i: (0, 0)
    return pl.pallas_call(
        _outproj_kernel,
        grid=(rows // row_tile,),
        in_specs=[
            pl.BlockSpec((row_tile, ATTN_WIDTH), row),
            pl.BlockSpec((row_tile, SSM_WIDTH), row),
            pl.BlockSpec((row_tile, D_MODEL), row),
            pl.BlockSpec((D_MODEL, D_MODEL), const),
            pl.BlockSpec((1, D_MODEL), const),
            pl.BlockSpec((1, D_MODEL), const),
            pl.BlockSpec((D_MODEL, 2 * ROUTER_LANES), const),
            pl.BlockSpec((1, ROUTER_LANES), const),
        ],
        out_specs=[
            pl.BlockSpec((row_tile, D_MODEL), row),
            pl.BlockSpec((row_tile, D_MODEL // 2), row),
            pl.BlockSpec((row_tile, ROUTER_LANES), row),
        ],
        out_shape=[
            jax.ShapeDtypeStruct((rows, D_MODEL), F32),
            jax.ShapeDtypeStruct((rows, D_MODEL // 2), jnp.uint32),
            jax.ShapeDtypeStruct((rows, ROUTER_LANES), F32),
        ],
        compiler_params=pltpu.CompilerParams(
            dimension_semantics=("arbitrary",), vmem_limit_bytes=VMEM_LIMIT),
        name="outproj",
    )(y_attn, y_ssm, xf, w_out_bf, b_out, g_ffn, w_router_pad, b_router_pad)


def _moe_kernel(ie_ref, ir_ref, in_ref, ni_ref, rt_ref, rd_ref,
                xn_hbm, w1a_ref, w1b_ref, b1g_ref, b1u_ref, w2_ref, b2_ref, perm_ref, yk_hbm,
                stage, x_scr, acc, w1g, w1u, w2b, issued, sem_g, sem_out):
    i = pl.program_id(0)
    f = pl.program_id(1)
    nf = pl.num_programs(1)
    max_items = pl.num_programs(0)
    tf = w2_ref.shape[1]

    @pl.when(i < ni_ref[0])
    def _():
        r0 = ir_ref[i]
        n = in_ref[i]
        nxt = jnp.minimum(i + 1, max_items - 1)
        n_next = jnp.where(i + 1 < ni_ref[0], in_ref[nxt], 0)
        r0_next = ir_ref[nxt]

        def gather_row(first_row, j):
            tok = rt_ref[first_row + j]
            pltpu.make_async_copy(xn_hbm.at[pl.ds(tok, 1), :], stage.at[pl.ds(j, 1), :],
                                  sem_g).start()

        def gather_wait_chunk():
            pltpu.make_async_copy(xn_hbm.at[pl.ds(0, MOE_PAD), :],
                                  stage.at[pl.ds(0, MOE_PAD), :], sem_g).wait()

        def scatter_wait_chunk():
            pltpu.make_async_copy(acc.at[pl.ds(0, MOE_PAD), :],
                                  yk_hbm.at[pl.ds(0, MOE_PAD), :], sem_out).wait()

        @pl.when(f == 0)
        def _():
            @pl.when(i == 0)
            def _():
                def first(j, c):
                    gather_row(r0, j)
                    return c

                lax.fori_loop(0, n, first, 0)

            def landed(c, carry):
                gather_wait_chunk()
                return carry

            lax.fori_loop(0, n // MOE_PAD, landed, 0)

            def unpack(c, carry):
                rows_c = pl.ds(pl.multiple_of(c * MOE_PAD, MOE_PAD), MOE_PAD)
                u = stage[rows_c, :]
                x_scr[rows_c, :D_MODEL // 2] = lax.bitcast_convert_type(u << 16, F32).astype(BF16)
                x_scr[rows_c, D_MODEL // 2:] = lax.bitcast_convert_type(
                    u & jnp.uint32(0xFFFF0000), F32).astype(BF16)
                return carry

            lax.fori_loop(0, n // MOE_PAD, unpack, 0)
            issued[0] = 0
            bias = jnp.broadcast_to(b2_ref[0], (MOE_PAD, D_MODEL))

            def init(c, _):
                acc[pl.ds(pl.multiple_of(c * MOE_PAD, MOE_PAD), MOE_PAD), :] = bias
                return 0

            lax.fori_loop(0, n // MOE_PAD, init, 0)

        half = PERM_COLS // 2
        k_half = D_MODEL // 2
        for hh, w1_ref in enumerate((w1a_ref, w1b_ref)):
            for c in range(2 * tf // PERM_COLS):
                t = jnp.dot(w1_ref[0, :, c * PERM_COLS:(c + 1) * PERM_COLS].astype(BF16),
                            perm_ref[...], preferred_element_type=F32)
                rs = slice(hh * k_half, (hh + 1) * k_half)
                w1g[rs, c * half:(c + 1) * half] = t[:, :half].astype(BF16)
                w1u[rs, c * half:(c + 1) * half] = t[:, half:].astype(BF16)
        w2b[...] = w2_ref[0].astype(BF16)

        def prefetch(q):
            base = issued[0]

            @pl.when(base + q <= n_next)
            def _():
                for t in range(q):
                    gather_row(r0_next, base + t)
                issued[0] = base + q

        def scatter(r, rows):
            for t in range(rows):
                dst = rd_ref[r0 + r + t]
                pltpu.make_async_copy(acc.at[pl.ds(r + t, 1), :], yk_hbm.at[pl.ds(dst, 1), :],
                                      sem_out).start()

        def block(r, rows):
            prefetch(rows // 4)
            sl = pl.ds(r, rows)
            xb = x_scr[sl, :]
            g = jnp.dot(xb, w1g[...], preferred_element_type=F32) + b1g_ref[0]
            up = jnp.dot(xb, w1u[...], preferred_element_type=F32) + b1u_ref[0]
            g = jnp.minimum(g, SWIGLU_LIMIT)
            up = jnp.clip(up, -SWIGLU_LIMIT, SWIGLU_LIMIT)
            act = g * _sigmoid(SWIGLU_ALPHA * g) * (up + 1.0)
            acc[sl, :] += jnp.dot(act.astype(BF16), w2b[...], preferred_element_type=F32)

            @pl.when(f == nf - 1)
            def _():
                scatter(r, rows)

        def big(b, _):
            block(pl.multiple_of(b * MOE_BLOCK, MOE_BLOCK), MOE_BLOCK)
            return 0

        n_big = n // MOE_BLOCK
        rem = n % MOE_BLOCK
        lax.fori_loop(0, n_big, big, 0)

        @pl.when(rem >= 2 * MOE_PAD)
        def _():
            block(pl.multiple_of(n_big * MOE_BLOCK, MOE_BLOCK), 2 * MOE_PAD)

        @pl.when(rem % (2 * MOE_PAD) != 0)
        def _():
            block(pl.multiple_of(n - MOE_PAD, MOE_PAD), MOE_PAD)

        @pl.when(f == nf - 1)
        def _():
            def left(c, carry):
                scatter_wait_chunk()
                return carry

            lax.fori_loop(0, n // MOE_PAD, left, 0)

            def rest(j, c):
                gather_row(r0_next, j)
                return c

            lax.fori_loop(issued[0], n_next, rest, 0)


def _experts(xn, item_e, item_r0, item_n, n_items, row_tok, row_dst, n_out_rows,
             w1, b1g, b1u, w2, b2, tf):
    max_items = item_e.shape[0]
    nf = D_FF // tf
    idx = jnp.arange(PERM_COLS)
    dst = jnp.where(idx % 2 == 0, idx // 2, PERM_COLS // 2 + idx // 2)
    perm = (dst[:, None] == jnp.arange(PERM_COLS)[None, :]).astype(BF16)

    def ex(i, ie, ni):
        return ie[jnp.minimum(i, ni[0] - 1)]

    def ff(i, f, ni):
        return jnp.where(i < ni[0], f, nf - 1)

    grid_spec = pltpu.PrefetchScalarGridSpec(
        num_scalar_prefetch=6,
        grid=(max_items, nf),
        in_specs=[
            pl.BlockSpec(memory_space=pl.ANY),
            pl.BlockSpec((1, D_MODEL // 2, 2 * tf), lambda i, f, ie, ir, im, ni, rt, rd: (ex(i, ie, ni), 0, ff(i, f, ni))),
            pl.BlockSpec((1, D_MODEL // 2, 2 * tf), lambda i, f, ie, ir, im, ni, rt, rd: (ex(i, ie, ni), 1, ff(i, f, ni))),
            pl.BlockSpec((1, 1, tf), lambda i, f, ie, ir, im, ni, rt, rd: (ex(i, ie, ni), 0, ff(i, f, ni))),
            pl.BlockSpec((1, 1, tf), lambda i, f, ie, ir, im, ni, rt, rd: (ex(i, ie, ni), 0, ff(i, f, ni))),
            pl.BlockSpec((1, tf, D_MODEL), lambda i, f, ie, ir, im, ni, rt, rd: (ex(i, ie, ni), ff(i, f, ni), 0)),
            pl.BlockSpec((1, 1, D_MODEL), lambda i, f, ie, ir, im, ni, rt, rd: (ex(i, ie, ni), 0, 0)),
            pl.BlockSpec((PERM_COLS, PERM_COLS), lambda i, f, ie, ir, im, ni, rt, rd: (0, 0)),
        ],
        out_specs=pl.BlockSpec(memory_space=pl.ANY),
        scratch_shapes=[
            pltpu.VMEM((MOE_ITEM_ROWS, D_MODEL // 2), jnp.uint32),
            pltpu.VMEM((MOE_ITEM_ROWS, D_MODEL), BF16),
            pltpu.VMEM((MOE_ITEM_ROWS, D_MODEL), F32),
            pltpu.VMEM((D_MODEL, tf), BF16),
            pltpu.VMEM((D_MODEL, tf), BF16),
            pltpu.VMEM((tf, D_MODEL), BF16),
            pltpu.SMEM((1,), jnp.int32),
            pltpu.SemaphoreType.DMA(()),
            pltpu.SemaphoreType.DMA(()),
        ],
    )
    return pl.pallas_call(
        _moe_kernel,
        grid_spec=grid_spec,
        out_shape=jax.ShapeDtypeStruct((n_out_rows, D_MODEL), F32),
        compiler_params=pltpu.CompilerParams(
            dimension_semantics=("arbitrary", "arbitrary"), vmem_limit_bytes=VMEM_LIMIT),
        name="experts",
    )(item_e, item_r0, item_n, n_items, row_tok, row_dst, xn, w1, w1, b1g, b1u, w2, b2, perm)


def _combine_kernel(h_ref, y0_ref, y1_ref, y2_ref, y3_ref, gate_ref, g_ref, o_ref):
    acc = h_ref[...]
    gates = gate_ref[...]
    for k, y_ref in enumerate((y0_ref, y1_ref, y2_ref, y3_ref)):
        acc = acc + gates[:, k:k + 1] * y_ref[...]
    o_ref[...] = _rms(acc, g_ref[...])


def _combine(h1, yk, gates_pad, g_final, row_tile):
    rows = h1.shape[0]
    n_tiles = rows // row_tile
    row = lambda i: (i, 0)
    const = lambda i: (0, 0)
    slot = lambda k: (lambda i: (k * n_tiles + i, 0))
    return pl.pallas_call(
        _combine_kernel,
        grid=(n_tiles,),
        in_specs=[pl.BlockSpec((row_tile, D_MODEL), row)]
        + [pl.BlockSpec((row_tile, D_MODEL), slot(k)) for k in range(TOP_K)]
        + [pl.BlockSpec((row_tile, LANES), row), pl.BlockSpec((1, D_MODEL), const)],
        out_specs=pl.BlockSpec((row_tile, D_MODEL), row),
        out_shape=jax.ShapeDtypeStruct((rows, D_MODEL), F32),
        compiler_params=pltpu.CompilerParams(
            dimension_semantics=("arbitrary",), vmem_limit_bytes=VMEM_LIMIT),
        name="combine",
    )(h1, yk, yk, yk, yk, gates_pad, g_final)


def kernel(x, meta_tokens, g_mix, w_in, b_in, attn_sinks, ssm_a_re, ssm_a_im, ssm_log_dt,
           ssm_b_re, ssm_b_im, ssm_c_re, ssm_c_im, ssm_d, w_glu, b_glu, g_attn_out, g_ssm_out,
           w_out, b_out, g_ffn, w_router, b_router, w_mlp1, b_mlp1, w_mlp2, b_mlp2, g_final):
    bsz, seq, _ = x.shape
    rows = bsz * seq
    row_tile = min(ROW_TILE, seq)
    chunk = min(SSM_CHUNK, seq)
    assert seq % ATTN_BLOCK == 0 and seq % row_tile == 0 and seq % chunk == 0
    xf = x.reshape(rows, D_MODEL)

    w_in_bf = w_in[0].astype(BF16)
    cos_r, sin_r = _rope_tables(N_META + jnp.arange(seq))
    cos_m, sin_m = _rope_tables(jnp.arange(N_META))
    qt, k, vt, u = _inproj(xf, g_mix, w_in_bf, b_in, cos_r, sin_r, row_tile, True)
    _, k_meta, v_meta, u_meta = _inproj(meta_tokens, g_mix, w_in_bf, b_in, cos_m, sin_m,
                                        N_META, False)

    y_attn = _attention(qt, k, vt, k_meta, v_meta.T, attn_sinks, g_attn_out, bsz, seq)

    prm = _ssm_params(ssm_a_re[0], ssm_a_im[0], ssm_log_dt[0], ssm_b_re[0], ssm_b_im[0],
                      ssm_c_re[0], ssm_c_im[0], chunk // SUBLANES)
    y_ssm = _ssm(u, u_meta, prm, ssm_d, w_glu[0].astype(BF16), b_glu, g_ssm_out, bsz, seq, chunk)

    w_router_pad = jnp.pad(w_router[0], ((0, 0), (0, ROUTER_LANES - N_EXPERTS)))
    b_router_pad = jnp.pad(b_router, ((0, 0), (0, ROUTER_LANES - N_EXPERTS)))
    w_router_hi = w_router_pad.astype(BF16)
    w_router_lo = (w_router_pad - w_router_hi.astype(F32)).astype(BF16)
    w_router_pad = jnp.concatenate([w_router_hi, w_router_lo], axis=1)
    h1, xn, logits = _outproj(y_attn, y_ssm, xf, w_out[0].astype(BF16), b_out, g_ffn,
                              w_router_pad, b_router_pad, row_tile)

    top_val, top_idx = lax.top_k(logits[:, :N_EXPERTS], TOP_K)
    gates = jax.nn.softmax(top_val, axis=-1)
    chosen = (top_idx[:, :, None] == jnp.arange(N_EXPERTS)[None, None, :])
    counts = jnp.sum(chosen.astype(jnp.int32), axis=(0, 1))
    padded = ((counts + MOE_PAD - 1) // MOE_PAD) * MOE_PAD
    pad_ends = jnp.cumsum(padded)
    pad_starts = pad_ends - padded
    n_assign = rows * TOP_K
    n_rows = n_assign + MOE_DUMP_ROWS
    order = jnp.argsort(top_idx.reshape(-1), stable=True).astype(jnp.int32)
    starts = jnp.cumsum(counts) - counts
    r = jnp.arange(n_rows, dtype=jnp.int32)
    r_e = jnp.minimum(jnp.sum(r[:, None] >= pad_ends[None, :], axis=1), N_EXPERTS - 1)
    r_j = r - pad_starts[r_e]
    real = r_j < counts[r_e]
    assign = order[jnp.minimum(starts[r_e] + r_j, n_assign - 1)]
    row_tok = jnp.where(real, assign // TOP_K, 0).astype(jnp.int32)
    dump = n_assign + r_e * MOE_PAD + jnp.clip(r_j - counts[r_e], 0, MOE_PAD - 1)
    row_dst = jnp.where(real, (assign % TOP_K) * rows + assign // TOP_K, dump).astype(jnp.int32)
    per_e = (padded + MOE_ITEM_ROWS - 1) // MOE_ITEM_ROWS
    item_ends = jnp.cumsum(per_e)
    max_items = N_EXPERTS + n_assign // MOE_ITEM_ROWS
    slot = jnp.arange(max_items, dtype=jnp.int32)
    item_e = jnp.minimum(jnp.searchsorted(item_ends, slot, side='right'), N_EXPERTS - 1)
    piece = slot - (item_ends - per_e)[item_e]
    item_r0 = (pad_starts[item_e] + piece * MOE_ITEM_ROWS).astype(jnp.int32)
    item_n = jnp.clip(padded[item_e] - piece * MOE_ITEM_ROWS, 0, MOE_ITEM_ROWS).astype(jnp.int32)
    n_items = item_ends[-1].astype(jnp.int32).reshape(1)

    tf = MOE_FF_TILE
    b1 = b_mlp1[0].reshape(N_EXPERTS, 1, D_FF, 2)
    b2 = b_mlp2[0].reshape(N_EXPERTS, 1, D_MODEL)
    yk = _experts(xn, item_e.astype(jnp.int32), item_r0, item_n, n_items, row_tok, row_dst,
                  n_assign + MOE_DUMP_ROWS, w_mlp1[0], b1[..., 0], b1[..., 1], w_mlp2[0], b2, tf)

    gates_pad = jnp.pad(gates, ((0, 0), (0, LANES - TOP_K)))
    out = _combine(h1, yk, gates_pad, g_final.reshape(1, D_MODEL), row_tile)
    return out.reshape(bsz, seq, D_MODEL)
```

```python
import functools
import math

import jax
import jax.numpy as jnp
from jax import lax
from jax.experimental import pallas as pl
from jax.experimental.pallas import tpu as pltpu

F32 = jnp.float32
BF16 = jnp.bfloat16

D_MODEL = 2048
N_META = 16
HEAD_DIM = 64
ATTN_WIDTH = 1024
N_Q_HEADS = 16
N_KV_HEADS = 4
Q_PER_KV = 4
KV_WIDTH = 256
ATTN_BLOCK = 128
ROPE_THETA = 10000.0
SSM_WIDTH = 1024
SSM_GROUP = 16
N_SSM_GROUPS = 64
SSM_STATE = 64
SSM_LANES = N_SSM_GROUPS * SSM_STATE
IN_WIDTH = 2560
N_EXPERTS = 32
TOP_K = 4
D_FF = 2048
SWIGLU_LIMIT = 7.0
SWIGLU_ALPHA = 1.702
NORM_EPS = 1e-5

LANES = 128
SUBLANES = 8
VMEM_LIMIT = 58 * 1024 * 1024

ROW_TILE = 512
SSM_CHUNK = 256
SSM_COL = 256
SSM_COL_TILES = SSM_WIDTH // SSM_COL
SSM_COL_LANES = SSM_LANES // SSM_COL_TILES
SCAN_LANES = 512
MOE_PAD = 128
MOE_BLOCK = 512
MOE_ITEM_ROWS = 2176
MOE_FF_TILE = 256
MOE_DUMP_ROWS = N_EXPERTS * MOE_PAD
ROUTER_LANES = 128
PERM_COLS = 256


def _rms(t, gain):
    return t * lax.rsqrt(jnp.mean(t * t, axis=-1, keepdims=True) + NORM_EPS) * gain


def _sigmoid(t):
    return 1.0 / (1.0 + jnp.exp(-t))


def _inproj_kernel(x_ref, g_ref, w_ref, b_ref, cos_ref, sin_ref, q_ref, k_ref, v_ref, u_ref,
                   *, transposed):
    n = _rms(x_ref[...], g_ref[...]).astype(BF16)
    cos = cos_ref[...]
    sin = sin_ref[...]
    lane = lax.broadcasted_iota(jnp.int32, cos.shape, 1)
    first_half = (lane % HEAD_DIM) < (HEAD_DIM // 2)

    def proj(c0, c1):
        return jnp.dot(n, w_ref[:, c0:c1], preferred_element_type=F32) + b_ref[:, c0:c1]

    def rope(t):
        partner = jnp.where(first_half, pltpu.roll(t, LANES - HEAD_DIM // 2, 1),
                            pltpu.roll(t, HEAD_DIM // 2, 1))
        return t * cos + partner * sin

    def put(ref, j, t):
        if transposed:
            ref[j * LANES:(j + 1) * LANES, :] = t.T.astype(BF16)
        else:
            ref[:, j * LANES:(j + 1) * LANES] = t.astype(BF16)

    scale = HEAD_DIM ** -0.5
    for j in range(ATTN_WIDTH // LANES):
        put(q_ref, j, rope(proj(j * LANES, (j + 1) * LANES)) * scale)
    for j in range(KV_WIDTH // LANES):
        c0 = ATTN_WIDTH + j * LANES
        k_ref[:, j * LANES:(j + 1) * LANES] = rope(proj(c0, c0 + LANES)).astype(BF16)
    for j in range(KV_WIDTH // LANES):
        c0 = ATTN_WIDTH + KV_WIDTH + j * LANES
        put(v_ref, j, proj(c0, c0 + LANES))
    c0 = ATTN_WIDTH + 2 * KV_WIDTH
    u_ref[...] = proj(c0, c0 + SSM_WIDTH)


def _inproj(xf, g_mix, w_in_bf, b_in, cos_t, sin_t, row_tile, transposed):
    rows = xf.shape[0]
    tab_blocks = cos_t.shape[0] // row_tile
    row = lambda i: (i, 0)
    col = lambda i: (0, i)
    tab = lambda i: (i % tab_blocks, 0)
    const = lambda i: (0, 0)
    if transposed:
        q_spec, q_shape = pl.BlockSpec((ATTN_WIDTH, row_tile), col), (ATTN_WIDTH, rows)
        v_spec, v_shape = pl.BlockSpec((KV_WIDTH, row_tile), col), (KV_WIDTH, rows)
    else:
        q_spec, q_shape = pl.BlockSpec((row_tile, ATTN_WIDTH), row), (rows, ATTN_WIDTH)
        v_spec, v_shape = pl.BlockSpec((row_tile, KV_WIDTH), row), (rows, KV_WIDTH)
    return pl.pallas_call(
        functools.partial(_inproj_kernel, transposed=transposed),
        grid=(rows // row_tile,),
        in_specs=[
            pl.BlockSpec((row_tile, D_MODEL), row),
            pl.BlockSpec((1, D_MODEL), const),
            pl.BlockSpec((D_MODEL, IN_WIDTH), const),
            pl.BlockSpec((1, IN_WIDTH), const),
            pl.BlockSpec((row_tile, LANES), tab),
            pl.BlockSpec((row_tile, LANES), tab),
        ],
        out_specs=[
            q_spec,
            pl.BlockSpec((row_tile, KV_WIDTH), row),
            v_spec,
            pl.BlockSpec((row_tile, SSM_WIDTH), row),
        ],
        out_shape=[
            jax.ShapeDtypeStruct(q_shape, BF16),
            jax.ShapeDtypeStruct((rows, KV_WIDTH), BF16),
            jax.ShapeDtypeStruct(v_shape, BF16),
            jax.ShapeDtypeStruct((rows, SSM_WIDTH), F32),
        ],
        compiler_params=pltpu.CompilerParams(
            dimension_semantics=("arbitrary",), vmem_limit_bytes=VMEM_LIMIT),
        name="inproj",
    )(xf, g_mix, w_in_bf, b_in, cos_t, sin_t)


def _rope_tables(positions):
    half = HEAD_DIM // 2
    inv_freq = jnp.power(ROPE_THETA, -jnp.arange(half, dtype=F32) / half)
    ang = positions.astype(F32)[:, None] * inv_freq[None, :]
    cos = jnp.tile(jnp.cos(ang), (1, LANES // half))
    sin = jnp.tile(jnp.sin(ang), (1, LANES // half))
    sign = jnp.where((jnp.arange(LANES) % HEAD_DIM) < half, -1.0, 1.0).astype(F32)
    return cos, sin * sign[None, :]


def _attn_kernel(qt_ref, kp_ref, kc_ref, km_ref, vtp_ref, vtc_ref, vtm_ref, sink_ref, g_ref, o_ref,
                 yt_scr):
    n = pl.program_id(1)
    n_keys = 2 * ATTN_BLOCK + N_META
    key = lax.broadcasted_iota(jnp.int32, (n_keys, ATTN_BLOCK), 0)
    qi = lax.broadcasted_iota(jnp.int32, (n_keys, ATTN_BLOCK), 1)
    no_prev = jnp.where(n > 0, 0, ATTN_BLOCK)
    cur_j = key - ATTN_BLOCK
    valid = ((key >= 2 * ATTN_BLOCK) | ((cur_j >= 0) & (cur_j <= qi))
             | ((key < ATTN_BLOCK) & (key > qi + no_prev)))
    bias = jnp.where(valid, 0.0, -1e30)
    bias = jnp.concatenate([bias] * Q_PER_KV, axis=1)
    zeros = jnp.zeros((HEAD_DIM, ATTN_BLOCK), BF16)
    ssq = jnp.zeros((1, ATTN_BLOCK), F32)
    for hk in range(N_KV_HEADS):
        lt = slice((hk // 2) * LANES, (hk // 2 + 1) * LANES)
        kt = jnp.concatenate([kp_ref[:, lt], kc_ref[:, lt], km_ref[:, lt]], axis=0)
        rs = slice(hk * HEAD_DIM, (hk + 1) * HEAD_DIM)
        vt = jnp.concatenate([vtp_ref[rs, :], vtc_ref[rs, :], vtm_ref[rs, :]], axis=1)
        qs, sinks = [], []
        for g in range(Q_PER_KV):
            h = hk * Q_PER_KV + g
            qh = qt_ref[h * HEAD_DIM:(h + 1) * HEAD_DIM, :]
            qs.append(jnp.concatenate([qh, zeros] if hk % 2 == 0 else [zeros, qh], axis=0))
            sinks.append(jnp.broadcast_to(sink_ref[:, h:h + 1], (1, ATTN_BLOCK)))
        sink = jnp.concatenate(sinks, axis=1)
        s = jnp.dot(kt, jnp.concatenate(qs, axis=1), preferred_element_type=F32) + bias
        m = jnp.maximum(jnp.max(s, axis=0, keepdims=True), sink)
        p = jnp.exp(s - m)
        denom = jnp.sum(p, axis=0, keepdims=True) + jnp.exp(sink - m)
        o = jnp.dot(vt, p.astype(BF16), preferred_element_type=F32) * (1.0 / denom)
        for g in range(Q_PER_KV):
            h = hk * Q_PER_KV + g
            og = o[:, g * ATTN_BLOCK:(g + 1) * ATTN_BLOCK]
            yt_scr[h * HEAD_DIM:(h + 1) * HEAD_DIM, :] = og
            ssq = ssq + jnp.sum(og * og, axis=0, keepdims=True)
    inv = lax.rsqrt(ssq * (1.0 / ATTN_WIDTH) + NORM_EPS)
    o_ref[...] = ((yt_scr[...] * inv).T * g_ref[...]).astype(BF16)


def _attention(qt, k, vt, k_meta, vt_meta, sinks, g_attn, bsz, seq):
    nb = seq // ATTN_BLOCK
    cur = lambda b, n: (b * nb + n, 0)
    prev = lambda b, n: (b * nb + jnp.maximum(n - 1, 0), 0)
    cur_t = lambda b, n: (0, b * nb + n)
    prev_t = lambda b, n: (0, b * nb + jnp.maximum(n - 1, 0))
    const = lambda b, n: (0, 0)
    return pl.pallas_call(
        _attn_kernel,
        grid=(bsz, nb),
        in_specs=[
            pl.BlockSpec((ATTN_WIDTH, ATTN_BLOCK), cur_t),
            pl.BlockSpec((ATTN_BLOCK, KV_WIDTH), prev),
            pl.BlockSpec((ATTN_BLOCK, KV_WIDTH), cur),
            pl.BlockSpec((N_META, KV_WIDTH), const),
            pl.BlockSpec((KV_WIDTH, ATTN_BLOCK), prev_t),
            pl.BlockSpec((KV_WIDTH, ATTN_BLOCK), cur_t),
            pl.BlockSpec((KV_WIDTH, N_META), const),
            pl.BlockSpec((1, N_Q_HEADS), const),
            pl.BlockSpec((1, ATTN_WIDTH), const),
        ],
        out_specs=pl.BlockSpec((ATTN_BLOCK, ATTN_WIDTH), cur),
        out_shape=jax.ShapeDtypeStruct((bsz * seq, ATTN_WIDTH), BF16),
        scratch_shapes=[pltpu.VMEM((ATTN_WIDTH, ATTN_BLOCK), F32)],
        compiler_params=pltpu.CompilerParams(
            dimension_semantics=("arbitrary", "arbitrary"), vmem_limit_bytes=VMEM_LIMIT),
        name="attention",
    )(qt, k, k, k_meta, vt, vt, vt_meta, sinks, g_attn)


def _ssm_params(a_re, a_im, log_dt, b_re, b_im, c_re, c_im, seg_len):
    dt = jnp.exp(log_dt.astype(F32))[:, None]
    lam_re = jnp.minimum(a_re.astype(F32), -1e-4)
    lam_im = a_im.astype(F32)
    z_re, z_im = lam_re * dt, lam_im * dt
    mag = jnp.exp(z_re)
    abar_re, abar_im = mag * jnp.cos(z_im), mag * jnp.sin(z_im)
    den = lam_re * lam_re + lam_im * lam_im
    n_re, n_im = abar_re - 1.0, abar_im
    coef_re = (n_re * lam_re + n_im * lam_im) / den
    coef_im = (n_im * lam_re - n_re * lam_im) / den
    br, bi = b_re.astype(F32), b_im.astype(F32)
    bb_re = coef_re[..., None] * br - coef_im[..., None] * bi
    bb_im = coef_re[..., None] * bi + coef_im[..., None] * br

    groups_per_tile = SSM_COL // SSM_GROUP
    eye = jnp.eye(groups_per_tile, dtype=F32)

    def in_tile(bb):
        t = bb.reshape(SSM_COL_TILES, groups_per_tile, SSM_STATE, SSM_GROUP)
        t = jnp.einsum('tgpc,gh->tgchp', t, eye)
        return t.reshape(SSM_COL_TILES, SSM_COL, SSM_COL_LANES).astype(BF16)

    def out_tile(cc):
        t = cc.reshape(SSM_COL_TILES, groups_per_tile, SSM_GROUP, SSM_STATE)
        t = jnp.einsum('tgcp,gh->tgphc', t, eye)
        return t.reshape(SSM_COL_TILES, SSM_COL_LANES, SSM_COL).astype(BF16)

    def powers(exps):
        e = exps.astype(F32)[:, None, None]
        pm = jnp.exp(e * z_re[None])
        return jnp.stack([(pm * jnp.cos(e * z_im[None])).reshape(len(exps), SSM_LANES),
                          (pm * jnp.sin(e * z_im[None])).reshape(len(exps), SSM_LANES)])

    return dict(
        b_re=in_tile(bb_re), b_im=in_tile(bb_im),
        c_re=out_tile(c_re.astype(F32)), c_im=out_tile(-c_im.astype(F32)),
        a=jnp.stack([abar_re.reshape(1, SSM_LANES), abar_im.reshape(1, SSM_LANES)]),
        a_seg=powers(jnp.array([seg_len])),
        a_pow=powers(jnp.arange(1, seg_len + 1)),
        a_meta=powers(jnp.arange(N_META - 1, -1, -1)),
    )


def _gelu_tanh(t):
    return 0.5 * t * (1.0 + jnp.tanh(math.sqrt(2.0 / math.pi) * (t + 0.044715 * (t * t * t))))


def _ssm_kernel(u_ref, um_ref, bre_ref, bim_ref, cre_ref, cim_ref, a_ref, aseg_ref, apow_ref,
                ameta_ref, d_ref, wglu_ref, bglu_ref, g_ref, o_ref,
                xre, xim, hre, him, car_re, car_im, cin_re, cin_im, u_scr, y_scr):
    chunk = u_ref.shape[0]
    seg = chunk // SUBLANES

    @pl.when(pl.program_id(1) == 0)
    def _():
        um = um_ref[...].astype(BF16)
        for ct in range(SSM_COL_TILES):
            ub = um[:, ct * SSM_COL:(ct + 1) * SSM_COL]
            ls = slice(ct * SSM_COL_LANES, (ct + 1) * SSM_COL_LANES)
            xr = jnp.dot(ub, bre_ref[ct], preferred_element_type=F32)
            xi = jnp.dot(ub, bim_ref[ct], preferred_element_type=F32)
            pr = ameta_ref[0, :, ls]
            pi = ameta_ref[1, :, ls]
            car_re[:, ls] = jnp.sum(pr * xr - pi * xi, axis=0, keepdims=True)
            car_im[:, ls] = jnp.sum(pr * xi + pi * xr, axis=0, keepdims=True)

    n_lt = SSM_WIDTH // LANES
    for j in range(n_lt):
        u_scr[j] = u_ref[:, j * LANES:(j + 1) * LANES]
    up = jnp.concatenate(
        [jnp.concatenate([u_scr[j, pl.ds(k, SUBLANES, stride=seg), :] for j in range(n_lt)], axis=1)
         for k in range(seg)], axis=0).astype(BF16)
    for ct in range(SSM_COL_TILES):
        ub = up[:, ct * SSM_COL:(ct + 1) * SSM_COL]
        ls = slice(ct * SSM_COL_LANES, (ct + 1) * SSM_COL_LANES)
        xre[:, ls] = jnp.dot(ub, bre_ref[ct], preferred_element_type=F32)
        xim[:, ls] = jnp.dot(ub, bim_ref[ct], preferred_element_type=F32)

    for lb in range(SSM_LANES // SCAN_LANES):
        ls = slice(lb * SCAN_LANES, (lb + 1) * SCAN_LANES)
        ar = jnp.broadcast_to(a_ref[0, :, ls], (SUBLANES, SCAN_LANES))
        ai = jnp.broadcast_to(a_ref[1, :, ls], (SUBLANES, SCAN_LANES))

        def scan_body(k, carry):
            hr, hi = carry
            rows = pl.ds(pl.multiple_of(k * SUBLANES, SUBLANES), SUBLANES)
            nr = ar * hr - ai * hi + xre[rows, ls]
            ni = ar * hi + ai * hr + xim[rows, ls]
            xre[rows, ls] = nr
            xim[rows, ls] = ni
            return nr, ni

        zero = jnp.zeros((SUBLANES, SCAN_LANES), F32)
        lax.fori_loop(0, seg, scan_body, (zero, zero))

    cr = car_re[...]
    ci = car_im[...]
    sr = aseg_ref[0]
    si = aseg_ref[1]
    for r in range(SUBLANES):
        cin_re[r:r + 1, :] = cr
        cin_im[r:r + 1, :] = ci
        er = xre[chunk - SUBLANES + r:chunk - SUBLANES + r + 1, :]
        ei = xim[chunk - SUBLANES + r:chunk - SUBLANES + r + 1, :]
        cr, ci = sr * cr - si * ci + er, sr * ci + si * cr + ei
    car_re[...] = cr
    car_im[...] = ci

    pair = 2 * SUBLANES
    for lb in range(SSM_LANES // SCAN_LANES):
        ls = slice(lb * SCAN_LANES, (lb + 1) * SCAN_LANES)
        er = jnp.concatenate([cin_re[:, ls], cin_re[:, ls]], axis=0)
        ei = jnp.concatenate([cin_im[:, ls], cin_im[:, ls]], axis=0)

        def fix_body(k2, _):
            rows = pl.ds(pl.multiple_of(k2 * pair, pair), pair)
            k = 2 * k2
            pr = jnp.concatenate(
                [jnp.broadcast_to(apow_ref[0, pl.ds(k, 1), ls], (SUBLANES, SCAN_LANES)),
                 jnp.broadcast_to(apow_ref[0, pl.ds(k + 1, 1), ls], (SUBLANES, SCAN_LANES))], axis=0)
            pi = jnp.concatenate(
                [jnp.broadcast_to(apow_ref[1, pl.ds(k, 1), ls], (SUBLANES, SCAN_LANES)),
                 jnp.broadcast_to(apow_ref[1, pl.ds(k + 1, 1), ls], (SUBLANES, SCAN_LANES))], axis=0)
            hre[rows, ls] = (xre[rows, ls] + pr * er - pi * ei).astype(BF16)
            him[rows, ls] = (xim[rows, ls] + pr * ei + pi * er).astype(BF16)
            return 0

        lax.fori_loop(0, seg // 2, fix_body, 0)

    for ct in range(SSM_COL_TILES):
        ls = slice(ct * SSM_COL_LANES, (ct + 1) * SSM_COL_LANES)
        y = (jnp.dot(hre[:, ls], cre_ref[ct], preferred_element_type=F32)
             + jnp.dot(him[:, ls], cim_ref[ct], preferred_element_type=F32))
        for k in range(seg):
            for jj in range(SSM_COL // LANES):
                y_scr[ct * (SSM_COL // LANES) + jj, pl.ds(k, SUBLANES, stride=seg), :] = (
                    y[k * SUBLANES:(k + 1) * SUBLANES, jj * LANES:(jj + 1) * LANES])

    y = jnp.concatenate([y_scr[j] for j in range(n_lt)], axis=1)
    y = _gelu_tanh(y + d_ref[...] * u_ref[...])
    gate = jnp.dot(y.astype(BF16), wglu_ref[...], preferred_element_type=F32) + bglu_ref[...]
    y = y * _sigmoid(gate)
    o_ref[...] = _rms(y, g_ref[...]).astype(BF16)


def _ssm(u, u_meta, prm, d_skip, w_glu_bf, b_glu, g_ssm, bsz, seq, chunk):
    nc = seq // chunk
    seg = chunk // SUBLANES
    row = lambda b, c: (b * nc + c, 0)
    c2 = lambda b, c: (0, 0)
    c3 = lambda b, c: (0, 0, 0)
    return pl.pallas_call(
        _ssm_kernel,
        grid=(bsz, nc),
        in_specs=[
            pl.BlockSpec((chunk, SSM_WIDTH), row),
            pl.BlockSpec((N_META, SSM_WIDTH), c2),
            pl.BlockSpec((SSM_COL_TILES, SSM_COL, SSM_COL_LANES), c3),
            pl.BlockSpec((SSM_COL_TILES, SSM_COL, SSM_COL_LANES), c3),
            pl.BlockSpec((SSM_COL_TILES, SSM_COL_LANES, SSM_COL), c3),
            pl.BlockSpec((SSM_COL_TILES, SSM_COL_LANES, SSM_COL), c3),
            pl.BlockSpec((2, 1, SSM_LANES), c3),
            pl.BlockSpec((2, 1, SSM_LANES), c3),
            pl.BlockSpec((2, seg, SSM_LANES), c3),
            pl.BlockSpec((2, N_META, SSM_LANES), c3),
            pl.BlockSpec((1, SSM_WIDTH), c2),
            pl.BlockSpec((SSM_WIDTH, SSM_WIDTH), c2),
            pl.BlockSpec((1, SSM_WIDTH), c2),
            pl.BlockSpec((1, SSM_WIDTH), c2),
        ],
        out_specs=pl.BlockSpec((chunk, SSM_WIDTH), row),
        out_shape=jax.ShapeDtypeStruct((bsz * seq, SSM_WIDTH), BF16),
        scratch_shapes=[
            pltpu.VMEM((chunk, SSM_LANES), F32),
            pltpu.VMEM((chunk, SSM_LANES), F32),
            pltpu.VMEM((chunk, SSM_LANES), BF16),
            pltpu.VMEM((chunk, SSM_LANES), BF16),
            pltpu.VMEM((1, SSM_LANES), F32),
            pltpu.VMEM((1, SSM_LANES), F32),
            pltpu.VMEM((SUBLANES, SSM_LANES), F32),
            pltpu.VMEM((SUBLANES, SSM_LANES), F32),
            pltpu.VMEM((SSM_WIDTH // LANES, chunk, LANES), F32),
            pltpu.VMEM((SSM_WIDTH // LANES, chunk, LANES), F32),
        ],
        compiler_params=pltpu.CompilerParams(
            dimension_semantics=("arbitrary", "arbitrary"), vmem_limit_bytes=VMEM_LIMIT),
        name="ssm",
    )(u, u_meta, prm["b_re"], prm["b_im"], prm["c_re"], prm["c_im"], prm["a"], prm["a_seg"],
      prm["a_pow"], prm["a_meta"], d_skip, w_glu_bf, b_glu, g_ssm)


def _outproj_kernel(ya_ref, ys_ref, x_ref, wo_ref, bo_ref, gf_ref, wr_ref, br_ref,
                    h_ref, xn_ref, lg_ref):
    mix = (jnp.dot(ya_ref[...], wo_ref[:ATTN_WIDTH, :], preferred_element_type=F32)
           + jnp.dot(ys_ref[...], wo_ref[ATTN_WIDTH:, :], preferred_element_type=F32))
    h = x_ref[...] + mix + bo_ref[...]
    h_ref[...] = h
    n = _rms(h, gf_ref[...])
    n_hi = n.astype(BF16)
    bits = lax.bitcast_convert_type(n_hi.astype(F32), jnp.uint32)
    xn_ref[...] = (bits[:, :D_MODEL // 2] >> 16) | (bits[:, D_MODEL // 2:] & jnp.uint32(0xFFFF0000))
    n_lo = (n - n_hi.astype(F32)).astype(BF16)
    hi = jnp.dot(n_hi, wr_ref[...], preferred_element_type=F32)
    lo = jnp.dot(n_lo, wr_ref[:, :ROUTER_LANES], preferred_element_type=F32)
    lg_ref[...] = hi[:, :ROUTER_LANES] + hi[:, ROUTER_LANES:] + lo + br_ref[...]


def _outproj(y_attn, y_ssm, xf, w_out_bf, b_out, g_ffn, w_router_pad, b_router_pad, row_tile):
    rows = xf.shape[0]
    row = lambda i: (i, 0)
    const = lambda i: (0, 0)
    return pl.pallas_call(
        _outproj_kernel,
        grid=(rows // row_tile,),
        in_specs=[
            pl.BlockSpec((row_tile, ATTN_WIDTH), row),
            pl.BlockSpec((row_tile, SSM_WIDTH), row),
            pl.BlockSpec((row_tile, D_MODEL), row),
            pl.BlockSpec((D_MODEL, D_MODEL), const),
            pl.BlockSpec((1, D_MODEL), const),
            pl.BlockSpec((1, D_MODEL), const),
            pl.BlockSpec((D_MODEL, 2 * ROUTER_LANES), const),
            pl.BlockSpec((1, ROUTER_LANES), const),
        ],
        out_specs=[
            pl.BlockSpec((row_tile, D_MODEL), row),
            pl.BlockSpec((row_tile, D_MODEL // 2), row),
            pl.BlockSpec((row_tile, ROUTER_LANES), row),
        ],
        out_shape=[
            jax.ShapeDtypeStruct((rows, D_MODEL), F32),
            jax.ShapeDtypeStruct((rows, D_MODEL // 2), jnp.uint32),
            jax.ShapeDtypeStruct((rows, ROUTER_LANES), F32),
        ],
        compiler_params=pltpu.CompilerParams(
            dimension_semantics=("arbitrary",), vmem_limit_bytes=VMEM_LIMIT),
        name="outproj",
    )(y_attn, y_ssm, xf, w_out_bf, b_out, g_ffn, w_router_pad, b_router_pad)


def _moe_kernel(ie_ref, ir_ref, in_ref, ni_ref, rt_ref, rd_ref,
                xn_hbm, w1a_ref, w1b_ref, b1g_ref, b1u_ref, w2_ref, b2_ref, perm_ref, yk_hbm,
                stage, x_scr, acc, w1g, w1u, w2b, issued, sem_g, sem_out):
    i = pl.program_id(0)
    f = pl.program_id(1)
    nf = pl.num_programs(1)
    max_items = pl.num_programs(0)
    tf = w2_ref.shape[1]

    @pl.when(i < ni_ref[0])
    def _():
        r0 = ir_ref[i]
        n = in_ref[i]
        nxt = jnp.minimum(i + 1, max_items - 1)
        n_next = jnp.where(i + 1 < ni_ref[0], in_ref[nxt], 0)
        r0_next = ir_ref[nxt]

        def gather_row(first_row, j):
            tok = rt_ref[first_row + j]
            pltpu.make_async_copy(xn_hbm.at[pl.ds(tok, 1), :], stage.at[pl.ds(j, 1), :],
                                  sem_g).start()

        def gather_wait_chunk():
            pltpu.make_async_copy(xn_hbm.at[pl.ds(0, MOE_PAD), :],
                                  stage.at[pl.ds(0, MOE_PAD), :], sem_g).wait()

        @pl.when(f == 0)
        def _():
            @pl.when(i == 0)
            def _():
                def first(j, c):
                    gather_row(r0, j)
                    return c

                lax.fori_loop(0, n, first, 0)

            def landed(c, carry):
                gather_wait_chunk()
                return carry

            lax.fori_loop(0, n // MOE_PAD, landed, 0)

            def unpack(c, carry):
                rows_c = pl.ds(pl.multiple_of(c * MOE_PAD, MOE_PAD), MOE_PAD)
                u = stage[rows_c, :]
                x_scr[rows_c, :D_MODEL // 2] = lax.bitcast_convert_type(u << 16, F32).astype(BF16)
                x_scr[rows_c, D_MODEL // 2:] = lax.bitcast_convert_type(
                    u & jnp.uint32(0xFFFF0000), F32).astype(BF16)
                return carry

            lax.fori_loop(0, n // MOE_PAD, unpack, 0)
            issued[0] = 0
            bias = jnp.broadcast_to(b2_ref[0], (MOE_PAD, D_MODEL))

            def init(c, _):
                acc[pl.ds(pl.multiple_of(c * MOE_PAD, MOE_PAD), MOE_PAD), :] = bias
                return 0

            lax.fori_loop(0, n // MOE_PAD, init, 0)

        half = PERM_COLS // 2
        k_half = D_MODEL // 2
        for hh, w1_ref in enumerate((w1a_ref, w1b_ref)):
            for c in range(2 * tf // PERM_COLS):
                t = jnp.dot(w1_ref[0, :, c * PERM_COLS:(c + 1) * PERM_COLS].astype(BF16),
                            perm_ref[...], preferred_element_type=F32)
                rs = slice(hh * k_half, (hh + 1) * k_half)
                w1g[rs, c * half:(c + 1) * half] = t[:, :half].astype(BF16)
                w1u[rs, c * half:(c + 1) * half] = t[:, half:].astype(BF16)
        w2b[...] = w2_ref[0].astype(BF16)

        def scatter(r, rows):
            for t in range(rows):
                dst = rd_ref[r0 + r + t]
                pltpu.make_async_copy(acc.at[pl.ds(r + t, 1), :], yk_hbm.at[pl.ds(dst, 1), :],
                                      sem_out).start()

        def block(r, rows, prefetch):
            if prefetch:
                base = issued[0]
                for t in range(rows // 4):
                    gather_row(r0_next, base + t)
                issued[0] = base + rows // 4
            sl = pl.ds(r, rows)
            xb = x_scr[sl, :]
            g = jnp.dot(xb, w1g[...], preferred_element_type=F32) + b1g_ref[0]
            up = jnp.dot(xb, w1u[...], preferred_element_type=F32) + b1u_ref[0]
            g = jnp.minimum(g, SWIGLU_LIMIT)
            up = jnp.clip(up, -SWIGLU_LIMIT, SWIGLU_LIMIT)
            act = g * _sigmoid(SWIGLU_ALPHA * g) * (up + 1.0)
            acc[sl, :] += jnp.dot(act.astype(BF16), w2b[...], preferred_element_type=F32)

        def run(r, rows):
            can = issued[0] + rows // 4 <= n_next

            @pl.when(can)
            def _():
                block(r, rows, True)

            @pl.when(jnp.logical_not(can))
            def _():
                block(r, rows, False)

        n_big = n // MOE_BLOCK
        rem = n % MOE_BLOCK
        last = f == nf - 1

        def big(b, _):
            run(pl.multiple_of(b * MOE_BLOCK, MOE_BLOCK), MOE_BLOCK)
            return 0

        @pl.when(jnp.logical_not(last))
        def _():
            lax.fori_loop(0, n_big, big, 0)

        @pl.when(last)
        def _():
            for b in range(MOE_ITEM_ROWS // MOE_BLOCK):
                @pl.when(b < n_big)
                def _(b=b):
                    run(b * MOE_BLOCK, MOE_BLOCK)
                    scatter(b * MOE_BLOCK, MOE_BLOCK)

        def tail(r, rows):
            run(r, rows)

            @pl.when(last)
            def _():
                scatter(r, rows)

        @pl.when(rem >= 2 * MOE_PAD)
        def _():
            tail(pl.multiple_of(n_big * MOE_BLOCK, MOE_BLOCK), 2 * MOE_PAD)

        @pl.when(rem % (2 * MOE_PAD) != 0)
        def _():
            tail(pl.multiple_of(n - MOE_PAD, MOE_PAD), MOE_PAD)

        @pl.when(last)
        def _():
            def left(c, carry):
                pltpu.make_async_copy(acc.at[pl.ds(0, MOE_PAD), :],
                                      yk_hbm.at[pl.ds(0, MOE_PAD), :], sem_out).wait()
                return carry

            lax.fori_loop(0, n // MOE_PAD, left, 0)

            def rest(j, c):
                gather_row(r0_next, j)
                return c

            lax.fori_loop(issued[0], n_next, rest, 0)


def _experts(xn, item_e, item_r0, item_n, n_items, row_tok, row_dst, n_out_rows,
             w1, b1g, b1u, w2, b2, tf):
    max_items = item_e.shape[0]
    nf = D_FF // tf
    idx = jnp.arange(PERM_COLS)
    dst = jnp.where(idx % 2 == 0, idx // 2, PERM_COLS // 2 + idx // 2)
    perm = (dst[:, None] == jnp.arange(PERM_COLS)[None, :]).astype(BF16)

    def ex(i, ie, ni):
        return ie[jnp.minimum(i, ni[0] - 1)]

    def ff(i, f, ni):
        return jnp.where(i < ni[0], f, nf - 1)

    grid_spec = pltpu.PrefetchScalarGridSpec(
        num_scalar_prefetch=6,
        grid=(max_items, nf),
        in_specs=[
            pl.BlockSpec(memory_space=pl.ANY),
            pl.BlockSpec((1, D_MODEL // 2, 2 * tf), lambda i, f, ie, ir, im, ni, rt, rd: (ex(i, ie, ni), 0, ff(i, f, ni))),
            pl.BlockSpec((1, D_MODEL // 2, 2 * tf), lambda i, f, ie, ir, im, ni, rt, rd: (ex(i, ie, ni), 1, ff(i, f, ni))),
            pl.BlockSpec((1, 1, tf), lambda i, f, ie, ir, im, ni, rt, rd: (ex(i, ie, ni), 0, ff(i, f, ni))),
            pl.BlockSpec((1, 1, tf), lambda i, f, ie, ir, im, ni, rt, rd: (ex(i, ie, ni), 0, ff(i, f, ni))),
            pl.BlockSpec((1, tf, D_MODEL), lambda i, f, ie, ir, im, ni, rt, rd: (ex(i, ie, ni), ff(i, f, ni), 0)),
            pl.BlockSpec((1, 1, D_MODEL), lambda i, f, ie, ir, im, ni, rt, rd: (ex(i, ie, ni), 0, 0)),
            pl.BlockSpec((PERM_COLS, PERM_COLS), lambda i, f, ie, ir, im, ni, rt, rd: (0, 0)),
        ],
        out_specs=pl.BlockSpec(memory_space=pl.ANY),
        scratch_shapes=[
            pltpu.VMEM((MOE_ITEM_ROWS, D_MODEL // 2), jnp.uint32),
            pltpu.VMEM((MOE_ITEM_ROWS, D_MODEL), BF16),
            pltpu.VMEM((MOE_ITEM_ROWS, D_MODEL), F32),
            pltpu.VMEM((D_MODEL, tf), BF16),
            pltpu.VMEM((D_MODEL, tf), BF16),
            pltpu.VMEM((tf, D_MODEL), BF16),
            pltpu.SMEM((1,), jnp.int32),
            pltpu.SemaphoreType.DMA(()),
            pltpu.SemaphoreType.DMA(()),
        ],
    )
    return pl.pallas_call(
        _moe_kernel,
        grid_spec=grid_spec,
        out_shape=jax.ShapeDtypeStruct((n_out_rows, D_MODEL), F32),
        compiler_params=pltpu.CompilerParams(
            dimension_semantics=("arbitrary", "arbitrary"), vmem_limit_bytes=VMEM_LIMIT),
        name="experts",
    )(item_e, item_r0, item_n, n_items, row_tok, row_dst, xn, w1, w1, b1g, b1u, w2, b2, perm)


def _combine_kernel(h_ref, y0_ref, y1_ref, y2_ref, y3_ref, gate_ref, g_ref, o_ref):
    acc = h_ref[...]
    gates = gate_ref[...]
    for k, y_ref in enumerate((y0_ref, y1_ref, y2_ref, y3_ref)):
        acc = acc + gates[:, k:k + 1] * y_ref[...]
    o_ref[...] = _rms(acc, g_ref[...])


def _combine(h1, yk, gates_pad, g_final, row_tile):
    rows = h1.shape[0]
    n_tiles = rows // row_tile
    row = lambda i: (i, 0)
    const = lambda i: (0, 0)
    slot = lambda k: (lambda i: (k * n_tiles + i, 0))
    return pl.pallas_call(
        _combine_kernel,
        grid=(n_tiles,),
        in_specs=[pl.BlockSpec((row_tile, D_MODEL), row)]
        + [pl.BlockSpec((row_tile, D_MODEL), slot(k)) for k in range(TOP_K)]
        + [pl.BlockSpec((row_tile, LANES), row), pl.BlockSpec((1, D_MODEL), const)],
        out_specs=pl.BlockSpec((row_tile, D_MODEL), row),
        out_shape=jax.ShapeDtypeStruct((rows, D_MODEL), F32),
        compiler_params=pltpu.CompilerParams(
            dimension_semantics=("arbitrary",), vmem_limit_bytes=VMEM_LIMIT),
        name="combine",
    )(h1, yk, yk, yk, yk, gates_pad, g_final)


def kernel(x, meta_tokens, g_mix, w_in, b_in, attn_sinks, ssm_a_re, ssm_a_im, ssm_log_dt,
           ssm_b_re, ssm_b_im, ssm_c_re, ssm_c_im, ssm_d, w_glu, b_glu, g_attn_out, g_ssm_out,
           w_out, b_out, g_ffn, w_router, b_router, w_mlp1, b_mlp1, w_mlp2, b_mlp2, g_final):
    bsz, seq, _ = x.shape
    rows = bsz * seq
    row_tile = min(ROW_TILE, seq)
    chunk = min(SSM_CHUNK, seq)
    assert seq % ATTN_BLOCK == 0 and seq % row_tile == 0 and seq % chunk == 0
    xf = x.reshape(rows, D_MODEL)

    w_in_bf = w_in[0].astype(BF16)
    cos_r, sin_r = _rope_tables(N_META + jnp.arange(seq))
    cos_m, sin_m = _rope_tables(jnp.arange(N_META))
    qt, k, vt, u = _inproj(xf, g_mix, w_in_bf, b_in, cos_r, sin_r, row_tile, True)
    _, k_meta, v_meta, u_meta = _inproj(meta_tokens, g_mix, w_in_bf, b_in, cos_m, sin_m,
                                        N_META, False)

    y_attn = _attention(qt, k, vt, k_meta, v_meta.T, attn_sinks, g_attn_out, bsz, seq)

    prm = _ssm_params(ssm_a_re[0], ssm_a_im[0], ssm_log_dt[0], ssm_b_re[0], ssm_b_im[0],
                      ssm_c_re[0], ssm_c_im[0], chunk // SUBLANES)
    y_ssm = _ssm(u, u_meta, prm, ssm_d, w_glu[0].astype(BF16), b_glu, g_ssm_out, bsz, seq, chunk)

    w_router_pad = jnp.pad(w_router[0], ((0, 0), (0, ROUTER_LANES - N_EXPERTS)))
    b_router_pad = jnp.pad(b_router, ((0, 0), (0, ROUTER_LANES - N_EXPERTS)))
    w_router_hi = w_router_pad.astype(BF16)
    w_router_lo = (w_router_pad - w_router_hi.astype(F32)).astype(BF16)
    w_router_pad = jnp.concatenate([w_router_hi, w_router_lo], axis=1)
    h1, xn, logits = _outproj(y_attn, y_ssm, xf, w_out[0].astype(BF16), b_out, g_ffn,
                              w_router_pad, b_router_pad, row_tile)

    top_val, top_idx = lax.top_k(logits[:, :N_EXPERTS], TOP_K)
    gates = jax.nn.softmax(top_val, axis=-1)
    chosen = (top_idx[:, :, None] == jnp.arange(N_EXPERTS)[None, None, :])
    counts = jnp.sum(chosen.astype(jnp.int32), axis=(0, 1))
    padded = ((counts + MOE_PAD - 1) // MOE_PAD) * MOE_PAD
    pad_ends = jnp.cumsum(padded)
    pad_starts = pad_ends - padded
    n_assign = rows * TOP_K
    n_rows = n_assign + MOE_DUMP_ROWS
    order = jnp.argsort(top_idx.reshape(-1), stable=True).astype(jnp.int32)
    starts = jnp.cumsum(counts) - counts
    r = jnp.arange(n_rows, dtype=jnp.int32)
    r_e = jnp.minimum(jnp.sum(r[:, None] >= pad_ends[None, :], axis=1), N_EXPERTS - 1)
    r_j = r - pad_starts[r_e]
    real = r_j < counts[r_e]
    assign = order[jnp.minimum(starts[r_e] + r_j, n_assign - 1)]
    row_tok = jnp.where(real, assign // TOP_K, 0).astype(jnp.int32)
    dump = n_assign + r_e * MOE_PAD + jnp.clip(r_j - counts[r_e], 0, MOE_PAD - 1)
    row_dst = jnp.where(real, (assign % TOP_K) * rows + assign // TOP_K, dump).astype(jnp.int32)
    per_e = (padded + MOE_ITEM_ROWS - 1) // MOE_ITEM_ROWS
    item_ends = jnp.cumsum(per_e)
    max_items = N_EXPERTS + n_assign // MOE_ITEM_ROWS
    slot = jnp.arange(max_items, dtype=jnp.int32)
    item_e = jnp.minimum(jnp.searchsorted(item_ends, slot, side='right'), N_EXPERTS - 1)
    piece = slot - (item_ends - per_e)[item_e]
    item_r0 = (pad_starts[item_e] + piece * MOE_ITEM_ROWS).astype(jnp.int32)
    item_n = jnp.clip(padded[item_e] - piece * MOE_ITEM_ROWS, 0, MOE_ITEM_ROWS).astype(jnp.int32)
    n_items = item_ends[-1].astype(jnp.int32).reshape(1)

    tf = MOE_FF_TILE
    b1 = b_mlp1[0].reshape(N_EXPERTS, 1, D_FF, 2)
    b2 = b_mlp2[0].reshape(N_EXPERTS, 1, D_MODEL)
    yk = _experts(xn, item_e.astype(jnp.int32), item_r0, item_n, n_items, row_tok, row_dst,
                  n_assign + MOE_DUMP_ROWS, w_mlp1[0], b1[..., 0], b1[..., 1], w_mlp2[0], b2, tf)

    gates_pad = jnp.pad(gates, ((0, 0), (0, LANES - TOP_K)))
    out = _combine(h1, yk, gates_pad, g_final.reshape(1, D_MODEL), row_tile)
    return out.reshape(bsz, seq, D_MODEL)
```

```python
import functools
import math

import jax
import jax.numpy as jnp
from jax import lax
from jax.experimental import pallas as pl
from jax.experimental.pallas import tpu as pltpu

F32 = jnp.float32
BF16 = jnp.bfloat16

D_MODEL = 2048
N_META = 16
HEAD_DIM = 64
ATTN_WIDTH = 1024
N_Q_HEADS = 16
N_KV_HEADS = 4
Q_PER_KV = 4
KV_WIDTH = 256
ATTN_BLOCK = 128
ROPE_THETA = 10000.0
SSM_WIDTH = 1024
SSM_GROUP = 16
N_SSM_GROUPS = 64
SSM_STATE = 64
SSM_LANES = N_SSM_GROUPS * SSM_STATE
IN_WIDTH = 2560
N_EXPERTS = 32
TOP_K = 4
D_FF = 2048
SWIGLU_LIMIT = 7.0
SWIGLU_ALPHA = 1.702
NORM_EPS = 1e-5

LANES = 128
SUBLANES = 8
VMEM_LIMIT = 58 * 1024 * 1024

ROW_TILE = 512
SSM_CHUNK = 256
SSM_COL = 256
SSM_COL_TILES = SSM_WIDTH // SSM_COL
SSM_COL_LANES = SSM_LANES // SSM_COL_TILES
SCAN_LANES = 512
MOE_PAD = 128
MOE_BLOCK = 512
MOE_ITEM_ROWS = 2176
MOE_FF_TILE = 256
MOE_DUMP_ROWS = N_EXPERTS * MOE_PAD
ROUTER_LANES = 128
PERM_COLS = 256


def _rms(t, gain):
    return t * lax.rsqrt(jnp.mean(t * t, axis=-1, keepdims=True) + NORM_EPS) * gain


def _sigmoid(t):
    return 1.0 / (1.0 + jnp.exp(-t))


def _inproj_kernel(x_ref, g_ref, w_ref, b_ref, cos_ref, sin_ref, q_ref, k_ref, v_ref, u_ref,
                   *, transposed):
    n = _rms(x_ref[...], g_ref[...]).astype(BF16)
    cos = cos_ref[...]
    sin = sin_ref[...]
    lane = lax.broadcasted_iota(jnp.int32, cos.shape, 1)
    first_half = (lane % HEAD_DIM) < (HEAD_DIM // 2)

    def proj(c0, c1):
        return jnp.dot(n, w_ref[:, c0:c1], preferred_element_type=F32) + b_ref[:, c0:c1]

    def rope(t):
        partner = jnp.where(first_half, pltpu.roll(t, LANES - HEAD_DIM // 2, 1),
                            pltpu.roll(t, HEAD_DIM // 2, 1))
        return t * cos + partner * sin

    def put(ref, j, t):
        if transposed:
            ref[j * LANES:(j + 1) * LANES, :] = t.T.astype(BF16)
        else:
            ref[:, j * LANES:(j + 1) * LANES] = t.astype(BF16)

    def proj_pair(c0):
        z = proj(c0, c0 + 2 * LANES)
        return z[:, :LANES], z[:, LANES:]

    scale = HEAD_DIM ** -0.5
    for jj in range(ATTN_WIDTH // (2 * LANES)):
        a, b = proj_pair(jj * 2 * LANES)
        put(q_ref, 2 * jj, rope(a) * scale)
        put(q_ref, 2 * jj + 1, rope(b) * scale)
    a, b = proj_pair(ATTN_WIDTH)
    k_ref[:, :LANES] = rope(a).astype(BF16)
    k_ref[:, LANES:] = rope(b).astype(BF16)
    a, b = proj_pair(ATTN_WIDTH + KV_WIDTH)
    put(v_ref, 0, a)
    put(v_ref, 1, b)
    c0 = ATTN_WIDTH + 2 * KV_WIDTH
    u_ref[...] = proj(c0, c0 + SSM_WIDTH)


def _inproj(xf, g_mix, w_in_bf, b_in, cos_t, sin_t, row_tile, transposed):
    rows = xf.shape[0]
    tab_blocks = cos_t.shape[0] // row_tile
    row = lambda i: (i, 0)
    col = lambda i: (0, i)
    tab = lambda i: (i % tab_blocks, 0)
    const = lambda i: (0, 0)
    if transposed:
        q_spec, q_shape = pl.BlockSpec((ATTN_WIDTH, row_tile), col), (ATTN_WIDTH, rows)
        v_spec, v_shape = pl.BlockSpec((KV_WIDTH, row_tile), col), (KV_WIDTH, rows)
    else:
        q_spec, q_shape = pl.BlockSpec((row_tile, ATTN_WIDTH), row), (rows, ATTN_WIDTH)
        v_spec, v_shape = pl.BlockSpec((row_tile, KV_WIDTH), row), (rows, KV_WIDTH)
    return pl.pallas_call(
        functools.partial(_inproj_kernel, transposed=transposed),
        grid=(rows // row_tile,),
        in_specs=[
            pl.BlockSpec((row_tile, D_MODEL), row),
            pl.BlockSpec((1, D_MODEL), const),
            pl.BlockSpec((D_MODEL, IN_WIDTH), const),
            pl.BlockSpec((1, IN_WIDTH), const),
            pl.BlockSpec((row_tile, LANES), tab),
            pl.BlockSpec((row_tile, LANES), tab),
        ],
        out_specs=[
            q_spec,
            pl.BlockSpec((row_tile, KV_WIDTH), row),
            v_spec,
            pl.BlockSpec((row_tile, SSM_WIDTH), row),
        ],
        out_shape=[
            jax.ShapeDtypeStruct(q_shape, BF16),
            jax.ShapeDtypeStruct((rows, KV_WIDTH), BF16),
            jax.ShapeDtypeStruct(v_shape, BF16),
            jax.ShapeDtypeStruct((rows, SSM_WIDTH), F32),
        ],
        compiler_params=pltpu.CompilerParams(
            dimension_semantics=("arbitrary",), vmem_limit_bytes=VMEM_LIMIT),
        name="inproj",
    )(xf, g_mix, w_in_bf, b_in, cos_t, sin_t)


def _rope_tables(positions):
    half = HEAD_DIM // 2
    inv_freq = jnp.power(ROPE_THETA, -jnp.arange(half, dtype=F32) / half)
    ang = positions.astype(F32)[:, None] * inv_freq[None, :]
    cos = jnp.tile(jnp.cos(ang), (1, LANES // half))
    sin = jnp.tile(jnp.sin(ang), (1, LANES // half))
    sign = jnp.where((jnp.arange(LANES) % HEAD_DIM) < half, -1.0, 1.0).astype(F32)
    return cos, sin * sign[None, :]


def _attn_kernel(qt_ref, kp_ref, kc_ref, km_ref, vtp_ref, vtc_ref, vtm_ref, sink_ref, g_ref, o_ref,
                 yt_scr):
    n = pl.program_id(1)
    n_keys = 2 * ATTN_BLOCK + N_META
    key = lax.broadcasted_iota(jnp.int32, (n_keys, ATTN_BLOCK), 0)
    qi = lax.broadcasted_iota(jnp.int32, (n_keys, ATTN_BLOCK), 1)
    no_prev = jnp.where(n > 0, 0, ATTN_BLOCK)
    cur_j = key - ATTN_BLOCK
    valid = ((key >= 2 * ATTN_BLOCK) | ((cur_j >= 0) & (cur_j <= qi))
             | ((key < ATTN_BLOCK) & (key > qi + no_prev)))
    bias = jnp.where(valid, 0.0, -1e30)
    bias = jnp.concatenate([bias] * Q_PER_KV, axis=1)
    zeros = jnp.zeros((HEAD_DIM, ATTN_BLOCK), BF16)
    ssq = jnp.zeros((1, ATTN_BLOCK), F32)
    for hk in range(N_KV_HEADS):
        lt = slice((hk // 2) * LANES, (hk // 2 + 1) * LANES)
        kt = jnp.concatenate([kp_ref[:, lt], kc_ref[:, lt], km_ref[:, lt]], axis=0)
        rs = slice(hk * HEAD_DIM, (hk + 1) * HEAD_DIM)
        vt = jnp.concatenate([vtp_ref[rs, :], vtc_ref[rs, :], vtm_ref[rs, :]], axis=1)
        qs, sinks = [], []
        for g in range(Q_PER_KV):
            h = hk * Q_PER_KV + g
            qh = qt_ref[h * HEAD_DIM:(h + 1) * HEAD_DIM, :]
            qs.append(jnp.concatenate([qh, zeros] if hk % 2 == 0 else [zeros, qh], axis=0))
            sinks.append(jnp.broadcast_to(sink_ref[:, h:h + 1], (1, ATTN_BLOCK)))
        sink = jnp.concatenate(sinks, axis=1)
        s = jnp.dot(kt, jnp.concatenate(qs, axis=1), preferred_element_type=F32) + bias
        m = jnp.maximum(jnp.max(s, axis=0, keepdims=True), sink)
        p = jnp.exp(s - m)
        denom = jnp.sum(p, axis=0, keepdims=True) + jnp.exp(sink - m)
        o = jnp.dot(vt, p.astype(BF16), preferred_element_type=F32) * (1.0 / denom)
        for g in range(Q_PER_KV):
            h = hk * Q_PER_KV + g
            og = o[:, g * ATTN_BLOCK:(g + 1) * ATTN_BLOCK]
            yt_scr[h * HEAD_DIM:(h + 1) * HEAD_DIM, :] = og
            ssq = ssq + jnp.sum(og * og, axis=0, keepdims=True)
    inv = lax.rsqrt(ssq * (1.0 / ATTN_WIDTH) + NORM_EPS)
    o_ref[...] = ((yt_scr[...] * inv).T * g_ref[...]).astype(BF16)


def _attention(qt, k, vt, k_meta, vt_meta, sinks, g_attn, bsz, seq):
    nb = seq // ATTN_BLOCK
    cur = lambda b, n: (b * nb + n, 0)
    prev = lambda b, n: (b * nb + jnp.maximum(n - 1, 0), 0)
    cur_t = lambda b, n: (0, b * nb + n)
    prev_t = lambda b, n: (0, b * nb + jnp.maximum(n - 1, 0))
    const = lambda b, n: (0, 0)
    return pl.pallas_call(
        _attn_kernel,
        grid=(bsz, nb),
        in_specs=[
            pl.BlockSpec((ATTN_WIDTH, ATTN_BLOCK), cur_t),
            pl.BlockSpec((ATTN_BLOCK, KV_WIDTH), prev),
            pl.BlockSpec((ATTN_BLOCK, KV_WIDTH), cur),
            pl.BlockSpec((N_META, KV_WIDTH), const),
            pl.BlockSpec((KV_WIDTH, ATTN_BLOCK), prev_t),
            pl.BlockSpec((KV_WIDTH, ATTN_BLOCK), cur_t),
            pl.BlockSpec((KV_WIDTH, N_META), const),
            pl.BlockSpec((1, N_Q_HEADS), const),
            pl.BlockSpec((1, ATTN_WIDTH), const),
        ],
        out_specs=pl.BlockSpec((ATTN_BLOCK, ATTN_WIDTH), cur),
        out_shape=jax.ShapeDtypeStruct((bsz * seq, ATTN_WIDTH), BF16),
        scratch_shapes=[pltpu.VMEM((ATTN_WIDTH, ATTN_BLOCK), F32)],
        compiler_params=pltpu.CompilerParams(
            dimension_semantics=("arbitrary", "arbitrary"), vmem_limit_bytes=VMEM_LIMIT),
        name="attention",
    )(qt, k, k, k_meta, vt, vt, vt_meta, sinks, g_attn)


def _ssm_params(a_re, a_im, log_dt, b_re, b_im, c_re, c_im, seg_len):
    dt = jnp.exp(log_dt.astype(F32))[:, None]
    lam_re = jnp.minimum(a_re.astype(F32), -1e-4)
    lam_im = a_im.astype(F32)
    z_re, z_im = lam_re * dt, lam_im * dt
    mag = jnp.exp(z_re)
    abar_re, abar_im = mag * jnp.cos(z_im), mag * jnp.sin(z_im)
    den = lam_re * lam_re + lam_im * lam_im
    n_re, n_im = abar_re - 1.0, abar_im
    coef_re = (n_re * lam_re + n_im * lam_im) / den
    coef_im = (n_im * lam_re - n_re * lam_im) / den
    br, bi = b_re.astype(F32), b_im.astype(F32)
    bb_re = coef_re[..., None] * br - coef_im[..., None] * bi
    bb_im = coef_re[..., None] * bi + coef_im[..., None] * br

    groups_per_tile = SSM_COL // SSM_GROUP
    eye = jnp.eye(groups_per_tile, dtype=F32)

    def in_tile(bb):
        t = bb.reshape(SSM_COL_TILES, groups_per_tile, SSM_STATE, SSM_GROUP)
        t = jnp.einsum('tgpc,gh->tgchp', t, eye)
        return t.reshape(SSM_COL_TILES, SSM_COL, SSM_COL_LANES).astype(BF16)

    def out_tile(cc):
        t = cc.reshape(SSM_COL_TILES, groups_per_tile, SSM_GROUP, SSM_STATE)
        t = jnp.einsum('tgcp,gh->tgphc', t, eye)
        return t.reshape(SSM_COL_TILES, SSM_COL_LANES, SSM_COL).astype(BF16)

    def powers(exps):
        e = exps.astype(F32)[:, None, None]
        pm = jnp.exp(e * z_re[None])
        return jnp.stack([(pm * jnp.cos(e * z_im[None])).reshape(len(exps), SSM_LANES),
                          (pm * jnp.sin(e * z_im[None])).reshape(len(exps), SSM_LANES)])

    return dict(
        b_re=in_tile(bb_re), b_im=in_tile(bb_im),
        c_re=out_tile(c_re.astype(F32)), c_im=out_tile(-c_im.astype(F32)),
        a=jnp.stack([abar_re.reshape(1, SSM_LANES), abar_im.reshape(1, SSM_LANES)]),
        a_seg=powers(jnp.array([seg_len])),
        a_pow=powers(jnp.arange(1, seg_len + 1)),
        a_meta=powers(jnp.arange(N_META - 1, -1, -1)),
    )


def _gelu_tanh(t):
    return 0.5 * t * (1.0 + jnp.tanh(math.sqrt(2.0 / math.pi) * (t + 0.044715 * (t * t * t))))


def _ssm_kernel(u_ref, um_ref, bre_ref, bim_ref, cre_ref, cim_ref, a_ref, aseg_ref, apow_ref,
                ameta_ref, d_ref, wglu_ref, bglu_ref, g_ref, o_ref,
                xre, xim, hre, him, car_re, car_im, cin_re, cin_im, u_scr, y_scr):
    chunk = u_ref.shape[0]
    seg = chunk // SUBLANES

    @pl.when(pl.program_id(1) == 0)
    def _():
        um = um_ref[...].astype(BF16)
        for ct in range(SSM_COL_TILES):
            ub = um[:, ct * SSM_COL:(ct + 1) * SSM_COL]
            ls = slice(ct * SSM_COL_LANES, (ct + 1) * SSM_COL_LANES)
            xr = jnp.dot(ub, bre_ref[ct], preferred_element_type=F32)
            xi = jnp.dot(ub, bim_ref[ct], preferred_element_type=F32)
            pr = ameta_ref[0, :, ls]
            pi = ameta_ref[1, :, ls]
            car_re[:, ls] = jnp.sum(pr * xr - pi * xi, axis=0, keepdims=True)
            car_im[:, ls] = jnp.sum(pr * xi + pi * xr, axis=0, keepdims=True)

    n_lt = SSM_WIDTH // LANES
    for j in range(n_lt):
        u_scr[j] = u_ref[:, j * LANES:(j + 1) * LANES]
    up = jnp.concatenate(
        [jnp.concatenate([u_scr[j, pl.ds(k, SUBLANES, stride=seg), :] for j in range(n_lt)], axis=1)
         for k in range(seg)], axis=0).astype(BF16)
    for ct in range(SSM_COL_TILES):
        ub = up[:, ct * SSM_COL:(ct + 1) * SSM_COL]
        ls = slice(ct * SSM_COL_LANES, (ct + 1) * SSM_COL_LANES)
        xre[:, ls] = jnp.dot(ub, bre_ref[ct], preferred_element_type=F32)
        xim[:, ls] = jnp.dot(ub, bim_ref[ct], preferred_element_type=F32)

    for lb in range(SSM_LANES // SCAN_LANES):
        ls = slice(lb * SCAN_LANES, (lb + 1) * SCAN_LANES)
        ar = jnp.broadcast_to(a_ref[0, :, ls], (SUBLANES, SCAN_LANES))
        ai = jnp.broadcast_to(a_ref[1, :, ls], (SUBLANES, SCAN_LANES))

        def scan_body(k, carry):
            hr, hi = carry
            rows = pl.ds(pl.multiple_of(k * SUBLANES, SUBLANES), SUBLANES)
            nr = ar * hr - ai * hi + xre[rows, ls]
            ni = ar * hi + ai * hr + xim[rows, ls]
            xre[rows, ls] = nr
            xim[rows, ls] = ni
            return nr, ni

        zero = jnp.zeros((SUBLANES, SCAN_LANES), F32)
        lax.fori_loop(0, seg, scan_body, (zero, zero))

    cr = car_re[...]
    ci = car_im[...]
    sr = aseg_ref[0]
    si = aseg_ref[1]
    for r in range(SUBLANES):
        cin_re[r:r + 1, :] = cr
        cin_im[r:r + 1, :] = ci
        er = xre[chunk - SUBLANES + r:chunk - SUBLANES + r + 1, :]
        ei = xim[chunk - SUBLANES + r:chunk - SUBLANES + r + 1, :]
        cr, ci = sr * cr - si * ci + er, sr * ci + si * cr + ei
    car_re[...] = cr
    car_im[...] = ci

    pair = 2 * SUBLANES
    for lb in range(SSM_LANES // SCAN_LANES):
        ls = slice(lb * SCAN_LANES, (lb + 1) * SCAN_LANES)
        er = jnp.concatenate([cin_re[:, ls], cin_re[:, ls]], axis=0)
        ei = jnp.concatenate([cin_im[:, ls], cin_im[:, ls]], axis=0)

        def fix_body(k2, _):
            rows = pl.ds(pl.multiple_of(k2 * pair, pair), pair)
            k = 2 * k2
            pr = jnp.concatenate(
                [jnp.broadcast_to(apow_ref[0, pl.ds(k, 1), ls], (SUBLANES, SCAN_LANES)),
                 jnp.broadcast_to(apow_ref[0, pl.ds(k + 1, 1), ls], (SUBLANES, SCAN_LANES))], axis=0)
            pi = jnp.concatenate(
                [jnp.broadcast_to(apow_ref[1, pl.ds(k, 1), ls], (SUBLANES, SCAN_LANES)),
                 jnp.broadcast_to(apow_ref[1, pl.ds(k + 1, 1), ls], (SUBLANES, SCAN_LANES))], axis=0)
            hre[rows, ls] = (xre[rows, ls] + pr * er - pi * ei).astype(BF16)
            him[rows, ls] = (xim[rows, ls] + pr * ei + pi * er).astype(BF16)
            return 0

        lax.fori_loop(0, seg // 2, fix_body, 0)

    for ct in range(SSM_COL_TILES):
        ls = slice(ct * SSM_COL_LANES, (ct + 1) * SSM_COL_LANES)
        y = (jnp.dot(hre[:, ls], cre_ref[ct], preferred_element_type=F32)
             + jnp.dot(him[:, ls], cim_ref[ct], preferred_element_type=F32))
        for k in range(seg):
            for jj in range(SSM_COL // LANES):
                y_scr[ct * (SSM_COL // LANES) + jj, pl.ds(k, SUBLANES, stride=seg), :] = (
                    y[k * SUBLANES:(k + 1) * SUBLANES, jj * LANES:(jj + 1) * LANES])

    y = jnp.concatenate([y_scr[j] for j in range(n_lt)], axis=1)
    y = _gelu_tanh(y + d_ref[...] * u_ref[...])
    gate = jnp.dot(y.astype(BF16), wglu_ref[...], preferred_element_type=F32) + bglu_ref[...]
    y = y * _sigmoid(gate)
    o_ref[...] = _rms(y, g_ref[...]).astype(BF16)


def _ssm(u, u_meta, prm, d_skip, w_glu_bf, b_glu, g_ssm, bsz, seq, chunk):
    nc = seq // chunk
    seg = chunk // SUBLANES
    row = lambda b, c: (b * nc + c, 0)
    c2 = lambda b, c: (0, 0)
    c3 = lambda b, c: (0, 0, 0)
    return pl.pallas_call(
        _ssm_kernel,
        grid=(bsz, nc),
        in_specs=[
            pl.BlockSpec((chunk, SSM_WIDTH), row),
            pl.BlockSpec((N_META, SSM_WIDTH), c2),
            pl.BlockSpec((SSM_COL_TILES, SSM_COL, SSM_COL_LANES), c3),
            pl.BlockSpec((SSM_COL_TILES, SSM_COL, SSM_COL_LANES), c3),
            pl.BlockSpec((SSM_COL_TILES, SSM_COL_LANES, SSM_COL), c3),
            pl.BlockSpec((SSM_COL_TILES, SSM_COL_LANES, SSM_COL), c3),
            pl.BlockSpec((2, 1, SSM_LANES), c3),
            pl.BlockSpec((2, 1, SSM_LANES), c3),
            pl.BlockSpec((2, seg, SSM_LANES), c3),
            pl.BlockSpec((2, N_META, SSM_LANES), c3),
            pl.BlockSpec((1, SSM_WIDTH), c2),
            pl.BlockSpec((SSM_WIDTH, SSM_WIDTH), c2),
            pl.BlockSpec((1, SSM_WIDTH), c2),
            pl.BlockSpec((1, SSM_WIDTH), c2),
        ],
        out_specs=pl.BlockSpec((chunk, SSM_WIDTH), row),
        out_shape=jax.ShapeDtypeStruct((bsz * seq, SSM_WIDTH), BF16),
        scratch_shapes=[
            pltpu.VMEM((chunk, SSM_LANES), F32),
            pltpu.VMEM((chunk, SSM_LANES), F32),
            pltpu.VMEM((chunk, SSM_LANES), BF16),
            pltpu.VMEM((chunk, SSM_LANES), BF16),
            pltpu.VMEM((1, SSM_LANES), F32),
            pltpu.VMEM((1, SSM_LANES), F32),
            pltpu.VMEM((SUBLANES, SSM_LANES), F32),
            pltpu.VMEM((SUBLANES, SSM_LANES), F32),
            pltpu.VMEM((SSM_WIDTH // LANES, chunk, LANES), F32),
            pltpu.VMEM((SSM_WIDTH // LANES, chunk, LANES), F32),
        ],
        compiler_params=pltpu.CompilerParams(
            dimension_semantics=("arbitrary", "arbitrary"), vmem_limit_bytes=VMEM_LIMIT),
        name="ssm",
    )(u, u_meta, prm["b_re"], prm["b_im"], prm["c_re"], prm["c_im"], prm["a"], prm["a_seg"],
      prm["a_pow"], prm["a_meta"], d_skip, w_glu_bf, b_glu, g_ssm)


def _outproj_kernel(ya_ref, ys_ref, x_ref, wo_ref, bo_ref, gf_ref, wr_ref, br_ref,
                    h_ref, xn_ref, lg_ref):
    mix = (jnp.dot(ya_ref[...], wo_ref[:ATTN_WIDTH, :], preferred_element_type=F32)
           + jnp.dot(ys_ref[...], wo_ref[ATTN_WIDTH:, :], preferred_element_type=F32))
    h = x_ref[...] + mix + bo_ref[...]
    h_ref[...] = h
    n = _rms(h, gf_ref[...])
    n_hi = n.astype(BF16)
    bits = lax.bitcast_convert_type(n_hi.astype(F32), jnp.uint32)
    xn_ref[...] = (bits[:, :D_MODEL // 2] >> 16) | (bits[:, D_MODEL // 2:] & jnp.uint32(0xFFFF0000))
    n_lo = (n - n_hi.astype(F32)).astype(BF16)
    hi = jnp.dot(n_hi, wr_ref[...], preferred_element_type=F32)
    lo = jnp.dot(n_lo, wr_ref[:, :ROUTER_LANES], preferred_element_type=F32)
    lg_ref[...] = hi[:, :ROUTER_LANES] + hi[:, ROUTER_LANES:] + lo + br_ref[...]


def _outproj(y_attn, y_ssm, xf, w_out_bf, b_out, g_ffn, w_router_pad, b_router_pad, row_tile):
    rows = xf.shape[0]
    row = lambda i: (i, 0)
    const = lambda i: (0, 0)
    return pl.pallas_call(
        _outproj_kernel,
        grid=(rows // row_tile,),
        in_specs=[
            pl.BlockSpec((row_tile, ATTN_WIDTH), row),
            pl.BlockSpec((row_tile, SSM_WIDTH), row),
            pl.BlockSpec((row_tile, D_MODEL), row),
            pl.BlockSpec((D_MODEL, D_MODEL), const),
            pl.BlockSpec((1, D_MODEL), const),
            pl.BlockSpec((1, D_MODEL), const),
            pl.BlockSpec((D_MODEL, 2 * ROUTER_LANES), const),
            pl.BlockSpec((1, ROUTER_LANES), const),
        ],
        out_specs=[
            pl.BlockSpec((row_tile, D_MODEL), row),
            pl.BlockSpec((row_tile, D_MODEL // 2), row),
            pl.BlockSpec((row_tile, ROUTER_LANES), row),
        ],
        out_shape=[
            jax.ShapeDtypeStruct((rows, D_MODEL), F32),
            jax.ShapeDtypeStruct((rows, D_MODEL // 2), jnp.uint32),
            jax.ShapeDtypeStruct((rows, ROUTER_LANES), F32),
        ],
        compiler_params=pltpu.CompilerParams(
            dimension_semantics=("arbitrary",), vmem_limit_bytes=VMEM_LIMIT),
        name="outproj",
    )(y_attn, y_ssm, xf, w_out_bf, b_out, g_ffn, w_router_pad, b_router_pad)


def _moe_kernel(ie_ref, ir_ref, in_ref, ni_ref, rt_ref, rd_ref,
                xn_hbm, w1a_ref, w1b_ref, b1g_ref, b1u_ref, w2_ref, b2_ref, perm_ref, yk_hbm,
                stage, x_scr, acc, w1g, w1u, w2b, issued, sem_g, sem_out):
    i = pl.program_id(0)
    f = pl.program_id(1)
    nf = pl.num_programs(1)
    max_items = pl.num_programs(0)
    tf = w2_ref.shape[1]

    @pl.when(i < ni_ref[0])
    def _():
        r0 = ir_ref[i]
        n = in_ref[i]
        nxt = jnp.minimum(i + 1, max_items - 1)
        n_next = jnp.where(i + 1 < ni_ref[0], in_ref[nxt], 0)
        r0_next = ir_ref[nxt]

        def gather_row(first_row, j):
            tok = rt_ref[first_row + j]
            pltpu.make_async_copy(xn_hbm.at[pl.ds(tok, 1), :], stage.at[pl.ds(j, 1), :],
                                  sem_g).start()

        def gather_wait_chunk():
            pltpu.make_async_copy(xn_hbm.at[pl.ds(0, MOE_PAD), :],
                                  stage.at[pl.ds(0, MOE_PAD), :], sem_g).wait()

        def scattered(c, carry):
            pltpu.make_async_copy(acc.at[pl.ds(0, MOE_PAD), :],
                                  yk_hbm.at[pl.ds(0, MOE_PAD), :], sem_out).wait()
            return carry

        @pl.when(f == 0)
        def _():
            @pl.when(i == 0)
            def _():
                def first(j, c):
                    gather_row(r0, j)
                    return c

                lax.fori_loop(0, n, first, 0)

            def landed(c, carry):
                gather_wait_chunk()
                return carry

            lax.fori_loop(0, n // MOE_PAD, landed, 0)

            def unpack(c, carry):
                rows_c = pl.ds(pl.multiple_of(c * MOE_PAD, MOE_PAD), MOE_PAD)
                u = stage[rows_c, :]
                x_scr[rows_c, :D_MODEL // 2] = lax.bitcast_convert_type(u << 16, F32).astype(BF16)
                x_scr[rows_c, D_MODEL // 2:] = lax.bitcast_convert_type(
                    u & jnp.uint32(0xFFFF0000), F32).astype(BF16)
                return carry

            lax.fori_loop(0, n // MOE_PAD, unpack, 0)
            issued[0] = 0

            @pl.when(i > 0)
            def _():
                lax.fori_loop(0, in_ref[jnp.maximum(i - 1, 0)] // MOE_PAD, scattered, 0)

            bias = jnp.broadcast_to(b2_ref[0], (MOE_PAD, D_MODEL))

            def init(c, _):
                acc[pl.ds(pl.multiple_of(c * MOE_PAD, MOE_PAD), MOE_PAD), :] = bias
                return 0

            lax.fori_loop(0, n // MOE_PAD, init, 0)

        half = PERM_COLS // 2
        k_half = D_MODEL // 2
        for hh, w1_ref in enumerate((w1a_ref, w1b_ref)):
            for c in range(2 * tf // PERM_COLS):
                t = jnp.dot(w1_ref[0, :, c * PERM_COLS:(c + 1) * PERM_COLS].astype(BF16),
                            perm_ref[...], preferred_element_type=F32)
                rs = slice(hh * k_half, (hh + 1) * k_half)
                w1g[rs, c * half:(c + 1) * half] = t[:, :half].astype(BF16)
                w1u[rs, c * half:(c + 1) * half] = t[:, half:].astype(BF16)
        w2b[...] = w2_ref[0].astype(BF16)

        def scatter(r, rows):
            for t in range(rows):
                dst = rd_ref[r0 + r + t]
                pltpu.make_async_copy(acc.at[pl.ds(r + t, 1), :], yk_hbm.at[pl.ds(dst, 1), :],
                                      sem_out).start()

        def block(r, rows, prefetch):
            if prefetch:
                base = issued[0]
                for t in range(rows // 4):
                    gather_row(r0_next, base + t)
                issued[0] = base + rows // 4
            sl = pl.ds(r, rows)
            xb = x_scr[sl, :]
            g = jnp.dot(xb, w1g[...], preferred_element_type=F32) + b1g_ref[0]
            up = jnp.dot(xb, w1u[...], preferred_element_type=F32) + b1u_ref[0]
            g = jnp.minimum(g, SWIGLU_LIMIT)
            up = jnp.clip(up, -SWIGLU_LIMIT, SWIGLU_LIMIT)
            act = g * _sigmoid(SWIGLU_ALPHA * g) * (up + 1.0)
            acc[sl, :] += jnp.dot(act.astype(BF16), w2b[...], preferred_element_type=F32)

        def run(r, rows):
            can = issued[0] + rows // 4 <= n_next

            @pl.when(can)
            def _():
                block(r, rows, True)

            @pl.when(jnp.logical_not(can))
            def _():
                block(r, rows, False)

        n_big = n // MOE_BLOCK
        rem = n % MOE_BLOCK
        last = f == nf - 1

        def big(b, _):
            run(pl.multiple_of(b * MOE_BLOCK, MOE_BLOCK), MOE_BLOCK)
            return 0

        @pl.when(jnp.logical_not(last))
        def _():
            lax.fori_loop(0, n_big, big, 0)

        @pl.when(last)
        def _():
            for b in range(MOE_ITEM_ROWS // MOE_BLOCK):
                @pl.when(b < n_big)
                def _(b=b):
                    run(b * MOE_BLOCK, MOE_BLOCK)
                    scatter(b * MOE_BLOCK, MOE_BLOCK)

        def tail(r, rows):
            run(r, rows)

            @pl.when(last)
            def _():
                scatter(r, rows)

        @pl.when(rem >= 2 * MOE_PAD)
        def _():
            tail(pl.multiple_of(n_big * MOE_BLOCK, MOE_BLOCK), 2 * MOE_PAD)

        @pl.when(rem % (2 * MOE_PAD) != 0)
        def _():
            tail(pl.multiple_of(n - MOE_PAD, MOE_PAD), MOE_PAD)

        @pl.when(last)
        def _():
            @pl.when(i == ni_ref[0] - 1)
            def _():
                lax.fori_loop(0, n // MOE_PAD, scattered, 0)

            def rest(j, c):
                gather_row(r0_next, j)
                return c

            lax.fori_loop(issued[0], n_next, rest, 0)


def _experts(xn, item_e, item_r0, item_n, n_items, row_tok, row_dst, n_out_rows,
             w1, b1g, b1u, w2, b2, tf):
    max_items = item_e.shape[0]
    nf = D_FF // tf
    idx = jnp.arange(PERM_COLS)
    dst = jnp.where(idx % 2 == 0, idx // 2, PERM_COLS // 2 + idx // 2)
    perm = (dst[:, None] == jnp.arange(PERM_COLS)[None, :]).astype(BF16)

    def ex(i, ie, ni):
        return ie[jnp.minimum(i, ni[0] - 1)]

    def ff(i, f, ni):
        return jnp.where(i < ni[0], f, nf - 1)

    grid_spec = pltpu.PrefetchScalarGridSpec(
        num_scalar_prefetch=6,
        grid=(max_items, nf),
        in_specs=[
            pl.BlockSpec(memory_space=pl.ANY),
            pl.BlockSpec((1, D_MODEL // 2, 2 * tf), lambda i, f, ie, ir, im, ni, rt, rd: (ex(i, ie, ni), 0, ff(i, f, ni))),
            pl.BlockSpec((1, D_MODEL // 2, 2 * tf), lambda i, f, ie, ir, im, ni, rt, rd: (ex(i, ie, ni), 1, ff(i, f, ni))),
            pl.BlockSpec((1, 1, tf), lambda i, f, ie, ir, im, ni, rt, rd: (ex(i, ie, ni), 0, ff(i, f, ni))),
            pl.BlockSpec((1, 1, tf), lambda i, f, ie, ir, im, ni, rt, rd: (ex(i, ie, ni), 0, ff(i, f, ni))),
            pl.BlockSpec((1, tf, D_MODEL), lambda i, f, ie, ir, im, ni, rt, rd: (ex(i, ie, ni), ff(i, f, ni), 0)),
            pl.BlockSpec((1, 1, D_MODEL), lambda i, f, ie, ir, im, ni, rt, rd: (ex(i, ie, ni), 0, 0)),
            pl.BlockSpec((PERM_COLS, PERM_COLS), lambda i, f, ie, ir, im, ni, rt, rd: (0, 0)),
        ],
        out_specs=pl.BlockSpec(memory_space=pl.ANY),
        scratch_shapes=[
            pltpu.VMEM((MOE_ITEM_ROWS, D_MODEL // 2), jnp.uint32),
            pltpu.VMEM((MOE_ITEM_ROWS, D_MODEL), BF16),
            pltpu.VMEM((MOE_ITEM_ROWS, D_MODEL), F32),
            pltpu.VMEM((D_MODEL, tf), BF16),
            pltpu.VMEM((D_MODEL, tf), BF16),
            pltpu.VMEM((tf, D_MODEL), BF16),
            pltpu.SMEM((1,), jnp.int32),
            pltpu.SemaphoreType.DMA(()),
            pltpu.SemaphoreType.DMA(()),
        ],
    )
    return pl.pallas_call(
        _moe_kernel,
        grid_spec=grid_spec,
        out_shape=jax.ShapeDtypeStruct((n_out_rows, D_MODEL), F32),
        compiler_params=pltpu.CompilerParams(
            dimension_semantics=("arbitrary", "arbitrary"), vmem_limit_bytes=VMEM_LIMIT),
        name="experts",
    )(item_e, item_r0, item_n, n_items, row_tok, row_dst, xn, w1, w1, b1g, b1u, w2, b2, perm)


def _combine_kernel(h_ref, y0_ref, y1_ref, y2_ref, y3_ref, gate_ref, g_ref, o_ref):
    acc = h_ref[...]
    gates = gate_ref[...]
    for k, y_ref in enumerate((y0_ref, y1_ref, y2_ref, y3_ref)):
        acc = acc + gates[:, k:k + 1] * y_ref[...]
    o_ref[...] = _rms(acc, g_ref[...])


def _combine(h1, yk, gates_pad, g_final, row_tile):
    rows = h1.shape[0]
    n_tiles = rows // row_tile
    row = lambda i: (i, 0)
    const = lambda i: (0, 0)
    slot = lambda k: (lambda i: (k * n_tiles + i, 0))
    return pl.pallas_call(
        _combine_kernel,
        grid=(n_tiles,),
        in_specs=[pl.BlockSpec((row_tile, D_MODEL), row)]
        + [pl.BlockSpec((row_tile, D_MODEL), slot(k)) for k in range(TOP_K)]
        + [pl.BlockSpec((row_tile, LANES), row), pl.BlockSpec((1, D_MODEL), const)],
        out_specs=pl.BlockSpec((row_tile, D_MODEL), row),
        out_shape=jax.ShapeDtypeStruct((rows, D_MODEL), F32),
        compiler_params=pltpu.CompilerParams(
            dimension_semantics=("arbitrary",), vmem_limit_bytes=VMEM_LIMIT),
        name="combine",
    )(h1, yk, yk, yk, yk, gates_pad, g_final)


def kernel(x, meta_tokens, g_mix, w_in, b_in, attn_sinks, ssm_a_re, ssm_a_im, ssm_log_dt,
           ssm_b_re, ssm_b_im, ssm_c_re, ssm_c_im, ssm_d, w_glu, b_glu, g_attn_out, g_ssm_out,
           w_out, b_out, g_ffn, w_router, b_router, w_mlp1, b_mlp1, w_mlp2, b_mlp2, g_final):
    bsz, seq, _ = x.shape
    rows = bsz * seq
    row_tile = min(ROW_TILE, seq)
    chunk = min(SSM_CHUNK, seq)
    assert seq % ATTN_BLOCK == 0 and seq % row_tile == 0 and seq % chunk == 0
    xf = x.reshape(rows, D_MODEL)

    w_in_bf = w_in[0].astype(BF16)
    cos_r, sin_r = _rope_tables(N_META + jnp.arange(seq))
    cos_m, sin_m = _rope_tables(jnp.arange(N_META))
    qt, k, vt, u = _inproj(xf, g_mix, w_in_bf, b_in, cos_r, sin_r, row_tile, True)
    _, k_meta, v_meta, u_meta = _inproj(meta_tokens, g_mix, w_in_bf, b_in, cos_m, sin_m,
                                        N_META, False)

    y_attn = _attention(qt, k, vt, k_meta, v_meta.T, attn_sinks, g_attn_out, bsz, seq)

    prm = _ssm_params(ssm_a_re[0], ssm_a_im[0], ssm_log_dt[0], ssm_b_re[0], ssm_b_im[0],
                      ssm_c_re[0], ssm_c_im[0], chunk // SUBLANES)
    y_ssm = _ssm(u, u_meta, prm, ssm_d, w_glu[0].astype(BF16), b_glu, g_ssm_out, bsz, seq, chunk)

    w_router_pad = jnp.pad(w_router[0], ((0, 0), (0, ROUTER_LANES - N_EXPERTS)))
    b_router_pad = jnp.pad(b_router, ((0, 0), (0, ROUTER_LANES - N_EXPERTS)))
    w_router_hi = w_router_pad.astype(BF16)
    w_router_lo = (w_router_pad - w_router_hi.astype(F32)).astype(BF16)
    w_router_pad = jnp.concatenate([w_router_hi, w_router_lo], axis=1)
    h1, xn, logits = _outproj(y_attn, y_ssm, xf, w_out[0].astype(BF16), b_out, g_ffn,
                              w_router_pad, b_router_pad, row_tile)

    top_val, top_idx = lax.top_k(logits[:, :N_EXPERTS], TOP_K)
    gates = jax.nn.softmax(top_val, axis=-1)
    chosen = (top_idx[:, :, None] == jnp.arange(N_EXPERTS)[None, None, :])
    counts = jnp.sum(chosen.astype(jnp.int32), axis=(0, 1))
    padded = ((counts + MOE_PAD - 1) // MOE_PAD) * MOE_PAD
    pad_ends = jnp.cumsum(padded)
    pad_starts = pad_ends - padded
    n_assign = rows * TOP_K
    n_rows = n_assign + MOE_DUMP_ROWS
    order = jnp.argsort(top_idx.reshape(-1), stable=True).astype(jnp.int32)
    starts = jnp.cumsum(counts) - counts
    r = jnp.arange(n_rows, dtype=jnp.int32)
    r_e = jnp.minimum(jnp.sum(r[:, None] >= pad_ends[None, :], axis=1), N_EXPERTS - 1)
    r_j = r - pad_starts[r_e]
    real = r_j < counts[r_e]
    assign = order[jnp.minimum(starts[r_e] + r_j, n_assign - 1)]
    row_tok = jnp.where(real, assign // TOP_K, 0).astype(jnp.int32)
    dump = n_assign + r_e * MOE_PAD + jnp.clip(r_j - counts[r_e], 0, MOE_PAD - 1)
    row_dst = jnp.where(real, (assign % TOP_K) * rows + assign // TOP_K, dump).astype(jnp.int32)
    per_e = (padded + MOE_ITEM_ROWS - 1) // MOE_ITEM_ROWS
    item_ends = jnp.cumsum(per_e)
    max_items = N_EXPERTS + n_assign // MOE_ITEM_ROWS
    slot = jnp.arange(max_items, dtype=jnp.int32)
    item_e = jnp.minimum(jnp.searchsorted(item_ends, slot, side='right'), N_EXPERTS - 1)
    piece = slot - (item_ends - per_e)[item_e]
    item_r0 = (pad_starts[item_e] + piece * MOE_ITEM_ROWS).astype(jnp.int32)
    item_n = jnp.clip(padded[item_e] - piece * MOE_ITEM_ROWS, 0, MOE_ITEM_ROWS).astype(jnp.int32)
    n_items = item_ends[-1].astype(jnp.int32).reshape(1)

    tf = MOE_FF_TILE
    b1 = b_mlp1[0].reshape(N_EXPERTS, 1, D_FF, 2)
    b2 = b_mlp2[0].reshape(N_EXPERTS, 1, D_MODEL)
    yk = _experts(xn, item_e.astype(jnp.int32), item_r0, item_n, n_items, row_tok, row_dst,
                  n_assign + MOE_DUMP_ROWS, w_mlp1[0], b1[..., 0], b1[..., 1], w_mlp2[0], b2, tf)

    gates_pad = jnp.pad(gates, ((0, 0), (0, LANES - TOP_K)))
    out = _combine(h1, yk, gates_pad, g_final.reshape(1, D_MODEL), row_tile)
    return out.reshape(bsz, seq, D_MODEL)
```

```python
import functools
import math

import jax
import jax.numpy as jnp
from jax import lax
from jax.experimental import pallas as pl
from jax.experimental.pallas import tpu as pltpu

F32 = jnp.float32
BF16 = jnp.bfloat16

D_MODEL = 2048
N_META = 16
HEAD_DIM = 64
ATTN_WIDTH = 1024
N_Q_HEADS = 16
N_KV_HEADS = 4
Q_PER_KV = 4
KV_WIDTH = 256
ATTN_BLOCK = 128
ROPE_THETA = 10000.0
SSM_WIDTH = 1024
SSM_GROUP = 16
N_SSM_GROUPS = 64
SSM_STATE = 64
SSM_LANES = N_SSM_GROUPS * SSM_STATE
IN_WIDTH = 2560
N_EXPERTS = 32
TOP_K = 4
D_FF = 2048
SWIGLU_LIMIT = 7.0
SWIGLU_ALPHA = 1.702
NORM_EPS = 1e-5

LANES = 128
SUBLANES = 8
VMEM_LIMIT = 58 * 1024 * 1024

ROW_TILE = 512
SSM_CHUNK = 256
SSM_COL = 256
SSM_COL_TILES = SSM_WIDTH // SSM_COL
SSM_COL_LANES = SSM_LANES // SSM_COL_TILES
SCAN_LANES = 512
MOE_PAD = 128
MOE_BLOCK = 512
MOE_ITEM_ROWS = 2176
MOE_FF_TILE = 256
MOE_DUMP_ROWS = N_EXPERTS * MOE_PAD
ROUTER_LANES = 128
PERM_COLS = 256


def _rms(t, gain):
    return t * lax.rsqrt(jnp.mean(t * t, axis=-1, keepdims=True) + NORM_EPS) * gain


def _sigmoid(t):
    return 1.0 / (1.0 + jnp.exp(-t))


def _inproj_kernel(x_ref, g_ref, w_ref, b_ref, cos_ref, sin_ref, q_ref, k_ref, v_ref, u_ref,
                   *, transposed):
    n = _rms(x_ref[...], g_ref[...]).astype(BF16)
    cos = cos_ref[...]
    sin = sin_ref[...]
    lane = lax.broadcasted_iota(jnp.int32, cos.shape, 1)
    first_half = (lane % HEAD_DIM) < (HEAD_DIM // 2)

    def proj(c0, c1):
        return jnp.dot(n, w_ref[:, c0:c1], preferred_element_type=F32) + b_ref[:, c0:c1]

    def rope(t):
        partner = jnp.where(first_half, pltpu.roll(t, LANES - HEAD_DIM // 2, 1),
                            pltpu.roll(t, HEAD_DIM // 2, 1))
        return t * cos + partner * sin

    def put(ref, j, t):
        if transposed:
            ref[j * LANES:(j + 1) * LANES, :] = t.T.astype(BF16)
        else:
            ref[:, j * LANES:(j + 1) * LANES] = t.astype(BF16)

    def proj_pair(c0):
        z = proj(c0, c0 + 2 * LANES)
        return z[:, :LANES], z[:, LANES:]

    scale = HEAD_DIM ** -0.5
    for jj in range(ATTN_WIDTH // (2 * LANES)):
        a, b = proj_pair(jj * 2 * LANES)
        put(q_ref, 2 * jj, rope(a) * scale)
        put(q_ref, 2 * jj + 1, rope(b) * scale)
    a, b = proj_pair(ATTN_WIDTH)
    k_ref[:, :LANES] = rope(a).astype(BF16)
    k_ref[:, LANES:] = rope(b).astype(BF16)
    a, b = proj_pair(ATTN_WIDTH + KV_WIDTH)
    put(v_ref, 0, a)
    put(v_ref, 1, b)
    c0 = ATTN_WIDTH + 2 * KV_WIDTH
    u_ref[...] = proj(c0, c0 + SSM_WIDTH)


def _inproj(xf, g_mix, w_in_bf, b_in, cos_t, sin_t, row_tile, transposed):
    rows = xf.shape[0]
    tab_blocks = cos_t.shape[0] // row_tile
    row = lambda i: (i, 0)
    col = lambda i: (0, i)
    tab = lambda i: (i % tab_blocks, 0)
    const = lambda i: (0, 0)
    if transposed:
        q_spec, q_shape = pl.BlockSpec((ATTN_WIDTH, row_tile), col), (ATTN_WIDTH, rows)
        v_spec, v_shape = pl.BlockSpec((KV_WIDTH, row_tile), col), (KV_WIDTH, rows)
    else:
        q_spec, q_shape = pl.BlockSpec((row_tile, ATTN_WIDTH), row), (rows, ATTN_WIDTH)
        v_spec, v_shape = pl.BlockSpec((row_tile, KV_WIDTH), row), (rows, KV_WIDTH)
    return pl.pallas_call(
        functools.partial(_inproj_kernel, transposed=transposed),
        grid=(rows // row_tile,),
        in_specs=[
            pl.BlockSpec((row_tile, D_MODEL), row),
            pl.BlockSpec((1, D_MODEL), const),
            pl.BlockSpec((D_MODEL, IN_WIDTH), const),
            pl.BlockSpec((1, IN_WIDTH), const),
            pl.BlockSpec((row_tile, LANES), tab),
            pl.BlockSpec((row_tile, LANES), tab),
        ],
        out_specs=[
            q_spec,
            pl.BlockSpec((row_tile, KV_WIDTH), row),
            v_spec,
            pl.BlockSpec((row_tile, SSM_WIDTH), row),
        ],
        out_shape=[
            jax.ShapeDtypeStruct(q_shape, BF16),
            jax.ShapeDtypeStruct((rows, KV_WIDTH), BF16),
            jax.ShapeDtypeStruct(v_shape, BF16),
            jax.ShapeDtypeStruct((rows, SSM_WIDTH), F32),
        ],
        compiler_params=pltpu.CompilerParams(
            dimension_semantics=("arbitrary",), vmem_limit_bytes=VMEM_LIMIT),
        name="inproj",
    )(xf, g_mix, w_in_bf, b_in, cos_t, sin_t)


def _rope_tables(positions):
    half = HEAD_DIM // 2
    inv_freq = jnp.power(ROPE_THETA, -jnp.arange(half, dtype=F32) / half)
    ang = positions.astype(F32)[:, None] * inv_freq[None, :]
    cos = jnp.tile(jnp.cos(ang), (1, LANES // half))
    sin = jnp.tile(jnp.sin(ang), (1, LANES // half))
    sign = jnp.where((jnp.arange(LANES) % HEAD_DIM) < half, -1.0, 1.0).astype(F32)
    return cos, sin * sign[None, :]


def _attn_kernel(qt_ref, kp_ref, kc_ref, km_ref, vtp_ref, vtc_ref, vtm_ref, sink_ref, g_ref, o_ref,
                 yt_scr):
    n = pl.program_id(1)
    n_keys = 2 * ATTN_BLOCK + N_META
    key = lax.broadcasted_iota(jnp.int32, (n_keys, ATTN_BLOCK), 0)
    qi = lax.broadcasted_iota(jnp.int32, (n_keys, ATTN_BLOCK), 1)
    no_prev = jnp.where(n > 0, 0, ATTN_BLOCK)
    cur_j = key - ATTN_BLOCK
    valid = ((key >= 2 * ATTN_BLOCK) | ((cur_j >= 0) & (cur_j <= qi))
             | ((key < ATTN_BLOCK) & (key > qi + no_prev)))
    bias = jnp.where(valid, 0.0, -1e30)
    bias = jnp.concatenate([bias] * Q_PER_KV, axis=1)
    zeros = jnp.zeros((HEAD_DIM, ATTN_BLOCK), BF16)
    ssq = jnp.zeros((1, ATTN_BLOCK), F32)
    for hk in range(N_KV_HEADS):
        lt = slice((hk // 2) * LANES, (hk // 2 + 1) * LANES)
        kt = jnp.concatenate([kp_ref[:, lt], kc_ref[:, lt], km_ref[:, lt]], axis=0)
        rs = slice(hk * HEAD_DIM, (hk + 1) * HEAD_DIM)
        vt = jnp.concatenate([vtp_ref[rs, :], vtc_ref[rs, :], vtm_ref[rs, :]], axis=1)
        qs, sinks = [], []
        for g in range(Q_PER_KV):
            h = hk * Q_PER_KV + g
            qh = qt_ref[h * HEAD_DIM:(h + 1) * HEAD_DIM, :]
            qs.append(jnp.concatenate([qh, zeros] if hk % 2 == 0 else [zeros, qh], axis=0))
            sinks.append(jnp.broadcast_to(sink_ref[:, h:h + 1], (1, ATTN_BLOCK)))
        sink = jnp.concatenate(sinks, axis=1)
        s = jnp.dot(kt, jnp.concatenate(qs, axis=1), preferred_element_type=F32) + bias
        m = jnp.maximum(jnp.max(s, axis=0, keepdims=True), sink)
        p = jnp.exp(s - m)
        denom = jnp.sum(p, axis=0, keepdims=True) + jnp.exp(sink - m)
        o = jnp.dot(vt, p.astype(BF16), preferred_element_type=F32) * (1.0 / denom)
        for g in range(Q_PER_KV):
            h = hk * Q_PER_KV + g
            og = o[:, g * ATTN_BLOCK:(g + 1) * ATTN_BLOCK]
            yt_scr[h * HEAD_DIM:(h + 1) * HEAD_DIM, :] = og
            ssq = ssq + jnp.sum(og * og, axis=0, keepdims=True)
    inv = lax.rsqrt(ssq * (1.0 / ATTN_WIDTH) + NORM_EPS)
    o_ref[...] = ((yt_scr[...] * inv).T * g_ref[...]).astype(BF16)


def _attention(qt, k, vt, k_meta, vt_meta, sinks, g_attn, bsz, seq):
    nb = seq // ATTN_BLOCK
    cur = lambda b, n: (b * nb + n, 0)
    prev = lambda b, n: (b * nb + jnp.maximum(n - 1, 0), 0)
    cur_t = lambda b, n: (0, b * nb + n)
    prev_t = lambda b, n: (0, b * nb + jnp.maximum(n - 1, 0))
    const = lambda b, n: (0, 0)
    return pl.pallas_call(
        _attn_kernel,
        grid=(bsz, nb),
        in_specs=[
            pl.BlockSpec((ATTN_WIDTH, ATTN_BLOCK), cur_t),
            pl.BlockSpec((ATTN_BLOCK, KV_WIDTH), prev),
            pl.BlockSpec((ATTN_BLOCK, KV_WIDTH), cur),
            pl.BlockSpec((N_META, KV_WIDTH), const),
            pl.BlockSpec((KV_WIDTH, ATTN_BLOCK), prev_t),
            pl.BlockSpec((KV_WIDTH, ATTN_BLOCK), cur_t),
            pl.BlockSpec((KV_WIDTH, N_META), const),
            pl.BlockSpec((1, N_Q_HEADS), const),
            pl.BlockSpec((1, ATTN_WIDTH), const),
        ],
        out_specs=pl.BlockSpec((ATTN_BLOCK, ATTN_WIDTH), cur),
        out_shape=jax.ShapeDtypeStruct((bsz * seq, ATTN_WIDTH), BF16),
        scratch_shapes=[pltpu.VMEM((ATTN_WIDTH, ATTN_BLOCK), F32)],
        compiler_params=pltpu.CompilerParams(
            dimension_semantics=("arbitrary", "arbitrary"), vmem_limit_bytes=VMEM_LIMIT),
        name="attention",
    )(qt, k, k, k_meta, vt, vt, vt_meta, sinks, g_attn)


def _ssm_params(a_re, a_im, log_dt, b_re, b_im, c_re, c_im, seg_len):
    dt = jnp.exp(log_dt.astype(F32))[:, None]
    lam_re = jnp.minimum(a_re.astype(F32), -1e-4)
    lam_im = a_im.astype(F32)
    z_re, z_im = lam_re * dt, lam_im * dt
    mag = jnp.exp(z_re)
    abar_re, abar_im = mag * jnp.cos(z_im), mag * jnp.sin(z_im)
    den = lam_re * lam_re + lam_im * lam_im
    n_re, n_im = abar_re - 1.0, abar_im
    coef_re = (n_re * lam_re + n_im * lam_im) / den
    coef_im = (n_im * lam_re - n_re * lam_im) / den
    br, bi = b_re.astype(F32), b_im.astype(F32)
    bb_re = coef_re[..., None] * br - coef_im[..., None] * bi
    bb_im = coef_re[..., None] * bi + coef_im[..., None] * br

    groups_per_tile = SSM_COL // SSM_GROUP
    eye = jnp.eye(groups_per_tile, dtype=F32)

    def in_tile(bb):
        t = bb.reshape(SSM_COL_TILES, groups_per_tile, SSM_STATE, SSM_GROUP)
        t = jnp.einsum('tgpc,gh->tgchp', t, eye)
        return t.reshape(SSM_COL_TILES, SSM_COL, SSM_COL_LANES).astype(BF16)

    def out_tile(cc):
        t = cc.reshape(SSM_COL_TILES, groups_per_tile, SSM_GROUP, SSM_STATE)
        t = jnp.einsum('tgcp,gh->tgphc', t, eye)
        return t.reshape(SSM_COL_TILES, SSM_COL_LANES, SSM_COL).astype(BF16)

    def powers(exps):
        e = exps.astype(F32)[:, None, None]
        pm = jnp.exp(e * z_re[None])
        return jnp.stack([(pm * jnp.cos(e * z_im[None])).reshape(len(exps), SSM_LANES),
                          (pm * jnp.sin(e * z_im[None])).reshape(len(exps), SSM_LANES)])

    return dict(
        b_re=in_tile(bb_re), b_im=in_tile(bb_im),
        c_re=out_tile(c_re.astype(F32)), c_im=out_tile(-c_im.astype(F32)),
        a=jnp.stack([abar_re.reshape(1, SSM_LANES), abar_im.reshape(1, SSM_LANES)]),
        a_seg=powers(jnp.array([seg_len])),
        a_meta=powers(jnp.arange(N_META - 1, -1, -1)),
    )


def _gelu_tanh(t):
    return 0.5 * t * (1.0 + jnp.tanh(math.sqrt(2.0 / math.pi) * (t + 0.044715 * (t * t * t))))


def _ssm_kernel(u_ref, um_ref, bre_ref, bim_ref, cre_ref, cim_ref, a_ref, aseg_ref,
                ameta_ref, perm_ref, d_ref, wglu_ref, bglu_ref, g_ref, o_ref,
                xre, xim, hre, him, car_re, car_im, cin_re, cin_im, y_scr):
    chunk = u_ref.shape[0]
    seg = chunk // SUBLANES

    @pl.when(pl.program_id(1) == 0)
    def _():
        um = um_ref[...].astype(BF16)
        for ct in range(SSM_COL_TILES):
            ub = um[:, ct * SSM_COL:(ct + 1) * SSM_COL]
            ls = slice(ct * SSM_COL_LANES, (ct + 1) * SSM_COL_LANES)
            xr = jnp.dot(ub, bre_ref[ct], preferred_element_type=F32)
            xi = jnp.dot(ub, bim_ref[ct], preferred_element_type=F32)
            pr = ameta_ref[0, :, ls]
            pi = ameta_ref[1, :, ls]
            car_re[:, ls] = jnp.sum(pr * xr - pi * xi, axis=0, keepdims=True)
            car_im[:, ls] = jnp.sum(pr * xi + pi * xr, axis=0, keepdims=True)

    up = jnp.dot(perm_ref[0], u_ref[...].astype(BF16), preferred_element_type=F32).astype(BF16)
    for ct in range(SSM_COL_TILES):
        ub = up[:, ct * SSM_COL:(ct + 1) * SSM_COL]
        ls = slice(ct * SSM_COL_LANES, (ct + 1) * SSM_COL_LANES)
        xre[:, ls] = jnp.dot(ub, bre_ref[ct], preferred_element_type=F32)
        xim[:, ls] = jnp.dot(ub, bim_ref[ct], preferred_element_type=F32)

    for lb in range(SSM_LANES // SCAN_LANES):
        ls = slice(lb * SCAN_LANES, (lb + 1) * SCAN_LANES)
        ar = jnp.broadcast_to(a_ref[0, :, ls], (SUBLANES, SCAN_LANES))
        ai = jnp.broadcast_to(a_ref[1, :, ls], (SUBLANES, SCAN_LANES))

        def end_body(k, carry):
            hr, hi = carry
            rows = pl.ds(pl.multiple_of(k * SUBLANES, SUBLANES), SUBLANES)
            return ar * hr - ai * hi + xre[rows, ls], ar * hi + ai * hr + xim[rows, ls]

        zero = jnp.zeros((SUBLANES, SCAN_LANES), F32)
        er, ei = lax.fori_loop(0, seg, end_body, (zero, zero))
        cin_re[:, ls] = er
        cin_im[:, ls] = ei

    cr = car_re[...]
    ci = car_im[...]
    sr = aseg_ref[0]
    si = aseg_ref[1]
    for r in range(SUBLANES):
        er = cin_re[r:r + 1, :]
        ei = cin_im[r:r + 1, :]
        cin_re[r:r + 1, :] = cr
        cin_im[r:r + 1, :] = ci
        cr, ci = sr * cr - si * ci + er, sr * ci + si * cr + ei
    car_re[...] = cr
    car_im[...] = ci

    pair = 2 * SUBLANES
    for lb in range(SSM_LANES // SCAN_LANES):
        ls = slice(lb * SCAN_LANES, (lb + 1) * SCAN_LANES)
        ar = jnp.broadcast_to(a_ref[0, :, ls], (SUBLANES, SCAN_LANES))
        ai = jnp.broadcast_to(a_ref[1, :, ls], (SUBLANES, SCAN_LANES))

        def scan_body(k2, carry):
            hr, hi = carry
            r0 = pl.multiple_of(k2 * pair, pair)
            first = pl.ds(r0, SUBLANES)
            second = pl.ds(r0 + SUBLANES, SUBLANES)
            h0r = ar * hr - ai * hi + xre[first, ls]
            h0i = ar * hi + ai * hr + xim[first, ls]
            h1r = ar * h0r - ai * h0i + xre[second, ls]
            h1i = ar * h0i + ai * h0r + xim[second, ls]
            hre[pl.ds(r0, pair), ls] = jnp.concatenate([h0r, h1r], axis=0).astype(BF16)
            him[pl.ds(r0, pair), ls] = jnp.concatenate([h0i, h1i], axis=0).astype(BF16)
            return h1r, h1i

        lax.fori_loop(0, seg // 2, scan_body, (cin_re[:, ls], cin_im[:, ls]))

    n_lt = SSM_WIDTH // LANES
    for ct in range(SSM_COL_TILES):
        ls = slice(ct * SSM_COL_LANES, (ct + 1) * SSM_COL_LANES)
        y = (jnp.dot(hre[:, ls], cre_ref[ct], preferred_element_type=F32)
             + jnp.dot(him[:, ls], cim_ref[ct], preferred_element_type=F32))
        for k in range(seg):
            for jj in range(SSM_COL // LANES):
                y_scr[ct * (SSM_COL // LANES) + jj, pl.ds(k, SUBLANES, stride=seg), :] = (
                    y[k * SUBLANES:(k + 1) * SUBLANES, jj * LANES:(jj + 1) * LANES])

    y = jnp.concatenate([y_scr[j] for j in range(n_lt)], axis=1)
    y = _gelu_tanh(y + d_ref[...] * u_ref[...])
    gate = jnp.dot(y.astype(BF16), wglu_ref[...], preferred_element_type=F32) + bglu_ref[...]
    y = y * _sigmoid(gate)
    o_ref[...] = _rms(y, g_ref[...]).astype(BF16)


def _ssm(u, u_meta, prm, d_skip, w_glu_bf, b_glu, g_ssm, bsz, seq, chunk):
    nc = seq // chunk
    seg = chunk // SUBLANES
    row = lambda b, c: (b * nc + c, 0)
    c2 = lambda b, c: (0, 0)
    c3 = lambda b, c: (0, 0, 0)
    p = jnp.arange(chunk)
    fwd = ((p % SUBLANES) * seg + p // SUBLANES)[:, None] == p[None, :]
    perm = fwd.astype(BF16)[None]
    return pl.pallas_call(
        _ssm_kernel,
        grid=(bsz, nc),
        in_specs=[
            pl.BlockSpec((chunk, SSM_WIDTH), row),
            pl.BlockSpec((N_META, SSM_WIDTH), c2),
            pl.BlockSpec((SSM_COL_TILES, SSM_COL, SSM_COL_LANES), c3),
            pl.BlockSpec((SSM_COL_TILES, SSM_COL, SSM_COL_LANES), c3),
            pl.BlockSpec((SSM_COL_TILES, SSM_COL_LANES, SSM_COL), c3),
            pl.BlockSpec((SSM_COL_TILES, SSM_COL_LANES, SSM_COL), c3),
            pl.BlockSpec((2, 1, SSM_LANES), c3),
            pl.BlockSpec((2, 1, SSM_LANES), c3),
            pl.BlockSpec((2, N_META, SSM_LANES), c3),
            pl.BlockSpec((1, chunk, chunk), c3),
            pl.BlockSpec((1, SSM_WIDTH), c2),
            pl.BlockSpec((SSM_WIDTH, SSM_WIDTH), c2),
            pl.BlockSpec((1, SSM_WIDTH), c2),
            pl.BlockSpec((1, SSM_WIDTH), c2),
        ],
        out_specs=pl.BlockSpec((chunk, SSM_WIDTH), row),
        out_shape=jax.ShapeDtypeStruct((bsz * seq, SSM_WIDTH), BF16),
        scratch_shapes=[
            pltpu.VMEM((chunk, SSM_LANES), F32),
            pltpu.VMEM((chunk, SSM_LANES), F32),
            pltpu.VMEM((chunk, SSM_LANES), BF16),
            pltpu.VMEM((chunk, SSM_LANES), BF16),
            pltpu.VMEM((1, SSM_LANES), F32),
            pltpu.VMEM((1, SSM_LANES), F32),
            pltpu.VMEM((SUBLANES, SSM_LANES), F32),
            pltpu.VMEM((SUBLANES, SSM_LANES), F32),
            pltpu.VMEM((SSM_WIDTH // LANES, chunk, LANES), F32),
        ],
        compiler_params=pltpu.CompilerParams(
            dimension_semantics=("arbitrary", "arbitrary"), vmem_limit_bytes=VMEM_LIMIT),
        name="ssm",
    )(u, u_meta, prm["b_re"], prm["b_im"], prm["c_re"], prm["c_im"], prm["a"], prm["a_seg"],
      prm["a_meta"], perm, d_skip, w_glu_bf, b_glu, g_ssm)


def _outproj_kernel(ya_ref, ys_ref, x_ref, wo_ref, bo_ref, gf_ref, wr_ref, br_ref,
                    h_ref, xn_ref, lg_ref):
    mix = (jnp.dot(ya_ref[...], wo_ref[:ATTN_WIDTH, :], preferred_element_type=F32)
           + jnp.dot(ys_ref[...], wo_ref[ATTN_WIDTH:, :], preferred_element_type=F32))
    h = x_ref[...] + mix + bo_ref[...]
    h_ref[...] = h
    n = _rms(h, gf_ref[...])
    n_hi = n.astype(BF16)
    bits = lax.bitcast_convert_type(n_hi.astype(F32), jnp.uint32)
    xn_ref[...] = (bits[:, :D_MODEL // 2] >> 16) | (bits[:, D_MODEL // 2:] & jnp.uint32(0xFFFF0000))
    n_lo = (n - n_hi.astype(F32)).astype(BF16)
    hi = jnp.dot(n_hi, wr_ref[...], preferred_element_type=F32)
    lo = jnp.dot(n_lo, wr_ref[:, :ROUTER_LANES], preferred_element_type=F32)
    lg_ref[...] = hi[:, :ROUTER_LANES] + hi[:, ROUTER_LANES:] + lo + br_ref[...]


def _outproj(y_attn, y_ssm, xf, w_out_bf, b_out, g_ffn, w_router_pad, b_router_pad, row_tile):
    rows = xf.shape[0]
    row = lambda i: (i, 0)
    const = lambda i: (0, 0)
    return pl.pallas_call(
        _outproj_kernel,
        grid=(rows // row_tile,),
        in_specs=[
            pl.BlockSpec((row_tile, ATTN_WIDTH), row),
            pl.BlockSpec((row_tile, SSM_WIDTH), row),
            pl.BlockSpec((row_tile, D_MODEL), row),
            pl.BlockSpec((D_MODEL, D_MODEL), const),
            pl.BlockSpec((1, D_MODEL), const),
            pl.BlockSpec((1, D_MODEL), const),
            pl.BlockSpec((D_MODEL, 2 * ROUTER_LANES), const),
            pl.BlockSpec((1, ROUTER_LANES), const),
        ],
        out_specs=[
            pl.BlockSpec((row_tile, D_MODEL), row),
            pl.BlockSpec((row_tile, D_MODEL // 2), row),
            pl.BlockSpec((row_tile, ROUTER_LANES), row),
        ],
        out_shape=[
            jax.ShapeDtypeStruct((rows, D_MODEL), F32),
            jax.ShapeDtypeStruct((rows, D_MODEL // 2), jnp.uint32),
            jax.ShapeDtypeStruct((rows, ROUTER_LANES), F32),
        ],
        compiler_params=pltpu.CompilerParams(
            dimension_semantics=("arbitrary",), vmem_limit_bytes=VMEM_LIMIT),
        name="outproj",
    )(y_attn, y_ssm, xf, w_out_bf, b_out, g_ffn, w_router_pad, b_router_pad)


def _moe_kernel(ie_ref, ir_ref, in_ref, ni_ref, rt_ref, rd_ref,
                xn_hbm, w1a_ref, w1b_ref, b1g_ref, b1u_ref, w2_ref, b2_ref, perm_ref, yk_hbm,
                stage, x_scr, acc, w1g, w1u, w2b, issued, sem_g, sem_out):
    i = pl.program_id(0)
    f = pl.program_id(1)
    nf = pl.num_programs(1)
    max_items = pl.num_programs(0)
    tf = w2_ref.shape[1]

    @pl.when(i < ni_ref[0])
    def _():
        r0 = ir_ref[i]
        n = in_ref[i]
        nxt = jnp.minimum(i + 1, max_items - 1)
        n_next = jnp.where(i + 1 < ni_ref[0], in_ref[nxt], 0)
        r0_next = ir_ref[nxt]

        def gather_row(first_row, j):
            tok = rt_ref[first_row + j]
            pltpu.make_async_copy(xn_hbm.at[pl.ds(tok, 1), :], stage.at[pl.ds(j, 1), :],
                                  sem_g).start()

        def gather_wait_chunk():
            pltpu.make_async_copy(xn_hbm.at[pl.ds(0, MOE_PAD), :],
                                  stage.at[pl.ds(0, MOE_PAD), :], sem_g).wait()

        def scattered(c, carry):
            pltpu.make_async_copy(acc.at[pl.ds(0, MOE_PAD), :],
                                  yk_hbm.at[pl.ds(0, MOE_PAD), :], sem_out).wait()
            return carry

        @pl.when(f == 0)
        def _():
            @pl.when(i == 0)
            def _():
                def first(j, c):
                    gather_row(r0, j)
                    return c

                lax.fori_loop(0, n, first, 0)

            def landed(c, carry):
                gather_wait_chunk()
                return carry

            lax.fori_loop(0, n // MOE_PAD, landed, 0)

            def unpack(c, carry):
                rows_c = pl.ds(pl.multiple_of(c * MOE_PAD, MOE_PAD), MOE_PAD)
                u = stage[rows_c, :]
                x_scr[rows_c, :D_MODEL // 2] = lax.bitcast_convert_type(u << 16, F32).astype(BF16)
                x_scr[rows_c, D_MODEL // 2:] = lax.bitcast_convert_type(
                    u & jnp.uint32(0xFFFF0000), F32).astype(BF16)
                return carry

            lax.fori_loop(0, n // MOE_PAD, unpack, 0)
            issued[0] = 0

            @pl.when(i > 0)
            def _():
                lax.fori_loop(0, in_ref[jnp.maximum(i - 1, 0)] // MOE_PAD, scattered, 0)

            bias = jnp.broadcast_to(b2_ref[0], (MOE_PAD, D_MODEL))

            def init(c, _):
                acc[pl.ds(pl.multiple_of(c * MOE_PAD, MOE_PAD), MOE_PAD), :] = bias
                return 0

            lax.fori_loop(0, n // MOE_PAD, init, 0)

        half = PERM_COLS // 2
        k_half = D_MODEL // 2
        for hh, w1_ref in enumerate((w1a_ref, w1b_ref)):
            for c in range(2 * tf // PERM_COLS):
                t = jnp.dot(w1_ref[0, :, c * PERM_COLS:(c + 1) * PERM_COLS].astype(BF16),
                            perm_ref[...], preferred_element_type=F32)
                rs = slice(hh * k_half, (hh + 1) * k_half)
                w1g[rs, c * half:(c + 1) * half] = t[:, :half].astype(BF16)
                w1u[rs, c * half:(c + 1) * half] = t[:, half:].astype(BF16)
        w2b[...] = w2_ref[0].astype(BF16)

        def scatter(r, rows):
            for t in range(rows):
                dst = rd_ref[r0 + r + t]
                pltpu.make_async_copy(acc.at[pl.ds(r + t, 1), :], yk_hbm.at[pl.ds(dst, 1), :],
                                      sem_out).start()

        def block(r, rows, prefetch):
            if prefetch:
                base = issued[0]
                for t in range(rows // 4):
                    gather_row(r0_next, base + t)
                issued[0] = base + rows // 4
            sl = pl.ds(r, rows)
            xb = x_scr[sl, :]
            g = jnp.dot(xb, w1g[...], preferred_element_type=F32) + b1g_ref[0]
            up = jnp.dot(xb, w1u[...], preferred_element_type=F32) + b1u_ref[0]
            g = jnp.minimum(g, SWIGLU_LIMIT)
            up = jnp.clip(up, -SWIGLU_LIMIT, SWIGLU_LIMIT)
            act = g * _sigmoid(SWIGLU_ALPHA * g) * (up + 1.0)
            acc[sl, :] += jnp.dot(act.astype(BF16), w2b[...], preferred_element_type=F32)

        def run(r, rows):
            can = issued[0] + rows // 4 <= n_next

            @pl.when(can)
            def _():
                block(r, rows, True)

            @pl.when(jnp.logical_not(can))
            def _():
                block(r, rows, False)

        n_big = n // MOE_BLOCK
        rem = n % MOE_BLOCK
        last = f == nf - 1

        def big(b, _):
            run(pl.multiple_of(b * MOE_BLOCK, MOE_BLOCK), MOE_BLOCK)
            return 0

        @pl.when(jnp.logical_not(last))
        def _():
            lax.fori_loop(0, n_big, big, 0)

        @pl.when(last)
        def _():
            for b in range(MOE_ITEM_ROWS // MOE_BLOCK):
                @pl.when(b < n_big)
                def _(b=b):
                    run(b * MOE_BLOCK, MOE_BLOCK)
                    scatter(b * MOE_BLOCK, MOE_BLOCK)

        def tail(r, rows):
            run(r, rows)

            @pl.when(last)
            def _():
                scatter(r, rows)

        @pl.when(rem >= 2 * MOE_PAD)
        def _():
            tail(pl.multiple_of(n_big * MOE_BLOCK, MOE_BLOCK), 2 * MOE_PAD)

        @pl.when(rem % (2 * MOE_PAD) != 0)
        def _():
            tail(pl.multiple_of(n - MOE_PAD, MOE_PAD), MOE_PAD)

        @pl.when(last)
        def _():
            @pl.when(i == ni_ref[0] - 1)
            def _():
                lax.fori_loop(0, n // MOE_PAD, scattered, 0)

            def rest(j, c):
                gather_row(r0_next, j)
                return c

            lax.fori_loop(issued[0], n_next, rest, 0)


def _experts(xn, item_e, item_r0, item_n, n_items, row_tok, row_dst, n_out_rows,
             w1, b1g, b1u, w2, b2, tf):
    max_items = item_e.shape[0]
    nf = D_FF // tf
    idx = jnp.arange(PERM_COLS)
    dst = jnp.where(idx % 2 == 0, idx // 2, PERM_COLS // 2 + idx // 2)
    perm = (dst[:, None] == jnp.arange(PERM_COLS)[None, :]).astype(BF16)

    def ex(i, ie, ni):
        return ie[jnp.minimum(i, ni[0] - 1)]

    def ff(i, f, ni):
        return jnp.where(i < ni[0], f, nf - 1)

    grid_spec = pltpu.PrefetchScalarGridSpec(
        num_scalar_prefetch=6,
        grid=(max_items, nf),
        in_specs=[
            pl.BlockSpec(memory_space=pl.ANY),
            pl.BlockSpec((1, D_MODEL // 2, 2 * tf), lambda i, f, ie, ir, im, ni, rt, rd: (ex(i, ie, ni), 0, ff(i, f, ni))),
            pl.BlockSpec((1, D_MODEL // 2, 2 * tf), lambda i, f, ie, ir, im, ni, rt, rd: (ex(i, ie, ni), 1, ff(i, f, ni))),
            pl.BlockSpec((1, 1, tf), lambda i, f, ie, ir, im, ni, rt, rd: (ex(i, ie, ni), 0, ff(i, f, ni))),
            pl.BlockSpec((1, 1, tf), lambda i, f, ie, ir, im, ni, rt, rd: (ex(i, ie, ni), 0, ff(i, f, ni))),
            pl.BlockSpec((1, tf, D_MODEL), lambda i, f, ie, ir, im, ni, rt, rd: (ex(i, ie, ni), ff(i, f, ni), 0)),
            pl.BlockSpec((1, 1, D_MODEL), lambda i, f, ie, ir, im, ni, rt, rd: (ex(i, ie, ni), 0, 0)),
            pl.BlockSpec((PERM_COLS, PERM_COLS), lambda i, f, ie, ir, im, ni, rt, rd: (0, 0)),
        ],
        out_specs=pl.BlockSpec(memory_space=pl.ANY),
        scratch_shapes=[
            pltpu.VMEM((MOE_ITEM_ROWS, D_MODEL // 2), jnp.uint32),
            pltpu.VMEM((MOE_ITEM_ROWS, D_MODEL), BF16),
            pltpu.VMEM((MOE_ITEM_ROWS, D_MODEL), F32),
            pltpu.VMEM((D_MODEL, tf), BF16),
            pltpu.VMEM((D_MODEL, tf), BF16),
            pltpu.VMEM((tf, D_MODEL), BF16),
            pltpu.SMEM((1,), jnp.int32),
            pltpu.SemaphoreType.DMA(()),
            pltpu.SemaphoreType.DMA(()),
        ],
    )
    return pl.pallas_call(
        _moe_kernel,
        grid_spec=grid_spec,
        out_shape=jax.ShapeDtypeStruct((n_out_rows, D_MODEL), F32),
        compiler_params=pltpu.CompilerParams(
            dimension_semantics=("arbitrary", "arbitrary"), vmem_limit_bytes=VMEM_LIMIT),
        name="experts",
    )(item_e, item_r0, item_n, n_items, row_tok, row_dst, xn, w1, w1, b1g, b1u, w2, b2, perm)


def _combine_kernel(h_ref, y0_ref, y1_ref, y2_ref, y3_ref, gate_ref, g_ref, o_ref):
    acc = h_ref[...]
    gates = gate_ref[...]
    for k, y_ref in enumerate((y0_ref, y1_ref, y2_ref, y3_ref)):
        acc = acc + gates[:, k:k + 1] * y_ref[...]
    o_ref[...] = _rms(acc, g_ref[...])


def _combine(h1, yk, gates_pad, g_final, row_tile):
    rows = h1.shape[0]
    n_tiles = rows // row_tile
    row = lambda i: (i, 0)
    const = lambda i: (0, 0)
    slot = lambda k: (lambda i: (k * n_tiles + i, 0))
    return pl.pallas_call(
        _combine_kernel,
        grid=(n_tiles,),
        in_specs=[pl.BlockSpec((row_tile, D_MODEL), row)]
        + [pl.BlockSpec((row_tile, D_MODEL), slot(k)) for k in range(TOP_K)]
        + [pl.BlockSpec((row_tile, LANES), row), pl.BlockSpec((1, D_MODEL), const)],
        out_specs=pl.BlockSpec((row_tile, D_MODEL), row),
        out_shape=jax.ShapeDtypeStruct((rows, D_MODEL), F32),
        compiler_params=pltpu.CompilerParams(
            dimension_semantics=("arbitrary",), vmem_limit_bytes=VMEM_LIMIT),
        name="combine",
    )(h1, yk, yk, yk, yk, gates_pad, g_final)


def kernel(x, meta_tokens, g_mix, w_in, b_in, attn_sinks, ssm_a_re, ssm_a_im, ssm_log_dt,
           ssm_b_re, ssm_b_im, ssm_c_re, ssm_c_im, ssm_d, w_glu, b_glu, g_attn_out, g_ssm_out,
           w_out, b_out, g_ffn, w_router, b_router, w_mlp1, b_mlp1, w_mlp2, b_mlp2, g_final):
    bsz, seq, _ = x.shape
    rows = bsz * seq
    row_tile = min(ROW_TILE, seq)
    chunk = min(SSM_CHUNK, seq)
    assert seq % ATTN_BLOCK == 0 and seq % row_tile == 0 and seq % chunk == 0
    xf = x.reshape(rows, D_MODEL)

    w_in_bf = w_in[0].astype(BF16)
    cos_r, sin_r = _rope_tables(N_META + jnp.arange(seq))
    cos_m, sin_m = _rope_tables(jnp.arange(N_META))
    qt, k, vt, u = _inproj(xf, g_mix, w_in_bf, b_in, cos_r, sin_r, row_tile, True)
    _, k_meta, v_meta, u_meta = _inproj(meta_tokens, g_mix, w_in_bf, b_in, cos_m, sin_m,
                                        N_META, False)

    y_attn = _attention(qt, k, vt, k_meta, v_meta.T, attn_sinks, g_attn_out, bsz, seq)

    prm = _ssm_params(ssm_a_re[0], ssm_a_im[0], ssm_log_dt[0], ssm_b_re[0], ssm_b_im[0],
                      ssm_c_re[0], ssm_c_im[0], chunk // SUBLANES)
    y_ssm = _ssm(u, u_meta, prm, ssm_d, w_glu[0].astype(BF16), b_glu, g_ssm_out, bsz, seq, chunk)

    w_router_pad = jnp.pad(w_router[0], ((0, 0), (0, ROUTER_LANES - N_EXPERTS)))
    b_router_pad = jnp.pad(b_router, ((0, 0), (0, ROUTER_LANES - N_EXPERTS)))
    w_router_hi = w_router_pad.astype(BF16)
    w_router_lo = (w_router_pad - w_router_hi.astype(F32)).astype(BF16)
    w_router_pad = jnp.concatenate([w_router_hi, w_router_lo], axis=1)
    h1, xn, logits = _outproj(y_attn, y_ssm, xf, w_out[0].astype(BF16), b_out, g_ffn,
                              w_router_pad, b_router_pad, row_tile)

    top_val, top_idx = lax.top_k(logits[:, :N_EXPERTS], TOP_K)
    gates = jax.nn.softmax(top_val, axis=-1)
    chosen = (top_idx[:, :, None] == jnp.arange(N_EXPERTS)[None, None, :])
    counts = jnp.sum(chosen.astype(jnp.int32), axis=(0, 1))
    padded = ((counts + MOE_PAD - 1) // MOE_PAD) * MOE_PAD
    pad_ends = jnp.cumsum(padded)
    pad_starts = pad_ends - padded
    n_assign = rows * TOP_K
    n_rows = n_assign + MOE_DUMP_ROWS
    order = jnp.argsort(top_idx.reshape(-1), stable=True).astype(jnp.int32)
    starts = jnp.cumsum(counts) - counts
    r = jnp.arange(n_rows, dtype=jnp.int32)
    r_e = jnp.minimum(jnp.sum(r[:, None] >= pad_ends[None, :], axis=1), N_EXPERTS - 1)
    r_j = r - pad_starts[r_e]
    real = r_j < counts[r_e]
    assign = order[jnp.minimum(starts[r_e] + r_j, n_assign - 1)]
    row_tok = jnp.where(real, assign // TOP_K, 0).astype(jnp.int32)
    dump = n_assign + r_e * MOE_PAD + jnp.clip(r_j - counts[r_e], 0, MOE_PAD - 1)
    row_dst = jnp.where(real, (assign % TOP_K) * rows + assign // TOP_K, dump).astype(jnp.int32)
    per_e = (padded + MOE_ITEM_ROWS - 1) // MOE_ITEM_ROWS
    item_ends = jnp.cumsum(per_e)
    max_items = N_EXPERTS + n_assign // MOE_ITEM_ROWS
    slot = jnp.arange(max_items, dtype=jnp.int32)
    item_e = jnp.minimum(jnp.searchsorted(item_ends, slot, side='right'), N_EXPERTS - 1)
    piece = slot - (item_ends - per_e)[item_e]
    item_r0 = (pad_starts[item_e] + piece * MOE_ITEM_ROWS).astype(jnp.int32)
    item_n = jnp.clip(padded[item_e] - piece * MOE_ITEM_ROWS, 0, MOE_ITEM_ROWS).astype(jnp.int32)
    n_items = item_ends[-1].astype(jnp.int32).reshape(1)

    tf = MOE_FF_TILE
    b1 = b_mlp1[0].reshape(N_EXPERTS, 1, D_FF, 2)
    b2 = b_mlp2[0].reshape(N_EXPERTS, 1, D_MODEL)
    yk = _experts(xn, item_e.astype(jnp.int32), item_r0, item_n, n_items, row_tok, row_dst,
                  n_assign + MOE_DUMP_ROWS, w_mlp1[0], b1[..., 0], b1[..., 1], w_mlp2[0], b2, tf)

    gates_pad = jnp.pad(gates, ((0, 0), (0, LANES - TOP_K)))
    out = _combine(h1, yk, gates_pad, g_final.reshape(1, D_MODEL), row_tile)
    return out.reshape(bsz, seq, D_MODEL)
```

```python
import functools
import math

import jax
import jax.numpy as jnp
from jax import lax
from jax.experimental import pallas as pl
from jax.experimental.pallas import tpu as pltpu

F32 = jnp.float32
BF16 = jnp.bfloat16

D_MODEL = 2048
N_META = 16
HEAD_DIM = 64
ATTN_WIDTH = 1024
N_Q_HEADS = 16
N_KV_HEADS = 4
Q_PER_KV = 4
KV_WIDTH = 256
ATTN_BLOCK = 128
ROPE_THETA = 10000.0
SSM_WIDTH = 1024
SSM_GROUP = 16
N_SSM_GROUPS = 64
SSM_STATE = 64
SSM_LANES = N_SSM_GROUPS * SSM_STATE
IN_WIDTH = 2560
N_EXPERTS = 32
TOP_K = 4
D_FF = 2048
SWIGLU_LIMIT = 7.0
SWIGLU_ALPHA = 1.702
NORM_EPS = 1e-5

LANES = 128
SUBLANES = 8
VMEM_LIMIT = 58 * 1024 * 1024

ROW_TILE = 512
SSM_CHUNK = 256
SSM_COL = 256
SSM_COL_TILES = SSM_WIDTH // SSM_COL
SSM_COL_LANES = SSM_LANES // SSM_COL_TILES
SCAN_LANES = 512
MOE_PAD = 128
MOE_BLOCK = 512
MOE_ITEM_ROWS = 2176
MOE_FF_TILE = 256
ROUTER_LANES = 128
PERM_COLS = 256


def _rms(t, gain):
    return t * lax.rsqrt(jnp.mean(t * t, axis=-1, keepdims=True) + NORM_EPS) * gain


def _sigmoid(t):
    return 1.0 / (1.0 + jnp.exp(-t))


def _inproj_kernel(x_ref, g_ref, w_ref, b_ref, cos_ref, sin_ref, q_ref, k_ref, v_ref, u_ref,
                   *, transposed):
    n = _rms(x_ref[...], g_ref[...]).astype(BF16)
    cos = cos_ref[...]
    sin = sin_ref[...]
    lane = lax.broadcasted_iota(jnp.int32, cos.shape, 1)
    first_half = (lane % HEAD_DIM) < (HEAD_DIM // 2)

    def proj(c0, c1):
        return jnp.dot(n, w_ref[:, c0:c1], preferred_element_type=F32) + b_ref[:, c0:c1]

    def rope(t):
        partner = jnp.where(first_half, pltpu.roll(t, LANES - HEAD_DIM // 2, 1),
                            pltpu.roll(t, HEAD_DIM // 2, 1))
        return t * cos + partner * sin

    def put(ref, j, t):
        if transposed:
            ref[j * LANES:(j + 1) * LANES, :] = t.T.astype(BF16)
        else:
            ref[:, j * LANES:(j + 1) * LANES] = t.astype(BF16)

    def proj_pair(c0):
        z = proj(c0, c0 + 2 * LANES)
        return z[:, :LANES], z[:, LANES:]

    scale = HEAD_DIM ** -0.5
    for jj in range(ATTN_WIDTH // (2 * LANES)):
        a, b = proj_pair(jj * 2 * LANES)
        put(q_ref, 2 * jj, rope(a) * scale)
        put(q_ref, 2 * jj + 1, rope(b) * scale)
    a, b = proj_pair(ATTN_WIDTH)
    k_ref[:, :LANES] = rope(a).astype(BF16)
    k_ref[:, LANES:] = rope(b).astype(BF16)
    a, b = proj_pair(ATTN_WIDTH + KV_WIDTH)
    put(v_ref, 0, a)
    put(v_ref, 1, b)
    c0 = ATTN_WIDTH + 2 * KV_WIDTH
    u_ref[...] = proj(c0, c0 + SSM_WIDTH)


def _inproj(xf, g_mix, w_in_bf, b_in, cos_t, sin_t, row_tile, transposed):
    rows = xf.shape[0]
    tab_blocks = cos_t.shape[0] // row_tile
    row = lambda i: (i, 0)
    col = lambda i: (0, i)
    tab = lambda i: (i % tab_blocks, 0)
    const = lambda i: (0, 0)
    if transposed:
        q_spec, q_shape = pl.BlockSpec((ATTN_WIDTH, row_tile), col), (ATTN_WIDTH, rows)
        v_spec, v_shape = pl.BlockSpec((KV_WIDTH, row_tile), col), (KV_WIDTH, rows)
    else:
        q_spec, q_shape = pl.BlockSpec((row_tile, ATTN_WIDTH), row), (rows, ATTN_WIDTH)
        v_spec, v_shape = pl.BlockSpec((row_tile, KV_WIDTH), row), (rows, KV_WIDTH)
    return pl.pallas_call(
        functools.partial(_inproj_kernel, transposed=transposed),
        grid=(rows // row_tile,),
        in_specs=[
            pl.BlockSpec((row_tile, D_MODEL), row),
            pl.BlockSpec((1, D_MODEL), const),
            pl.BlockSpec((D_MODEL, IN_WIDTH), const),
            pl.BlockSpec((1, IN_WIDTH), const),
            pl.BlockSpec((row_tile, LANES), tab),
            pl.BlockSpec((row_tile, LANES), tab),
        ],
        out_specs=[
            q_spec,
            pl.BlockSpec((row_tile, KV_WIDTH), row),
            v_spec,
            pl.BlockSpec((row_tile, SSM_WIDTH), row),
        ],
        out_shape=[
            jax.ShapeDtypeStruct(q_shape, BF16),
            jax.ShapeDtypeStruct((rows, KV_WIDTH), BF16),
            jax.ShapeDtypeStruct(v_shape, BF16),
            jax.ShapeDtypeStruct((rows, SSM_WIDTH), F32),
        ],
        compiler_params=pltpu.CompilerParams(
            dimension_semantics=("arbitrary",), vmem_limit_bytes=VMEM_LIMIT),
        name="inproj",
    )(xf, g_mix, w_in_bf, b_in, cos_t, sin_t)


def _rope_tables(positions):
    half = HEAD_DIM // 2
    inv_freq = jnp.power(ROPE_THETA, -jnp.arange(half, dtype=F32) / half)
    ang = positions.astype(F32)[:, None] * inv_freq[None, :]
    cos = jnp.tile(jnp.cos(ang), (1, LANES // half))
    sin = jnp.tile(jnp.sin(ang), (1, LANES // half))
    sign = jnp.where((jnp.arange(LANES) % HEAD_DIM) < half, -1.0, 1.0).astype(F32)
    return cos, sin * sign[None, :]


def _attn_kernel(qt_ref, kp_ref, kc_ref, km_ref, vtp_ref, vtc_ref, vtm_ref, sink_ref, g_ref, o_ref,
                 yt_scr):
    n = pl.program_id(1)
    n_keys = 2 * ATTN_BLOCK + N_META
    key = lax.broadcasted_iota(jnp.int32, (n_keys, ATTN_BLOCK), 0)
    qi = lax.broadcasted_iota(jnp.int32, (n_keys, ATTN_BLOCK), 1)
    no_prev = jnp.where(n > 0, 0, ATTN_BLOCK)
    cur_j = key - ATTN_BLOCK
    valid = ((key >= 2 * ATTN_BLOCK) | ((cur_j >= 0) & (cur_j <= qi))
             | ((key < ATTN_BLOCK) & (key > qi + no_prev)))
    bias = jnp.where(valid, 0.0, -1e30)
    bias = jnp.concatenate([bias] * Q_PER_KV, axis=1)
    zeros = jnp.zeros((HEAD_DIM, ATTN_BLOCK), BF16)
    ssq = jnp.zeros((1, ATTN_BLOCK), F32)
    for hk in range(N_KV_HEADS):
        lt = slice((hk // 2) * LANES, (hk // 2 + 1) * LANES)
        kt = jnp.concatenate([kp_ref[:, lt], kc_ref[:, lt], km_ref[:, lt]], axis=0)
        rs = slice(hk * HEAD_DIM, (hk + 1) * HEAD_DIM)
        vt = jnp.concatenate([vtp_ref[rs, :], vtc_ref[rs, :], vtm_ref[rs, :]], axis=1)
        qs, sinks = [], []
        for g in range(Q_PER_KV):
            h = hk * Q_PER_KV + g
            qh = qt_ref[h * HEAD_DIM:(h + 1) * HEAD_DIM, :]
            qs.append(jnp.concatenate([qh, zeros] if hk % 2 == 0 else [zeros, qh], axis=0))
            sinks.append(jnp.broadcast_to(sink_ref[:, h:h + 1], (1, ATTN_BLOCK)))
        sink = jnp.concatenate(sinks, axis=1)
        s = jnp.dot(kt, jnp.concatenate(qs, axis=1), preferred_element_type=F32) + bias
        m = jnp.maximum(jnp.max(s, axis=0, keepdims=True), sink)
        p = jnp.exp(s - m)
        denom = jnp.sum(p, axis=0, keepdims=True) + jnp.exp(sink - m)
        o = jnp.dot(vt, p.astype(BF16), preferred_element_type=F32) * (1.0 / denom)
        for g in range(Q_PER_KV):
            h = hk * Q_PER_KV + g
            og = o[:, g * ATTN_BLOCK:(g + 1) * ATTN_BLOCK]
            yt_scr[h * HEAD_DIM:(h + 1) * HEAD_DIM, :] = og
            ssq = ssq + jnp.sum(og * og, axis=0, keepdims=True)
    inv = lax.rsqrt(ssq * (1.0 / ATTN_WIDTH) + NORM_EPS)
    o_ref[...] = ((yt_scr[...] * inv).T * g_ref[...]).astype(BF16)


def _attention(qt, k, vt, k_meta, vt_meta, sinks, g_attn, bsz, seq):
    nb = seq // ATTN_BLOCK
    cur = lambda b, n: (b * nb + n, 0)
    prev = lambda b, n: (b * nb + jnp.maximum(n - 1, 0), 0)
    cur_t = lambda b, n: (0, b * nb + n)
    prev_t = lambda b, n: (0, b * nb + jnp.maximum(n - 1, 0))
    const = lambda b, n: (0, 0)
    return pl.pallas_call(
        _attn_kernel,
        grid=(bsz, nb),
        in_specs=[
            pl.BlockSpec((ATTN_WIDTH, ATTN_BLOCK), cur_t),
            pl.BlockSpec((ATTN_BLOCK, KV_WIDTH), prev),
            pl.BlockSpec((ATTN_BLOCK, KV_WIDTH), cur),
            pl.BlockSpec((N_META, KV_WIDTH), const),
            pl.BlockSpec((KV_WIDTH, ATTN_BLOCK), prev_t),
            pl.BlockSpec((KV_WIDTH, ATTN_BLOCK), cur_t),
            pl.BlockSpec((KV_WIDTH, N_META), const),
            pl.BlockSpec((1, N_Q_HEADS), const),
            pl.BlockSpec((1, ATTN_WIDTH), const),
        ],
        out_specs=pl.BlockSpec((ATTN_BLOCK, ATTN_WIDTH), cur),
        out_shape=jax.ShapeDtypeStruct((bsz * seq, ATTN_WIDTH), BF16),
        scratch_shapes=[pltpu.VMEM((ATTN_WIDTH, ATTN_BLOCK), F32)],
        compiler_params=pltpu.CompilerParams(
            dimension_semantics=("arbitrary", "arbitrary"), vmem_limit_bytes=VMEM_LIMIT),
        name="attention",
    )(qt, k, k, k_meta, vt, vt, vt_meta, sinks, g_attn)


def _ssm_params(a_re, a_im, log_dt, b_re, b_im, c_re, c_im, seg_len):
    dt = jnp.exp(log_dt.astype(F32))[:, None]
    lam_re = jnp.minimum(a_re.astype(F32), -1e-4)
    lam_im = a_im.astype(F32)
    z_re, z_im = lam_re * dt, lam_im * dt
    mag = jnp.exp(z_re)
    abar_re, abar_im = mag * jnp.cos(z_im), mag * jnp.sin(z_im)
    den = lam_re * lam_re + lam_im * lam_im
    n_re, n_im = abar_re - 1.0, abar_im
    coef_re = (n_re * lam_re + n_im * lam_im) / den
    coef_im = (n_im * lam_re - n_re * lam_im) / den
    br, bi = b_re.astype(F32), b_im.astype(F32)
    bb_re = coef_re[..., None] * br - coef_im[..., None] * bi
    bb_im = coef_re[..., None] * bi + coef_im[..., None] * br

    groups_per_tile = SSM_COL // SSM_GROUP
    eye = jnp.eye(groups_per_tile, dtype=F32)

    def in_tile(bb):
        t = bb.reshape(SSM_COL_TILES, groups_per_tile, SSM_STATE, SSM_GROUP)
        t = jnp.einsum('tgpc,gh->tgchp', t, eye)
        return t.reshape(SSM_COL_TILES, SSM_COL, SSM_COL_LANES).astype(BF16)

    def out_tile(cc):
        t = cc.reshape(SSM_COL_TILES, groups_per_tile, SSM_GROUP, SSM_STATE)
        t = jnp.einsum('tgcp,gh->tgphc', t, eye)
        return t.reshape(SSM_COL_TILES, SSM_COL_LANES, SSM_COL).astype(BF16)

    def powers(exps):
        e = exps.astype(F32)[:, None, None]
        pm = jnp.exp(e * z_re[None])
        return jnp.stack([(pm * jnp.cos(e * z_im[None])).reshape(len(exps), SSM_LANES),
                          (pm * jnp.sin(e * z_im[None])).reshape(len(exps), SSM_LANES)])

    return dict(
        b_re=in_tile(bb_re), b_im=in_tile(bb_im),
        c_re=out_tile(c_re.astype(F32)), c_im=out_tile(-c_im.astype(F32)),
        a=jnp.stack([abar_re.reshape(1, SSM_LANES), abar_im.reshape(1, SSM_LANES)]),
        a_seg=powers(jnp.array([seg_len])),
        a_meta=powers(jnp.arange(N_META - 1, -1, -1)),
    )


def _gelu_tanh(t):
    return 0.5 * t * (1.0 + jnp.tanh(math.sqrt(2.0 / math.pi) * (t + 0.044715 * (t * t * t))))


def _ssm_kernel(u_ref, um_ref, bre_ref, bim_ref, cre_ref, cim_ref, a_ref, aseg_ref,
                ameta_ref, perm_ref, d_ref, wglu_ref, bglu_ref, g_ref, o_ref,
                xre, xim, hre, him, car_re, car_im, cin_re, cin_im, y_scr):
    chunk = u_ref.shape[0]
    seg = chunk // SUBLANES

    @pl.when(pl.program_id(1) == 0)
    def _():
        um = um_ref[...].astype(BF16)
        for ct in range(SSM_COL_TILES):
            ub = um[:, ct * SSM_COL:(ct + 1) * SSM_COL]
            ls = slice(ct * SSM_COL_LANES, (ct + 1) * SSM_COL_LANES)
            xr = jnp.dot(ub, bre_ref[ct], preferred_element_type=F32)
            xi = jnp.dot(ub, bim_ref[ct], preferred_element_type=F32)
            pr = ameta_ref[0, :, ls]
            pi = ameta_ref[1, :, ls]
            car_re[:, ls] = jnp.sum(pr * xr - pi * xi, axis=0, keepdims=True)
            car_im[:, ls] = jnp.sum(pr * xi + pi * xr, axis=0, keepdims=True)

    up = jnp.dot(perm_ref[0], u_ref[...].astype(BF16), preferred_element_type=F32).astype(BF16)
    for ct in range(SSM_COL_TILES):
        ub = up[:, ct * SSM_COL:(ct + 1) * SSM_COL]
        ls = slice(ct * SSM_COL_LANES, (ct + 1) * SSM_COL_LANES)
        xre[:, ls] = jnp.dot(ub, bre_ref[ct], preferred_element_type=F32)
        xim[:, ls] = jnp.dot(ub, bim_ref[ct], preferred_element_type=F32)

    for lb in range(SSM_LANES // SCAN_LANES):
        ls = slice(lb * SCAN_LANES, (lb + 1) * SCAN_LANES)
        ar = jnp.broadcast_to(a_ref[0, :, ls], (SUBLANES, SCAN_LANES))
        ai = jnp.broadcast_to(a_ref[1, :, ls], (SUBLANES, SCAN_LANES))

        def end_body(k, carry):
            hr, hi = carry
            rows = pl.ds(pl.multiple_of(k * SUBLANES, SUBLANES), SUBLANES)
            return ar * hr - ai * hi + xre[rows, ls], ar * hi + ai * hr + xim[rows, ls]

        zero = jnp.zeros((SUBLANES, SCAN_LANES), F32)
        er, ei = lax.fori_loop(0, seg, end_body, (zero, zero))
        cin_re[:, ls] = er
        cin_im[:, ls] = ei

    cr = car_re[...]
    ci = car_im[...]
    sr = aseg_ref[0]
    si = aseg_ref[1]
    for r in range(SUBLANES):
        er = cin_re[r:r + 1, :]
        ei = cin_im[r:r + 1, :]
        cin_re[r:r + 1, :] = cr
        cin_im[r:r + 1, :] = ci
        cr, ci = sr * cr - si * ci + er, sr * ci + si * cr + ei
    car_re[...] = cr
    car_im[...] = ci

    pair = 2 * SUBLANES
    for lb in range(SSM_LANES // SCAN_LANES):
        ls = slice(lb * SCAN_LANES, (lb + 1) * SCAN_LANES)
        ar = jnp.broadcast_to(a_ref[0, :, ls], (SUBLANES, SCAN_LANES))
        ai = jnp.broadcast_to(a_ref[1, :, ls], (SUBLANES, SCAN_LANES))

        def scan_body(k2, carry):
            hr, hi = carry
            r0 = pl.multiple_of(k2 * pair, pair)
            first = pl.ds(r0, SUBLANES)
            second = pl.ds(r0 + SUBLANES, SUBLANES)
            h0r = ar * hr - ai * hi + xre[first, ls]
            h0i = ar * hi + ai * hr + xim[first, ls]
            h1r = ar * h0r - ai * h0i + xre[second, ls]
            h1i = ar * h0i + ai * h0r + xim[second, ls]
            hre[pl.ds(r0, pair), ls] = jnp.concatenate([h0r, h1r], axis=0).astype(BF16)
            him[pl.ds(r0, pair), ls] = jnp.concatenate([h0i, h1i], axis=0).astype(BF16)
            return h1r, h1i

        lax.fori_loop(0, seg // 2, scan_body, (cin_re[:, ls], cin_im[:, ls]))

    n_lt = SSM_WIDTH // LANES
    for ct in range(SSM_COL_TILES):
        ls = slice(ct * SSM_COL_LANES, (ct + 1) * SSM_COL_LANES)
        y = (jnp.dot(hre[:, ls], cre_ref[ct], preferred_element_type=F32)
             + jnp.dot(him[:, ls], cim_ref[ct], preferred_element_type=F32))
        for k in range(seg):
            for jj in range(SSM_COL // LANES):
                y_scr[ct * (SSM_COL // LANES) + jj, pl.ds(k, SUBLANES, stride=seg), :] = (
                    y[k * SUBLANES:(k + 1) * SUBLANES, jj * LANES:(jj + 1) * LANES])

    y = jnp.concatenate([y_scr[j] for j in range(n_lt)], axis=1)
    y = _gelu_tanh(y + d_ref[...] * u_ref[...])
    gate = jnp.dot(y.astype(BF16), wglu_ref[...], preferred_element_type=F32) + bglu_ref[...]
    y = y * _sigmoid(gate)
    o_ref[...] = _rms(y, g_ref[...]).astype(BF16)


def _ssm(u, u_meta, prm, d_skip, w_glu_bf, b_glu, g_ssm, bsz, seq, chunk):
    nc = seq // chunk
    seg = chunk // SUBLANES
    row = lambda b, c: (b * nc + c, 0)
    c2 = lambda b, c: (0, 0)
    c3 = lambda b, c: (0, 0, 0)
    p = jnp.arange(chunk)
    fwd = ((p % SUBLANES) * seg + p // SUBLANES)[:, None] == p[None, :]
    perm = fwd.astype(BF16)[None]
    return pl.pallas_call(
        _ssm_kernel,
        grid=(bsz, nc),
        in_specs=[
            pl.BlockSpec((chunk, SSM_WIDTH), row),
            pl.BlockSpec((N_META, SSM_WIDTH), c2),
            pl.BlockSpec((SSM_COL_TILES, SSM_COL, SSM_COL_LANES), c3),
            pl.BlockSpec((SSM_COL_TILES, SSM_COL, SSM_COL_LANES), c3),
            pl.BlockSpec((SSM_COL_TILES, SSM_COL_LANES, SSM_COL), c3),
            pl.BlockSpec((SSM_COL_TILES, SSM_COL_LANES, SSM_COL), c3),
            pl.BlockSpec((2, 1, SSM_LANES), c3),
            pl.BlockSpec((2, 1, SSM_LANES), c3),
            pl.BlockSpec((2, N_META, SSM_LANES), c3),
            pl.BlockSpec((1, chunk, chunk), c3),
            pl.BlockSpec((1, SSM_WIDTH), c2),
            pl.BlockSpec((SSM_WIDTH, SSM_WIDTH), c2),
            pl.BlockSpec((1, SSM_WIDTH), c2),
            pl.BlockSpec((1, SSM_WIDTH), c2),
        ],
        out_specs=pl.BlockSpec((chunk, SSM_WIDTH), row),
        out_shape=jax.ShapeDtypeStruct((bsz * seq, SSM_WIDTH), BF16),
        scratch_shapes=[
            pltpu.VMEM((chunk, SSM_LANES), F32),
            pltpu.VMEM((chunk, SSM_LANES), F32),
            pltpu.VMEM((chunk, SSM_LANES), BF16),
            pltpu.VMEM((chunk, SSM_LANES), BF16),
            pltpu.VMEM((1, SSM_LANES), F32),
            pltpu.VMEM((1, SSM_LANES), F32),
            pltpu.VMEM((SUBLANES, SSM_LANES), F32),
            pltpu.VMEM((SUBLANES, SSM_LANES), F32),
            pltpu.VMEM((SSM_WIDTH // LANES, chunk, LANES), F32),
        ],
        compiler_params=pltpu.CompilerParams(
            dimension_semantics=("arbitrary", "arbitrary"), vmem_limit_bytes=VMEM_LIMIT),
        name="ssm",
    )(u, u_meta, prm["b_re"], prm["b_im"], prm["c_re"], prm["c_im"], prm["a"], prm["a_seg"],
      prm["a_meta"], perm, d_skip, w_glu_bf, b_glu, g_ssm)


def _outproj_kernel(ya_ref, ys_ref, x_ref, wo_ref, bo_ref, gf_ref, wr_ref, br_ref,
                    h_ref, xn_ref, lg_ref):
    mix = (jnp.dot(ya_ref[...], wo_ref[:ATTN_WIDTH, :], preferred_element_type=F32)
           + jnp.dot(ys_ref[...], wo_ref[ATTN_WIDTH:, :], preferred_element_type=F32))
    h = x_ref[...] + mix + bo_ref[...]
    h_ref[...] = h
    n = _rms(h, gf_ref[...])
    n_hi = n.astype(BF16)
    bits = lax.bitcast_convert_type(n_hi.astype(F32), jnp.uint32)
    xn_ref[...] = (bits[:, :D_MODEL // 2] >> 16) | (bits[:, D_MODEL // 2:] & jnp.uint32(0xFFFF0000))
    n_lo = (n - n_hi.astype(F32)).astype(BF16)
    hi = jnp.dot(n_hi, wr_ref[...], preferred_element_type=F32)
    lo = jnp.dot(n_lo, wr_ref[:, :ROUTER_LANES], preferred_element_type=F32)
    lg_ref[...] = hi[:, :ROUTER_LANES] + hi[:, ROUTER_LANES:] + lo + br_ref[...]


def _outproj(y_attn, y_ssm, xf, w_out_bf, b_out, g_ffn, w_router_pad, b_router_pad, row_tile):
    rows = xf.shape[0]
    row = lambda i: (i, 0)
    const = lambda i: (0, 0)
    return pl.pallas_call(
        _outproj_kernel,
        grid=(rows // row_tile,),
        in_specs=[
            pl.BlockSpec((row_tile, ATTN_WIDTH), row),
            pl.BlockSpec((row_tile, SSM_WIDTH), row),
            pl.BlockSpec((row_tile, D_MODEL), row),
            pl.BlockSpec((D_MODEL, D_MODEL), const),
            pl.BlockSpec((1, D_MODEL), const),
            pl.BlockSpec((1, D_MODEL), const),
            pl.BlockSpec((D_MODEL, 2 * ROUTER_LANES), const),
            pl.BlockSpec((1, ROUTER_LANES), const),
        ],
        out_specs=[
            pl.BlockSpec((row_tile, D_MODEL), row),
            pl.BlockSpec((row_tile, D_MODEL // 2), row),
            pl.BlockSpec((row_tile, ROUTER_LANES), row),
        ],
        out_shape=[
            jax.ShapeDtypeStruct((rows, D_MODEL), F32),
            jax.ShapeDtypeStruct((rows, D_MODEL // 2), jnp.uint32),
            jax.ShapeDtypeStruct((rows, ROUTER_LANES), F32),
        ],
        compiler_params=pltpu.CompilerParams(
            dimension_semantics=("arbitrary",), vmem_limit_bytes=VMEM_LIMIT),
        name="outproj",
    )(y_attn, y_ssm, xf, w_out_bf, b_out, g_ffn, w_router_pad, b_router_pad)


def _moe_kernel(ie_ref, ir_ref, in_ref, ic_ref, ni_ref, rt_ref, rd_ref,
                xn_hbm, w1a_ref, w1b_ref, b1g_ref, b1u_ref, w2_ref, b2_ref, perm_ref, yk_hbm,
                stage, x_scr, acc, w1g, w1u, w2b, issued, sem_g, sem_out):
    i = pl.program_id(0)
    f = pl.program_id(1)
    nf = pl.num_programs(1)
    max_items = pl.num_programs(0)
    tf = w2_ref.shape[1]

    @pl.when(i < ni_ref[0])
    def _():
        r0 = ir_ref[i]
        n = in_ref[i]
        cnt = ic_ref[i]
        dump_base = yk_hbm.shape[0] - (max_items - i) * MOE_PAD
        nxt = jnp.minimum(i + 1, max_items - 1)
        n_next = jnp.where(i + 1 < ni_ref[0], in_ref[nxt], 0)
        r0_next = ir_ref[nxt]

        def gather_row(first_row, j):
            tok = rt_ref[first_row + j]
            pltpu.make_async_copy(xn_hbm.at[pl.ds(tok, 1), :], stage.at[pl.ds(j, 1), :],
                                  sem_g).start()

        def gather_wait_chunk():
            pltpu.make_async_copy(xn_hbm.at[pl.ds(0, MOE_PAD), :],
                                  stage.at[pl.ds(0, MOE_PAD), :], sem_g).wait()

        def scattered(c, carry):
            pltpu.make_async_copy(acc.at[pl.ds(0, MOE_PAD), :],
                                  yk_hbm.at[pl.ds(0, MOE_PAD), :], sem_out).wait()
            return carry

        @pl.when(f == 0)
        def _():
            @pl.when(i == 0)
            def _():
                def first(j, c):
                    gather_row(r0, j)
                    return c

                lax.fori_loop(0, n, first, 0)

            def landed(c, carry):
                gather_wait_chunk()
                return carry

            lax.fori_loop(0, n // MOE_PAD, landed, 0)

            def unpack(c, carry):
                rows_c = pl.ds(pl.multiple_of(c * MOE_PAD, MOE_PAD), MOE_PAD)
                u = stage[rows_c, :]
                x_scr[rows_c, :D_MODEL // 2] = lax.bitcast_convert_type(u << 16, F32).astype(BF16)
                x_scr[rows_c, D_MODEL // 2:] = lax.bitcast_convert_type(
                    u & jnp.uint32(0xFFFF0000), F32).astype(BF16)
                return carry

            lax.fori_loop(0, n // MOE_PAD, unpack, 0)
            issued[0] = 0

            @pl.when(i > 0)
            def _():
                lax.fori_loop(0, in_ref[jnp.maximum(i - 1, 0)] // MOE_PAD, scattered, 0)

            bias = jnp.broadcast_to(b2_ref[0], (MOE_PAD, D_MODEL))

            def init(c, _):
                acc[pl.ds(pl.multiple_of(c * MOE_PAD, MOE_PAD), MOE_PAD), :] = bias
                return 0

            lax.fori_loop(0, n // MOE_PAD, init, 0)

        half = PERM_COLS // 2
        k_half = D_MODEL // 2
        for hh, w1_ref in enumerate((w1a_ref, w1b_ref)):
            for c in range(2 * tf // PERM_COLS):
                t = jnp.dot(w1_ref[0, :, c * PERM_COLS:(c + 1) * PERM_COLS].astype(BF16),
                            perm_ref[...], preferred_element_type=F32)
                rs = slice(hh * k_half, (hh + 1) * k_half)
                w1g[rs, c * half:(c + 1) * half] = t[:, :half].astype(BF16)
                w1u[rs, c * half:(c + 1) * half] = t[:, half:].astype(BF16)
        w2b[...] = w2_ref[0].astype(BF16)

        def scatter(r, rows):
            for t in range(rows):
                dst = rd_ref[r0 + r + t]
                if t >= rows - MOE_PAD:
                    dst = jnp.where(r + t < cnt, dst, dump_base + (t - (rows - MOE_PAD)))
                pltpu.make_async_copy(acc.at[pl.ds(r + t, 1), :], yk_hbm.at[pl.ds(dst, 1), :],
                                      sem_out).start()

        def block(r, rows, prefetch):
            if prefetch:
                base = issued[0]
                for t in range(rows // 4):
                    gather_row(r0_next, base + t)
                issued[0] = base + rows // 4
            sl = pl.ds(r, rows)
            xb = x_scr[sl, :]
            g = jnp.dot(xb, w1g[...], preferred_element_type=F32) + b1g_ref[0]
            up = jnp.dot(xb, w1u[...], preferred_element_type=F32) + b1u_ref[0]
            g = jnp.minimum(g, SWIGLU_LIMIT)
            up = jnp.clip(up, -SWIGLU_LIMIT, SWIGLU_LIMIT)
            act = g * _sigmoid(SWIGLU_ALPHA * g) * (up + 1.0)
            acc[sl, :] += jnp.dot(act.astype(BF16), w2b[...], preferred_element_type=F32)

        def run(r, rows):
            can = issued[0] + rows // 4 <= n_next

            @pl.when(can)
            def _():
                block(r, rows, True)

            @pl.when(jnp.logical_not(can))
            def _():
                block(r, rows, False)

        n_big = n // MOE_BLOCK
        rem = n % MOE_BLOCK
        last = f == nf - 1

        def big(b, _):
            run(pl.multiple_of(b * MOE_BLOCK, MOE_BLOCK), MOE_BLOCK)
            return 0

        @pl.when(jnp.logical_not(last))
        def _():
            lax.fori_loop(0, n_big, big, 0)

        @pl.when(last)
        def _():
            for b in range(MOE_ITEM_ROWS // MOE_BLOCK):
                @pl.when(b < n_big)
                def _(b=b):
                    run(b * MOE_BLOCK, MOE_BLOCK)
                    scatter(b * MOE_BLOCK, MOE_BLOCK)

        def tail(r, rows):
            run(r, rows)

            @pl.when(last)
            def _():
                scatter(r, rows)

        @pl.when(rem >= 2 * MOE_PAD)
        def _():
            tail(pl.multiple_of(n_big * MOE_BLOCK, MOE_BLOCK), 2 * MOE_PAD)

        @pl.when(rem % (2 * MOE_PAD) != 0)
        def _():
            tail(pl.multiple_of(n - MOE_PAD, MOE_PAD), MOE_PAD)

        @pl.when(last)
        def _():
            @pl.when(i == ni_ref[0] - 1)
            def _():
                lax.fori_loop(0, n // MOE_PAD, scattered, 0)

            def rest(j, c):
                gather_row(r0_next, j)
                return c

            lax.fori_loop(issued[0], n_next, rest, 0)


def _experts(xn, item_e, item_r0, item_n, item_cnt, n_items, row_tok, row_dst, n_out_rows,
             w1, b1g, b1u, w2, b2, tf):
    max_items = item_e.shape[0]
    nf = D_FF // tf
    idx = jnp.arange(PERM_COLS)
    dst = jnp.where(idx % 2 == 0, idx // 2, PERM_COLS // 2 + idx // 2)
    perm = (dst[:, None] == jnp.arange(PERM_COLS)[None, :]).astype(BF16)

    def ex(i, ie, ni):
        return ie[jnp.minimum(i, ni[0] - 1)]

    def ff(i, f, ni):
        return jnp.where(i < ni[0], f, nf - 1)

    grid_spec = pltpu.PrefetchScalarGridSpec(
        num_scalar_prefetch=7,
        grid=(max_items, nf),
        in_specs=[
            pl.BlockSpec(memory_space=pl.ANY),
            pl.BlockSpec((1, D_MODEL // 2, 2 * tf), lambda i, f, ie, ir, im, ic, ni, rt, rd: (ex(i, ie, ni), 0, ff(i, f, ni))),
            pl.BlockSpec((1, D_MODEL // 2, 2 * tf), lambda i, f, ie, ir, im, ic, ni, rt, rd: (ex(i, ie, ni), 1, ff(i, f, ni))),
            pl.BlockSpec((1, 1, tf), lambda i, f, ie, ir, im, ic, ni, rt, rd: (ex(i, ie, ni), 0, ff(i, f, ni))),
            pl.BlockSpec((1, 1, tf), lambda i, f, ie, ir, im, ic, ni, rt, rd: (ex(i, ie, ni), 0, ff(i, f, ni))),
            pl.BlockSpec((1, tf, D_MODEL), lambda i, f, ie, ir, im, ic, ni, rt, rd: (ex(i, ie, ni), ff(i, f, ni), 0)),
            pl.BlockSpec((1, 1, D_MODEL), lambda i, f, ie, ir, im, ic, ni, rt, rd: (ex(i, ie, ni), 0, 0)),
            pl.BlockSpec((PERM_COLS, PERM_COLS), lambda i, f, ie, ir, im, ic, ni, rt, rd: (0, 0)),
        ],
        out_specs=pl.BlockSpec(memory_space=pl.ANY),
        scratch_shapes=[
            pltpu.VMEM((MOE_ITEM_ROWS, D_MODEL // 2), jnp.uint32),
            pltpu.VMEM((MOE_ITEM_ROWS, D_MODEL), BF16),
            pltpu.VMEM((MOE_ITEM_ROWS, D_MODEL), F32),
            pltpu.VMEM((D_MODEL, tf), BF16),
            pltpu.VMEM((D_MODEL, tf), BF16),
            pltpu.VMEM((tf, D_MODEL), BF16),
            pltpu.SMEM((1,), jnp.int32),
            pltpu.SemaphoreType.DMA(()),
            pltpu.SemaphoreType.DMA(()),
        ],
    )
    return pl.pallas_call(
        _moe_kernel,
        grid_spec=grid_spec,
        out_shape=jax.ShapeDtypeStruct((n_out_rows, D_MODEL), F32),
        compiler_params=pltpu.CompilerParams(
            dimension_semantics=("arbitrary", "arbitrary"), vmem_limit_bytes=VMEM_LIMIT),
        name="experts",
    )(item_e, item_r0, item_n, item_cnt, n_items, row_tok, row_dst, xn, w1, w1, b1g, b1u, w2, b2, perm)


def _combine_kernel(h_ref, y0_ref, y1_ref, y2_ref, y3_ref, gate_ref, g_ref, o_ref):
    acc = h_ref[...]
    gates = gate_ref[...]
    for k, y_ref in enumerate((y0_ref, y1_ref, y2_ref, y3_ref)):
        acc = acc + gates[:, k:k + 1] * y_ref[...]
    o_ref[...] = _rms(acc, g_ref[...])


def _combine(h1, yk, gates_pad, g_final, row_tile):
    rows = h1.shape[0]
    n_tiles = rows // row_tile
    row = lambda i: (i, 0)
    const = lambda i: (0, 0)
    slot = lambda k: (lambda i: (k * n_tiles + i, 0))
    return pl.pallas_call(
        _combine_kernel,
        grid=(n_tiles,),
        in_specs=[pl.BlockSpec((row_tile, D_MODEL), row)]
        + [pl.BlockSpec((row_tile, D_MODEL), slot(k)) for k in range(TOP_K)]
        + [pl.BlockSpec((row_tile, LANES), row), pl.BlockSpec((1, D_MODEL), const)],
        out_specs=pl.BlockSpec((row_tile, D_MODEL), row),
        out_shape=jax.ShapeDtypeStruct((rows, D_MODEL), F32),
        compiler_params=pltpu.CompilerParams(
            dimension_semantics=("arbitrary",), vmem_limit_bytes=VMEM_LIMIT),
        name="combine",
    )(h1, yk, yk, yk, yk, gates_pad, g_final)


def kernel(x, meta_tokens, g_mix, w_in, b_in, attn_sinks, ssm_a_re, ssm_a_im, ssm_log_dt,
           ssm_b_re, ssm_b_im, ssm_c_re, ssm_c_im, ssm_d, w_glu, b_glu, g_attn_out, g_ssm_out,
           w_out, b_out, g_ffn, w_router, b_router, w_mlp1, b_mlp1, w_mlp2, b_mlp2, g_final):
    bsz, seq, _ = x.shape
    rows = bsz * seq
    row_tile = min(ROW_TILE, seq)
    chunk = min(SSM_CHUNK, seq)
    assert seq % ATTN_BLOCK == 0 and seq % row_tile == 0 and seq % chunk == 0
    xf = x.reshape(rows, D_MODEL)

    w_in_bf = w_in[0].astype(BF16)
    cos_r, sin_r = _rope_tables(N_META + jnp.arange(seq))
    cos_m, sin_m = _rope_tables(jnp.arange(N_META))
    qt, k, vt, u = _inproj(xf, g_mix, w_in_bf, b_in, cos_r, sin_r, row_tile, True)
    _, k_meta, v_meta, u_meta = _inproj(meta_tokens, g_mix, w_in_bf, b_in, cos_m, sin_m,
                                        N_META, False)

    y_attn = _attention(qt, k, vt, k_meta, v_meta.T, attn_sinks, g_attn_out, bsz, seq)

    prm = _ssm_params(ssm_a_re[0], ssm_a_im[0], ssm_log_dt[0], ssm_b_re[0], ssm_b_im[0],
                      ssm_c_re[0], ssm_c_im[0], chunk // SUBLANES)
    y_ssm = _ssm(u, u_meta, prm, ssm_d, w_glu[0].astype(BF16), b_glu, g_ssm_out, bsz, seq, chunk)

    w_router_pad = jnp.pad(w_router[0], ((0, 0), (0, ROUTER_LANES - N_EXPERTS)))
    b_router_pad = jnp.pad(b_router, ((0, 0), (0, ROUTER_LANES - N_EXPERTS)))
    w_router_hi = w_router_pad.astype(BF16)
    w_router_lo = (w_router_pad - w_router_hi.astype(F32)).astype(BF16)
    w_router_pad = jnp.concatenate([w_router_hi, w_router_lo], axis=1)
    h1, xn, logits = _outproj(y_attn, y_ssm, xf, w_out[0].astype(BF16), b_out, g_ffn,
                              w_router_pad, b_router_pad, row_tile)

    top_val, top_idx = lax.top_k(logits[:, :N_EXPERTS], TOP_K)
    gates = jax.nn.softmax(top_val, axis=-1)
    chosen = (top_idx[:, :, None] == jnp.arange(N_EXPERTS)[None, None, :])
    counts = jnp.sum(chosen.astype(jnp.int32), axis=(0, 1))
    padded = ((counts + MOE_PAD - 1) // MOE_PAD) * MOE_PAD
    n_assign = rows * TOP_K
    order = jnp.argsort(top_idx.reshape(-1), stable=True).astype(jnp.int32)
    row_tok = jnp.pad(order // TOP_K, (0, MOE_PAD))
    row_dst = jnp.pad((order % TOP_K) * rows + order // TOP_K, (0, MOE_PAD))
    starts = jnp.cumsum(counts) - counts
    per_e = (padded + MOE_ITEM_ROWS - 1) // MOE_ITEM_ROWS
    item_ends = jnp.cumsum(per_e)
    max_items = N_EXPERTS + n_assign // MOE_ITEM_ROWS
    slot = jnp.arange(max_items, dtype=jnp.int32)
    item_e = jnp.minimum(jnp.searchsorted(item_ends, slot, side='right'), N_EXPERTS - 1)
    piece = slot - (item_ends - per_e)[item_e]
    item_r0 = (starts[item_e] + piece * MOE_ITEM_ROWS).astype(jnp.int32)
    item_n = jnp.clip(padded[item_e] - piece * MOE_ITEM_ROWS, 0, MOE_ITEM_ROWS).astype(jnp.int32)
    item_cnt = jnp.clip(counts[item_e] - piece * MOE_ITEM_ROWS, 0, MOE_ITEM_ROWS).astype(jnp.int32)
    n_items = item_ends[-1].astype(jnp.int32).reshape(1)

    tf = MOE_FF_TILE
    b1 = b_mlp1[0].reshape(N_EXPERTS, 1, D_FF, 2)
    b2 = b_mlp2[0].reshape(N_EXPERTS, 1, D_MODEL)
    yk = _experts(xn, item_e.astype(jnp.int32), item_r0, item_n, item_cnt, n_items, row_tok, row_dst,
                  n_assign + max_items * MOE_PAD, w_mlp1[0], b1[..., 0], b1[..., 1], w_mlp2[0], b2, tf)

    gates_pad = jnp.pad(gates, ((0, 0), (0, LANES - TOP_K)))
    out = _combine(h1, yk, gates_pad, g_final.reshape(1, D_MODEL), row_tile)
    return out.reshape(bsz, seq, D_MODEL)
```

```python
import functools
import math

import jax
import jax.numpy as jnp
from jax import lax
from jax.experimental import pallas as pl
from jax.experimental.pallas import tpu as pltpu

F32 = jnp.float32
BF16 = jnp.bfloat16

D_MODEL = 2048
N_META = 16
HEAD_DIM = 64
ATTN_WIDTH = 1024
N_Q_HEADS = 16
N_KV_HEADS = 4
Q_PER_KV = 4
KV_WIDTH = 256
ATTN_BLOCK = 128
ROPE_THETA = 10000.0
SSM_WIDTH = 1024
SSM_GROUP = 16
N_SSM_GROUPS = 64
SSM_STATE = 64
SSM_LANES = N_SSM_GROUPS * SSM_STATE
IN_WIDTH = 2560
N_EXPERTS = 32
TOP_K = 4
D_FF = 2048
SWIGLU_LIMIT = 7.0
SWIGLU_ALPHA = 1.702
NORM_EPS = 1e-5

LANES = 128
SUBLANES = 8
VMEM_LIMIT = 58 * 1024 * 1024

ROW_TILE = 512
SSM_CHUNK = 256
SSM_COL = 256
SSM_COL_TILES = SSM_WIDTH // SSM_COL
SSM_COL_LANES = SSM_LANES // SSM_COL_TILES
SCAN_LANES = 512
MOE_PAD = 128
MOE_BLOCK = 512
MOE_ITEM_ROWS = 2176
MOE_FF_TILE = 256
ROUTER_LANES = 128
PERM_COLS = 256


def _rms(t, gain):
    return t * lax.rsqrt(jnp.mean(t * t, axis=-1, keepdims=True) + NORM_EPS) * gain


def _sigmoid(t):
    return 1.0 / (1.0 + jnp.exp(-t))


def _inproj_kernel(x_ref, g_ref, w_ref, b_ref, cos_ref, sin_ref, q_ref, k_ref, v_ref, u_ref,
                   *, transposed):
    n = _rms(x_ref[...], g_ref[...]).astype(BF16)
    cos = cos_ref[...]
    sin = sin_ref[...]
    lane = lax.broadcasted_iota(jnp.int32, cos.shape, 1)
    first_half = (lane % HEAD_DIM) < (HEAD_DIM // 2)

    def proj(c0, c1):
        return jnp.dot(n, w_ref[:, c0:c1], preferred_element_type=F32) + b_ref[:, c0:c1]

    def rope(t):
        partner = jnp.where(first_half, pltpu.roll(t, LANES - HEAD_DIM // 2, 1),
                            pltpu.roll(t, HEAD_DIM // 2, 1))
        return t * cos + partner * sin

    def put(ref, j, t):
        if transposed:
            ref[j * LANES:(j + 1) * LANES, :] = t.T.astype(BF16)
        else:
            ref[:, j * LANES:(j + 1) * LANES] = t.astype(BF16)

    def proj_pair(c0):
        z = proj(c0, c0 + 2 * LANES)
        return z[:, :LANES], z[:, LANES:]

    scale = HEAD_DIM ** -0.5
    for jj in range(ATTN_WIDTH // (2 * LANES)):
        a, b = proj_pair(jj * 2 * LANES)
        put(q_ref, 2 * jj, rope(a) * scale)
        put(q_ref, 2 * jj + 1, rope(b) * scale)
    a, b = proj_pair(ATTN_WIDTH)
    k_ref[:, :LANES] = rope(a).astype(BF16)
    k_ref[:, LANES:] = rope(b).astype(BF16)
    a, b = proj_pair(ATTN_WIDTH + KV_WIDTH)
    put(v_ref, 0, a)
    put(v_ref, 1, b)
    c0 = ATTN_WIDTH + 2 * KV_WIDTH
    u_ref[...] = proj(c0, c0 + SSM_WIDTH)


def _inproj(xf, g_mix, w_in_bf, b_in, cos_t, sin_t, row_tile, transposed):
    rows = xf.shape[0]
    tab_blocks = cos_t.shape[0] // row_tile
    row = lambda i: (i, 0)
    col = lambda i: (0, i)
    tab = lambda i: (i % tab_blocks, 0)
    const = lambda i: (0, 0)
    if transposed:
        q_spec, q_shape = pl.BlockSpec((ATTN_WIDTH, row_tile), col), (ATTN_WIDTH, rows)
        v_spec, v_shape = pl.BlockSpec((KV_WIDTH, row_tile), col), (KV_WIDTH, rows)
    else:
        q_spec, q_shape = pl.BlockSpec((row_tile, ATTN_WIDTH), row), (rows, ATTN_WIDTH)
        v_spec, v_shape = pl.BlockSpec((row_tile, KV_WIDTH), row), (rows, KV_WIDTH)
    return pl.pallas_call(
        functools.partial(_inproj_kernel, transposed=transposed),
        grid=(rows // row_tile,),
        in_specs=[
            pl.BlockSpec((row_tile, D_MODEL), row),
            pl.BlockSpec((1, D_MODEL), const),
            pl.BlockSpec((D_MODEL, IN_WIDTH), const),
            pl.BlockSpec((1, IN_WIDTH), const),
            pl.BlockSpec((row_tile, LANES), tab),
            pl.BlockSpec((row_tile, LANES), tab),
        ],
        out_specs=[
            q_spec,
            pl.BlockSpec((row_tile, KV_WIDTH), row),
            v_spec,
            pl.BlockSpec((row_tile, SSM_WIDTH), row),
        ],
        out_shape=[
            jax.ShapeDtypeStruct(q_shape, BF16),
            jax.ShapeDtypeStruct((rows, KV_WIDTH), BF16),
            jax.ShapeDtypeStruct(v_shape, BF16),
            jax.ShapeDtypeStruct((rows, SSM_WIDTH), F32),
        ],
        compiler_params=pltpu.CompilerParams(
            dimension_semantics=("arbitrary",), vmem_limit_bytes=VMEM_LIMIT),
        name="inproj",
    )(xf, g_mix, w_in_bf, b_in, cos_t, sin_t)


def _rope_tables(positions):
    half = HEAD_DIM // 2
    inv_freq = jnp.power(ROPE_THETA, -jnp.arange(half, dtype=F32) / half)
    ang = positions.astype(F32)[:, None] * inv_freq[None, :]
    cos = jnp.tile(jnp.cos(ang), (1, LANES // half))
    sin = jnp.tile(jnp.sin(ang), (1, LANES // half))
    sign = jnp.where((jnp.arange(LANES) % HEAD_DIM) < half, -1.0, 1.0).astype(F32)
    return cos, sin * sign[None, :]


def _attn_kernel(qt_ref, kp_ref, kc_ref, km_ref, vtp_ref, vtc_ref, vtm_ref, sink_ref, g_ref, o_ref,
                 yt_scr):
    n = pl.program_id(1)
    n_keys = 2 * ATTN_BLOCK + N_META
    key = lax.broadcasted_iota(jnp.int32, (n_keys, ATTN_BLOCK), 0)
    qi = lax.broadcasted_iota(jnp.int32, (n_keys, ATTN_BLOCK), 1)
    no_prev = jnp.where(n > 0, 0, ATTN_BLOCK)
    cur_j = key - ATTN_BLOCK
    valid = ((key >= 2 * ATTN_BLOCK) | ((cur_j >= 0) & (cur_j <= qi))
             | ((key < ATTN_BLOCK) & (key > qi + no_prev)))
    bias = jnp.where(valid, 0.0, -1e30)
    bias = jnp.concatenate([bias] * Q_PER_KV, axis=1)
    zeros = jnp.zeros((HEAD_DIM, ATTN_BLOCK), BF16)
    ssq = jnp.zeros((1, ATTN_BLOCK), F32)
    for hk in range(N_KV_HEADS):
        lt = slice((hk // 2) * LANES, (hk // 2 + 1) * LANES)
        kt = jnp.concatenate([kp_ref[:, lt], kc_ref[:, lt], km_ref[:, lt]], axis=0)
        rs = slice(hk * HEAD_DIM, (hk + 1) * HEAD_DIM)
        vt = jnp.concatenate([vtp_ref[rs, :], vtc_ref[rs, :], vtm_ref[rs, :]], axis=1)
        qs, sinks = [], []
        for g in range(Q_PER_KV):
            h = hk * Q_PER_KV + g
            qh = qt_ref[h * HEAD_DIM:(h + 1) * HEAD_DIM, :]
            qs.append(jnp.concatenate([qh, zeros] if hk % 2 == 0 else [zeros, qh], axis=0))
            sinks.append(jnp.broadcast_to(sink_ref[:, h:h + 1], (1, ATTN_BLOCK)))
        sink = jnp.concatenate(sinks, axis=1)
        s = jnp.dot(kt, jnp.concatenate(qs, axis=1), preferred_element_type=F32) + bias
        m = jnp.maximum(jnp.max(s, axis=0, keepdims=True), sink)
        p = jnp.exp(s - m)
        denom = jnp.sum(p, axis=0, keepdims=True) + jnp.exp(sink - m)
        o = jnp.dot(vt, p.astype(BF16), preferred_element_type=F32) * (1.0 / denom)
        for g in range(Q_PER_KV):
            h = hk * Q_PER_KV + g
            og = o[:, g * ATTN_BLOCK:(g + 1) * ATTN_BLOCK]
            yt_scr[h * HEAD_DIM:(h + 1) * HEAD_DIM, :] = og
            ssq = ssq + jnp.sum(og * og, axis=0, keepdims=True)
    inv = lax.rsqrt(ssq * (1.0 / ATTN_WIDTH) + NORM_EPS)
    o_ref[...] = ((yt_scr[...] * inv).T * g_ref[...]).astype(BF16)


def _attention(qt, k, vt, k_meta, vt_meta, sinks, g_attn, bsz, seq):
    nb = seq // ATTN_BLOCK
    cur = lambda b, n: (b * nb + n, 0)
    prev = lambda b, n: (b * nb + jnp.maximum(n - 1, 0), 0)
    cur_t = lambda b, n: (0, b * nb + n)
    prev_t = lambda b, n: (0, b * nb + jnp.maximum(n - 1, 0))
    const = lambda b, n: (0, 0)
    return pl.pallas_call(
        _attn_kernel,
        grid=(bsz, nb),
        in_specs=[
            pl.BlockSpec((ATTN_WIDTH, ATTN_BLOCK), cur_t),
            pl.BlockSpec((ATTN_BLOCK, KV_WIDTH), prev),
            pl.BlockSpec((ATTN_BLOCK, KV_WIDTH), cur),
            pl.BlockSpec((N_META, KV_WIDTH), const),
            pl.BlockSpec((KV_WIDTH, ATTN_BLOCK), prev_t),
            pl.BlockSpec((KV_WIDTH, ATTN_BLOCK), cur_t),
            pl.BlockSpec((KV_WIDTH, N_META), const),
            pl.BlockSpec((1, N_Q_HEADS), const),
            pl.BlockSpec((1, ATTN_WIDTH), const),
        ],
        out_specs=pl.BlockSpec((ATTN_BLOCK, ATTN_WIDTH), cur),
        out_shape=jax.ShapeDtypeStruct((bsz * seq, ATTN_WIDTH), BF16),
        scratch_shapes=[pltpu.VMEM((ATTN_WIDTH, ATTN_BLOCK), F32)],
        compiler_params=pltpu.CompilerParams(
            dimension_semantics=("arbitrary", "arbitrary"), vmem_limit_bytes=VMEM_LIMIT),
        name="attention",
    )(qt, k, k, k_meta, vt, vt, vt_meta, sinks, g_attn)


def _ssm_params(a_re, a_im, log_dt, b_re, b_im, c_re, c_im, seg_len):
    dt = jnp.exp(log_dt.astype(F32))[:, None]
    lam_re = jnp.minimum(a_re.astype(F32), -1e-4)
    lam_im = a_im.astype(F32)
    z_re, z_im = lam_re * dt, lam_im * dt
    mag = jnp.exp(z_re)
    abar_re, abar_im = mag * jnp.cos(z_im), mag * jnp.sin(z_im)
    den = lam_re * lam_re + lam_im * lam_im
    n_re, n_im = abar_re - 1.0, abar_im
    coef_re = (n_re * lam_re + n_im * lam_im) / den
    coef_im = (n_im * lam_re - n_re * lam_im) / den
    br, bi = b_re.astype(F32), b_im.astype(F32)
    bb_re = coef_re[..., None] * br - coef_im[..., None] * bi
    bb_im = coef_re[..., None] * bi + coef_im[..., None] * br

    groups_per_tile = SSM_COL // SSM_GROUP
    eye = jnp.eye(groups_per_tile, dtype=F32)

    def in_tile(bb):
        t = bb.reshape(2, SSM_COL_TILES, groups_per_tile, SSM_STATE, SSM_GROUP)
        t = jnp.einsum('xtgpc,gh->xtgchp', t, eye)
        return t.reshape(2, SSM_COL_TILES, SSM_COL, SSM_COL_LANES).astype(BF16)

    def out_tile(cc):
        t = cc.reshape(2, SSM_COL_TILES, groups_per_tile, SSM_GROUP, SSM_STATE)
        t = jnp.einsum('xtgcp,gh->xtgphc', t, eye)
        return t.reshape(2, SSM_COL_TILES, SSM_COL_LANES, SSM_COL).astype(BF16)

    def powers(exps):
        e = exps.astype(F32)[:, None, None]
        pm = jnp.exp(e * z_re[None])
        return jnp.stack([(pm * jnp.cos(e * z_im[None])).reshape(len(exps), SSM_LANES),
                          (pm * jnp.sin(e * z_im[None])).reshape(len(exps), SSM_LANES)])

    return dict(
        b=in_tile(jnp.stack([bb_re, bb_im])),
        c=out_tile(jnp.stack([c_re.astype(F32), -c_im.astype(F32)])),
        a=jnp.stack([abar_re.reshape(1, SSM_LANES), abar_im.reshape(1, SSM_LANES)]),
        a_seg=powers(jnp.array([seg_len])),
        a_meta=powers(jnp.arange(N_META - 1, -1, -1)),
    )


def _gelu_tanh(t):
    return 0.5 * t * (1.0 + jnp.tanh(math.sqrt(2.0 / math.pi) * (t + 0.044715 * (t * t * t))))


def _ssm_kernel(u_ref, um_ref, b_ref, c_ref, a_ref, aseg_ref,
                ameta_ref, perm_ref, d_ref, wglu_ref, bglu_ref, g_ref, o_ref,
                xre, xim, hre, him, car_re, car_im, cin_re, cin_im, y_scr):
    chunk = u_ref.shape[0]
    seg = chunk // SUBLANES

    @pl.when(pl.program_id(1) == 0)
    def _():
        um = um_ref[...].astype(BF16)
        for ct in range(SSM_COL_TILES):
            ub = um[:, ct * SSM_COL:(ct + 1) * SSM_COL]
            ls = slice(ct * SSM_COL_LANES, (ct + 1) * SSM_COL_LANES)
            xr = jnp.dot(ub, b_ref[0, ct], preferred_element_type=F32)
            xi = jnp.dot(ub, b_ref[1, ct], preferred_element_type=F32)
            pr = ameta_ref[0, :, ls]
            pi = ameta_ref[1, :, ls]
            car_re[:, ls] = jnp.sum(pr * xr - pi * xi, axis=0, keepdims=True)
            car_im[:, ls] = jnp.sum(pr * xi + pi * xr, axis=0, keepdims=True)

    up = jnp.dot(perm_ref[0], u_ref[...].astype(BF16), preferred_element_type=F32).astype(BF16)
    for ct in range(SSM_COL_TILES):
        ub = up[:, ct * SSM_COL:(ct + 1) * SSM_COL]
        ls = slice(ct * SSM_COL_LANES, (ct + 1) * SSM_COL_LANES)
        xre[:, ls] = jnp.dot(ub, b_ref[0, ct], preferred_element_type=F32)
        xim[:, ls] = jnp.dot(ub, b_ref[1, ct], preferred_element_type=F32)

    for lb in range(SSM_LANES // SCAN_LANES):
        ls = slice(lb * SCAN_LANES, (lb + 1) * SCAN_LANES)
        ar = jnp.broadcast_to(a_ref[0, :, ls], (SUBLANES, SCAN_LANES))
        ai = jnp.broadcast_to(a_ref[1, :, ls], (SUBLANES, SCAN_LANES))

        def end_body(k2, carry):
            hr, hi = carry
            r0 = pl.multiple_of(k2 * 2 * SUBLANES, 2 * SUBLANES)
            first = pl.ds(r0, SUBLANES)
            second = pl.ds(r0 + SUBLANES, SUBLANES)
            h0r = ar * hr - ai * hi + xre[first, ls]
            h0i = ar * hi + ai * hr + xim[first, ls]
            return ar * h0r - ai * h0i + xre[second, ls], ar * h0i + ai * h0r + xim[second, ls]

        zero = jnp.zeros((SUBLANES, SCAN_LANES), F32)
        er, ei = lax.fori_loop(0, seg // 2, end_body, (zero, zero))
        cin_re[:, ls] = er
        cin_im[:, ls] = ei

    cr = car_re[...]
    ci = car_im[...]
    sr = aseg_ref[0]
    si = aseg_ref[1]
    for r in range(SUBLANES):
        er = cin_re[r:r + 1, :]
        ei = cin_im[r:r + 1, :]
        cin_re[r:r + 1, :] = cr
        cin_im[r:r + 1, :] = ci
        cr, ci = sr * cr - si * ci + er, sr * ci + si * cr + ei
    car_re[...] = cr
    car_im[...] = ci

    pair = 2 * SUBLANES
    for lb in range(SSM_LANES // SCAN_LANES):
        ls = slice(lb * SCAN_LANES, (lb + 1) * SCAN_LANES)
        ar = jnp.broadcast_to(a_ref[0, :, ls], (SUBLANES, SCAN_LANES))
        ai = jnp.broadcast_to(a_ref[1, :, ls], (SUBLANES, SCAN_LANES))

        def scan_body(k2, carry):
            hr, hi = carry
            r0 = pl.multiple_of(k2 * pair, pair)
            first = pl.ds(r0, SUBLANES)
            second = pl.ds(r0 + SUBLANES, SUBLANES)
            h0r = ar * hr - ai * hi + xre[first, ls]
            h0i = ar * hi + ai * hr + xim[first, ls]
            h1r = ar * h0r - ai * h0i + xre[second, ls]
            h1i = ar * h0i + ai * h0r + xim[second, ls]
            hre[pl.ds(r0, pair), ls] = jnp.concatenate([h0r, h1r], axis=0).astype(BF16)
            him[pl.ds(r0, pair), ls] = jnp.concatenate([h0i, h1i], axis=0).astype(BF16)
            return h1r, h1i

        lax.fori_loop(0, seg // 2, scan_body, (cin_re[:, ls], cin_im[:, ls]))

    n_lt = SSM_WIDTH // LANES
    for ct in range(SSM_COL_TILES):
        ls = slice(ct * SSM_COL_LANES, (ct + 1) * SSM_COL_LANES)
        y = (jnp.dot(hre[:, ls], c_ref[0, ct], preferred_element_type=F32)
             + jnp.dot(him[:, ls], c_ref[1, ct], preferred_element_type=F32))
        for k in range(seg):
            for jj in range(SSM_COL // LANES):
                y_scr[ct * (SSM_COL // LANES) + jj, pl.ds(k, SUBLANES, stride=seg), :] = (
                    y[k * SUBLANES:(k + 1) * SUBLANES, jj * LANES:(jj + 1) * LANES])

    y = jnp.concatenate([y_scr[j] for j in range(n_lt)], axis=1)
    y = _gelu_tanh(y + d_ref[...] * u_ref[...])
    gate = jnp.dot(y.astype(BF16), wglu_ref[...], preferred_element_type=F32) + bglu_ref[...]
    y = y * _sigmoid(gate)
    o_ref[...] = _rms(y, g_ref[...]).astype(BF16)


def _ssm(u, u_meta, prm, d_skip, w_glu_bf, b_glu, g_ssm, bsz, seq, chunk):
    nc = seq // chunk
    seg = chunk // SUBLANES
    row = lambda b, c: (b * nc + c, 0)
    c2 = lambda b, c: (0, 0)
    c3 = lambda b, c: (0, 0, 0)
    p = jnp.arange(chunk)
    fwd = ((p % SUBLANES) * seg + p // SUBLANES)[:, None] == p[None, :]
    perm = fwd.astype(BF16)[None]
    return pl.pallas_call(
        _ssm_kernel,
        grid=(bsz, nc),
        in_specs=[
            pl.BlockSpec((chunk, SSM_WIDTH), row),
            pl.BlockSpec((N_META, SSM_WIDTH), c2),
            pl.BlockSpec((2, SSM_COL_TILES, SSM_COL, SSM_COL_LANES), lambda b, c: (0, 0, 0, 0)),
            pl.BlockSpec((2, SSM_COL_TILES, SSM_COL_LANES, SSM_COL), lambda b, c: (0, 0, 0, 0)),
            pl.BlockSpec((2, 1, SSM_LANES), c3),
            pl.BlockSpec((2, 1, SSM_LANES), c3),
            pl.BlockSpec((2, N_META, SSM_LANES), c3),
            pl.BlockSpec((1, chunk, chunk), c3),
            pl.BlockSpec((1, SSM_WIDTH), c2),
            pl.BlockSpec((SSM_WIDTH, SSM_WIDTH), c2),
            pl.BlockSpec((1, SSM_WIDTH), c2),
            pl.BlockSpec((1, SSM_WIDTH), c2),
        ],
        out_specs=pl.BlockSpec((chunk, SSM_WIDTH), row),
        out_shape=jax.ShapeDtypeStruct((bsz * seq, SSM_WIDTH), BF16),
        scratch_shapes=[
            pltpu.VMEM((chunk, SSM_LANES), F32),
            pltpu.VMEM((chunk, SSM_LANES), F32),
            pltpu.VMEM((chunk, SSM_LANES), BF16),
            pltpu.VMEM((chunk, SSM_LANES), BF16),
            pltpu.VMEM((1, SSM_LANES), F32),
            pltpu.VMEM((1, SSM_LANES), F32),
            pltpu.VMEM((SUBLANES, SSM_LANES), F32),
            pltpu.VMEM((SUBLANES, SSM_LANES), F32),
            pltpu.VMEM((SSM_WIDTH // LANES, chunk, LANES), F32),
        ],
        compiler_params=pltpu.CompilerParams(
            dimension_semantics=("arbitrary", "arbitrary"), vmem_limit_bytes=VMEM_LIMIT),
        name="ssm",
    )(u, u_meta, prm["b"], prm["c"], prm["a"], prm["a_seg"],
      prm["a_meta"], perm, d_skip, w_glu_bf, b_glu, g_ssm)


def _outproj_kernel(ya_ref, ys_ref, x_ref, wo_ref, bo_ref, gf_ref, wr_ref, br_ref,
                    h_ref, xn_ref, lg_ref):
    mix = (jnp.dot(ya_ref[...], wo_ref[:ATTN_WIDTH, :], preferred_element_type=F32)
           + jnp.dot(ys_ref[...], wo_ref[ATTN_WIDTH:, :], preferred_element_type=F32))
    h = x_ref[...] + mix + bo_ref[...]
    h_ref[...] = h
    n = _rms(h, gf_ref[...])
    n_hi = n.astype(BF16)
    bits = lax.bitcast_convert_type(n_hi.astype(F32), jnp.uint32)
    xn_ref[...] = (bits[:, :D_MODEL // 2] >> 16) | (bits[:, D_MODEL // 2:] & jnp.uint32(0xFFFF0000))
    n_lo = (n - n_hi.astype(F32)).astype(BF16)
    hi = jnp.dot(n_hi, wr_ref[...], preferred_element_type=F32)
    lo = jnp.dot(n_lo, wr_ref[:, :ROUTER_LANES], preferred_element_type=F32)
    lg_ref[...] = hi[:, :ROUTER_LANES] + hi[:, ROUTER_LANES:] + lo + br_ref[...]


def _outproj(y_attn, y_ssm, xf, w_out_bf, b_out, g_ffn, w_router_pad, b_router_pad, row_tile):
    rows = xf.shape[0]
    row = lambda i: (i, 0)
    const = lambda i: (0, 0)
    return pl.pallas_call(
        _outproj_kernel,
        grid=(rows // row_tile,),
        in_specs=[
            pl.BlockSpec((row_tile, ATTN_WIDTH), row),
            pl.BlockSpec((row_tile, SSM_WIDTH), row),
            pl.BlockSpec((row_tile, D_MODEL), row),
            pl.BlockSpec((D_MODEL, D_MODEL), const),
            pl.BlockSpec((1, D_MODEL), const),
            pl.BlockSpec((1, D_MODEL), const),
            pl.BlockSpec((D_MODEL, 2 * ROUTER_LANES), const),
            pl.BlockSpec((1, ROUTER_LANES), const),
        ],
        out_specs=[
            pl.BlockSpec((row_tile, D_MODEL), row),
            pl.BlockSpec((row_tile, D_MODEL // 2), row),
            pl.BlockSpec((row_tile, ROUTER_LANES), row),
        ],
        out_shape=[
            jax.ShapeDtypeStruct((rows, D_MODEL), F32),
            jax.ShapeDtypeStruct((rows, D_MODEL // 2), jnp.uint32),
            jax.ShapeDtypeStruct((rows, ROUTER_LANES), F32),
        ],
        compiler_params=pltpu.CompilerParams(
            dimension_semantics=("arbitrary",), vmem_limit_bytes=VMEM_LIMIT),
        name="outproj",
    )(y_attn, y_ssm, xf, w_out_bf, b_out, g_ffn, w_router_pad, b_router_pad)


def _moe_kernel(ie_ref, ir_ref, in_ref, ic_ref, ni_ref, rt_ref, rd_ref,
                xn_hbm, w1a_ref, w1b_ref, b1g_ref, b1u_ref, w2_ref, b2_ref, perm_ref, yk_hbm,
                stage, x_scr, acc, w1g, w1u, w2b, issued, sem_g, sem_out):
    i = pl.program_id(0)
    f = pl.program_id(1)
    nf = pl.num_programs(1)
    max_items = pl.num_programs(0)
    tf = w2_ref.shape[1]

    @pl.when(i < ni_ref[0])
    def _():
        r0 = ir_ref[i]
        n = in_ref[i]
        cnt = ic_ref[i]
        dump_base = yk_hbm.shape[0] - (max_items - i) * MOE_PAD
        nxt = jnp.minimum(i + 1, max_items - 1)
        n_next = jnp.where(i + 1 < ni_ref[0], in_ref[nxt], 0)
        r0_next = ir_ref[nxt]

        def gather_row(first_row, j):
            tok = rt_ref[first_row + j]
            pltpu.make_async_copy(xn_hbm.at[pl.ds(tok, 1), :], stage.at[pl.ds(j, 1), :],
                                  sem_g).start()

        def gather_wait_chunk():
            pltpu.make_async_copy(xn_hbm.at[pl.ds(0, MOE_PAD), :],
                                  stage.at[pl.ds(0, MOE_PAD), :], sem_g).wait()

        def scattered(c, carry):
            pltpu.make_async_copy(acc.at[pl.ds(0, MOE_PAD), :],
                                  yk_hbm.at[pl.ds(0, MOE_PAD), :], sem_out).wait()
            return carry

        @pl.when(f == 0)
        def _():
            @pl.when(i == 0)
            def _():
                def first(j, c):
                    gather_row(r0, j)
                    return c

                lax.fori_loop(0, n, first, 0)

            def landed(c, carry):
                gather_wait_chunk()
                return carry

            lax.fori_loop(0, n // MOE_PAD, landed, 0)

            def unpack(c, carry):
                rows_c = pl.ds(pl.multiple_of(c * MOE_PAD, MOE_PAD), MOE_PAD)
                u = stage[rows_c, :]
                x_scr[rows_c, :D_MODEL // 2] = lax.bitcast_convert_type(u << 16, F32).astype(BF16)
                x_scr[rows_c, D_MODEL // 2:] = lax.bitcast_convert_type(
                    u & jnp.uint32(0xFFFF0000), F32).astype(BF16)
                return carry

            lax.fori_loop(0, n // MOE_PAD, unpack, 0)
            issued[0] = 0

            @pl.when(i > 0)
            def _():
                lax.fori_loop(0, in_ref[jnp.maximum(i - 1, 0)] // MOE_PAD, scattered, 0)

            bias = jnp.broadcast_to(b2_ref[0], (MOE_PAD, D_MODEL))

            def init(c, _):
                acc[pl.ds(pl.multiple_of(c * MOE_PAD, MOE_PAD), MOE_PAD), :] = bias
                return 0

            lax.fori_loop(0, n // MOE_PAD, init, 0)

        half = PERM_COLS // 2
        k_half = D_MODEL // 2
        for hh, w1_ref in enumerate((w1a_ref, w1b_ref)):
            for c in range(2 * tf // PERM_COLS):
                t = jnp.dot(w1_ref[0, :, c * PERM_COLS:(c + 1) * PERM_COLS].astype(BF16),
                            perm_ref[...], preferred_element_type=F32)
                rs = slice(hh * k_half, (hh + 1) * k_half)
                w1g[rs, c * half:(c + 1) * half] = t[:, :half].astype(BF16)
                w1u[rs, c * half:(c + 1) * half] = t[:, half:].astype(BF16)
        w2b[...] = w2_ref[0].astype(BF16)

        def scatter(r, rows):
            for t in range(rows):
                dst = rd_ref[r0 + r + t]
                if t >= rows - MOE_PAD:
                    dst = jnp.where(r + t < cnt, dst, dump_base + (t - (rows - MOE_PAD)))
                pltpu.make_async_copy(acc.at[pl.ds(r + t, 1), :], yk_hbm.at[pl.ds(dst, 1), :],
                                      sem_out).start()

        def block(r, rows, prefetch):
            if prefetch:
                base = issued[0]
                for t in range(rows // 4):
                    gather_row(r0_next, base + t)
                issued[0] = base + rows // 4
            sl = pl.ds(r, rows)
            xb = x_scr[sl, :]
            g = jnp.dot(xb, w1g[...], preferred_element_type=F32) + b1g_ref[0]
            up = jnp.dot(xb, w1u[...], preferred_element_type=F32) + b1u_ref[0]
            g = jnp.minimum(g, SWIGLU_LIMIT)
            up = jnp.clip(up, -SWIGLU_LIMIT, SWIGLU_LIMIT)
            act = g * _sigmoid(SWIGLU_ALPHA * g) * (up + 1.0)
            acc[sl, :] += jnp.dot(act.astype(BF16), w2b[...], preferred_element_type=F32)

        def run(r, rows):
            can = issued[0] + rows // 4 <= n_next

            @pl.when(can)
            def _():
                block(r, rows, True)

            @pl.when(jnp.logical_not(can))
            def _():
                block(r, rows, False)

        n_big = n // MOE_BLOCK
        rem = n % MOE_BLOCK
        last = f == nf - 1

        def big(b, _):
            run(pl.multiple_of(b * MOE_BLOCK, MOE_BLOCK), MOE_BLOCK)
            return 0

        @pl.when(jnp.logical_not(last))
        def _():
            lax.fori_loop(0, n_big, big, 0)

        @pl.when(last)
        def _():
            for b in range(MOE_ITEM_ROWS // MOE_BLOCK):
                @pl.when(b < n_big)
                def _(b=b):
                    run(b * MOE_BLOCK, MOE_BLOCK)
                    scatter(b * MOE_BLOCK, MOE_BLOCK)

        def tail(r, rows):
            run(r, rows)

            @pl.when(last)
            def _():
                scatter(r, rows)

        @pl.when(rem >= 2 * MOE_PAD)
        def _():
            tail(pl.multiple_of(n_big * MOE_BLOCK, MOE_BLOCK), 2 * MOE_PAD)

        @pl.when(rem % (2 * MOE_PAD) != 0)
        def _():
            tail(pl.multiple_of(n - MOE_PAD, MOE_PAD), MOE_PAD)

        @pl.when(last)
        def _():
            @pl.when(i == ni_ref[0] - 1)
            def _():
                lax.fori_loop(0, n // MOE_PAD, scattered, 0)

            def rest(j, c):
                gather_row(r0_next, j)
                return c

            lax.fori_loop(issued[0], n_next, rest, 0)


def _experts(xn, item_e, item_r0, item_n, item_cnt, n_items, row_tok, row_dst, n_out_rows,
             w1, b1g, b1u, w2, b2, tf):
    max_items = item_e.shape[0]
    nf = D_FF // tf
    idx = jnp.arange(PERM_COLS)
    dst = jnp.where(idx % 2 == 0, idx // 2, PERM_COLS // 2 + idx // 2)
    perm = (dst[:, None] == jnp.arange(PERM_COLS)[None, :]).astype(BF16)

    def ex(i, ie, ni):
        return ie[jnp.minimum(i, ni[0] - 1)]

    def ff(i, f, ni):
        return jnp.where(i < ni[0], f, nf - 1)

    grid_spec = pltpu.PrefetchScalarGridSpec(
        num_scalar_prefetch=7,
        grid=(max_items, nf),
        in_specs=[
            pl.BlockSpec(memory_space=pl.ANY),
            pl.BlockSpec((1, D_MODEL // 2, 2 * tf), lambda i, f, ie, ir, im, ic, ni, rt, rd: (ex(i, ie, ni), 0, ff(i, f, ni))),
            pl.BlockSpec((1, D_MODEL // 2, 2 * tf), lambda i, f, ie, ir, im, ic, ni, rt, rd: (ex(i, ie, ni), 1, ff(i, f, ni))),
            pl.BlockSpec((1, 1, tf), lambda i, f, ie, ir, im, ic, ni, rt, rd: (ex(i, ie, ni), 0, ff(i, f, ni))),
            pl.BlockSpec((1, 1, tf), lambda i, f, ie, ir, im, ic, ni, rt, rd: (ex(i, ie, ni), 0, ff(i, f, ni))),
            pl.BlockSpec((1, tf, D_MODEL), lambda i, f, ie, ir, im, ic, ni, rt, rd: (ex(i, ie, ni), ff(i, f, ni), 0)),
            pl.BlockSpec((1, 1, D_MODEL), lambda i, f, ie, ir, im, ic, ni, rt, rd: (ex(i, ie, ni), 0, 0)),
            pl.BlockSpec((PERM_COLS, PERM_COLS), lambda i, f, ie, ir, im, ic, ni, rt, rd: (0, 0)),
        ],
        out_specs=pl.BlockSpec(memory_space=pl.ANY),
        scratch_shapes=[
            pltpu.VMEM((MOE_ITEM_ROWS, D_MODEL // 2), jnp.uint32),
            pltpu.VMEM((MOE_ITEM_ROWS, D_MODEL), BF16),
            pltpu.VMEM((MOE_ITEM_ROWS, D_MODEL), F32),
            pltpu.VMEM((D_MODEL, tf), BF16),
            pltpu.VMEM((D_MODEL, tf), BF16),
            pltpu.VMEM((tf, D_MODEL), BF16),
            pltpu.SMEM((1,), jnp.int32),
            pltpu.SemaphoreType.DMA(()),
            pltpu.SemaphoreType.DMA(()),
        ],
    )
    return pl.pallas_call(
        _moe_kernel,
        grid_spec=grid_spec,
        out_shape=jax.ShapeDtypeStruct((n_out_rows, D_MODEL), F32),
        compiler_params=pltpu.CompilerParams(
            dimension_semantics=("arbitrary", "arbitrary"), vmem_limit_bytes=VMEM_LIMIT),
        name="experts",
    )(item_e, item_r0, item_n, item_cnt, n_items, row_tok, row_dst, xn, w1, w1, b1g, b1u, w2, b2, perm)


def _combine_kernel(h_ref, y0_ref, y1_ref, y2_ref, y3_ref, gate_ref, g_ref, o_ref):
    acc = h_ref[...]
    gates = gate_ref[...]
    for k, y_ref in enumerate((y0_ref, y1_ref, y2_ref, y3_ref)):
        acc = acc + gates[:, k:k + 1] * y_ref[...]
    o_ref[...] = _rms(acc, g_ref[...])


def _combine(h1, yk, gates_pad, g_final, row_tile):
    rows = h1.shape[0]
    n_tiles = rows // row_tile
    row = lambda i: (i, 0)
    const = lambda i: (0, 0)
    slot = lambda k: (lambda i: (k * n_tiles + i, 0))
    return pl.pallas_call(
        _combine_kernel,
        grid=(n_tiles,),
        in_specs=[pl.BlockSpec((row_tile, D_MODEL), row)]
        + [pl.BlockSpec((row_tile, D_MODEL), slot(k)) for k in range(TOP_K)]
        + [pl.BlockSpec((row_tile, LANES), row), pl.BlockSpec((1, D_MODEL), const)],
        out_specs=pl.BlockSpec((row_tile, D_MODEL), row),
        out_shape=jax.ShapeDtypeStruct((rows, D_MODEL), F32),
        compiler_params=pltpu.CompilerParams(
            dimension_semantics=("arbitrary",), vmem_limit_bytes=VMEM_LIMIT),
        name="combine",
    )(h1, yk, yk, yk, yk, gates_pad, g_final)


def kernel(x, meta_tokens, g_mix, w_in, b_in, attn_sinks, ssm_a_re, ssm_a_im, ssm_log_dt,
           ssm_b_re, ssm_b_im, ssm_c_re, ssm_c_im, ssm_d, w_glu, b_glu, g_attn_out, g_ssm_out,
           w_out, b_out, g_ffn, w_router, b_router, w_mlp1, b_mlp1, w_mlp2, b_mlp2, g_final):
    bsz, seq, _ = x.shape
    rows = bsz * seq
    row_tile = min(ROW_TILE, seq)
    chunk = min(SSM_CHUNK, seq)
    assert seq % ATTN_BLOCK == 0 and seq % row_tile == 0 and seq % chunk == 0
    xf = x.reshape(rows, D_MODEL)

    w_in_bf = w_in[0].astype(BF16)
    cos_r, sin_r = _rope_tables(N_META + jnp.arange(seq))
    cos_m, sin_m = _rope_tables(jnp.arange(N_META))
    qt, k, vt, u = _inproj(xf, g_mix, w_in_bf, b_in, cos_r, sin_r, row_tile, True)
    _, k_meta, v_meta, u_meta = _inproj(meta_tokens, g_mix, w_in_bf, b_in, cos_m, sin_m,
                                        N_META, False)

    y_attn = _attention(qt, k, vt, k_meta, v_meta.T, attn_sinks, g_attn_out, bsz, seq)

    prm = _ssm_params(ssm_a_re[0], ssm_a_im[0], ssm_log_dt[0], ssm_b_re[0], ssm_b_im[0],
                      ssm_c_re[0], ssm_c_im[0], chunk // SUBLANES)
    y_ssm = _ssm(u, u_meta, prm, ssm_d, w_glu[0].astype(BF16), b_glu, g_ssm_out, bsz, seq, chunk)

    w_router_pad = jnp.pad(w_router[0], ((0, 0), (0, ROUTER_LANES - N_EXPERTS)))
    b_router_pad = jnp.pad(b_router, ((0, 0), (0, ROUTER_LANES - N_EXPERTS)))
    w_router_hi = w_router_pad.astype(BF16)
    w_router_lo = (w_router_pad - w_router_hi.astype(F32)).astype(BF16)
    w_router_pad = jnp.concatenate([w_router_hi, w_router_lo], axis=1)
    h1, xn, logits = _outproj(y_attn, y_ssm, xf, w_out[0].astype(BF16), b_out, g_ffn,
                              w_router_pad, b_router_pad, row_tile)

    top_val, top_idx = lax.top_k(logits[:, :N_EXPERTS], TOP_K)
    gates = jax.nn.softmax(top_val, axis=-1)
    chosen = (top_idx[:, :, None] == jnp.arange(N_EXPERTS)[None, None, :])
    counts = jnp.sum(chosen.astype(jnp.int32), axis=(0, 1))
    padded = ((counts + MOE_PAD - 1) // MOE_PAD) * MOE_PAD
    n_assign = rows * TOP_K
    order = jnp.argsort(top_idx.reshape(-1), stable=True).astype(jnp.int32)
    row_tok = jnp.pad(order // TOP_K, (0, MOE_PAD))
    row_dst = jnp.pad((order % TOP_K) * rows + order // TOP_K, (0, MOE_PAD))
    starts = jnp.cumsum(counts) - counts
    per_e = (padded + MOE_ITEM_ROWS - 1) // MOE_ITEM_ROWS
    item_ends = jnp.cumsum(per_e)
    max_items = N_EXPERTS + n_assign // MOE_ITEM_ROWS
    slot = jnp.arange(max_items, dtype=jnp.int32)
    item_e = jnp.minimum(jnp.searchsorted(item_ends, slot, side='right'), N_EXPERTS - 1)
    piece = slot - (item_ends - per_e)[item_e]
    item_r0 = (starts[item_e] + piece * MOE_ITEM_ROWS).astype(jnp.int32)
    item_n = jnp.clip(padded[item_e] - piece * MOE_ITEM_ROWS, 0, MOE_ITEM_ROWS).astype(jnp.int32)
    item_cnt = jnp.clip(counts[item_e] - piece * MOE_ITEM_ROWS, 0, MOE_ITEM_ROWS).astype(jnp.int32)
    n_items = item_ends[-1].astype(jnp.int32).reshape(1)

    tf = MOE_FF_TILE
    b1 = b_mlp1[0].reshape(N_EXPERTS, 1, D_FF, 2)
    b2 = b_mlp2[0].reshape(N_EXPERTS, 1, D_MODEL)
    yk = _experts(xn, item_e.astype(jnp.int32), item_r0, item_n, item_cnt, n_items, row_tok, row_dst,
                  n_assign + max_items * MOE_PAD, w_mlp1[0], b1[..., 0], b1[..., 1], w_mlp2[0], b2, tf)

    gates_pad = jnp.pad(gates, ((0, 0), (0, LANES - TOP_K)))
    out = _combine(h1, yk, gates_pad, g_final.reshape(1, D_MODEL), row_tile)
    return out.reshape(bsz, seq, D_MODEL)
```

```python
import functools
import math

import jax
import jax.numpy as jnp
from jax import lax
from jax.experimental import pallas as pl
from jax.experimental.pallas import tpu as pltpu

F32 = jnp.float32
BF16 = jnp.bfloat16

D_MODEL = 2048
N_META = 16
HEAD_DIM = 64
ATTN_WIDTH = 1024
N_Q_HEADS = 16
N_KV_HEADS = 4
Q_PER_KV = 4
KV_WIDTH = 256
ATTN_BLOCK = 128
ROPE_THETA = 10000.0
SSM_WIDTH = 1024
SSM_GROUP = 16
N_SSM_GROUPS = 64
SSM_STATE = 64
SSM_LANES = N_SSM_GROUPS * SSM_STATE
IN_WIDTH = 2560
N_EXPERTS = 32
TOP_K = 4
D_FF = 2048
SWIGLU_LIMIT = 7.0
SWIGLU_ALPHA = 1.702
NORM_EPS = 1e-5

LANES = 128
SUBLANES = 8
VMEM_LIMIT = 58 * 1024 * 1024

ROW_TILE = 512
SSM_CHUNK = 256
SSM_COL = 256
SSM_COL_TILES = SSM_WIDTH // SSM_COL
SSM_COL_LANES = SSM_LANES // SSM_COL_TILES
SCAN_LANES = 512
MOE_PAD = 128
MOE_BLOCK = 512
MOE_ITEM_ROWS = 2176
MOE_FF_TILE = 256
ROUTER_LANES = 128
PERM_COLS = 256


def _rms(t, gain):
    return t * lax.rsqrt(jnp.mean(t * t, axis=-1, keepdims=True) + NORM_EPS) * gain


def _sigmoid(t):
    return 1.0 / (1.0 + jnp.exp(-t))


def _inproj_kernel(x_ref, g_ref, w_ref, b_ref, cos_ref, sin_ref, q_ref, k_ref, v_ref, u_ref,
                   *, transposed):
    n = _rms(x_ref[...], g_ref[...]).astype(BF16)
    cos = cos_ref[...]
    sin = sin_ref[...]
    lane = lax.broadcasted_iota(jnp.int32, cos.shape, 1)
    first_half = (lane % HEAD_DIM) < (HEAD_DIM // 2)

    def proj(c0, c1):
        return jnp.dot(n, w_ref[:, c0:c1], preferred_element_type=F32) + b_ref[:, c0:c1]

    def rope(t):
        partner = jnp.where(first_half, pltpu.roll(t, LANES - HEAD_DIM // 2, 1),
                            pltpu.roll(t, HEAD_DIM // 2, 1))
        return t * cos + partner * sin

    def put(ref, j, t):
        if transposed:
            ref[j * LANES:(j + 1) * LANES, :] = t.T.astype(BF16)
        else:
            ref[:, j * LANES:(j + 1) * LANES] = t.astype(BF16)

    def proj_pair(c0):
        z = proj(c0, c0 + 2 * LANES)
        return z[:, :LANES], z[:, LANES:]

    scale = HEAD_DIM ** -0.5
    for jj in range(ATTN_WIDTH // (2 * LANES)):
        a, b = proj_pair(jj * 2 * LANES)
        put(q_ref, 2 * jj, rope(a) * scale)
        put(q_ref, 2 * jj + 1, rope(b) * scale)
    a, b = proj_pair(ATTN_WIDTH)
    k_ref[:, :LANES] = rope(a).astype(BF16)
    k_ref[:, LANES:] = rope(b).astype(BF16)
    a, b = proj_pair(ATTN_WIDTH + KV_WIDTH)
    put(v_ref, 0, a)
    put(v_ref, 1, b)
    c0 = ATTN_WIDTH + 2 * KV_WIDTH
    u_ref[...] = proj(c0, c0 + SSM_WIDTH)


def _inproj(xf, g_mix, w_in_bf, b_in, cos_t, sin_t, row_tile, transposed):
    rows = xf.shape[0]
    tab_blocks = cos_t.shape[0] // row_tile
    row = lambda i: (i, 0)
    col = lambda i: (0, i)
    tab = lambda i: (i % tab_blocks, 0)
    const = lambda i: (0, 0)
    if transposed:
        q_spec, q_shape = pl.BlockSpec((ATTN_WIDTH, row_tile), col), (ATTN_WIDTH, rows)
        v_spec, v_shape = pl.BlockSpec((KV_WIDTH, row_tile), col), (KV_WIDTH, rows)
    else:
        q_spec, q_shape = pl.BlockSpec((row_tile, ATTN_WIDTH), row), (rows, ATTN_WIDTH)
        v_spec, v_shape = pl.BlockSpec((row_tile, KV_WIDTH), row), (rows, KV_WIDTH)
    return pl.pallas_call(
        functools.partial(_inproj_kernel, transposed=transposed),
        grid=(rows // row_tile,),
        in_specs=[
            pl.BlockSpec((row_tile, D_MODEL), row),
            pl.BlockSpec((1, D_MODEL), const),
            pl.BlockSpec((D_MODEL, IN_WIDTH), const),
            pl.BlockSpec((1, IN_WIDTH), const),
            pl.BlockSpec((row_tile, LANES), tab),
            pl.BlockSpec((row_tile, LANES), tab),
        ],
        out_specs=[
            q_spec,
            pl.BlockSpec((row_tile, KV_WIDTH), row),
            v_spec,
            pl.BlockSpec((row_tile, SSM_WIDTH), row),
        ],
        out_shape=[
            jax.ShapeDtypeStruct(q_shape, BF16),
            jax.ShapeDtypeStruct((rows, KV_WIDTH), BF16),
            jax.ShapeDtypeStruct(v_shape, BF16),
            jax.ShapeDtypeStruct((rows, SSM_WIDTH), F32),
        ],
        compiler_params=pltpu.CompilerParams(
            dimension_semantics=("arbitrary",), vmem_limit_bytes=VMEM_LIMIT),
        name="inproj",
    )(xf, g_mix, w_in_bf, b_in, cos_t, sin_t)


def _rope_tables(positions):
    half = HEAD_DIM // 2
    inv_freq = jnp.power(ROPE_THETA, -jnp.arange(half, dtype=F32) / half)
    ang = positions.astype(F32)[:, None] * inv_freq[None, :]
    cos = jnp.tile(jnp.cos(ang), (1, LANES // half))
    sin = jnp.tile(jnp.sin(ang), (1, LANES // half))
    sign = jnp.where((jnp.arange(LANES) % HEAD_DIM) < half, -1.0, 1.0).astype(F32)
    return cos, sin * sign[None, :]


def _attn_kernel(qt_ref, kp_ref, kc_ref, km_ref, vtp_ref, vtc_ref, vtm_ref, sink_ref, g_ref, o_ref,
                 yt_scr):
    n = pl.program_id(1)
    n_keys = 2 * ATTN_BLOCK + N_META
    key = lax.broadcasted_iota(jnp.int32, (n_keys, ATTN_BLOCK), 0)
    qi = lax.broadcasted_iota(jnp.int32, (n_keys, ATTN_BLOCK), 1)
    cur_j = key - ATTN_BLOCK
    zeros = jnp.zeros((HEAD_DIM, ATTN_BLOCK), BF16)
    invs = []
    for sub in range(2):
        qcols = slice(sub * ATTN_BLOCK, (sub + 1) * ATTN_BLOCK)
        no_prev = jnp.where(n > 0, 0, ATTN_BLOCK) if sub == 0 else 0
        valid = ((key >= 2 * ATTN_BLOCK) | ((cur_j >= 0) & (cur_j <= qi))
                 | ((key < ATTN_BLOCK) & (key > qi + no_prev)))
        bias = jnp.where(valid, 0.0, -1e30)
        bias = jnp.concatenate([bias] * Q_PER_KV, axis=1)
        ssq = jnp.zeros((1, ATTN_BLOCK), F32)
        for hk in range(N_KV_HEADS):
            lt = slice((hk // 2) * LANES, (hk // 2 + 1) * LANES)
            rs = slice(hk * HEAD_DIM, (hk + 1) * HEAD_DIM)
            if sub == 0:
                k_prev, v_prev = kp_ref[:, lt], vtp_ref[rs, :]
            else:
                k_prev, v_prev = kc_ref[:ATTN_BLOCK, lt], vtc_ref[rs, :ATTN_BLOCK]
            k_own = kc_ref[sub * ATTN_BLOCK:(sub + 1) * ATTN_BLOCK, lt]
            v_own = vtc_ref[rs, qcols]
            kt = jnp.concatenate([k_prev, k_own, km_ref[:, lt]], axis=0)
            vt = jnp.concatenate([v_prev, v_own, vtm_ref[rs, :]], axis=1)
            qs, sinks = [], []
            for g in range(Q_PER_KV):
                h = hk * Q_PER_KV + g
                qh = qt_ref[h * HEAD_DIM:(h + 1) * HEAD_DIM, qcols]
                qs.append(jnp.concatenate([qh, zeros] if hk % 2 == 0 else [zeros, qh], axis=0))
                sinks.append(jnp.broadcast_to(sink_ref[:, h:h + 1], (1, ATTN_BLOCK)))
            sink = jnp.concatenate(sinks, axis=1)
            s = jnp.dot(kt, jnp.concatenate(qs, axis=1), preferred_element_type=F32) + bias
            m = jnp.maximum(jnp.max(s, axis=0, keepdims=True), sink)
            p = jnp.exp(s - m)
            denom = jnp.sum(p, axis=0, keepdims=True) + jnp.exp(sink - m)
            o = jnp.dot(vt, p.astype(BF16), preferred_element_type=F32) * (1.0 / denom)
            for g in range(Q_PER_KV):
                h = hk * Q_PER_KV + g
                og = o[:, g * ATTN_BLOCK:(g + 1) * ATTN_BLOCK]
                yt_scr[h * HEAD_DIM:(h + 1) * HEAD_DIM, qcols] = og
                ssq = ssq + jnp.sum(og * og, axis=0, keepdims=True)
        invs.append(lax.rsqrt(ssq * (1.0 / ATTN_WIDTH) + NORM_EPS))
    inv = jnp.concatenate(invs, axis=1)
    o_ref[...] = ((yt_scr[...] * inv).T * g_ref[...]).astype(BF16)


def _attention(qt, k, vt, k_meta, vt_meta, sinks, g_attn, bsz, seq):
    nb = seq // ATTN_BLOCK
    npair = nb // 2
    own = lambda b, n: (b * npair + n, 0)
    prev = lambda b, n: (b * nb + jnp.maximum(2 * n - 1, 0), 0)
    own_t = lambda b, n: (0, b * npair + n)
    prev_t = lambda b, n: (0, b * nb + jnp.maximum(2 * n - 1, 0))
    const = lambda b, n: (0, 0)
    return pl.pallas_call(
        _attn_kernel,
        grid=(bsz, npair),
        in_specs=[
            pl.BlockSpec((ATTN_WIDTH, 2 * ATTN_BLOCK), own_t),
            pl.BlockSpec((ATTN_BLOCK, KV_WIDTH), prev),
            pl.BlockSpec((2 * ATTN_BLOCK, KV_WIDTH), own),
            pl.BlockSpec((N_META, KV_WIDTH), const),
            pl.BlockSpec((KV_WIDTH, ATTN_BLOCK), prev_t),
            pl.BlockSpec((KV_WIDTH, 2 * ATTN_BLOCK), own_t),
            pl.BlockSpec((KV_WIDTH, N_META), const),
            pl.BlockSpec((1, N_Q_HEADS), const),
            pl.BlockSpec((1, ATTN_WIDTH), const),
        ],
        out_specs=pl.BlockSpec((2 * ATTN_BLOCK, ATTN_WIDTH), own),
        out_shape=jax.ShapeDtypeStruct((bsz * seq, ATTN_WIDTH), BF16),
        scratch_shapes=[pltpu.VMEM((ATTN_WIDTH, 2 * ATTN_BLOCK), F32)],
        compiler_params=pltpu.CompilerParams(
            dimension_semantics=("arbitrary", "arbitrary"), vmem_limit_bytes=VMEM_LIMIT),
        name="attention",
    )(qt, k, k, k_meta, vt, vt, vt_meta, sinks, g_attn)


def _ssm_params(a_re, a_im, log_dt, b_re, b_im, c_re, c_im, seg_len):
    dt = jnp.exp(log_dt.astype(F32))[:, None]
    lam_re = jnp.minimum(a_re.astype(F32), -1e-4)
    lam_im = a_im.astype(F32)
    z_re, z_im = lam_re * dt, lam_im * dt
    mag = jnp.exp(z_re)
    abar_re, abar_im = mag * jnp.cos(z_im), mag * jnp.sin(z_im)
    den = lam_re * lam_re + lam_im * lam_im
    n_re, n_im = abar_re - 1.0, abar_im
    coef_re = (n_re * lam_re + n_im * lam_im) / den
    coef_im = (n_im * lam_re - n_re * lam_im) / den
    br, bi = b_re.astype(F32), b_im.astype(F32)
    bb_re = coef_re[..., None] * br - coef_im[..., None] * bi
    bb_im = coef_re[..., None] * bi + coef_im[..., None] * br

    groups_per_tile = SSM_COL // SSM_GROUP
    eye = jnp.eye(groups_per_tile, dtype=F32)

    def in_tile(bb):
        t = bb.reshape(2, SSM_COL_TILES, groups_per_tile, SSM_STATE, SSM_GROUP)
        t = jnp.einsum('xtgpc,gh->xtgchp', t, eye)
        return t.reshape(2, SSM_COL_TILES, SSM_COL, SSM_COL_LANES).astype(BF16)

    def out_tile(cc):
        t = cc.reshape(2, SSM_COL_TILES, groups_per_tile, SSM_GROUP, SSM_STATE)
        t = jnp.einsum('xtgcp,gh->xtgphc', t, eye)
        return t.reshape(2, SSM_COL_TILES, SSM_COL_LANES, SSM_COL).astype(BF16)

    def powers(exps):
        e = exps.astype(F32)[:, None, None]
        pm = jnp.exp(e * z_re[None])
        return jnp.stack([(pm * jnp.cos(e * z_im[None])).reshape(len(exps), SSM_LANES),
                          (pm * jnp.sin(e * z_im[None])).reshape(len(exps), SSM_LANES)])

    return dict(
        b=in_tile(jnp.stack([bb_re, bb_im])),
        c=out_tile(jnp.stack([c_re.astype(F32), -c_im.astype(F32)])),
        a=jnp.stack([abar_re.reshape(1, SSM_LANES), abar_im.reshape(1, SSM_LANES)]),
        a_seg=powers(jnp.array([seg_len])),
        a_meta=powers(jnp.arange(N_META - 1, -1, -1)),
    )


def _gelu_tanh(t):
    return 0.5 * t * (1.0 + jnp.tanh(math.sqrt(2.0 / math.pi) * (t + 0.044715 * (t * t * t))))


def _ssm_kernel(u_ref, um_ref, b_ref, c_ref, a_ref, aseg_ref,
                ameta_ref, perm_ref, d_ref, wglu_ref, bglu_ref, g_ref, o_ref,
                xre, xim, hre, him, car_re, car_im, cin_re, cin_im, y_scr):
    chunk = u_ref.shape[0]
    seg = chunk // SUBLANES

    @pl.when(pl.program_id(1) == 0)
    def _():
        um = um_ref[...].astype(BF16)
        for ct in range(SSM_COL_TILES):
            ub = um[:, ct * SSM_COL:(ct + 1) * SSM_COL]
            ls = slice(ct * SSM_COL_LANES, (ct + 1) * SSM_COL_LANES)
            xr = jnp.dot(ub, b_ref[0, ct], preferred_element_type=F32)
            xi = jnp.dot(ub, b_ref[1, ct], preferred_element_type=F32)
            pr = ameta_ref[0, :, ls]
            pi = ameta_ref[1, :, ls]
            car_re[:, ls] = jnp.sum(pr * xr - pi * xi, axis=0, keepdims=True)
            car_im[:, ls] = jnp.sum(pr * xi + pi * xr, axis=0, keepdims=True)

    up = jnp.dot(perm_ref[0], u_ref[...].astype(BF16), preferred_element_type=F32).astype(BF16)
    for ct in range(SSM_COL_TILES):
        ub = up[:, ct * SSM_COL:(ct + 1) * SSM_COL]
        ls = slice(ct * SSM_COL_LANES, (ct + 1) * SSM_COL_LANES)
        xre[:, ls] = jnp.dot(ub, b_ref[0, ct], preferred_element_type=F32)
        xim[:, ls] = jnp.dot(ub, b_ref[1, ct], preferred_element_type=F32)

    for lb in range(SSM_LANES // SCAN_LANES):
        ls = slice(lb * SCAN_LANES, (lb + 1) * SCAN_LANES)
        ar = jnp.broadcast_to(a_ref[0, :, ls], (SUBLANES, SCAN_LANES))
        ai = jnp.broadcast_to(a_ref[1, :, ls], (SUBLANES, SCAN_LANES))

        def end_body(k2, carry):
            hr, hi = carry
            r0 = pl.multiple_of(k2 * 2 * SUBLANES, 2 * SUBLANES)
            first = pl.ds(r0, SUBLANES)
            second = pl.ds(r0 + SUBLANES, SUBLANES)
            h0r = ar * hr - ai * hi + xre[first, ls]
            h0i = ar * hi + ai * hr + xim[first, ls]
            return ar * h0r - ai * h0i + xre[second, ls], ar * h0i + ai * h0r + xim[second, ls]

        zero = jnp.zeros((SUBLANES, SCAN_LANES), F32)
        er, ei = lax.fori_loop(0, seg // 2, end_body, (zero, zero))
        cin_re[:, ls] = er
        cin_im[:, ls] = ei

    cr = car_re[...]
    ci = car_im[...]
    sr = aseg_ref[0]
    si = aseg_ref[1]
    for r in range(SUBLANES):
        er = cin_re[r:r + 1, :]
        ei = cin_im[r:r + 1, :]
        cin_re[r:r + 1, :] = cr
        cin_im[r:r + 1, :] = ci
        cr, ci = sr * cr - si * ci + er, sr * ci + si * cr + ei
    car_re[...] = cr
    car_im[...] = ci

    pair = 2 * SUBLANES
    for lb in range(SSM_LANES // SCAN_LANES):
        ls = slice(lb * SCAN_LANES, (lb + 1) * SCAN_LANES)
        ar = jnp.broadcast_to(a_ref[0, :, ls], (SUBLANES, SCAN_LANES))
        ai = jnp.broadcast_to(a_ref[1, :, ls], (SUBLANES, SCAN_LANES))

        def scan_body(k2, carry):
            hr, hi = carry
            r0 = pl.multiple_of(k2 * pair, pair)
            first = pl.ds(r0, SUBLANES)
            second = pl.ds(r0 + SUBLANES, SUBLANES)
            h0r = ar * hr - ai * hi + xre[first, ls]
            h0i = ar * hi + ai * hr + xim[first, ls]
            h1r = ar * h0r - ai * h0i + xre[second, ls]
            h1i = ar * h0i + ai * h0r + xim[second, ls]
            hre[pl.ds(r0, pair), ls] = jnp.concatenate([h0r, h1r], axis=0).astype(BF16)
            him[pl.ds(r0, pair), ls] = jnp.concatenate([h0i, h1i], axis=0).astype(BF16)
            return h1r, h1i

        lax.fori_loop(0, seg // 2, scan_body, (cin_re[:, ls], cin_im[:, ls]))

    n_lt = SSM_WIDTH // LANES
    for ct in range(SSM_COL_TILES):
        ls = slice(ct * SSM_COL_LANES, (ct + 1) * SSM_COL_LANES)
        y = (jnp.dot(hre[:, ls], c_ref[0, ct], preferred_element_type=F32)
             + jnp.dot(him[:, ls], c_ref[1, ct], preferred_element_type=F32))
        for k in range(seg):
            for jj in range(SSM_COL // LANES):
                y_scr[ct * (SSM_COL // LANES) + jj, pl.ds(k, SUBLANES, stride=seg), :] = (
                    y[k * SUBLANES:(k + 1) * SUBLANES, jj * LANES:(jj + 1) * LANES])

    y = jnp.concatenate([y_scr[j] for j in range(n_lt)], axis=1)
    y = _gelu_tanh(y + d_ref[...] * u_ref[...])
    gate = jnp.dot(y.astype(BF16), wglu_ref[...], preferred_element_type=F32) + bglu_ref[...]
    y = y * _sigmoid(gate)
    o_ref[...] = _rms(y, g_ref[...]).astype(BF16)


def _ssm(u, u_meta, prm, d_skip, w_glu_bf, b_glu, g_ssm, bsz, seq, chunk):
    nc = seq // chunk
    seg = chunk // SUBLANES
    row = lambda b, c: (b * nc + c, 0)
    c2 = lambda b, c: (0, 0)
    c3 = lambda b, c: (0, 0, 0)
    p = jnp.arange(chunk)
    fwd = ((p % SUBLANES) * seg + p // SUBLANES)[:, None] == p[None, :]
    perm = fwd.astype(BF16)[None]
    return pl.pallas_call(
        _ssm_kernel,
        grid=(bsz, nc),
        in_specs=[
            pl.BlockSpec((chunk, SSM_WIDTH), row),
            pl.BlockSpec((N_META, SSM_WIDTH), c2),
            pl.BlockSpec((2, SSM_COL_TILES, SSM_COL, SSM_COL_LANES), lambda b, c: (0, 0, 0, 0)),
            pl.BlockSpec((2, SSM_COL_TILES, SSM_COL_LANES, SSM_COL), lambda b, c: (0, 0, 0, 0)),
            pl.BlockSpec((2, 1, SSM_LANES), c3),
            pl.BlockSpec((2, 1, SSM_LANES), c3),
            pl.BlockSpec((2, N_META, SSM_LANES), c3),
            pl.BlockSpec((1, chunk, chunk), c3),
            pl.BlockSpec((1, SSM_WIDTH), c2),
            pl.BlockSpec((SSM_WIDTH, SSM_WIDTH), c2),
            pl.BlockSpec((1, SSM_WIDTH), c2),
            pl.BlockSpec((1, SSM_WIDTH), c2),
        ],
        out_specs=pl.BlockSpec((chunk, SSM_WIDTH), row),
        out_shape=jax.ShapeDtypeStruct((bsz * seq, SSM_WIDTH), BF16),
        scratch_shapes=[
            pltpu.VMEM((chunk, SSM_LANES), F32),
            pltpu.VMEM((chunk, SSM_LANES), F32),
            pltpu.VMEM((chunk, SSM_LANES), BF16),
            pltpu.VMEM((chunk, SSM_LANES), BF16),
            pltpu.VMEM((1, SSM_LANES), F32),
            pltpu.VMEM((1, SSM_LANES), F32),
            pltpu.VMEM((SUBLANES, SSM_LANES), F32),
            pltpu.VMEM((SUBLANES, SSM_LANES), F32),
            pltpu.VMEM((SSM_WIDTH // LANES, chunk, LANES), F32),
        ],
        compiler_params=pltpu.CompilerParams(
            dimension_semantics=("arbitrary", "arbitrary"), vmem_limit_bytes=VMEM_LIMIT),
        name="ssm",
    )(u, u_meta, prm["b"], prm["c"], prm["a"], prm["a_seg"],
      prm["a_meta"], perm, d_skip, w_glu_bf, b_glu, g_ssm)


def _outproj_kernel(ya_ref, ys_ref, x_ref, wo_ref, bo_ref, gf_ref, wr_ref, br_ref,
                    h_ref, xn_ref, lg_ref):
    mix = (jnp.dot(ya_ref[...], wo_ref[:ATTN_WIDTH, :], preferred_element_type=F32)
           + jnp.dot(ys_ref[...], wo_ref[ATTN_WIDTH:, :], preferred_element_type=F32))
    h = x_ref[...] + mix + bo_ref[...]
    h_ref[...] = h
    n = _rms(h, gf_ref[...])
    n_hi = n.astype(BF16)
    bits = lax.bitcast_convert_type(n_hi.astype(F32), jnp.uint32)
    xn_ref[...] = (bits[:, :D_MODEL // 2] >> 16) | (bits[:, D_MODEL // 2:] & jnp.uint32(0xFFFF0000))
    n_lo = (n - n_hi.astype(F32)).astype(BF16)
    hi = jnp.dot(n_hi, wr_ref[...], preferred_element_type=F32)
    lo = jnp.dot(n_lo, wr_ref[:, :ROUTER_LANES], preferred_element_type=F32)
    lg_ref[...] = hi[:, :ROUTER_LANES] + hi[:, ROUTER_LANES:] + lo + br_ref[...]


def _outproj(y_attn, y_ssm, xf, w_out_bf, b_out, g_ffn, w_router_pad, b_router_pad, row_tile):
    rows = xf.shape[0]
    row = lambda i: (i, 0)
    const = lambda i: (0, 0)
    return pl.pallas_call(
        _outproj_kernel,
        grid=(rows // row_tile,),
        in_specs=[
            pl.BlockSpec((row_tile, ATTN_WIDTH), row),
            pl.BlockSpec((row_tile, SSM_WIDTH), row),
            pl.BlockSpec((row_tile, D_MODEL), row),
            pl.BlockSpec((D_MODEL, D_MODEL), const),
            pl.BlockSpec((1, D_MODEL), const),
            pl.BlockSpec((1, D_MODEL), const),
            pl.BlockSpec((D_MODEL, 2 * ROUTER_LANES), const),
            pl.BlockSpec((1, ROUTER_LANES), const),
        ],
        out_specs=[
            pl.BlockSpec((row_tile, D_MODEL), row),
            pl.BlockSpec((row_tile, D_MODEL // 2), row),
            pl.BlockSpec((row_tile, ROUTER_LANES), row),
        ],
        out_shape=[
            jax.ShapeDtypeStruct((rows, D_MODEL), F32),
            jax.ShapeDtypeStruct((rows, D_MODEL // 2), jnp.uint32),
            jax.ShapeDtypeStruct((rows, ROUTER_LANES), F32),
        ],
        compiler_params=pltpu.CompilerParams(
            dimension_semantics=("arbitrary",), vmem_limit_bytes=VMEM_LIMIT),
        name="outproj",
    )(y_attn, y_ssm, xf, w_out_bf, b_out, g_ffn, w_router_pad, b_router_pad)


def _moe_kernel(ie_ref, ir_ref, in_ref, ic_ref, ni_ref, rt_ref, rd_ref,
                xn_hbm, w1a_ref, w1b_ref, b1g_ref, b1u_ref, w2_ref, b2_ref, perm_ref, yk_hbm,
                stage, x_scr, acc, w1g, w1u, w2b, issued, sem_g, sem_out):
    i = pl.program_id(0)
    f = pl.program_id(1)
    nf = pl.num_programs(1)
    max_items = pl.num_programs(0)
    tf = w2_ref.shape[1]

    @pl.when(i < ni_ref[0])
    def _():
        r0 = ir_ref[i]
        n = in_ref[i]
        cnt = ic_ref[i]
        dump_base = yk_hbm.shape[0] - (max_items - i) * MOE_PAD
        nxt = jnp.minimum(i + 1, max_items - 1)
        n_next = jnp.where(i + 1 < ni_ref[0], in_ref[nxt], 0)
        r0_next = ir_ref[nxt]

        def gather_row(first_row, j):
            tok = rt_ref[first_row + j]
            pltpu.make_async_copy(xn_hbm.at[pl.ds(tok, 1), :], stage.at[pl.ds(j, 1), :],
                                  sem_g).start()

        def gather_wait_chunk():
            pltpu.make_async_copy(xn_hbm.at[pl.ds(0, MOE_PAD), :],
                                  stage.at[pl.ds(0, MOE_PAD), :], sem_g).wait()

        def scattered(c, carry):
            pltpu.make_async_copy(acc.at[pl.ds(0, MOE_PAD), :],
                                  yk_hbm.at[pl.ds(0, MOE_PAD), :], sem_out).wait()
            return carry

        @pl.when(f == 0)
        def _():
            @pl.when(i == 0)
            def _():
                def first(j, c):
                    gather_row(r0, j)
                    return c

                lax.fori_loop(0, n, first, 0)

            def landed(c, carry):
                gather_wait_chunk()
                return carry

            lax.fori_loop(0, n // MOE_PAD, landed, 0)

            def unpack(c, carry):
                rows_c = pl.ds(pl.multiple_of(c * MOE_PAD, MOE_PAD), MOE_PAD)
                u = stage[rows_c, :]
                x_scr[rows_c, :D_MODEL // 2] = lax.bitcast_convert_type(u << 16, F32).astype(BF16)
                x_scr[rows_c, D_MODEL // 2:] = lax.bitcast_convert_type(
                    u & jnp.uint32(0xFFFF0000), F32).astype(BF16)
                return carry

            lax.fori_loop(0, n // MOE_PAD, unpack, 0)
            issued[0] = 0

            @pl.when(i > 0)
            def _():
                lax.fori_loop(0, in_ref[jnp.maximum(i - 1, 0)] // MOE_PAD, scattered, 0)

            bias = jnp.broadcast_to(b2_ref[0], (MOE_PAD, D_MODEL))

            def init(c, _):
                acc[pl.ds(pl.multiple_of(c * MOE_PAD, MOE_PAD), MOE_PAD), :] = bias
                return 0

            lax.fori_loop(0, n // MOE_PAD, init, 0)

        half = PERM_COLS // 2
        k_half = D_MODEL // 2
        for hh, w1_ref in enumerate((w1a_ref, w1b_ref)):
            for c in range(2 * tf // PERM_COLS):
                t = jnp.dot(w1_ref[0, :, c * PERM_COLS:(c + 1) * PERM_COLS].astype(BF16),
                            perm_ref[...], preferred_element_type=F32)
                rs = slice(hh * k_half, (hh + 1) * k_half)
                w1g[rs, c * half:(c + 1) * half] = t[:, :half].astype(BF16)
                w1u[rs, c * half:(c + 1) * half] = t[:, half:].astype(BF16)
        w2b[...] = w2_ref[0].astype(BF16)

        def scatter(r, rows):
            for t in range(rows):
                dst = rd_ref[r0 + r + t]
                if t >= rows - MOE_PAD:
                    dst = jnp.where(r + t < cnt, dst, dump_base + (t - (rows - MOE_PAD)))
                pltpu.make_async_copy(acc.at[pl.ds(r + t, 1), :], yk_hbm.at[pl.ds(dst, 1), :],
                                      sem_out).start()

        def block(r, rows, prefetch):
            if prefetch:
                base = issued[0]
                for t in range(rows // 4):
                    gather_row(r0_next, base + t)
                issued[0] = base + rows // 4
            sl = pl.ds(r, rows)
            xb = x_scr[sl, :]
            g = jnp.dot(xb, w1g[...], preferred_element_type=F32) + b1g_ref[0]
            up = jnp.dot(xb, w1u[...], preferred_element_type=F32) + b1u_ref[0]
            g = jnp.minimum(g, SWIGLU_LIMIT)
            up = jnp.clip(up, -SWIGLU_LIMIT, SWIGLU_LIMIT)
            act = g * _sigmoid(SWIGLU_ALPHA * g) * (up + 1.0)
            acc[sl, :] += jnp.dot(act.astype(BF16), w2b[...], preferred_element_type=F32)

        def run(r, rows):
            can = issued[0] + rows // 4 <= n_next

            @pl.when(can)
            def _():
                block(r, rows, True)

            @pl.when(jnp.logical_not(can))
            def _():
                block(r, rows, False)

        n_big = n // MOE_BLOCK
        rem = n % MOE_BLOCK
        last = f == nf - 1

        def big(b, _):
            run(pl.multiple_of(b * MOE_BLOCK, MOE_BLOCK), MOE_BLOCK)
            return 0

        @pl.when(jnp.logical_not(last))
        def _():
            lax.fori_loop(0, n_big, big, 0)

        @pl.when(last)
        def _():
            for b in range(MOE_ITEM_ROWS // MOE_BLOCK):
                @pl.when(b < n_big)
                def _(b=b):
                    run(b * MOE_BLOCK, MOE_BLOCK)
                    scatter(b * MOE_BLOCK, MOE_BLOCK)

        def tail(r, rows):
            run(r, rows)

            @pl.when(last)
            def _():
                scatter(r, rows)

        @pl.when(rem >= 2 * MOE_PAD)
        def _():
            tail(pl.multiple_of(n_big * MOE_BLOCK, MOE_BLOCK), 2 * MOE_PAD)

        @pl.when(rem % (2 * MOE_PAD) != 0)
        def _():
            tail(pl.multiple_of(n - MOE_PAD, MOE_PAD), MOE_PAD)

        @pl.when(last)
        def _():
            @pl.when(i == ni_ref[0] - 1)
            def _():
                lax.fori_loop(0, n // MOE_PAD, scattered, 0)

            def rest(j, c):
                gather_row(r0_next, j)
                return c

            lax.fori_loop(issued[0], n_next, rest, 0)


def _experts(xn, item_e, item_r0, item_n, item_cnt, n_items, row_tok, row_dst, n_out_rows,
             w1, b1g, b1u, w2, b2, tf):
    max_items = item_e.shape[0]
    nf = D_FF // tf
    idx = jnp.arange(PERM_COLS)
    dst = jnp.where(idx % 2 == 0, idx // 2, PERM_COLS // 2 + idx // 2)
    perm = (dst[:, None] == jnp.arange(PERM_COLS)[None, :]).astype(BF16)

    def ex(i, ie, ni):
        return ie[jnp.minimum(i, ni[0] - 1)]

    def ff(i, f, ni):
        return jnp.where(i < ni[0], f, nf - 1)

    grid_spec = pltpu.PrefetchScalarGridSpec(
        num_scalar_prefetch=7,
        grid=(max_items, nf),
        in_specs=[
            pl.BlockSpec(memory_space=pl.ANY),
            pl.BlockSpec((1, D_MODEL // 2, 2 * tf), lambda i, f, ie, ir, im, ic, ni, rt, rd: (ex(i, ie, ni), 0, ff(i, f, ni))),
            pl.BlockSpec((1, D_MODEL // 2, 2 * tf), lambda i, f, ie, ir, im, ic, ni, rt, rd: (ex(i, ie, ni), 1, ff(i, f, ni))),
            pl.BlockSpec((1, 1, tf), lambda i, f, ie, ir, im, ic, ni, rt, rd: (ex(i, ie, ni), 0, ff(i, f, ni))),
            pl.BlockSpec((1, 1, tf), lambda i, f, ie, ir, im, ic, ni, rt, rd: (ex(i, ie, ni), 0, ff(i, f, ni))),
            pl.BlockSpec((1, tf, D_MODEL), lambda i, f, ie, ir, im, ic, ni, rt, rd: (ex(i, ie, ni), ff(i, f, ni), 0)),
            pl.BlockSpec((1, 1, D_MODEL), lambda i, f, ie, ir, im, ic, ni, rt, rd: (ex(i, ie, ni), 0, 0)),
            pl.BlockSpec((PERM_COLS, PERM_COLS), lambda i, f, ie, ir, im, ic, ni, rt, rd: (0, 0)),
        ],
        out_specs=pl.BlockSpec(memory_space=pl.ANY),
        scratch_shapes=[
            pltpu.VMEM((MOE_ITEM_ROWS, D_MODEL // 2), jnp.uint32),
            pltpu.VMEM((MOE_ITEM_ROWS, D_MODEL), BF16),
            pltpu.VMEM((MOE_ITEM_ROWS, D_MODEL), F32),
            pltpu.VMEM((D_MODEL, tf), BF16),
            pltpu.VMEM((D_MODEL, tf), BF16),
            pltpu.VMEM((tf, D_MODEL), BF16),
            pltpu.SMEM((1,), jnp.int32),
            pltpu.SemaphoreType.DMA(()),
            pltpu.SemaphoreType.DMA(()),
        ],
    )
    return pl.pallas_call(
        _moe_kernel,
        grid_spec=grid_spec,
        out_shape=jax.ShapeDtypeStruct((n_out_rows, D_MODEL), F32),
        compiler_params=pltpu.CompilerParams(
            dimension_semantics=("arbitrary", "arbitrary"), vmem_limit_bytes=VMEM_LIMIT),
        name="experts",
    )(item_e, item_r0, item_n, item_cnt, n_items, row_tok, row_dst, xn, w1, w1, b1g, b1u, w2, b2, perm)


def _combine_kernel(h_ref, y0_ref, y1_ref, y2_ref, y3_ref, gate_ref, g_ref, o_ref):
    acc = h_ref[...]
    gates = gate_ref[...]
    for k, y_ref in enumerate((y0_ref, y1_ref, y2_ref, y3_ref)):
        acc = acc + gates[:, k:k + 1] * y_ref[...]
    o_ref[...] = _rms(acc, g_ref[...])


def _combine(h1, yk, gates_pad, g_final, row_tile):
    rows = h1.shape[0]
    n_tiles = rows // row_tile
    row = lambda i: (i, 0)
    const = lambda i: (0, 0)
    slot = lambda k: (lambda i: (k * n_tiles + i, 0))
    return pl.pallas_call(
        _combine_kernel,
        grid=(n_tiles,),
        in_specs=[pl.BlockSpec((row_tile, D_MODEL), row)]
        + [pl.BlockSpec((row_tile, D_MODEL), slot(k)) for k in range(TOP_K)]
        + [pl.BlockSpec((row_tile, LANES), row), pl.BlockSpec((1, D_MODEL), const)],
        out_specs=pl.BlockSpec((row_tile, D_MODEL), row),
        out_shape=jax.ShapeDtypeStruct((rows, D_MODEL), F32),
        compiler_params=pltpu.CompilerParams(
            dimension_semantics=("arbitrary",), vmem_limit_bytes=VMEM_LIMIT),
        name="combine",
    )(h1, yk, yk, yk, yk, gates_pad, g_final)


def kernel(x, meta_tokens, g_mix, w_in, b_in, attn_sinks, ssm_a_re, ssm_a_im, ssm_log_dt,
           ssm_b_re, ssm_b_im, ssm_c_re, ssm_c_im, ssm_d, w_glu, b_glu, g_attn_out, g_ssm_out,
           w_out, b_out, g_ffn, w_router, b_router, w_mlp1, b_mlp1, w_mlp2, b_mlp2, g_final):
    bsz, seq, _ = x.shape
    rows = bsz * seq
    row_tile = min(ROW_TILE, seq)
    chunk = min(SSM_CHUNK, seq)
    assert seq % (2 * ATTN_BLOCK) == 0 and seq % row_tile == 0 and seq % chunk == 0
    xf = x.reshape(rows, D_MODEL)

    w_in_bf = w_in[0].astype(BF16)
    cos_r, sin_r = _rope_tables(N_META + jnp.arange(seq))
    cos_m, sin_m = _rope_tables(jnp.arange(N_META))
    qt, k, vt, u = _inproj(xf, g_mix, w_in_bf, b_in, cos_r, sin_r, row_tile, True)
    _, k_meta, v_meta, u_meta = _inproj(meta_tokens, g_mix, w_in_bf, b_in, cos_m, sin_m,
                                        N_META, False)

    y_attn = _attention(qt, k, vt, k_meta, v_meta.T, attn_sinks, g_attn_out, bsz, seq)

    prm = _ssm_params(ssm_a_re[0], ssm_a_im[0], ssm_log_dt[0], ssm_b_re[0], ssm_b_im[0],
                      ssm_c_re[0], ssm_c_im[0], chunk // SUBLANES)
    y_ssm = _ssm(u, u_meta, prm, ssm_d, w_glu[0].astype(BF16), b_glu, g_ssm_out, bsz, seq, chunk)

    w_router_pad = jnp.pad(w_router[0], ((0, 0), (0, ROUTER_LANES - N_EXPERTS)))
    b_router_pad = jnp.pad(b_router, ((0, 0), (0, ROUTER_LANES - N_EXPERTS)))
    w_router_hi = w_router_pad.astype(BF16)
    w_router_lo = (w_router_pad - w_router_hi.astype(F32)).astype(BF16)
    w_router_pad = jnp.concatenate([w_router_hi, w_router_lo], axis=1)
    h1, xn, logits = _outproj(y_attn, y_ssm, xf, w_out[0].astype(BF16), b_out, g_ffn,
                              w_router_pad, b_router_pad, row_tile)

    top_val, top_idx = lax.top_k(logits[:, :N_EXPERTS], TOP_K)
    gates = jax.nn.softmax(top_val, axis=-1)
    chosen = (top_idx[:, :, None] == jnp.arange(N_EXPERTS)[None, None, :])
    counts = jnp.sum(chosen.astype(jnp.int32), axis=(0, 1))
    padded = ((counts + MOE_PAD - 1) // MOE_PAD) * MOE_PAD
    n_assign = rows * TOP_K
    order = jnp.argsort(top_idx.reshape(-1), stable=True).astype(jnp.int32)
    row_tok = jnp.pad(order // TOP_K, (0, MOE_PAD))
    row_dst = jnp.pad((order % TOP_K) * rows + order // TOP_K, (0, MOE_PAD))
    starts = jnp.cumsum(counts) - counts
    per_e = (padded + MOE_ITEM_ROWS - 1) // MOE_ITEM_ROWS
    item_ends = jnp.cumsum(per_e)
    max_items = N_EXPERTS + n_assign // MOE_ITEM_ROWS
    slot = jnp.arange(max_items, dtype=jnp.int32)
    item_e = jnp.minimum(jnp.searchsorted(item_ends, slot, side='right'), N_EXPERTS - 1)
    piece = slot - (item_ends - per_e)[item_e]
    item_r0 = (starts[item_e] + piece * MOE_ITEM_ROWS).astype(jnp.int32)
    item_n = jnp.clip(padded[item_e] - piece * MOE_ITEM_ROWS, 0, MOE_ITEM_ROWS).astype(jnp.int32)
    item_cnt = jnp.clip(counts[item_e] - piece * MOE_ITEM_ROWS, 0, MOE_ITEM_ROWS).astype(jnp.int32)
    n_items = item_ends[-1].astype(jnp.int32).reshape(1)

    tf = MOE_FF_TILE
    b1 = b_mlp1[0].reshape(N_EXPERTS, 1, D_FF, 2)
    b2 = b_mlp2[0].reshape(N_EXPERTS, 1, D_MODEL)
    yk = _experts(xn, item_e.astype(jnp.int32), item_r0, item_n, item_cnt, n_items, row_tok, row_dst,
                  n_assign + max_items * MOE_PAD, w_mlp1[0], b1[..., 0], b1[..., 1], w_mlp2[0], b2, tf)

    gates_pad = jnp.pad(gates, ((0, 0), (0, LANES - TOP_K)))
    out = _combine(h1, yk, gates_pad, g_final.reshape(1, D_MODEL), row_tile)
    return out.reshape(bsz, seq, D_MODEL)
```

```python
import functools
import math

import jax
import jax.numpy as jnp
from jax import lax
from jax.experimental import pallas as pl
from jax.experimental.pallas import tpu as pltpu

F32 = jnp.float32
BF16 = jnp.bfloat16

D_MODEL = 2048
N_META = 16
HEAD_DIM = 64
ATTN_WIDTH = 1024
N_Q_HEADS = 16
N_KV_HEADS = 4
Q_PER_KV = 4
KV_WIDTH = 256
ATTN_BLOCK = 128
ROPE_THETA = 10000.0
SSM_WIDTH = 1024
SSM_GROUP = 16
N_SSM_GROUPS = 64
SSM_STATE = 64
SSM_LANES = N_SSM_GROUPS * SSM_STATE
IN_WIDTH = 2560
N_EXPERTS = 32
TOP_K = 4
D_FF = 2048
SWIGLU_LIMIT = 7.0
SWIGLU_ALPHA = 1.702
NORM_EPS = 1e-5

LANES = 128
SUBLANES = 8
VMEM_LIMIT = 58 * 1024 * 1024

ROW_TILE = 512
SSM_CHUNK = 256
SSM_COL = 256
SSM_COL_TILES = SSM_WIDTH // SSM_COL
SSM_COL_LANES = SSM_LANES // SSM_COL_TILES
SCAN_LANES = 512
MOE_PAD = 128
MOE_BLOCK = 512
MOE_ITEM_ROWS = 2176
MOE_FF_TILE = 256
ROUTER_LANES = 128
PERM_COLS = 256


def _rms(t, gain):
    return t * lax.rsqrt(jnp.mean(t * t, axis=-1, keepdims=True) + NORM_EPS) * gain


def _sigmoid(t):
    return 1.0 / (1.0 + jnp.exp(-t))


def _inproj_kernel(x_ref, g_ref, w_ref, b_ref, cos_ref, sin_ref, q_ref, k_ref, v_ref, u_ref,
                   *, transposed):
    n = _rms(x_ref[...], g_ref[...]).astype(BF16)
    cos = cos_ref[...]
    sin = sin_ref[...]
    lane = lax.broadcasted_iota(jnp.int32, cos.shape, 1)
    first_half = (lane % HEAD_DIM) < (HEAD_DIM // 2)

    def proj(c0, c1):
        return jnp.dot(n, w_ref[:, c0:c1], preferred_element_type=F32) + b_ref[:, c0:c1]

    def rope(t):
        partner = jnp.where(first_half, pltpu.roll(t, LANES - HEAD_DIM // 2, 1),
                            pltpu.roll(t, HEAD_DIM // 2, 1))
        return t * cos + partner * sin

    def put(ref, j, t):
        if transposed:
            ref[j * LANES:(j + 1) * LANES, :] = t.T.astype(BF16)
        else:
            ref[:, j * LANES:(j + 1) * LANES] = t.astype(BF16)

    def proj_pair(c0):
        z = proj(c0, c0 + 2 * LANES)
        return z[:, :LANES], z[:, LANES:]

    scale = HEAD_DIM ** -0.5
    for jj in range(ATTN_WIDTH // (2 * LANES)):
        a, b = proj_pair(jj * 2 * LANES)
        put(q_ref, 2 * jj, rope(a) * scale)
        put(q_ref, 2 * jj + 1, rope(b) * scale)
    a, b = proj_pair(ATTN_WIDTH)
    k_ref[:, :LANES] = rope(a).astype(BF16)
    k_ref[:, LANES:] = rope(b).astype(BF16)
    a, b = proj_pair(ATTN_WIDTH + KV_WIDTH)
    put(v_ref, 0, a)
    put(v_ref, 1, b)
    c0 = ATTN_WIDTH + 2 * KV_WIDTH
    u_ref[...] = proj(c0, c0 + SSM_WIDTH)


def _inproj(xf, g_mix, w_in_bf, b_in, cos_t, sin_t, row_tile, transposed):
    rows = xf.shape[0]
    tab_blocks = cos_t.shape[0] // row_tile
    row = lambda i: (i, 0)
    col = lambda i: (0, i)
    tab = lambda i: (i % tab_blocks, 0)
    const = lambda i: (0, 0)
    if transposed:
        q_spec, q_shape = pl.BlockSpec((ATTN_WIDTH, row_tile), col), (ATTN_WIDTH, rows)
        v_spec, v_shape = pl.BlockSpec((KV_WIDTH, row_tile), col), (KV_WIDTH, rows)
    else:
        q_spec, q_shape = pl.BlockSpec((row_tile, ATTN_WIDTH), row), (rows, ATTN_WIDTH)
        v_spec, v_shape = pl.BlockSpec((row_tile, KV_WIDTH), row), (rows, KV_WIDTH)
    return pl.pallas_call(
        functools.partial(_inproj_kernel, transposed=transposed),
        grid=(rows // row_tile,),
        in_specs=[
            pl.BlockSpec((row_tile, D_MODEL), row),
            pl.BlockSpec((1, D_MODEL), const),
            pl.BlockSpec((D_MODEL, IN_WIDTH), const),
            pl.BlockSpec((1, IN_WIDTH), const),
            pl.BlockSpec((row_tile, LANES), tab),
            pl.BlockSpec((row_tile, LANES), tab),
        ],
        out_specs=[
            q_spec,
            pl.BlockSpec((row_tile, KV_WIDTH), row),
            v_spec,
            pl.BlockSpec((row_tile, SSM_WIDTH), row),
        ],
        out_shape=[
            jax.ShapeDtypeStruct(q_shape, BF16),
            jax.ShapeDtypeStruct((rows, KV_WIDTH), BF16),
            jax.ShapeDtypeStruct(v_shape, BF16),
            jax.ShapeDtypeStruct((rows, SSM_WIDTH), F32),
        ],
        compiler_params=pltpu.CompilerParams(
            dimension_semantics=("arbitrary",), vmem_limit_bytes=VMEM_LIMIT),
        name="inproj",
    )(xf, g_mix, w_in_bf, b_in, cos_t, sin_t)


def _rope_tables(positions):
    half = HEAD_DIM // 2
    inv_freq = jnp.power(ROPE_THETA, -jnp.arange(half, dtype=F32) / half)
    ang = positions.astype(F32)[:, None] * inv_freq[None, :]
    cos = jnp.tile(jnp.cos(ang), (1, LANES // half))
    sin = jnp.tile(jnp.sin(ang), (1, LANES // half))
    sign = jnp.where((jnp.arange(LANES) % HEAD_DIM) < half, -1.0, 1.0).astype(F32)
    return cos, sin * sign[None, :]


def _attn_kernel(qt_ref, kp_ref, kc_ref, km_ref, vtp_ref, vtc_ref, vtm_ref, sink_ref, g_ref, o_ref,
                 yt_scr):
    n = pl.program_id(1)
    n_keys = 2 * ATTN_BLOCK + N_META
    key = lax.broadcasted_iota(jnp.int32, (n_keys, ATTN_BLOCK), 0)
    qi = lax.broadcasted_iota(jnp.int32, (n_keys, ATTN_BLOCK), 1)
    cur_j = key - ATTN_BLOCK
    zeros = jnp.zeros((HEAD_DIM, ATTN_BLOCK), BF16)
    invs = []
    for sub in range(2):
        qcols = slice(sub * ATTN_BLOCK, (sub + 1) * ATTN_BLOCK)
        no_prev = jnp.where(n > 0, 0, ATTN_BLOCK) if sub == 0 else 0
        valid = ((key >= 2 * ATTN_BLOCK) | ((cur_j >= 0) & (cur_j <= qi))
                 | ((key < ATTN_BLOCK) & (key > qi + no_prev)))
        bias = jnp.where(valid, 0.0, -1e30)
        bias = jnp.concatenate([bias] * Q_PER_KV, axis=1)
        ssq = jnp.zeros((1, ATTN_BLOCK), F32)
        for hk in range(N_KV_HEADS):
            lt = slice((hk // 2) * LANES, (hk // 2 + 1) * LANES)
            rs = slice(hk * HEAD_DIM, (hk + 1) * HEAD_DIM)
            if sub == 0:
                k_prev, v_prev = kp_ref[:, lt], vtp_ref[rs, :]
            else:
                k_prev, v_prev = kc_ref[:ATTN_BLOCK, lt], vtc_ref[rs, :ATTN_BLOCK]
            k_own = kc_ref[sub * ATTN_BLOCK:(sub + 1) * ATTN_BLOCK, lt]
            v_own = vtc_ref[rs, qcols]
            kt = jnp.concatenate([k_prev, k_own, km_ref[:, lt]], axis=0)
            vt = jnp.concatenate([v_prev, v_own, vtm_ref[rs, :]], axis=1)
            qs, sinks = [], []
            for g in range(Q_PER_KV):
                h = hk * Q_PER_KV + g
                qh = qt_ref[h * HEAD_DIM:(h + 1) * HEAD_DIM, qcols]
                qs.append(jnp.concatenate([qh, zeros] if hk % 2 == 0 else [zeros, qh], axis=0))
                sinks.append(jnp.broadcast_to(sink_ref[:, h:h + 1], (1, ATTN_BLOCK)))
            sink = jnp.concatenate(sinks, axis=1)
            s = jnp.dot(kt, jnp.concatenate(qs, axis=1), preferred_element_type=F32) + bias
            m = jnp.maximum(jnp.max(s, axis=0, keepdims=True), sink)
            p = jnp.exp(s - m)
            denom = jnp.sum(p, axis=0, keepdims=True) + jnp.exp(sink - m)
            o = jnp.dot(vt, p.astype(BF16), preferred_element_type=F32) * (1.0 / denom)
            for g in range(Q_PER_KV):
                h = hk * Q_PER_KV + g
                og = o[:, g * ATTN_BLOCK:(g + 1) * ATTN_BLOCK]
                yt_scr[h * HEAD_DIM:(h + 1) * HEAD_DIM, qcols] = og
                ssq = ssq + jnp.sum(og * og, axis=0, keepdims=True)
        invs.append(lax.rsqrt(ssq * (1.0 / ATTN_WIDTH) + NORM_EPS))
    inv = jnp.concatenate(invs, axis=1)
    o_ref[...] = ((yt_scr[...] * inv).T * g_ref[...]).astype(BF16)


def _attention(qt, k, vt, k_meta, vt_meta, sinks, g_attn, bsz, seq):
    nb = seq // ATTN_BLOCK
    npair = nb // 2
    own = lambda b, n: (b * npair + n, 0)
    prev = lambda b, n: (b * nb + jnp.maximum(2 * n - 1, 0), 0)
    own_t = lambda b, n: (0, b * npair + n)
    prev_t = lambda b, n: (0, b * nb + jnp.maximum(2 * n - 1, 0))
    const = lambda b, n: (0, 0)
    return pl.pallas_call(
        _attn_kernel,
        grid=(bsz, npair),
        in_specs=[
            pl.BlockSpec((ATTN_WIDTH, 2 * ATTN_BLOCK), own_t),
            pl.BlockSpec((ATTN_BLOCK, KV_WIDTH), prev),
            pl.BlockSpec((2 * ATTN_BLOCK, KV_WIDTH), own),
            pl.BlockSpec((N_META, KV_WIDTH), const),
            pl.BlockSpec((KV_WIDTH, ATTN_BLOCK), prev_t),
            pl.BlockSpec((KV_WIDTH, 2 * ATTN_BLOCK), own_t),
            pl.BlockSpec((KV_WIDTH, N_META), const),
            pl.BlockSpec((1, N_Q_HEADS), const),
            pl.BlockSpec((1, ATTN_WIDTH), const),
        ],
        out_specs=pl.BlockSpec((2 * ATTN_BLOCK, ATTN_WIDTH), own),
        out_shape=jax.ShapeDtypeStruct((bsz * seq, ATTN_WIDTH), BF16),
        scratch_shapes=[pltpu.VMEM((ATTN_WIDTH, 2 * ATTN_BLOCK), F32)],
        compiler_params=pltpu.CompilerParams(
            dimension_semantics=("arbitrary", "arbitrary"), vmem_limit_bytes=VMEM_LIMIT),
        name="attention",
    )(qt, k, k, k_meta, vt, vt, vt_meta, sinks, g_attn)


def _ssm_params(a_re, a_im, log_dt, b_re, b_im, c_re, c_im, seg_len):
    dt = jnp.exp(log_dt.astype(F32))[:, None]
    lam_re = jnp.minimum(a_re.astype(F32), -1e-4)
    lam_im = a_im.astype(F32)
    z_re, z_im = lam_re * dt, lam_im * dt
    mag = jnp.exp(z_re)
    abar_re, abar_im = mag * jnp.cos(z_im), mag * jnp.sin(z_im)
    den = lam_re * lam_re + lam_im * lam_im
    n_re, n_im = abar_re - 1.0, abar_im
    coef_re = (n_re * lam_re + n_im * lam_im) / den
    coef_im = (n_im * lam_re - n_re * lam_im) / den
    br, bi = b_re.astype(F32), b_im.astype(F32)
    bb_re = coef_re[..., None] * br - coef_im[..., None] * bi
    bb_im = coef_re[..., None] * bi + coef_im[..., None] * br

    groups_per_tile = SSM_COL // SSM_GROUP
    eye = jnp.eye(groups_per_tile, dtype=F32)

    def in_tile(bb):
        t = bb.reshape(2, SSM_COL_TILES, groups_per_tile, SSM_STATE, SSM_GROUP)
        t = jnp.einsum('xtgpc,gh->xtgchp', t, eye)
        return t.reshape(2, SSM_COL_TILES, SSM_COL, SSM_COL_LANES).astype(BF16)

    def out_tile(cc):
        t = cc.reshape(2, SSM_COL_TILES, groups_per_tile, SSM_GROUP, SSM_STATE)
        t = jnp.einsum('xtgcp,gh->xtgphc', t, eye)
        return t.reshape(2, SSM_COL_TILES, SSM_COL_LANES, SSM_COL).astype(BF16)

    def powers(exps):
        e = exps.astype(F32)[:, None, None]
        pm = jnp.exp(e * z_re[None])
        return jnp.stack([(pm * jnp.cos(e * z_im[None])).reshape(len(exps), SSM_LANES),
                          (pm * jnp.sin(e * z_im[None])).reshape(len(exps), SSM_LANES)])

    return dict(
        b=in_tile(jnp.stack([bb_re, bb_im])),
        c=out_tile(jnp.stack([c_re.astype(F32), -c_im.astype(F32)])),
        a=jnp.stack([abar_re.reshape(1, SSM_LANES), abar_im.reshape(1, SSM_LANES)]),
        a_seg=powers(jnp.array([seg_len])),
        a_meta=powers(jnp.arange(N_META - 1, -1, -1)),
    )


def _gelu_tanh(t):
    return 0.5 * t * (1.0 + jnp.tanh(math.sqrt(2.0 / math.pi) * (t + 0.044715 * (t * t * t))))


def _ssm_kernel(u_ref, um_ref, b_ref, c_ref, a_ref, aseg_ref,
                ameta_ref, perm_ref, d_ref, wglu_ref, bglu_ref, g_ref, o_ref,
                xre, xim, hre, him, car_re, car_im, cin_re, cin_im, y_scr):
    chunk = u_ref.shape[0]
    seg = chunk // SUBLANES

    @pl.when(pl.program_id(1) == 0)
    def _():
        um = um_ref[...].astype(BF16)
        for ct in range(SSM_COL_TILES):
            ub = um[:, ct * SSM_COL:(ct + 1) * SSM_COL]
            ls = slice(ct * SSM_COL_LANES, (ct + 1) * SSM_COL_LANES)
            xr = jnp.dot(ub, b_ref[0, ct], preferred_element_type=F32)
            xi = jnp.dot(ub, b_ref[1, ct], preferred_element_type=F32)
            pr = ameta_ref[0, :, ls]
            pi = ameta_ref[1, :, ls]
            car_re[:, ls] = jnp.sum(pr * xr - pi * xi, axis=0, keepdims=True)
            car_im[:, ls] = jnp.sum(pr * xi + pi * xr, axis=0, keepdims=True)

    up = jnp.dot(perm_ref[0], u_ref[...].astype(BF16), preferred_element_type=F32).astype(BF16)
    for ct in range(SSM_COL_TILES):
        ub = up[:, ct * SSM_COL:(ct + 1) * SSM_COL]
        ls = slice(ct * SSM_COL_LANES, (ct + 1) * SSM_COL_LANES)
        xre[:, ls] = jnp.dot(ub, b_ref[0, ct], preferred_element_type=F32)
        xim[:, ls] = jnp.dot(ub, b_ref[1, ct], preferred_element_type=F32)

    for lb in range(SSM_LANES // SCAN_LANES):
        ls = slice(lb * SCAN_LANES, (lb + 1) * SCAN_LANES)
        ar = jnp.broadcast_to(a_ref[0, :, ls], (SUBLANES, SCAN_LANES))
        ai = jnp.broadcast_to(a_ref[1, :, ls], (SUBLANES, SCAN_LANES))

        def end_body(k2, carry):
            hr, hi = carry
            r0 = pl.multiple_of(k2 * 2 * SUBLANES, 2 * SUBLANES)
            first = pl.ds(r0, SUBLANES)
            second = pl.ds(r0 + SUBLANES, SUBLANES)
            h0r = ar * hr - ai * hi + xre[first, ls]
            h0i = ar * hi + ai * hr + xim[first, ls]
            return ar * h0r - ai * h0i + xre[second, ls], ar * h0i + ai * h0r + xim[second, ls]

        zero = jnp.zeros((SUBLANES, SCAN_LANES), F32)
        er, ei = lax.fori_loop(0, seg // 2, end_body, (zero, zero))
        cin_re[:, ls] = er
        cin_im[:, ls] = ei

    cr = car_re[...]
    ci = car_im[...]
    sr = aseg_ref[0]
    si = aseg_ref[1]
    for r in range(SUBLANES):
        er = cin_re[r:r + 1, :]
        ei = cin_im[r:r + 1, :]
        cin_re[r:r + 1, :] = cr
        cin_im[r:r + 1, :] = ci
        cr, ci = sr * cr - si * ci + er, sr * ci + si * cr + ei
    car_re[...] = cr
    car_im[...] = ci

    pair = 2 * SUBLANES
    for lb in range(SSM_LANES // SCAN_LANES):
        ls = slice(lb * SCAN_LANES, (lb + 1) * SCAN_LANES)
        ar = jnp.broadcast_to(a_ref[0, :, ls], (SUBLANES, SCAN_LANES))
        ai = jnp.broadcast_to(a_ref[1, :, ls], (SUBLANES, SCAN_LANES))

        def scan_body(k2, carry):
            hr, hi = carry
            r0 = pl.multiple_of(k2 * pair, pair)
            first = pl.ds(r0, SUBLANES)
            second = pl.ds(r0 + SUBLANES, SUBLANES)
            h0r = ar * hr - ai * hi + xre[first, ls]
            h0i = ar * hi + ai * hr + xim[first, ls]
            h1r = ar * h0r - ai * h0i + xre[second, ls]
            h1i = ar * h0i + ai * h0r + xim[second, ls]
            hre[pl.ds(r0, pair), ls] = jnp.concatenate([h0r, h1r], axis=0).astype(BF16)
            him[pl.ds(r0, pair), ls] = jnp.concatenate([h0i, h1i], axis=0).astype(BF16)
            return h1r, h1i

        lax.fori_loop(0, seg // 2, scan_body, (cin_re[:, ls], cin_im[:, ls]))

    n_lt = SSM_WIDTH // LANES
    for ct in range(SSM_COL_TILES):
        ls = slice(ct * SSM_COL_LANES, (ct + 1) * SSM_COL_LANES)
        y = (jnp.dot(hre[:, ls], c_ref[0, ct], preferred_element_type=F32)
             + jnp.dot(him[:, ls], c_ref[1, ct], preferred_element_type=F32))
        for k in range(seg):
            for jj in range(SSM_COL // LANES):
                y_scr[ct * (SSM_COL // LANES) + jj, pl.ds(k, SUBLANES, stride=seg), :] = (
                    y[k * SUBLANES:(k + 1) * SUBLANES, jj * LANES:(jj + 1) * LANES])

    y = jnp.concatenate([y_scr[j] for j in range(n_lt)], axis=1)
    y = _gelu_tanh(y + d_ref[...] * u_ref[...])
    gate = jnp.dot(y.astype(BF16), wglu_ref[...], preferred_element_type=F32) + bglu_ref[...]
    y = y * _sigmoid(gate)
    o_ref[...] = _rms(y, g_ref[...]).astype(BF16)


def _ssm(u, u_meta, prm, d_skip, w_glu_bf, b_glu, g_ssm, bsz, seq, chunk):
    nc = seq // chunk
    seg = chunk // SUBLANES
    row = lambda b, c: (b * nc + c, 0)
    c2 = lambda b, c: (0, 0)
    c3 = lambda b, c: (0, 0, 0)
    p = jnp.arange(chunk)
    fwd = ((p % SUBLANES) * seg + p // SUBLANES)[:, None] == p[None, :]
    perm = fwd.astype(BF16)[None]
    return pl.pallas_call(
        _ssm_kernel,
        grid=(bsz, nc),
        in_specs=[
            pl.BlockSpec((chunk, SSM_WIDTH), row),
            pl.BlockSpec((N_META, SSM_WIDTH), c2),
            pl.BlockSpec((2, SSM_COL_TILES, SSM_COL, SSM_COL_LANES), lambda b, c: (0, 0, 0, 0)),
            pl.BlockSpec((2, SSM_COL_TILES, SSM_COL_LANES, SSM_COL), lambda b, c: (0, 0, 0, 0)),
            pl.BlockSpec((2, 1, SSM_LANES), c3),
            pl.BlockSpec((2, 1, SSM_LANES), c3),
            pl.BlockSpec((2, N_META, SSM_LANES), c3),
            pl.BlockSpec((1, chunk, chunk), c3),
            pl.BlockSpec((1, SSM_WIDTH), c2),
            pl.BlockSpec((SSM_WIDTH, SSM_WIDTH), c2),
            pl.BlockSpec((1, SSM_WIDTH), c2),
            pl.BlockSpec((1, SSM_WIDTH), c2),
        ],
        out_specs=pl.BlockSpec((chunk, SSM_WIDTH), row),
        out_shape=jax.ShapeDtypeStruct((bsz * seq, SSM_WIDTH), BF16),
        scratch_shapes=[
            pltpu.VMEM((chunk, SSM_LANES), F32),
            pltpu.VMEM((chunk, SSM_LANES), F32),
            pltpu.VMEM((chunk, SSM_LANES), BF16),
            pltpu.VMEM((chunk, SSM_LANES), BF16),
            pltpu.VMEM((1, SSM_LANES), F32),
            pltpu.VMEM((1, SSM_LANES), F32),
            pltpu.VMEM((SUBLANES, SSM_LANES), F32),
            pltpu.VMEM((SUBLANES, SSM_LANES), F32),
            pltpu.VMEM((SSM_WIDTH // LANES, chunk, LANES), F32),
        ],
        compiler_params=pltpu.CompilerParams(
            dimension_semantics=("arbitrary", "arbitrary"), vmem_limit_bytes=VMEM_LIMIT),
        name="ssm",
    )(u, u_meta, prm["b"], prm["c"], prm["a"], prm["a_seg"],
      prm["a_meta"], perm, d_skip, w_glu_bf, b_glu, g_ssm)


def _outproj_kernel(ya_ref, ys_ref, x_ref, wo_ref, bo_ref, gf_ref, wr_ref, br_ref,
                    h_ref, xn_ref, lg_ref):
    mix = (jnp.dot(ya_ref[...], wo_ref[:ATTN_WIDTH, :], preferred_element_type=F32)
           + jnp.dot(ys_ref[...], wo_ref[ATTN_WIDTH:, :], preferred_element_type=F32))
    h = x_ref[...] + mix + bo_ref[...]
    h_ref[...] = h
    n = _rms(h, gf_ref[...])
    n_hi = n.astype(BF16)
    bits = lax.bitcast_convert_type(n_hi.astype(F32), jnp.uint32)
    xn_ref[...] = (bits[:, :D_MODEL // 2] >> 16) | (bits[:, D_MODEL // 2:] & jnp.uint32(0xFFFF0000))
    n_lo = (n - n_hi.astype(F32)).astype(BF16)
    hi = jnp.dot(n_hi, wr_ref[...], preferred_element_type=F32)
    lo = jnp.dot(n_lo, wr_ref[:, :ROUTER_LANES], preferred_element_type=F32)
    lg_ref[...] = hi[:, :ROUTER_LANES] + hi[:, ROUTER_LANES:] + lo + br_ref[...]


def _outproj(y_attn, y_ssm, xf, w_out_bf, b_out, g_ffn, w_router_pad, b_router_pad, row_tile):
    rows = xf.shape[0]
    row = lambda i: (i, 0)
    const = lambda i: (0, 0)
    return pl.pallas_call(
        _outproj_kernel,
        grid=(rows // row_tile,),
        in_specs=[
            pl.BlockSpec((row_tile, ATTN_WIDTH), row),
            pl.BlockSpec((row_tile, SSM_WIDTH), row),
            pl.BlockSpec((row_tile, D_MODEL), row),
            pl.BlockSpec((D_MODEL, D_MODEL), const),
            pl.BlockSpec((1, D_MODEL), const),
            pl.BlockSpec((1, D_MODEL), const),
            pl.BlockSpec((D_MODEL, 2 * ROUTER_LANES), const),
            pl.BlockSpec((1, ROUTER_LANES), const),
        ],
        out_specs=[
            pl.BlockSpec((row_tile, D_MODEL), row),
            pl.BlockSpec((row_tile, D_MODEL // 2), row),
            pl.BlockSpec((row_tile, ROUTER_LANES), row),
        ],
        out_shape=[
            jax.ShapeDtypeStruct((rows, D_MODEL), F32),
            jax.ShapeDtypeStruct((rows, D_MODEL // 2), jnp.uint32),
            jax.ShapeDtypeStruct((rows, ROUTER_LANES), F32),
        ],
        compiler_params=pltpu.CompilerParams(
            dimension_semantics=("arbitrary",), vmem_limit_bytes=VMEM_LIMIT),
        name="outproj",
    )(y_attn, y_ssm, xf, w_out_bf, b_out, g_ffn, w_router_pad, b_router_pad)


def _moe_kernel(ie_ref, ir_ref, in_ref, ic_ref, ni_ref, rt_ref, rd_ref,
                xn_hbm, w1a_ref, w1b_ref, b1g_ref, b1u_ref, w2_ref, b2_ref, perm_ref, yk_hbm,
                stage, x_scr, acc, w1g, w1u, w2b, issued, sem_g, sem_out):
    i = pl.program_id(0)
    f = pl.program_id(1)
    nf = pl.num_programs(1)
    max_items = pl.num_programs(0)
    tf = w2_ref.shape[1]

    @pl.when(i < ni_ref[0])
    def _():
        r0 = ir_ref[i]
        n = in_ref[i]
        cnt = ic_ref[i]
        dump_base = yk_hbm.shape[0] - (max_items - i) * MOE_PAD
        nxt = jnp.minimum(i + 1, max_items - 1)
        n_next = jnp.where(i + 1 < ni_ref[0], in_ref[nxt], 0)
        r0_next = ir_ref[nxt]

        def gather_row(first_row, j):
            tok = rt_ref[first_row + j]
            pltpu.make_async_copy(xn_hbm.at[pl.ds(tok, 1), :], stage.at[pl.ds(j, 1), :],
                                  sem_g).start()

        def gather_wait_chunk():
            pltpu.make_async_copy(xn_hbm.at[pl.ds(0, MOE_PAD), :],
                                  stage.at[pl.ds(0, MOE_PAD), :], sem_g).wait()

        def scattered(c, carry):
            pltpu.make_async_copy(acc.at[pl.ds(0, MOE_PAD), :],
                                  yk_hbm.at[pl.ds(0, MOE_PAD), :], sem_out).wait()
            return carry

        @pl.when(f == 0)
        def _():
            @pl.when(i == 0)
            def _():
                def first(j, c):
                    gather_row(r0, j)
                    return c

                lax.fori_loop(0, n, first, 0)

            def landed(c, carry):
                gather_wait_chunk()
                return carry

            lax.fori_loop(0, n // MOE_PAD, landed, 0)

            def unpack(c, carry):
                rows_c = pl.ds(pl.multiple_of(c * MOE_PAD, MOE_PAD), MOE_PAD)
                u = stage[rows_c, :]
                x_scr[rows_c, :D_MODEL // 2] = lax.bitcast_convert_type(u << 16, F32).astype(BF16)
                x_scr[rows_c, D_MODEL // 2:] = lax.bitcast_convert_type(
                    u & jnp.uint32(0xFFFF0000), F32).astype(BF16)
                return carry

            lax.fori_loop(0, n // MOE_PAD, unpack, 0)
            issued[0] = 0

            @pl.when(i > 0)
            def _():
                lax.fori_loop(0, in_ref[jnp.maximum(i - 1, 0)] // MOE_PAD, scattered, 0)

            bias = jnp.broadcast_to(b2_ref[0], (MOE_PAD, D_MODEL))

            def init(c, _):
                acc[pl.ds(pl.multiple_of(c * MOE_PAD, MOE_PAD), MOE_PAD), :] = bias
                return 0

            lax.fori_loop(0, n // MOE_PAD, init, 0)

        half = PERM_COLS // 2
        k_half = D_MODEL // 2
        for hh, w1_ref in enumerate((w1a_ref, w1b_ref)):
            for c in range(2 * tf // PERM_COLS):
                t = jnp.dot(w1_ref[0, :, c * PERM_COLS:(c + 1) * PERM_COLS].astype(BF16),
                            perm_ref[...], preferred_element_type=F32)
                rs = slice(hh * k_half, (hh + 1) * k_half)
                w1g[rs, c * half:(c + 1) * half] = t[:, :half].astype(BF16)
                w1u[rs, c * half:(c + 1) * half] = t[:, half:].astype(BF16)
        w2b[...] = w2_ref[0].astype(BF16)

        def scatter(r, rows):
            for t in range(rows):
                dst = rd_ref[r0 + r + t]
                if t >= rows - MOE_PAD:
                    dst = jnp.where(r + t < cnt, dst, dump_base + (t - (rows - MOE_PAD)))
                pltpu.make_async_copy(acc.at[pl.ds(r + t, 1), :], yk_hbm.at[pl.ds(dst, 1), :],
                                      sem_out).start(priority=t % 2)

        def block(r, rows, prefetch):
            if prefetch:
                base = issued[0]
                for t in range(rows // 4):
                    gather_row(r0_next, base + t)
                issued[0] = base + rows // 4
            sl = pl.ds(r, rows)
            xb = x_scr[sl, :]
            g = jnp.dot(xb, w1g[...], preferred_element_type=F32) + b1g_ref[0]
            up = jnp.dot(xb, w1u[...], preferred_element_type=F32) + b1u_ref[0]
            g = jnp.minimum(g, SWIGLU_LIMIT)
            up = jnp.clip(up, -SWIGLU_LIMIT, SWIGLU_LIMIT)
            act = g * _sigmoid(SWIGLU_ALPHA * g) * (up + 1.0)
            acc[sl, :] += jnp.dot(act.astype(BF16), w2b[...], preferred_element_type=F32)

        def run(r, rows):
            can = issued[0] + rows // 4 <= n_next

            @pl.when(can)
            def _():
                block(r, rows, True)

            @pl.when(jnp.logical_not(can))
            def _():
                block(r, rows, False)

        n_big = n // MOE_BLOCK
        rem = n % MOE_BLOCK
        last = f == nf - 1

        def big(b, _):
            run(pl.multiple_of(b * MOE_BLOCK, MOE_BLOCK), MOE_BLOCK)
            return 0

        @pl.when(jnp.logical_not(last))
        def _():
            lax.fori_loop(0, n_big, big, 0)

        @pl.when(last)
        def _():
            for b in range(MOE_ITEM_ROWS // MOE_BLOCK):
                @pl.when(b < n_big)
                def _(b=b):
                    run(b * MOE_BLOCK, MOE_BLOCK)
                    scatter(b * MOE_BLOCK, MOE_BLOCK)

        def tail(r, rows):
            run(r, rows)

            @pl.when(last)
            def _():
                scatter(r, rows)

        @pl.when(rem >= 2 * MOE_PAD)
        def _():
            tail(pl.multiple_of(n_big * MOE_BLOCK, MOE_BLOCK), 2 * MOE_PAD)

        @pl.when(rem % (2 * MOE_PAD) != 0)
        def _():
            tail(pl.multiple_of(n - MOE_PAD, MOE_PAD), MOE_PAD)

        @pl.when(last)
        def _():
            @pl.when(i == ni_ref[0] - 1)
            def _():
                lax.fori_loop(0, n // MOE_PAD, scattered, 0)

            def rest(j, c):
                gather_row(r0_next, j)
                return c

            lax.fori_loop(issued[0], n_next, rest, 0)


def _experts(xn, item_e, item_r0, item_n, item_cnt, n_items, row_tok, row_dst, n_out_rows,
             w1, b1g, b1u, w2, b2, tf):
    max_items = item_e.shape[0]
    nf = D_FF // tf
    idx = jnp.arange(PERM_COLS)
    dst = jnp.where(idx % 2 == 0, idx // 2, PERM_COLS // 2 + idx // 2)
    perm = (dst[:, None] == jnp.arange(PERM_COLS)[None, :]).astype(BF16)

    def ex(i, ie, ni):
        return ie[jnp.minimum(i, ni[0] - 1)]

    def ff(i, f, ni):
        return jnp.where(i < ni[0], f, nf - 1)

    grid_spec = pltpu.PrefetchScalarGridSpec(
        num_scalar_prefetch=7,
        grid=(max_items, nf),
        in_specs=[
            pl.BlockSpec(memory_space=pl.ANY),
            pl.BlockSpec((1, D_MODEL // 2, 2 * tf), lambda i, f, ie, ir, im, ic, ni, rt, rd: (ex(i, ie, ni), 0, ff(i, f, ni))),
            pl.BlockSpec((1, D_MODEL // 2, 2 * tf), lambda i, f, ie, ir, im, ic, ni, rt, rd: (ex(i, ie, ni), 1, ff(i, f, ni))),
            pl.BlockSpec((1, 1, tf), lambda i, f, ie, ir, im, ic, ni, rt, rd: (ex(i, ie, ni), 0, ff(i, f, ni))),
            pl.BlockSpec((1, 1, tf), lambda i, f, ie, ir, im, ic, ni, rt, rd: (ex(i, ie, ni), 0, ff(i, f, ni))),
            pl.BlockSpec((1, tf, D_MODEL), lambda i, f, ie, ir, im, ic, ni, rt, rd: (ex(i, ie, ni), ff(i, f, ni), 0)),
            pl.BlockSpec((1, 1, D_MODEL), lambda i, f, ie, ir, im, ic, ni, rt, rd: (ex(i, ie, ni), 0, 0)),
            pl.BlockSpec((PERM_COLS, PERM_COLS), lambda i, f, ie, ir, im, ic, ni, rt, rd: (0, 0)),
        ],
        out_specs=pl.BlockSpec(memory_space=pl.ANY),
        scratch_shapes=[
            pltpu.VMEM((MOE_ITEM_ROWS, D_MODEL // 2), jnp.uint32),
            pltpu.VMEM((MOE_ITEM_ROWS, D_MODEL), BF16),
            pltpu.VMEM((MOE_ITEM_ROWS, D_MODEL), F32),
            pltpu.VMEM((D_MODEL, tf), BF16),
            pltpu.VMEM((D_MODEL, tf), BF16),
            pltpu.VMEM((tf, D_MODEL), BF16),
            pltpu.SMEM((1,), jnp.int32),
            pltpu.SemaphoreType.DMA(()),
            pltpu.SemaphoreType.DMA(()),
        ],
    )
    return pl.pallas_call(
        _moe_kernel,
        grid_spec=grid_spec,
        out_shape=jax.ShapeDtypeStruct((n_out_rows, D_MODEL), F32),
        compiler_params=pltpu.CompilerParams(
            dimension_semantics=("arbitrary", "arbitrary"), vmem_limit_bytes=VMEM_LIMIT),
        name="experts",
    )(item_e, item_r0, item_n, item_cnt, n_items, row_tok, row_dst, xn, w1, w1, b1g, b1u, w2, b2, perm)


def _combine_kernel(h_ref, y0_ref, y1_ref, y2_ref, y3_ref, gate_ref, g_ref, o_ref):
    acc = h_ref[...]
    gates = gate_ref[...]
    for k, y_ref in enumerate((y0_ref, y1_ref, y2_ref, y3_ref)):
        acc = acc + gates[:, k:k + 1] * y_ref[...]
    o_ref[...] = _rms(acc, g_ref[...])


def _combine(h1, yk, gates_pad, g_final, row_tile):
    rows = h1.shape[0]
    n_tiles = rows // row_tile
    row = lambda i: (i, 0)
    const = lambda i: (0, 0)
    slot = lambda k: (lambda i: (k * n_tiles + i, 0))
    return pl.pallas_call(
        _combine_kernel,
        grid=(n_tiles,),
        in_specs=[pl.BlockSpec((row_tile, D_MODEL), row)]
        + [pl.BlockSpec((row_tile, D_MODEL), slot(k)) for k in range(TOP_K)]
        + [pl.BlockSpec((row_tile, LANES), row), pl.BlockSpec((1, D_MODEL), const)],
        out_specs=pl.BlockSpec((row_tile, D_MODEL), row),
        out_shape=jax.ShapeDtypeStruct((rows, D_MODEL), F32),
        compiler_params=pltpu.CompilerParams(
            dimension_semantics=("arbitrary",), vmem_limit_bytes=VMEM_LIMIT),
        name="combine",
    )(h1, yk, yk, yk, yk, gates_pad, g_final)


def kernel(x, meta_tokens, g_mix, w_in, b_in, attn_sinks, ssm_a_re, ssm_a_im, ssm_log_dt,
           ssm_b_re, ssm_b_im, ssm_c_re, ssm_c_im, ssm_d, w_glu, b_glu, g_attn_out, g_ssm_out,
           w_out, b_out, g_ffn, w_router, b_router, w_mlp1, b_mlp1, w_mlp2, b_mlp2, g_final):
    bsz, seq, _ = x.shape
    rows = bsz * seq
    row_tile = min(ROW_TILE, seq)
    chunk = min(SSM_CHUNK, seq)
    assert seq % (2 * ATTN_BLOCK) == 0 and seq % row_tile == 0 and seq % chunk == 0
    xf = x.reshape(rows, D_MODEL)

    w_in_bf = w_in[0].astype(BF16)
    cos_r, sin_r = _rope_tables(N_META + jnp.arange(seq))
    cos_m, sin_m = _rope_tables(jnp.arange(N_META))
    qt, k, vt, u = _inproj(xf, g_mix, w_in_bf, b_in, cos_r, sin_r, row_tile, True)
    _, k_meta, v_meta, u_meta = _inproj(meta_tokens, g_mix, w_in_bf, b_in, cos_m, sin_m,
                                        N_META, False)

    y_attn = _attention(qt, k, vt, k_meta, v_meta.T, attn_sinks, g_attn_out, bsz, seq)

    prm = _ssm_params(ssm_a_re[0], ssm_a_im[0], ssm_log_dt[0], ssm_b_re[0], ssm_b_im[0],
                      ssm_c_re[0], ssm_c_im[0], chunk // SUBLANES)
    y_ssm = _ssm(u, u_meta, prm, ssm_d, w_glu[0].astype(BF16), b_glu, g_ssm_out, bsz, seq, chunk)

    w_router_pad = jnp.pad(w_router[0], ((0, 0), (0, ROUTER_LANES - N_EXPERTS)))
    b_router_pad = jnp.pad(b_router, ((0, 0), (0, ROUTER_LANES - N_EXPERTS)))
    w_router_hi = w_router_pad.astype(BF16)
    w_router_lo = (w_router_pad - w_router_hi.astype(F32)).astype(BF16)
    w_router_pad = jnp.concatenate([w_router_hi, w_router_lo], axis=1)
    h1, xn, logits = _outproj(y_attn, y_ssm, xf, w_out[0].astype(BF16), b_out, g_ffn,
                              w_router_pad, b_router_pad, row_tile)

    top_val, top_idx = lax.top_k(logits[:, :N_EXPERTS], TOP_K)
    gates = jax.nn.softmax(top_val, axis=-1)
    chosen = (top_idx[:, :, None] == jnp.arange(N_EXPERTS)[None, None, :])
    counts = jnp.sum(chosen.astype(jnp.int32), axis=(0, 1))
    padded = ((counts + MOE_PAD - 1) // MOE_PAD) * MOE_PAD
    n_assign = rows * TOP_K
    order = jnp.argsort(top_idx.reshape(-1), stable=True).astype(jnp.int32)
    row_tok = jnp.pad(order // TOP_K, (0, MOE_PAD))
    row_dst = jnp.pad((order % TOP_K) * rows + order // TOP_K, (0, MOE_PAD))
    starts = jnp.cumsum(counts) - counts
    per_e = (padded + MOE_ITEM_ROWS - 1) // MOE_ITEM_ROWS
    item_ends = jnp.cumsum(per_e)
    max_items = N_EXPERTS + n_assign // MOE_ITEM_ROWS
    slot = jnp.arange(max_items, dtype=jnp.int32)
    item_e = jnp.minimum(jnp.searchsorted(item_ends, slot, side='right'), N_EXPERTS - 1)
    piece = slot - (item_ends - per_e)[item_e]
    item_r0 = (starts[item_e] + piece * MOE_ITEM_ROWS).astype(jnp.int32)
    item_n = jnp.clip(padded[item_e] - piece * MOE_ITEM_ROWS, 0, MOE_ITEM_ROWS).astype(jnp.int32)
    item_cnt = jnp.clip(counts[item_e] - piece * MOE_ITEM_ROWS, 0, MOE_ITEM_ROWS).astype(jnp.int32)
    n_items = item_ends[-1].astype(jnp.int32).reshape(1)

    tf = MOE_FF_TILE
    b1 = b_mlp1[0].reshape(N_EXPERTS, 1, D_FF, 2)
    b2 = b_mlp2[0].reshape(N_EXPERTS, 1, D_MODEL)
    yk = _experts(xn, item_e.astype(jnp.int32), item_r0, item_n, item_cnt, n_items, row_tok, row_dst,
                  n_assign + max_items * MOE_PAD, w_mlp1[0], b1[..., 0], b1[..., 1], w_mlp2[0], b2, tf)

    gates_pad = jnp.pad(gates, ((0, 0), (0, LANES - TOP_K)))
    out = _combine(h1, yk, gates_pad, g_final.reshape(1, D_MODEL), row_tile)
    return out.reshape(bsz, seq, D_MODEL)
```

```python
import functools
import math

import jax
import jax.numpy as jnp
from jax import lax
from jax.experimental import pallas as pl
from jax.experimental.pallas import tpu as pltpu

F32 = jnp.float32
BF16 = jnp.bfloat16

D_MODEL = 2048
N_META = 16
HEAD_DIM = 64
ATTN_WIDTH = 1024
N_Q_HEADS = 16
N_KV_HEADS = 4
Q_PER_KV = 4
KV_WIDTH = 256
ATTN_BLOCK = 128
ROPE_THETA = 10000.0
SSM_WIDTH = 1024
SSM_GROUP = 16
N_SSM_GROUPS = 64
SSM_STATE = 64
SSM_LANES = N_SSM_GROUPS * SSM_STATE
IN_WIDTH = 2560
N_EXPERTS = 32
TOP_K = 4
D_FF = 2048
SWIGLU_LIMIT = 7.0
SWIGLU_ALPHA = 1.702
NORM_EPS = 1e-5

LANES = 128
SUBLANES = 8
VMEM_LIMIT = 58 * 1024 * 1024

ROW_TILE = 512
SSM_CHUNK = 256
SSM_COL = 256
SSM_COL_TILES = SSM_WIDTH // SSM_COL
SSM_COL_LANES = SSM_LANES // SSM_COL_TILES
SCAN_LANES = 512
MOE_PAD = 128
MOE_BLOCK = 512
MOE_ITEM_ROWS = 2176
MOE_FF_TILE = 256
ROUTER_LANES = 128
PERM_COLS = 256


def _rms(t, gain):
    return t * lax.rsqrt(jnp.mean(t * t, axis=-1, keepdims=True) + NORM_EPS) * gain


def _sigmoid(t):
    return 1.0 / (1.0 + jnp.exp(-t))


def _inproj_kernel(x_ref, g_ref, w_ref, b_ref, cos_ref, sin_ref, q_ref, k_ref, v_ref, u_ref,
                   *, transposed):
    n = _rms(x_ref[...], g_ref[...]).astype(BF16)
    cos = cos_ref[...]
    sin = sin_ref[...]
    lane = lax.broadcasted_iota(jnp.int32, cos.shape, 1)
    first_half = (lane % HEAD_DIM) < (HEAD_DIM // 2)

    def proj(c0, c1):
        return jnp.dot(n, w_ref[:, c0:c1], preferred_element_type=F32) + b_ref[:, c0:c1]

    def rope(t):
        partner = jnp.where(first_half, pltpu.roll(t, LANES - HEAD_DIM // 2, 1),
                            pltpu.roll(t, HEAD_DIM // 2, 1))
        return t * cos + partner * sin

    def put(ref, j, t):
        if transposed:
            ref[j * LANES:(j + 1) * LANES, :] = t.T.astype(BF16)
        else:
            ref[:, j * LANES:(j + 1) * LANES] = t.astype(BF16)

    def proj_pair(c0):
        z = proj(c0, c0 + 2 * LANES)
        return z[:, :LANES], z[:, LANES:]

    scale = HEAD_DIM ** -0.5
    for jj in range(ATTN_WIDTH // (2 * LANES)):
        a, b = proj_pair(jj * 2 * LANES)
        put(q_ref, 2 * jj, rope(a) * scale)
        put(q_ref, 2 * jj + 1, rope(b) * scale)
    a, b = proj_pair(ATTN_WIDTH)
    k_ref[:, :LANES] = rope(a).astype(BF16)
    k_ref[:, LANES:] = rope(b).astype(BF16)
    a, b = proj_pair(ATTN_WIDTH + KV_WIDTH)
    put(v_ref, 0, a)
    put(v_ref, 1, b)
    c0 = ATTN_WIDTH + 2 * KV_WIDTH
    u_ref[...] = proj(c0, c0 + SSM_WIDTH)


def _inproj(xf, g_mix, w_in_bf, b_in, cos_t, sin_t, row_tile, transposed):
    rows = xf.shape[0]
    tab_blocks = cos_t.shape[0] // row_tile
    row = lambda i: (i, 0)
    col = lambda i: (0, i)
    tab = lambda i: (i % tab_blocks, 0)
    const = lambda i: (0, 0)
    if transposed:
        q_spec, q_shape = pl.BlockSpec((ATTN_WIDTH, row_tile), col), (ATTN_WIDTH, rows)
        v_spec, v_shape = pl.BlockSpec((KV_WIDTH, row_tile), col), (KV_WIDTH, rows)
    else:
        q_spec, q_shape = pl.BlockSpec((row_tile, ATTN_WIDTH), row), (rows, ATTN_WIDTH)
        v_spec, v_shape = pl.BlockSpec((row_tile, KV_WIDTH), row), (rows, KV_WIDTH)
    return pl.pallas_call(
        functools.partial(_inproj_kernel, transposed=transposed),
        grid=(rows // row_tile,),
        in_specs=[
            pl.BlockSpec((row_tile, D_MODEL), row),
            pl.BlockSpec((1, D_MODEL), const),
            pl.BlockSpec((D_MODEL, IN_WIDTH), const),
            pl.BlockSpec((1, IN_WIDTH), const),
            pl.BlockSpec((row_tile, LANES), tab),
            pl.BlockSpec((row_tile, LANES), tab),
        ],
        out_specs=[
            q_spec,
            pl.BlockSpec((row_tile, KV_WIDTH), row),
            v_spec,
            pl.BlockSpec((row_tile, SSM_WIDTH), row),
        ],
        out_shape=[
            jax.ShapeDtypeStruct(q_shape, BF16),
            jax.ShapeDtypeStruct((rows, KV_WIDTH), BF16),
            jax.ShapeDtypeStruct(v_shape, BF16),
            jax.ShapeDtypeStruct((rows, SSM_WIDTH), F32),
        ],
        compiler_params=pltpu.CompilerParams(
            dimension_semantics=("arbitrary",), vmem_limit_bytes=VMEM_LIMIT),
        name="inproj",
    )(xf, g_mix, w_in_bf, b_in, cos_t, sin_t)


def _rope_tables(positions):
    half = HEAD_DIM // 2
    inv_freq = jnp.power(ROPE_THETA, -jnp.arange(half, dtype=F32) / half)
    ang = positions.astype(F32)[:, None] * inv_freq[None, :]
    cos = jnp.tile(jnp.cos(ang), (1, LANES // half))
    sin = jnp.tile(jnp.sin(ang), (1, LANES // half))
    sign = jnp.where((jnp.arange(LANES) % HEAD_DIM) < half, -1.0, 1.0).astype(F32)
    return cos, sin * sign[None, :]


def _attn_kernel(qt_ref, kp_ref, kc_ref, km_ref, vtp_ref, vtc_ref, vtm_ref, sink_ref, g_ref, o_ref,
                 yt_scr):
    n = pl.program_id(1)
    n_keys = 2 * ATTN_BLOCK + N_META
    key = lax.broadcasted_iota(jnp.int32, (n_keys, ATTN_BLOCK), 0)
    qi = lax.broadcasted_iota(jnp.int32, (n_keys, ATTN_BLOCK), 1)
    cur_j = key - ATTN_BLOCK
    zeros = jnp.zeros((HEAD_DIM, ATTN_BLOCK), BF16)
    invs = []
    for sub in range(2):
        qcols = slice(sub * ATTN_BLOCK, (sub + 1) * ATTN_BLOCK)
        no_prev = jnp.where(n > 0, 0, ATTN_BLOCK) if sub == 0 else 0
        valid = ((key >= 2 * ATTN_BLOCK) | ((cur_j >= 0) & (cur_j <= qi))
                 | ((key < ATTN_BLOCK) & (key > qi + no_prev)))
        bias = jnp.where(valid, 0.0, -1e30)
        bias = jnp.concatenate([bias] * Q_PER_KV, axis=1)
        ssq = jnp.zeros((1, ATTN_BLOCK), F32)
        for hk in range(N_KV_HEADS):
            lt = slice((hk // 2) * LANES, (hk // 2 + 1) * LANES)
            rs = slice(hk * HEAD_DIM, (hk + 1) * HEAD_DIM)
            if sub == 0:
                k_prev, v_prev = kp_ref[:, lt], vtp_ref[rs, :]
            else:
                k_prev, v_prev = kc_ref[:ATTN_BLOCK, lt], vtc_ref[rs, :ATTN_BLOCK]
            k_own = kc_ref[sub * ATTN_BLOCK:(sub + 1) * ATTN_BLOCK, lt]
            v_own = vtc_ref[rs, qcols]
            kt = jnp.concatenate([k_prev, k_own, km_ref[:, lt]], axis=0)
            vt = jnp.concatenate([v_prev, v_own, vtm_ref[rs, :]], axis=1)
            qs, sinks = [], []
            for g in range(Q_PER_KV):
                h = hk * Q_PER_KV + g
                qh = qt_ref[h * HEAD_DIM:(h + 1) * HEAD_DIM, qcols]
                qs.append(jnp.concatenate([qh, zeros] if hk % 2 == 0 else [zeros, qh], axis=0))
                sinks.append(jnp.broadcast_to(sink_ref[:, h:h + 1], (1, ATTN_BLOCK)))
            sink = jnp.concatenate(sinks, axis=1)
            s = jnp.dot(kt, jnp.concatenate(qs, axis=1), preferred_element_type=F32) + bias
            m = jnp.maximum(jnp.max(s, axis=0, keepdims=True), sink)
            p = jnp.exp(s - m)
            denom = jnp.sum(p, axis=0, keepdims=True) + jnp.exp(sink - m)
            o = jnp.dot(vt, p.astype(BF16), preferred_element_type=F32) * (1.0 / denom)
            for g in range(Q_PER_KV):
                h = hk * Q_PER_KV + g
                og = o[:, g * ATTN_BLOCK:(g + 1) * ATTN_BLOCK]
                yt_scr[h * HEAD_DIM:(h + 1) * HEAD_DIM, qcols] = og
                ssq = ssq + jnp.sum(og * og, axis=0, keepdims=True)
        invs.append(lax.rsqrt(ssq * (1.0 / ATTN_WIDTH) + NORM_EPS))
    inv = jnp.concatenate(invs, axis=1)
    o_ref[...] = ((yt_scr[...] * inv).T * g_ref[...]).astype(BF16)


def _attention(qt, k, vt, k_meta, vt_meta, sinks, g_attn, bsz, seq):
    nb = seq // ATTN_BLOCK
    npair = nb // 2
    own = lambda b, n: (b * npair + n, 0)
    prev = lambda b, n: (b * nb + jnp.maximum(2 * n - 1, 0), 0)
    own_t = lambda b, n: (0, b * npair + n)
    prev_t = lambda b, n: (0, b * nb + jnp.maximum(2 * n - 1, 0))
    const = lambda b, n: (0, 0)
    return pl.pallas_call(
        _attn_kernel,
        grid=(bsz, npair),
        in_specs=[
            pl.BlockSpec((ATTN_WIDTH, 2 * ATTN_BLOCK), own_t),
            pl.BlockSpec((ATTN_BLOCK, KV_WIDTH), prev),
            pl.BlockSpec((2 * ATTN_BLOCK, KV_WIDTH), own),
            pl.BlockSpec((N_META, KV_WIDTH), const),
            pl.BlockSpec((KV_WIDTH, ATTN_BLOCK), prev_t),
            pl.BlockSpec((KV_WIDTH, 2 * ATTN_BLOCK), own_t),
            pl.BlockSpec((KV_WIDTH, N_META), const),
            pl.BlockSpec((1, N_Q_HEADS), const),
            pl.BlockSpec((1, ATTN_WIDTH), const),
        ],
        out_specs=pl.BlockSpec((2 * ATTN_BLOCK, ATTN_WIDTH), own),
        out_shape=jax.ShapeDtypeStruct((bsz * seq, ATTN_WIDTH), BF16),
        scratch_shapes=[pltpu.VMEM((ATTN_WIDTH, 2 * ATTN_BLOCK), F32)],
        compiler_params=pltpu.CompilerParams(
            dimension_semantics=("arbitrary", "arbitrary"), vmem_limit_bytes=VMEM_LIMIT),
        name="attention",
    )(qt, k, k, k_meta, vt, vt, vt_meta, sinks, g_attn)


def _ssm_params(a_re, a_im, log_dt, b_re, b_im, c_re, c_im, seg_len):
    dt = jnp.exp(log_dt.astype(F32))[:, None]
    lam_re = jnp.minimum(a_re.astype(F32), -1e-4)
    lam_im = a_im.astype(F32)
    z_re, z_im = lam_re * dt, lam_im * dt
    mag = jnp.exp(z_re)
    abar_re, abar_im = mag * jnp.cos(z_im), mag * jnp.sin(z_im)
    den = lam_re * lam_re + lam_im * lam_im
    n_re, n_im = abar_re - 1.0, abar_im
    coef_re = (n_re * lam_re + n_im * lam_im) / den
    coef_im = (n_im * lam_re - n_re * lam_im) / den
    br, bi = b_re.astype(F32), b_im.astype(F32)
    bb_re = coef_re[..., None] * br - coef_im[..., None] * bi
    bb_im = coef_re[..., None] * bi + coef_im[..., None] * br

    groups_per_tile = SSM_COL // SSM_GROUP
    eye = jnp.eye(groups_per_tile, dtype=F32)

    def in_tile(bb):
        t = bb.reshape(2, SSM_COL_TILES, groups_per_tile, SSM_STATE, SSM_GROUP)
        t = jnp.einsum('xtgpc,gh->xtgchp', t, eye)
        return t.reshape(2, SSM_COL_TILES, SSM_COL, SSM_COL_LANES).astype(BF16)

    def out_tile(cc):
        t = cc.reshape(2, SSM_COL_TILES, groups_per_tile, SSM_GROUP, SSM_STATE)
        t = jnp.einsum('xtgcp,gh->xtgphc', t, eye)
        return t.reshape(2, SSM_COL_TILES, SSM_COL_LANES, SSM_COL).astype(BF16)

    def powers(exps):
        e = exps.astype(F32)[:, None, None]
        pm = jnp.exp(e * z_re[None])
        return jnp.stack([(pm * jnp.cos(e * z_im[None])).reshape(len(exps), SSM_LANES),
                          (pm * jnp.sin(e * z_im[None])).reshape(len(exps), SSM_LANES)])

    return dict(
        b=in_tile(jnp.stack([bb_re, bb_im])),
        c=out_tile(jnp.stack([c_re.astype(F32), -c_im.astype(F32)])),
        a=jnp.stack([abar_re.reshape(1, SSM_LANES), abar_im.reshape(1, SSM_LANES)]),
        a_seg=powers(jnp.array([seg_len])),
        a_meta=powers(jnp.arange(N_META - 1, -1, -1)),
    )


def _gelu_tanh(t):
    return 0.5 * t * (1.0 + jnp.tanh(math.sqrt(2.0 / math.pi) * (t + 0.044715 * (t * t * t))))


def _ssm_kernel(u_ref, um_ref, b_ref, c_ref, a_ref, aseg_ref,
                ameta_ref, perm_ref, d_ref, wglu_ref, bglu_ref, g_ref, o_ref,
                xre, xim, hre, him, car_re, car_im, cin_re, cin_im, y_scr):
    chunk = u_ref.shape[0]
    seg = chunk // SUBLANES

    @pl.when(pl.program_id(1) == 0)
    def _():
        um = um_ref[...].astype(BF16)
        for ct in range(SSM_COL_TILES):
            ub = um[:, ct * SSM_COL:(ct + 1) * SSM_COL]
            ls = slice(ct * SSM_COL_LANES, (ct + 1) * SSM_COL_LANES)
            xr = jnp.dot(ub, b_ref[0, ct], preferred_element_type=F32)
            xi = jnp.dot(ub, b_ref[1, ct], preferred_element_type=F32)
            pr = ameta_ref[0, :, ls]
            pi = ameta_ref[1, :, ls]
            car_re[:, ls] = jnp.sum(pr * xr - pi * xi, axis=0, keepdims=True)
            car_im[:, ls] = jnp.sum(pr * xi + pi * xr, axis=0, keepdims=True)

    up = jnp.dot(perm_ref[0], u_ref[...].astype(BF16), preferred_element_type=F32).astype(BF16)
    for ct in range(SSM_COL_TILES):
        ub = up[:, ct * SSM_COL:(ct + 1) * SSM_COL]
        ls = slice(ct * SSM_COL_LANES, (ct + 1) * SSM_COL_LANES)
        xre[:, ls] = jnp.dot(ub, b_ref[0, ct], preferred_element_type=F32)
        xim[:, ls] = jnp.dot(ub, b_ref[1, ct], preferred_element_type=F32)

    for lb in range(SSM_LANES // SCAN_LANES):
        ls = slice(lb * SCAN_LANES, (lb + 1) * SCAN_LANES)
        ar = jnp.broadcast_to(a_ref[0, :, ls], (SUBLANES, SCAN_LANES))
        ai = jnp.broadcast_to(a_ref[1, :, ls], (SUBLANES, SCAN_LANES))

        def end_body(k2, carry):
            hr, hi = carry
            r0 = pl.multiple_of(k2 * 2 * SUBLANES, 2 * SUBLANES)
            first = pl.ds(r0, SUBLANES)
            second = pl.ds(r0 + SUBLANES, SUBLANES)
            h0r = ar * hr - ai * hi + xre[first, ls]
            h0i = ar * hi + ai * hr + xim[first, ls]
            return ar * h0r - ai * h0i + xre[second, ls], ar * h0i + ai * h0r + xim[second, ls]

        zero = jnp.zeros((SUBLANES, SCAN_LANES), F32)
        er, ei = lax.fori_loop(0, seg // 2, end_body, (zero, zero))
        cin_re[:, ls] = er
        cin_im[:, ls] = ei

    cr = car_re[...]
    ci = car_im[...]
    sr = aseg_ref[0]
    si = aseg_ref[1]
    for r in range(SUBLANES):
        er = cin_re[r:r + 1, :]
        ei = cin_im[r:r + 1, :]
        cin_re[r:r + 1, :] = cr
        cin_im[r:r + 1, :] = ci
        cr, ci = sr * cr - si * ci + er, sr * ci + si * cr + ei
    car_re[...] = cr
    car_im[...] = ci

    pair = 2 * SUBLANES
    for lb in range(SSM_LANES // SCAN_LANES):
        ls = slice(lb * SCAN_LANES, (lb + 1) * SCAN_LANES)
        ar = jnp.broadcast_to(a_ref[0, :, ls], (SUBLANES, SCAN_LANES))
        ai = jnp.broadcast_to(a_ref[1, :, ls], (SUBLANES, SCAN_LANES))

        def scan_body(k2, carry):
            hr, hi = carry
            r0 = pl.multiple_of(k2 * pair, pair)
            first = pl.ds(r0, SUBLANES)
            second = pl.ds(r0 + SUBLANES, SUBLANES)
            h0r = ar * hr - ai * hi + xre[first, ls]
            h0i = ar * hi + ai * hr + xim[first, ls]
            h1r = ar * h0r - ai * h0i + xre[second, ls]
            h1i = ar * h0i + ai * h0r + xim[second, ls]
            hre[pl.ds(r0, pair), ls] = jnp.concatenate([h0r, h1r], axis=0).astype(BF16)
            him[pl.ds(r0, pair), ls] = jnp.concatenate([h0i, h1i], axis=0).astype(BF16)
            return h1r, h1i

        lax.fori_loop(0, seg // 2, scan_body, (cin_re[:, ls], cin_im[:, ls]))

    n_lt = SSM_WIDTH // LANES
    for ct in range(SSM_COL_TILES):
        ls = slice(ct * SSM_COL_LANES, (ct + 1) * SSM_COL_LANES)
        y = (jnp.dot(hre[:, ls], c_ref[0, ct], preferred_element_type=F32)
             + jnp.dot(him[:, ls], c_ref[1, ct], preferred_element_type=F32))
        for k in range(seg):
            for jj in range(SSM_COL // LANES):
                y_scr[ct * (SSM_COL // LANES) + jj, pl.ds(k, SUBLANES, stride=seg), :] = (
                    y[k * SUBLANES:(k + 1) * SUBLANES, jj * LANES:(jj + 1) * LANES])

    y = jnp.concatenate([y_scr[j] for j in range(n_lt)], axis=1)
    y = _gelu_tanh(y + d_ref[...] * u_ref[...])
    gate = jnp.dot(y.astype(BF16), wglu_ref[...], preferred_element_type=F32) + bglu_ref[...]
    y = y * _sigmoid(gate)
    o_ref[...] = _rms(y, g_ref[...]).astype(BF16)


def _ssm(u, u_meta, prm, d_skip, w_glu_bf, b_glu, g_ssm, bsz, seq, chunk):
    nc = seq // chunk
    seg = chunk // SUBLANES
    row = lambda b, c: (b * nc + c, 0)
    c2 = lambda b, c: (0, 0)
    c3 = lambda b, c: (0, 0, 0)
    p = jnp.arange(chunk)
    fwd = ((p % SUBLANES) * seg + p // SUBLANES)[:, None] == p[None, :]
    perm = fwd.astype(BF16)[None]
    return pl.pallas_call(
        _ssm_kernel,
        grid=(bsz, nc),
        in_specs=[
            pl.BlockSpec((chunk, SSM_WIDTH), row),
            pl.BlockSpec((N_META, SSM_WIDTH), c2),
            pl.BlockSpec((2, SSM_COL_TILES, SSM_COL, SSM_COL_LANES), lambda b, c: (0, 0, 0, 0)),
            pl.BlockSpec((2, SSM_COL_TILES, SSM_COL_LANES, SSM_COL), lambda b, c: (0, 0, 0, 0)),
            pl.BlockSpec((2, 1, SSM_LANES), c3),
            pl.BlockSpec((2, 1, SSM_LANES), c3),
            pl.BlockSpec((2, N_META, SSM_LANES), c3),
            pl.BlockSpec((1, chunk, chunk), c3),
            pl.BlockSpec((1, SSM_WIDTH), c2),
            pl.BlockSpec((SSM_WIDTH, SSM_WIDTH), c2),
            pl.BlockSpec((1, SSM_WIDTH), c2),
            pl.BlockSpec((1, SSM_WIDTH), c2),
        ],
        out_specs=pl.BlockSpec((chunk, SSM_WIDTH), row),
        out_shape=jax.ShapeDtypeStruct((bsz * seq, SSM_WIDTH), BF16),
        scratch_shapes=[
            pltpu.VMEM((chunk, SSM_LANES), F32),
            pltpu.VMEM((chunk, SSM_LANES), F32),
            pltpu.VMEM((chunk, SSM_LANES), BF16),
            pltpu.VMEM((chunk, SSM_LANES), BF16),
            pltpu.VMEM((1, SSM_LANES), F32),
            pltpu.VMEM((1, SSM_LANES), F32),
            pltpu.VMEM((SUBLANES, SSM_LANES), F32),
            pltpu.VMEM((SUBLANES, SSM_LANES), F32),
            pltpu.VMEM((SSM_WIDTH // LANES, chunk, LANES), F32),
        ],
        compiler_params=pltpu.CompilerParams(
            dimension_semantics=("arbitrary", "arbitrary"), vmem_limit_bytes=VMEM_LIMIT),
        name="ssm",
    )(u, u_meta, prm["b"], prm["c"], prm["a"], prm["a_seg"],
      prm["a_meta"], perm, d_skip, w_glu_bf, b_glu, g_ssm)


def _outproj_kernel(ya_ref, ys_ref, x_ref, wo_ref, bo_ref, gf_ref, wr_ref, br_ref,
                    h_ref, xn_ref, lg_ref):
    mix = (jnp.dot(ya_ref[...], wo_ref[:ATTN_WIDTH, :], preferred_element_type=F32)
           + jnp.dot(ys_ref[...], wo_ref[ATTN_WIDTH:, :], preferred_element_type=F32))
    h = x_ref[...] + mix + bo_ref[...]
    h_ref[...] = h
    n = _rms(h, gf_ref[...])
    n_hi = n.astype(BF16)
    bits = lax.bitcast_convert_type(n_hi.astype(F32), jnp.uint32)
    xn_ref[...] = (bits[:, :D_MODEL // 2] >> 16) | (bits[:, D_MODEL // 2:] & jnp.uint32(0xFFFF0000))
    n_lo = (n - n_hi.astype(F32)).astype(BF16)
    hi = jnp.dot(n_hi, wr_ref[...], preferred_element_type=F32)
    lo = jnp.dot(n_lo, wr_ref[:, :ROUTER_LANES], preferred_element_type=F32)
    lg_ref[...] = hi[:, :ROUTER_LANES] + hi[:, ROUTER_LANES:] + lo + br_ref[...]


def _outproj(y_attn, y_ssm, xf, w_out_bf, b_out, g_ffn, w_router_pad, b_router_pad, row_tile):
    rows = xf.shape[0]
    row = lambda i: (i, 0)
    const = lambda i: (0, 0)
    return pl.pallas_call(
        _outproj_kernel,
        grid=(rows // row_tile,),
        in_specs=[
            pl.BlockSpec((row_tile, ATTN_WIDTH), row),
            pl.BlockSpec((row_tile, SSM_WIDTH), row),
            pl.BlockSpec((row_tile, D_MODEL), row),
            pl.BlockSpec((D_MODEL, D_MODEL), const),
            pl.BlockSpec((1, D_MODEL), const),
            pl.BlockSpec((1, D_MODEL), const),
            pl.BlockSpec((D_MODEL, 2 * ROUTER_LANES), const),
            pl.BlockSpec((1, ROUTER_LANES), const),
        ],
        out_specs=[
            pl.BlockSpec((row_tile, D_MODEL), row),
            pl.BlockSpec((row_tile, D_MODEL // 2), row),
            pl.BlockSpec((row_tile, ROUTER_LANES), row),
        ],
        out_shape=[
            jax.ShapeDtypeStruct((rows, D_MODEL), F32),
            jax.ShapeDtypeStruct((rows, D_MODEL // 2), jnp.uint32),
            jax.ShapeDtypeStruct((rows, ROUTER_LANES), F32),
        ],
        compiler_params=pltpu.CompilerParams(
            dimension_semantics=("arbitrary",), vmem_limit_bytes=VMEM_LIMIT),
        name="outproj",
    )(y_attn, y_ssm, xf, w_out_bf, b_out, g_ffn, w_router_pad, b_router_pad)


def _moe_kernel(ie_ref, ir_ref, in_ref, ic_ref, ni_ref, rt_ref, rd_ref,
                xn_hbm, w1a_ref, w1b_ref, b1g_ref, b1u_ref, w2_ref, b2_ref, perm_ref, yk_hbm,
                stage, x_scr, acc, w1g, w1u, w2b, issued, sem_g, sem_out):
    i = pl.program_id(0)
    f = pl.program_id(1)
    nf = pl.num_programs(1)
    max_items = pl.num_programs(0)
    tf = w2_ref.shape[1]

    @pl.when(i < ni_ref[0])
    def _():
        r0 = ir_ref[i]
        n = in_ref[i]
        cnt = ic_ref[i]
        dump_base = yk_hbm.shape[0] - (max_items - i) * MOE_PAD
        nxt = jnp.minimum(i + 1, max_items - 1)
        n_next = jnp.where(i + 1 < ni_ref[0], in_ref[nxt], 0)
        r0_next = ir_ref[nxt]

        def gather_row(first_row, j, priority=0):
            tok = rt_ref[first_row + j]
            pltpu.make_async_copy(xn_hbm.at[pl.ds(tok, 1), :], stage.at[pl.ds(j, 1), :],
                                  sem_g).start(priority=priority)

        def gather_wait_chunk():
            pltpu.make_async_copy(xn_hbm.at[pl.ds(0, MOE_PAD), :],
                                  stage.at[pl.ds(0, MOE_PAD), :], sem_g).wait()

        def scattered(c, carry):
            pltpu.make_async_copy(acc.at[pl.ds(0, MOE_PAD), :],
                                  yk_hbm.at[pl.ds(0, MOE_PAD), :], sem_out).wait()
            return carry

        @pl.when(f == 0)
        def _():
            @pl.when(i == 0)
            def _():
                def first(j, c):
                    gather_row(r0, j)
                    return c

                lax.fori_loop(0, n, first, 0)

            def landed(c, carry):
                gather_wait_chunk()
                return carry

            lax.fori_loop(0, n // MOE_PAD, landed, 0)

            def unpack(c, carry):
                rows_c = pl.ds(pl.multiple_of(c * MOE_PAD, MOE_PAD), MOE_PAD)
                u = stage[rows_c, :]
                x_scr[rows_c, :D_MODEL // 2] = lax.bitcast_convert_type(u << 16, F32).astype(BF16)
                x_scr[rows_c, D_MODEL // 2:] = lax.bitcast_convert_type(
                    u & jnp.uint32(0xFFFF0000), F32).astype(BF16)
                return carry

            lax.fori_loop(0, n // MOE_PAD, unpack, 0)
            issued[0] = 0

            @pl.when(i > 0)
            def _():
                lax.fori_loop(0, in_ref[jnp.maximum(i - 1, 0)] // MOE_PAD, scattered, 0)

            bias = jnp.broadcast_to(b2_ref[0], (MOE_PAD, D_MODEL))

            def init(c, _):
                acc[pl.ds(pl.multiple_of(c * MOE_PAD, MOE_PAD), MOE_PAD), :] = bias
                return 0

            lax.fori_loop(0, n // MOE_PAD, init, 0)

        half = PERM_COLS // 2
        k_half = D_MODEL // 2
        for hh, w1_ref in enumerate((w1a_ref, w1b_ref)):
            for c in range(2 * tf // PERM_COLS):
                t = jnp.dot(w1_ref[0, :, c * PERM_COLS:(c + 1) * PERM_COLS].astype(BF16),
                            perm_ref[...], preferred_element_type=F32)
                rs = slice(hh * k_half, (hh + 1) * k_half)
                w1g[rs, c * half:(c + 1) * half] = t[:, :half].astype(BF16)
                w1u[rs, c * half:(c + 1) * half] = t[:, half:].astype(BF16)
        w2b[...] = w2_ref[0].astype(BF16)

        def scatter(r, rows):
            for t in range(rows):
                dst = rd_ref[r0 + r + t]
                if t >= rows - MOE_PAD:
                    dst = jnp.where(r + t < cnt, dst, dump_base + (t - (rows - MOE_PAD)))
                pltpu.make_async_copy(acc.at[pl.ds(r + t, 1), :], yk_hbm.at[pl.ds(dst, 1), :],
                                      sem_out).start(priority=t % 2)

        def block(r, rows, prefetch):
            if prefetch:
                base = issued[0]
                for t in range(rows // 4):
                    gather_row(r0_next, base + t, t % 2)
                issued[0] = base + rows // 4
            sl = pl.ds(r, rows)
            xb = x_scr[sl, :]
            g = jnp.dot(xb, w1g[...], preferred_element_type=F32) + b1g_ref[0]
            up = jnp.dot(xb, w1u[...], preferred_element_type=F32) + b1u_ref[0]
            g = jnp.minimum(g, SWIGLU_LIMIT)
            up = jnp.clip(up, -SWIGLU_LIMIT, SWIGLU_LIMIT)
            act = g * _sigmoid(SWIGLU_ALPHA * g) * (up + 1.0)
            acc[sl, :] += jnp.dot(act.astype(BF16), w2b[...], preferred_element_type=F32)

        def run(r, rows):
            can = issued[0] + rows // 4 <= n_next

            @pl.when(can)
            def _():
                block(r, rows, True)

            @pl.when(jnp.logical_not(can))
            def _():
                block(r, rows, False)

        n_big = n // MOE_BLOCK
        rem = n % MOE_BLOCK
        last = f == nf - 1

        def big(b, _):
            run(pl.multiple_of(b * MOE_BLOCK, MOE_BLOCK), MOE_BLOCK)
            return 0

        @pl.when(jnp.logical_not(last))
        def _():
            lax.fori_loop(0, n_big, big, 0)

        @pl.when(last)
        def _():
            for b in range(MOE_ITEM_ROWS // MOE_BLOCK):
                @pl.when(b < n_big)
                def _(b=b):
                    run(b * MOE_BLOCK, MOE_BLOCK)
                    scatter(b * MOE_BLOCK, MOE_BLOCK)

        def tail(r, rows):
            run(r, rows)

            @pl.when(last)
            def _():
                scatter(r, rows)

        @pl.when(rem >= 2 * MOE_PAD)
        def _():
            tail(pl.multiple_of(n_big * MOE_BLOCK, MOE_BLOCK), 2 * MOE_PAD)

        @pl.when(rem % (2 * MOE_PAD) != 0)
        def _():
            tail(pl.multiple_of(n - MOE_PAD, MOE_PAD), MOE_PAD)

        @pl.when(last)
        def _():
            @pl.when(i == ni_ref[0] - 1)
            def _():
                lax.fori_loop(0, n // MOE_PAD, scattered, 0)

            def rest(j, c):
                gather_row(r0_next, j)
                return c

            lax.fori_loop(issued[0], n_next, rest, 0)


def _experts(xn, item_e, item_r0, item_n, item_cnt, n_items, row_tok, row_dst, n_out_rows,
             w1, b1g, b1u, w2, b2, tf):
    max_items = item_e.shape[0]
    nf = D_FF // tf
    idx = jnp.arange(PERM_COLS)
    dst = jnp.where(idx % 2 == 0, idx // 2, PERM_COLS // 2 + idx // 2)
    perm = (dst[:, None] == jnp.arange(PERM_COLS)[None, :]).astype(BF16)

    def ex(i, ie, ni):
        return ie[jnp.minimum(i, ni[0] - 1)]

    def ff(i, f, ni):
        return jnp.where(i < ni[0], f, nf - 1)

    grid_spec = pltpu.PrefetchScalarGridSpec(
        num_scalar_prefetch=7,
        grid=(max_items, nf),
        in_specs=[
            pl.BlockSpec(memory_space=pl.ANY),
            pl.BlockSpec((1, D_MODEL // 2, 2 * tf), lambda i, f, ie, ir, im, ic, ni, rt, rd: (ex(i, ie, ni), 0, ff(i, f, ni))),
            pl.BlockSpec((1, D_MODEL // 2, 2 * tf), lambda i, f, ie, ir, im, ic, ni, rt, rd: (ex(i, ie, ni), 1, ff(i, f, ni))),
            pl.BlockSpec((1, 1, tf), lambda i, f, ie, ir, im, ic, ni, rt, rd: (ex(i, ie, ni), 0, ff(i, f, ni))),
            pl.BlockSpec((1, 1, tf), lambda i, f, ie, ir, im, ic, ni, rt, rd: (ex(i, ie, ni), 0, ff(i, f, ni))),
            pl.BlockSpec((1, tf, D_MODEL), lambda i, f, ie, ir, im, ic, ni, rt, rd: (ex(i, ie, ni), ff(i, f, ni), 0)),
            pl.BlockSpec((1, 1, D_MODEL), lambda i, f, ie, ir, im, ic, ni, rt, rd: (ex(i, ie, ni), 0, 0)),
            pl.BlockSpec((PERM_COLS, PERM_COLS), lambda i, f, ie, ir, im, ic, ni, rt, rd: (0, 0)),
        ],
        out_specs=pl.BlockSpec(memory_space=pl.ANY),
        scratch_shapes=[
            pltpu.VMEM((MOE_ITEM_ROWS, D_MODEL // 2), jnp.uint32),
            pltpu.VMEM((MOE_ITEM_ROWS, D_MODEL), BF16),
            pltpu.VMEM((MOE_ITEM_ROWS, D_MODEL), F32),
            pltpu.VMEM((D_MODEL, tf), BF16),
            pltpu.VMEM((D_MODEL, tf), BF16),
            pltpu.VMEM((tf, D_MODEL), BF16),
            pltpu.SMEM((1,), jnp.int32),
            pltpu.SemaphoreType.DMA(()),
            pltpu.SemaphoreType.DMA(()),
        ],
    )
    return pl.pallas_call(
        _moe_kernel,
        grid_spec=grid_spec,
        out_shape=jax.ShapeDtypeStruct((n_out_rows, D_MODEL), F32),
        compiler_params=pltpu.CompilerParams(
            dimension_semantics=("arbitrary", "arbitrary"), vmem_limit_bytes=VMEM_LIMIT),
        name="experts",
    )(item_e, item_r0, item_n, item_cnt, n_items, row_tok, row_dst, xn, w1, w1, b1g, b1u, w2, b2, perm)


def _combine_kernel(h_ref, y0_ref, y1_ref, y2_ref, y3_ref, gate_ref, g_ref, o_ref):
    acc = h_ref[...]
    gates = gate_ref[...]
    for k, y_ref in enumerate((y0_ref, y1_ref, y2_ref, y3_ref)):
        acc = acc + gates[:, k:k + 1] * y_ref[...]
    o_ref[...] = _rms(acc, g_ref[...])


def _combine(h1, yk, gates_pad, g_final, row_tile):
    rows = h1.shape[0]
    n_tiles = rows // row_tile
    row = lambda i: (i, 0)
    const = lambda i: (0, 0)
    slot = lambda k: (lambda i: (k * n_tiles + i, 0))
    return pl.pallas_call(
        _combine_kernel,
        grid=(n_tiles,),
        in_specs=[pl.BlockSpec((row_tile, D_MODEL), row)]
        + [pl.BlockSpec((row_tile, D_MODEL), slot(k)) for k in range(TOP_K)]
        + [pl.BlockSpec((row_tile, LANES), row), pl.BlockSpec((1, D_MODEL), const)],
        out_specs=pl.BlockSpec((row_tile, D_MODEL), row),
        out_shape=jax.ShapeDtypeStruct((rows, D_MODEL), F32),
        compiler_params=pltpu.CompilerParams(
            dimension_semantics=("arbitrary",), vmem_limit_bytes=VMEM_LIMIT),
        name="combine",
    )(h1, yk, yk, yk, yk, gates_pad, g_final)


def kernel(x, meta_tokens, g_mix, w_in, b_in, attn_sinks, ssm_a_re, ssm_a_im, ssm_log_dt,
           ssm_b_re, ssm_b_im, ssm_c_re, ssm_c_im, ssm_d, w_glu, b_glu, g_attn_out, g_ssm_out,
           w_out, b_out, g_ffn, w_router, b_router, w_mlp1, b_mlp1, w_mlp2, b_mlp2, g_final):
    bsz, seq, _ = x.shape
    rows = bsz * seq
    row_tile = min(ROW_TILE, seq)
    chunk = min(SSM_CHUNK, seq)
    assert seq % (2 * ATTN_BLOCK) == 0 and seq % row_tile == 0 and seq % chunk == 0
    xf = x.reshape(rows, D_MODEL)

    w_in_bf = w_in[0].astype(BF16)
    cos_r, sin_r = _rope_tables(N_META + jnp.arange(seq))
    cos_m, sin_m = _rope_tables(jnp.arange(N_META))
    qt, k, vt, u = _inproj(xf, g_mix, w_in_bf, b_in, cos_r, sin_r, row_tile, True)
    _, k_meta, v_meta, u_meta = _inproj(meta_tokens, g_mix, w_in_bf, b_in, cos_m, sin_m,
                                        N_META, False)

    y_attn = _attention(qt, k, vt, k_meta, v_meta.T, attn_sinks, g_attn_out, bsz, seq)

    prm = _ssm_params(ssm_a_re[0], ssm_a_im[0], ssm_log_dt[0], ssm_b_re[0], ssm_b_im[0],
                      ssm_c_re[0], ssm_c_im[0], chunk // SUBLANES)
    y_ssm = _ssm(u, u_meta, prm, ssm_d, w_glu[0].astype(BF16), b_glu, g_ssm_out, bsz, seq, chunk)

    w_router_pad = jnp.pad(w_router[0], ((0, 0), (0, ROUTER_LANES - N_EXPERTS)))
    b_router_pad = jnp.pad(b_router, ((0, 0), (0, ROUTER_LANES - N_EXPERTS)))
    w_router_hi = w_router_pad.astype(BF16)
    w_router_lo = (w_router_pad - w_router_hi.astype(F32)).astype(BF16)
    w_router_pad = jnp.concatenate([w_router_hi, w_router_lo], axis=1)
    h1, xn, logits = _outproj(y_attn, y_ssm, xf, w_out[0].astype(BF16), b_out, g_ffn,
                              w_router_pad, b_router_pad, row_tile)

    top_val, top_idx = lax.top_k(logits[:, :N_EXPERTS], TOP_K)
    gates = jax.nn.softmax(top_val, axis=-1)
    chosen = (top_idx[:, :, None] == jnp.arange(N_EXPERTS)[None, None, :])
    counts = jnp.sum(chosen.astype(jnp.int32), axis=(0, 1))
    padded = ((counts + MOE_PAD - 1) // MOE_PAD) * MOE_PAD
    n_assign = rows * TOP_K
    order = jnp.argsort(top_idx.reshape(-1), stable=True).astype(jnp.int32)
    row_tok = jnp.pad(order // TOP_K, (0, MOE_PAD))
    row_dst = jnp.pad((order % TOP_K) * rows + order // TOP_K, (0, MOE_PAD))
    starts = jnp.cumsum(counts) - counts
    per_e = (padded + MOE_ITEM_ROWS - 1) // MOE_ITEM_ROWS
    item_ends = jnp.cumsum(per_e)
    max_items = N_EXPERTS + n_assign // MOE_ITEM_ROWS
    slot = jnp.arange(max_items, dtype=jnp.int32)
    item_e = jnp.minimum(jnp.searchsorted(item_ends, slot, side='right'), N_EXPERTS - 1)
    piece = slot - (item_ends - per_e)[item_e]
    item_r0 = (starts[item_e] + piece * MOE_ITEM_ROWS).astype(jnp.int32)
    item_n = jnp.clip(padded[item_e] - piece * MOE_ITEM_ROWS, 0, MOE_ITEM_ROWS).astype(jnp.int32)
    item_cnt = jnp.clip(counts[item_e] - piece * MOE_ITEM_ROWS, 0, MOE_ITEM_ROWS).astype(jnp.int32)
    n_items = item_ends[-1].astype(jnp.int32).reshape(1)

    tf = MOE_FF_TILE
    b1 = b_mlp1[0].reshape(N_EXPERTS, 1, D_FF, 2)
    b2 = b_mlp2[0].reshape(N_EXPERTS, 1, D_MODEL)
    yk = _experts(xn, item_e.astype(jnp.int32), item_r0, item_n, item_cnt, n_items, row_tok, row_dst,
                  n_assign + max_items * MOE_PAD, w_mlp1[0], b1[..., 0], b1[..., 1], w_mlp2[0], b2, tf)

    gates_pad = jnp.pad(gates, ((0, 0), (0, LANES - TOP_K)))
    out = _combine(h1, yk, gates_pad, g_final.reshape(1, D_MODEL), row_tile)
    return out.reshape(bsz, seq, D_MODEL)
```
